```python
import math
import jax
import jax.numpy as jnp
from jax import lax
import numpy as np

D_MODEL = 1024
BATCH = 8
SEQ = 4096
DEPTH = 1

N_HEADS = 8
HEAD_DIM = 64
N_KV = 2
HPG = N_HEADS // N_KV
CMP_STRIDE = 16
CMP_BLOCK = 2 * CMP_STRIDE
CMP_HIDDEN = 128
SLC_BLOCK = 64
N_SEL = 16
WINDOW = 512
Q_BLOCK = 128
REL_BUCKETS = 32
REL_MAX_DIST = 128
S5_WIDTH = 512
S5_GROUP = 16
S5_GROUPS = S5_WIDTH // S5_GROUP
S5_STATE = 64
S5_DT_MIN = 1e-3
S5_DT_MAX = 1e-1
N_EGROUPS = 8
EXPERTS_PER_GROUP = 8
N_EXPERTS = N_EGROUPS * EXPERTS_PER_GROUP
TOP_K_IN_GROUP = 2
D_EXPERT = 256
EXPERT_BLOCK = 128
DN_ALPHA = (2.0 * DEPTH) ** 0.25
DN_BETA = (8.0 * DEPTH) ** -0.25
LN_EPS = 1e-5
NEG_INF = -1e30
BIG = 1e9

ATTN_WIDTH = N_HEADS * HEAD_DIM
KV_WIDTH = N_KV * HEAD_DIM
KV_OFF = ATTN_WIDTH
NSA_GATE_OFF = KV_OFF + 6 * KV_WIDTH
S5_OFF = NSA_GATE_OFF + 3 * N_HEADS
MERGE_OFF = S5_OFF + S5_WIDTH
D_IN = MERGE_OFF + 2 * D_MODEL

kernel_name = 'hybrid_nsa_s5_hmoe_deepnorm'


def layer_norm(x, g, b):
    xf = x.astype(jnp.float32)
    mu = jnp.mean(xf, -1, keepdims=True)
    var = jnp.mean(jnp.square(xf - mu), -1, keepdims=True)
    return ((xf - mu) * lax.rsqrt(var + LN_EPS) * g.astype(jnp.float32) + b.astype(jnp.float32)).astype(x.dtype)


def t5_bucket(dist):
    n = jnp.maximum(dist, 0)
    max_exact = REL_BUCKETS // 2
    nf = jnp.maximum(n, 1).astype(jnp.float32)
    large = max_exact + (jnp.log(nf / max_exact) / math.log(REL_MAX_DIST / max_exact)
                         * (REL_BUCKETS - max_exact)).astype(jnp.int32)
    large = jnp.minimum(large, REL_BUCKETS - 1)
    return jnp.where(n < max_exact, n, large)


def masked_softmax(s, mask):
    p = jax.nn.softmax(jnp.where(mask, s, NEG_INF), axis=-1)
    return jnp.where(mask, p, 0.0)


def nsa_attention(q, kv, gates, rel_bias, cmp_pos, cmp_w1, cmp_b1, cmp_w2, cmp_b2):
    B, L = q.shape[:2]
    f32 = jnp.float32
    scale = HEAD_DIM ** -0.5
    qg = q.reshape(B, L, N_KV, HPG, HEAD_DIM).transpose(0, 2, 3, 1, 4)
    pos = jnp.arange(L)
    tbl = rel_bias.astype(f32)

    n_cmp = L // CMP_STRIDE - 1

    def compress(t, i):
        c = t.reshape(B, L // CMP_STRIDE, CMP_STRIDE, N_KV, HEAD_DIM)
        blk = jnp.concatenate([c[:, :-1], c[:, 1:]], axis=2)
        blk = blk + cmp_pos[i][None, None, :, None, :]
        blk = blk.transpose(0, 3, 1, 2, 4).reshape(B, N_KV, n_cmp, CMP_BLOCK * HEAD_DIM)
        hid = jax.nn.gelu(blk @ cmp_w1[i] + cmp_b1[i])
        return hid @ cmp_w2[i] + cmp_b2[i]

    kc = compress(kv[:, :, 0], 0)
    vc = compress(kv[:, :, 1], 1)
    cmp_start = jnp.arange(n_cmp) * CMP_STRIDE
    dist_c = pos[:, None] - (cmp_start + CMP_BLOCK - 1)[None, :]
    bias_c = tbl[t5_bucket(dist_c)].transpose(2, 0, 1).reshape(N_KV, HPG, L, n_cmp)
    s_c = jnp.einsum('bghqd,bgkd->bghqk', qg, kc).astype(f32) * scale + bias_c
    p_c = masked_softmax(s_c, dist_c >= 0)
    o_cmp = jnp.einsum('bghqk,bgkd->bghqd', p_c.astype(vc.dtype), vc)

    n_blk = L // SLC_BLOCK
    blk_ids = jnp.arange(n_blk)
    blk_start = blk_ids * SLC_BLOCK
    overlap = ((cmp_start[:, None] <= blk_start[None, :] + SLC_BLOCK - 1)
               & (cmp_start[:, None] + CMP_BLOCK - 1 >= blk_start[None, :])).astype(f32)
    imp = jnp.einsum('bghqk,kn->bgqn', p_c, overlap)
    cur = pos // SLC_BLOCK
    forced = (blk_ids[None, :] == 0) | (blk_ids[None, :] == cur[:, None]) | (blk_ids[None, :] == cur[:, None] - 1)
    valid = blk_start[None, :] <= pos[:, None]
    score = jnp.where(forced, BIG, jnp.where(valid, imp, -BIG))
    n_sel = min(N_SEL, n_blk)
    _, sel_idx = lax.top_k(score, n_sel)

    ks = kv[:, :, 2].transpose(0, 2, 1, 3).reshape(B, N_KV, n_blk, SLC_BLOCK, HEAD_DIM)
    vs = kv[:, :, 3].transpose(0, 2, 1, 3).reshape(B, N_KV, n_blk, SLC_BLOCK, HEAD_DIM)
    pad = ((0, 0), (0, 0), (WINDOW, 0), (0, 0))
    kw = jnp.pad(kv[:, :, 4].transpose(0, 2, 1, 3), pad)
    vw = jnp.pad(kv[:, :, 5].transpose(0, 2, 1, 3), pad)
    tbl_g = tbl.reshape(REL_BUCKETS, N_KV, HPG).transpose(1, 0, 2)
    bi = jnp.arange(B)[:, None, None, None]
    gi = jnp.arange(N_KV)[None, :, None, None]

    def query_block(c):
        q0 = c * Q_BLOCK
        qc = lax.dynamic_slice_in_dim(qg, q0, Q_BLOCK, axis=3)
        tq = q0 + jnp.arange(Q_BLOCK)
        idx = lax.dynamic_slice_in_dim(sel_idx, q0, Q_BLOCK, axis=2)
        kg = ks[bi, gi, idx]
        vg = vs[bi, gi, idx]
        kpos = idx[..., None] * SLC_BLOCK + jnp.arange(SLC_BLOCK)
        dist = tq[None, None, :, None, None] - kpos
        bias = tbl_g[gi[..., None], t5_bucket(dist)].transpose(0, 1, 5, 2, 3, 4)
        s = jnp.einsum('bghqd,bgqnsd->bghqns', qc, kg).astype(f32) * scale + bias
        s = s.reshape(B, N_KV, HPG, Q_BLOCK, n_sel * SLC_BLOCK)
        m = (dist >= 0).reshape(B, N_KV, 1, Q_BLOCK, n_sel * SLC_BLOCK)
        p = masked_softmax(s, m)
        o_s = jnp.einsum('bghqm,bgqmd->bghqd', p.astype(vg.dtype),
                         vg.reshape(B, N_KV, Q_BLOCK, n_sel * SLC_BLOCK, HEAD_DIM))
        kwc = lax.dynamic_slice_in_dim(kw, q0, Q_BLOCK + WINDOW, axis=2)
        vwc = lax.dynamic_slice_in_dim(vw, q0, Q_BLOCK + WINDOW, axis=2)
        kpos_w = q0 - WINDOW + jnp.arange(Q_BLOCK + WINDOW)
        dist_w = tq[:, None] - kpos_w[None, :]
        mask_w = (dist_w >= 0) & (dist_w < WINDOW) & (kpos_w[None, :] >= 0)
        bias_w = tbl[t5_bucket(dist_w)].transpose(2, 0, 1).reshape(N_KV, HPG, Q_BLOCK, Q_BLOCK + WINDOW)
        s_w = jnp.einsum('bghqd,bgkd->bghqk', qc, kwc).astype(f32) * scale + bias_w
        p_w = masked_softmax(s_w, mask_w)
        o_w = jnp.einsum('bghqk,bgkd->bghqd', p_w.astype(vwc.dtype), vwc)
        return o_s, o_w

    o_slc, o_win = lax.map(query_block, jnp.arange(L // Q_BLOCK))
    o_slc = o_slc.transpose(1, 0, 4, 2, 3, 5).reshape(B, L, N_HEADS, HEAD_DIM)
    o_win = o_win.transpose(1, 0, 4, 2, 3, 5).reshape(B, L, N_HEADS, HEAD_DIM)
    o_cmp = o_cmp.transpose(0, 3, 1, 2, 4).reshape(B, L, N_HEADS, HEAD_DIM)
    o = gates[..., 0:1] * o_cmp + gates[..., 1:2] * o_slc + gates[..., 2:3] * o_win
    return o.astype(q.dtype).reshape(B, L, ATTN_WIDTH)


def s5_layer(u, lam_re, lam_im, log_dt, b_re, b_im, c_re, c_im, d_skip):
    B, L, _ = u.shape
    f32 = jnp.float32
    uf = u.astype(f32).reshape(B, L, S5_GROUPS, S5_GROUP)
    dt = jnp.exp(log_dt.astype(f32))[:, None]
    lr, li = lam_re.astype(f32), lam_im.astype(f32)
    mag = jnp.exp(lr * dt)
    ab_re, ab_im = mag * jnp.cos(li * dt), mag * jnp.sin(li * dt)
    nr, ni = ab_re - 1.0, ab_im
    den = lr * lr + li * li
    fr, fi = (nr * lr + ni * li) / den, (ni * lr - nr * li) / den
    br, bim = b_re.astype(f32), b_im.astype(f32)
    bb_re = fr[..., None] * br - fi[..., None] * bim
    bb_im = fr[..., None] * bim + fi[..., None] * br
    bu_re = jnp.einsum('blgh,gph->blgp', uf, bb_re)
    bu_im = jnp.einsum('blgh,gph->blgp', uf, bb_im)
    a_re = jnp.broadcast_to(ab_re[None, None], (1, L, S5_GROUPS, S5_STATE))
    a_im = jnp.broadcast_to(ab_im[None, None], (1, L, S5_GROUPS, S5_STATE))

    def combine(e1, e2):
        a1r, a1i, b1r, b1i = e1
        a2r, a2i, b2r, b2i = e2
        return (a2r * a1r - a2i * a1i, a2r * a1i + a2i * a1r,
                a2r * b1r - a2i * b1i + b2r, a2r * b1i + a2i * b1r + b2i)

    _, _, xr, xi = lax.associative_scan(combine, (a_re, a_im, bu_re, bu_im), axis=1)
    y = (jnp.einsum('blgp,ghp->blgh', xr, c_re.astype(f32))
         - jnp.einsum('blgp,ghp->blgh', xi, c_im.astype(f32))
         + d_skip.astype(f32) * uf)
    return y.reshape(B, L, S5_WIDTH).astype(u.dtype)


def hier_moe(h, w_rg, b_rg, w_re, b_re, w_gate, w_up, w_down):
    B, L, D = h.shape
    N = B * L
    f32 = jnp.float32
    xt = h.reshape(N, D)
    tok_ids = jnp.arange(N)
    g_logits = (xt @ w_rg).astype(f32) + b_rg.astype(f32)
    g_prob = jax.nn.softmax(g_logits, -1)
    g_top = jnp.argmax(g_logits, -1)
    p_group = g_prob[tok_ids, g_top][:, None]
    e_logits = ((xt @ w_re).astype(f32) + b_re.astype(f32)).reshape(N, N_EGROUPS, EXPERTS_PER_GROUP)
    e_logits = e_logits[tok_ids, g_top]
    top_v, top_i = lax.top_k(e_logits, TOP_K_IN_GROUP)
    w = jax.nn.softmax(top_v, -1) * p_group
    expert = g_top[:, None].astype(jnp.int32) * EXPERTS_PER_GROUP + top_i
    A = N * TOP_K_IN_GROUP
    e_flat = expert.reshape(A)
    w_flat = w.reshape(A)
    tok = jnp.repeat(tok_ids, TOP_K_IN_GROUP)
    order = jnp.argsort(e_flat)
    e_s, tok_s, w_s = e_flat[order], tok[order], w_flat[order]
    counts = jnp.bincount(e_flat, length=N_EXPERTS)
    padded = (counts + EXPERT_BLOCK - 1) // EXPERT_BLOCK * EXPERT_BLOCK
    start = jnp.cumsum(counts) - counts
    pend = jnp.cumsum(padded)
    pstart = pend - padded
    dest = pstart[e_s] + (jnp.arange(A) - start[e_s])
    n_blocks = -(-A // EXPERT_BLOCK) + N_EXPERTS
    cap = n_blocks * EXPERT_BLOCK
    row_tok = jnp.full((cap,), N, jnp.int32).at[dest].set(tok_s)
    blk_expert = jnp.minimum(jnp.searchsorted(pend, jnp.arange(n_blocks) * EXPERT_BLOCK, side='right'),
                             N_EXPERTS - 1)
    x_pad = jnp.concatenate([xt, jnp.zeros((1, D), xt.dtype)], axis=0)

    def run_block(args):
        rows, e = args
        xb = x_pad[rows]
        hb = jax.nn.silu(xb @ w_gate[e]) * (xb @ w_up[e])
        return hb @ w_down[e]

    yb = lax.map(run_block, (row_tok.reshape(n_blocks, EXPERT_BLOCK), blk_expert))
    y_assign = yb.reshape(cap, D)[dest].astype(f32) * w_s[:, None]
    out = jax.ops.segment_sum(y_assign, tok_s, num_segments=N)
    return out.astype(h.dtype).reshape(B, L, D)


def setup_inputs(seed: int = 0) -> dict:
    key = jax.random.key(seed)
    ks = jax.random.split(key, 40)
    f32 = jnp.float32

    def nrm(k, shape, s):
        return jax.random.normal(k, shape, f32) * s

    Dp = DEPTH
    col_scale = np.ones((D_IN,), np.float32)
    for j in (1, 3, 5):
        col_scale[KV_OFF + j * KV_WIDTH: KV_OFF + (j + 1) * KV_WIDTH] = DN_BETA
    col_scale = jnp.asarray(col_scale)
    lam_im0 = jnp.broadcast_to(math.pi * jnp.arange(S5_STATE, dtype=f32), (Dp, S5_GROUPS, S5_STATE))
    return {
        'x': nrm(ks[0], (BATCH, SEQ, D_MODEL), 1.0),
        'rel_bias': nrm(ks[1], (REL_BUCKETS, N_HEADS), 0.1),
        'w_in': nrm(ks[2], (Dp, D_MODEL, D_IN), D_MODEL ** -0.5) * col_scale,
        'b_in': nrm(ks[3], (Dp, D_IN), 0.01),
        'cmp_pos': nrm(ks[4], (Dp, 2, CMP_BLOCK, HEAD_DIM), 0.1),
        'cmp_w1': nrm(ks[5], (Dp, 2, CMP_BLOCK * HEAD_DIM, CMP_HIDDEN), (CMP_BLOCK * HEAD_DIM) ** -0.5),
        'cmp_b1': nrm(ks[6], (Dp, 2, CMP_HIDDEN), 0.01),
        'cmp_w2': nrm(ks[7], (Dp, 2, CMP_HIDDEN, HEAD_DIM), CMP_HIDDEN ** -0.5),
        'cmp_b2': nrm(ks[8], (Dp, 2, HEAD_DIM), 0.01),
        'w_attn_up': nrm(ks[9], (Dp, ATTN_WIDTH, D_MODEL), ATTN_WIDTH ** -0.5) * DN_BETA,
        's5_lambda_re': -0.5 + nrm(ks[10], (Dp, S5_GROUPS, S5_STATE), 0.01),
        's5_lambda_im': lam_im0 + nrm(ks[11], (Dp, S5_GROUPS, S5_STATE), 0.01),
        's5_log_dt': jax.random.uniform(ks[12], (Dp, S5_GROUPS), f32, math.log(S5_DT_MIN), math.log(S5_DT_MAX)),
        's5_b_re': nrm(ks[13], (Dp, S5_GROUPS, S5_STATE, S5_GROUP), (2 * S5_GROUP) ** -0.5),
        's5_b_im': nrm(ks[14], (Dp, S5_GROUPS, S5_STATE, S5_GROUP), (2 * S5_GROUP) ** -0.5),
        's5_c_re': nrm(ks[15], (Dp, S5_GROUPS, S5_GROUP, S5_STATE), S5_STATE ** -0.5),
        's5_c_im': nrm(ks[16], (Dp, S5_GROUPS, S5_GROUP, S5_STATE), S5_STATE ** -0.5),
        's5_d': nrm(ks[17], (Dp, S5_GROUPS, S5_GROUP), 1.0),
        's5_w_val': nrm(ks[18], (Dp, S5_WIDTH, D_MODEL), S5_WIDTH ** -0.5) * DN_BETA,
        's5_w_gate': nrm(ks[19], (Dp, S5_WIDTH, D_MODEL), S5_WIDTH ** -0.5),
        's5_b_gate': nrm(ks[20], (Dp, D_MODEL), 0.01),
        'w_out': nrm(ks[21], (Dp, D_MODEL, D_MODEL), D_MODEL ** -0.5) * DN_BETA,
        'ln1_g': 1.0 + nrm(ks[22], (Dp, D_MODEL), 0.01),
        'ln1_b': nrm(ks[23], (Dp, D_MODEL), 0.01),
        'router_w_group': nrm(ks[24], (Dp, D_MODEL, N_EGROUPS), D_MODEL ** -0.5),
        'router_b_group': nrm(ks[25], (Dp, N_EGROUPS), 0.01),
        'router_w_expert': nrm(ks[26], (Dp, D_MODEL, N_EXPERTS), D_MODEL ** -0.5),
        'router_b_expert': nrm(ks[27], (Dp, N_EXPERTS), 0.01),
        'exp_w_gate': nrm(ks[28], (Dp, N_EXPERTS, D_MODEL, D_EXPERT), D_MODEL ** -0.5),
        'exp_w_up': nrm(ks[29], (Dp, N_EXPERTS, D_MODEL, D_EXPERT), D_MODEL ** -0.5) * DN_BETA,
        'exp_w_down': nrm(ks[30], (Dp, N_EXPERTS, D_EXPERT, D_MODEL), D_EXPERT ** -0.5) * DN_BETA,
        'ln2_g': 1.0 + nrm(ks[31], (Dp, D_MODEL), 0.01),
        'ln2_b': nrm(ks[32], (Dp, D_MODEL), 0.01),
    }


def reference(x, rel_bias, w_in, b_in, cmp_pos, cmp_w1, cmp_b1, cmp_w2, cmp_b2, w_attn_up,
              s5_lambda_re, s5_lambda_im, s5_log_dt, s5_b_re, s5_b_im, s5_c_re, s5_c_im, s5_d,
              s5_w_val, s5_w_gate, s5_b_gate, w_out, ln1_g, ln1_b,
              router_w_group, router_b_group, router_w_expert, router_b_expert,
              exp_w_gate, exp_w_up, exp_w_down, ln2_g, ln2_b):
    B, L, D = x.shape
    for l in range(DEPTH):
        proj = x @ w_in[l] + b_in[l]
        q = proj[..., :ATTN_WIDTH].reshape(B, L, N_HEADS, HEAD_DIM)
        kv = proj[..., KV_OFF:NSA_GATE_OFF].reshape(B, L, 6, N_KV, HEAD_DIM)
        nsa_g = jax.nn.sigmoid(proj[..., NSA_GATE_OFF:S5_OFF].astype(jnp.float32)).reshape(B, L, N_HEADS, 3)
        u = proj[..., S5_OFF:MERGE_OFF]
        merge = jax.nn.sigmoid(proj[..., MERGE_OFF:].astype(jnp.float32)).reshape(B, L, 2, D)
        y_a = nsa_attention(q, kv, nsa_g, rel_bias, cmp_pos[l], cmp_w1[l], cmp_b1[l],
                            cmp_w2[l], cmp_b2[l]) @ w_attn_up[l]
        y_s = s5_layer(u, s5_lambda_re[l], s5_lambda_im[l], s5_log_dt[l], s5_b_re[l], s5_b_im[l],
                       s5_c_re[l], s5_c_im[l], s5_d[l])
        z = jax.nn.gelu(y_s)
        y_b = (z @ s5_w_val[l]) * jax.nn.sigmoid(z @ s5_w_gate[l] + s5_b_gate[l])
        mixed = (merge[:, :, 0] * y_a + merge[:, :, 1] * y_b).astype(x.dtype)
        x = layer_norm(DN_ALPHA * x + mixed @ w_out[l], ln1_g[l], ln1_b[l])
        y_m = hier_moe(x, router_w_group[l], router_b_group[l], router_w_expert[l], router_b_expert[l],
                       exp_w_gate[l], exp_w_up[l], exp_w_down[l])
        x = layer_norm(DN_ALPHA * x + y_m, ln2_g[l], ln2_b[l])
    return x
```

```python
import functools
import math

import jax
import jax.numpy as jnp
from jax import lax
from jax.experimental import pallas as pl
from jax.experimental.pallas import tpu as pltpu

F32 = jnp.float32
BF16 = jnp.bfloat16

N_HEADS = 8
HEAD_DIM = 64
N_KV = 2
HPG = N_HEADS // N_KV
CMP_STRIDE = 16
CMP_BLOCK = 2 * CMP_STRIDE
CMP_HIDDEN = 128
SLC_BLOCK = 64
N_SEL = 16
WINDOW = 512
REL_BUCKETS = 32
REL_MAX_DIST = 128
S5_WIDTH = 512
S5_GROUP = 16
S5_GROUPS = S5_WIDTH // S5_GROUP
S5_STATE = 64
N_EGROUPS = 8
EXPERTS_PER_GROUP = 8
N_EXPERTS = N_EGROUPS * EXPERTS_PER_GROUP
TOP_K_IN_GROUP = 2
D_EXPERT = 256
EXPERT_BLOCK = 128
DEPTH = 1
DN_ALPHA = (2.0 * DEPTH) ** 0.25
LN_EPS = 1e-5
NEG_INF = -1e30
BIG = 1e9

ATTN_WIDTH = N_HEADS * HEAD_DIM
KV_WIDTH = N_KV * HEAD_DIM
KV_OFF = ATTN_WIDTH
NSA_GATE_OFF = KV_OFF + 6 * KV_WIDTH
S5_OFF = NSA_GATE_OFF + 3 * N_HEADS
MERGE_OFF = S5_OFF + S5_WIDTH

LANES = 128
SUBLANES = 8
VMEM_LIMIT_BYTES = 56 * 1024 * 1024

ATTN_TQ = 128
ATTN_TK = 256
GATE_PAD = LANES


def _gelu_tanh(x):
    c = math.sqrt(2.0 / math.pi)
    return x * (0.5 * (1.0 + jnp.tanh(c * (x + 0.044715 * (x * x * x)))))


def _dot(a, b):
    return jnp.dot(a, b, preferred_element_type=F32)


def _dot_nt(a, b):
    return lax.dot_general(a, b, (((1,), (1,)), ((), ())), preferred_element_type=F32)


def _compiler_params(semantics):
    return pltpu.CompilerParams(dimension_semantics=semantics, vmem_limit_bytes=VMEM_LIMIT_BYTES)


def _in_proj_layout(d_model):
    widths = (ATTN_WIDTH, 2 * KV_WIDTH, 2 * KV_WIDTH, 2 * KV_WIDTH, GATE_PAD, S5_WIDTH, 2 * d_model)
    offs = [0]
    for w in widths:
        offs.append(offs[-1] + w)
    return widths, offs


def _pack_in_proj(w_in, b_in, d_model):
    def kv_cols(j):
        return KV_OFF + j * KV_WIDTH

    def pair(jk, jv):
        cols = []
        for g in range(N_KV):
            cols.append(jnp.arange(kv_cols(jk) + g * HEAD_DIM, kv_cols(jk) + (g + 1) * HEAD_DIM))
            cols.append(jnp.arange(kv_cols(jv) + g * HEAD_DIM, kv_cols(jv) + (g + 1) * HEAD_DIM))
        return jnp.concatenate(cols)

    gate_cols = jnp.asarray([NSA_GATE_OFF + h * 3 + j for j in range(3) for h in range(N_HEADS)])
    idx = jnp.concatenate([
        jnp.arange(0, ATTN_WIDTH),
        pair(2, 3),
        pair(4, 5),
        jnp.arange(kv_cols(0), kv_cols(2)),
        gate_cols,
    ])
    idx2 = jnp.concatenate([jnp.arange(S5_OFF, S5_OFF + S5_WIDTH),
                            jnp.arange(MERGE_OFF, MERGE_OFF + 2 * d_model)])
    gpad = GATE_PAD - 3 * N_HEADS
    w = jnp.concatenate([w_in[:, idx], jnp.zeros((d_model, gpad), F32), w_in[:, idx2]], axis=1)
    b = jnp.concatenate([b_in[idx], jnp.zeros((gpad,), F32), b_in[idx2]])
    return w.astype(BF16), b.reshape(1, -1).astype(F32)


def _in_proj_kernel(offs, x_ref, w_ref, b_ref, q_ref, slc_ref, win_ref, cmp_ref, g_ref, u_ref, m_ref):
    xb = x_ref[...].astype(BF16)

    def proj(i):
        c0, c1 = offs[i], offs[i + 1]
        return _dot(xb, w_ref[:, c0:c1]) + b_ref[:, c0:c1]

    q_ref[...] = (proj(0) * (HEAD_DIM ** -0.5)).astype(BF16)
    slc_ref[...] = proj(1).astype(BF16)
    win_ref[...] = proj(2).astype(BF16)
    cmp_ref[...] = proj(3)
    g_ref[...] = jax.nn.sigmoid(proj(4))
    u_ref[...] = proj(5)
    m_ref[...] = jax.nn.sigmoid(proj(6))


def _in_proj(x2d, w_packed, b_packed, d_model, tm):
    n = x2d.shape[0]
    widths, offs = _in_proj_layout(d_model)
    ncols = offs[-1]
    dtypes = (BF16, BF16, BF16, F32, F32, F32, F32)
    return pl.pallas_call(
        functools.partial(_in_proj_kernel, tuple(offs)),
        grid=(n // tm,),
        in_specs=[
            pl.BlockSpec((tm, d_model), lambda i: (i, 0)),
            pl.BlockSpec((d_model, ncols), lambda i: (0, 0)),
            pl.BlockSpec((1, ncols), lambda i: (0, 0)),
        ],
        out_specs=[pl.BlockSpec((tm, w), lambda i: (i, 0)) for w in widths],
        out_shape=[jax.ShapeDtypeStruct((n, w), dt) for w, dt in zip(widths, dtypes)],
        compiler_params=_compiler_params(("parallel",)),
        name="in_proj",
    )(x2d, w_packed, b_packed)


def _compress_kernel(ck_ref, cv_ref, pos_ref, w1_ref, b1_ref, w2_ref, b2_ref, out_ref):
    n_c = ck_ref.shape[2]
    outs = []
    for i, c_ref in enumerate((ck_ref, cv_ref)):
        c = c_ref[0, 0]
        lo = (c + pos_ref[i, 0:1, :]).astype(BF16)
        hi = (c + pos_ref[i, 1:2, :]).astype(BF16)
        p_lo = _dot(lo, w1_ref[i, 0])
        p_hi = _dot(hi, w1_ref[i, 1])
        hid = p_lo + pltpu.roll(p_hi, n_c - 1, 0) + b1_ref[i]
        hid = _gelu_tanh(hid).astype(BF16)
        outs.append(_dot(hid, w2_ref[i]) + b2_ref[i])
    out_ref[0, 0] = jnp.concatenate(outs, axis=1).astype(BF16)


def _compress(cmp4, cmp_pos, cmp_w1, cmp_b1, cmp_w2, cmp_b2):
    b, _, n_c, cw = cmp4.shape
    half = CMP_STRIDE * HEAD_DIM
    pos = cmp_pos.reshape(2, 2, half).astype(F32)
    w1 = cmp_w1.reshape(2, 2, half, CMP_HIDDEN).astype(BF16)
    b1 = cmp_b1.reshape(2, 1, CMP_HIDDEN).astype(F32)
    w2 = cmp_w2.astype(BF16)
    b2 = cmp_b2.reshape(2, 1, HEAD_DIM).astype(F32)
    full = lambda shape: pl.BlockSpec(shape, lambda i, g: (0,) * len(shape))
    return pl.pallas_call(
        _compress_kernel,
        grid=(b, N_KV),
        in_specs=[
            pl.BlockSpec((1, 1, n_c, cw), lambda i, g: (i, g, 0, 0)),
            pl.BlockSpec((1, 1, n_c, cw), lambda i, g: (i, N_KV + g, 0, 0)),
            full((2, 2, half)),
            full((2, 2, half, CMP_HIDDEN)),
            full((2, 1, CMP_HIDDEN)),
            full((2, CMP_HIDDEN, HEAD_DIM)),
            full((2, 1, HEAD_DIM)),
        ],
        out_specs=pl.BlockSpec((1, 1, n_c, 2 * HEAD_DIM), lambda i, g: (i, g, 0, 0)),
        out_shape=jax.ShapeDtypeStruct((b, N_KV, n_c, 2 * HEAD_DIM), BF16),
        compiler_params=_compiler_params(("parallel", "parallel")),
        name="compress",
    )(cmp4, cmp4, pos, w1, b1, w2, b2)


def _t5_bucket(dist):
    n = jnp.maximum(dist, 0)
    max_exact = REL_BUCKETS // 2
    nf = jnp.maximum(n, 1).astype(F32)
    large = max_exact + (jnp.log(nf / max_exact) / math.log(REL_MAX_DIST / max_exact)
                         * (REL_BUCKETS - max_exact)).astype(jnp.int32)
    large = jnp.minimum(large, REL_BUCKETS - 1)
    return jnp.where(n < max_exact, n, large)


_KIND_DIAG, _KIND_PREV, _KIND_WIN_EDGE, _KIND_FAR = 0, 1, 2, 3


def _attention_bias_tables(rel_bias, seq):
    tbl = rel_bias.astype(F32)
    tq, tk = ATTN_TQ, ATTN_TK
    i = jnp.arange(tq)[:, None]
    j = jnp.arange(tk)[None, :]
    tiles = []
    for r in range(tk // tq):
        row = []
        for kind in range(4):
            if kind == _KIND_FAR:
                dist = jnp.full((tq, tk), 4 * REL_MAX_DIST, jnp.int32)
                mask = jnp.ones((tq, tk), bool)
            else:
                dist = kind * tk + r * tq + i - j
                mask = dist >= 0
                if kind == _KIND_WIN_EDGE:
                    mask = mask & (dist < WINDOW)
            bias = tbl[_t5_bucket(dist)]
            bias = jnp.where(mask[:, :, None], bias, NEG_INF)
            row.append(bias.transpose(2, 0, 1).reshape(N_KV, HPG * tq, tk))
        tiles.append(jnp.stack(row, axis=1))
    bias_tiles = jnp.stack(tiles, axis=1)
    n_c = seq // CMP_STRIDE
    pos = jnp.arange(seq)[:, None]
    dist_c = pos - (jnp.arange(n_c) * CMP_STRIDE + CMP_BLOCK - 1)[None, :]
    bias_c = jnp.where((dist_c >= 0)[:, :, None], tbl[_t5_bucket(dist_c)], NEG_INF)
    bias_c = bias_c.transpose(2, 0, 1)
    return bias_tiles, bias_c


def _nsa_constants(seq):
    n_c = seq // CMP_STRIDE
    n_blk = seq // SLC_BLOCK
    cmp_start = jnp.arange(n_c) * CMP_STRIDE
    blk_start = jnp.arange(n_blk) * SLC_BLOCK
    overlap = ((cmp_start[:, None] <= blk_start[None, :] + SLC_BLOCK - 1)
               & (cmp_start[:, None] + CMP_BLOCK - 1 >= blk_start[None, :]))
    overlap = overlap & (cmp_start[:, None] + CMP_BLOCK <= seq)
    expand = (jnp.arange(seq)[None, :] // SLC_BLOCK) == jnp.arange(n_blk)[:, None]
    return overlap.astype(BF16), expand.astype(BF16)


def _nsa_kernel(seq, q_ref, kcv_ref, slc_ref, win_ref, gate_ref, bias_c_ref, bias_t_ref,
                overlap_ref, expand_ref, o_ref, m_sc, l_sc, acc_sc):
    tq, tk = ATTN_TQ, ATTN_TK
    n_blk = seq // SLC_BLOCK
    n_sel = min(N_SEL, n_blk)
    rows = HPG * tq
    qt = pl.program_id(2)
    q0 = qt * tq
    kd = q0 // tk
    parity = qt % (tk // tq)
    lane = lax.broadcasted_iota(jnp.int32, (tq, LANES), 1)
    lo_half = lane < HEAD_DIM

    qs = []
    for pair_idx in range(HPG // 2):
        q2 = q_ref[0, :, pair_idx * LANES:(pair_idx + 1) * LANES].astype(F32)
        qs.append(jnp.where(lo_half, q2, 0.0))
        qs.append(jnp.where(lo_half, pltpu.roll(q2, HEAD_DIM, 1), 0.0))
    q_stack = jnp.concatenate(qs, axis=0).astype(BF16)

    kcv = kcv_ref[0, 0]
    s_c = _dot_nt(q_stack, kcv) + bias_c_ref[...].reshape(rows, -1)
    vis_c = s_c > 0.5 * NEG_INF
    m_c = jnp.max(s_c, axis=1, keepdims=True)
    e_c = jnp.exp(s_c - m_c)
    p_c = jnp.where(vis_c, e_c / jnp.sum(e_c, axis=1, keepdims=True), 0.0)
    p_cb = p_c.astype(BF16)
    o_cmp = _dot(p_cb, kcv)
    imp4 = _dot(p_cb, overlap_ref[...])
    imp = imp4[0:tq]
    for h in range(1, HPG):
        imp = imp + imp4[h * tq:(h + 1) * tq]

    blk = lax.broadcasted_iota(jnp.int32, (tq, n_blk), 1)
    pos = q0 + lax.broadcasted_iota(jnp.int32, (tq, n_blk), 0)
    cur = jnp.right_shift(pos, int(math.log2(SLC_BLOCK)))
    forced = (blk == 0) | (blk == cur) | (blk == cur - 1)
    valid = blk * SLC_BLOCK <= pos
    score = jnp.where(forced, BIG, jnp.where(valid, imp, -BIG))
    blk_f = blk.astype(F32)
    sel = jnp.zeros((tq, n_blk), F32)
    for _ in range(n_sel):
        top = jnp.max(score, axis=1, keepdims=True)
        first = jnp.min(jnp.where(score == top, blk_f, float(n_blk)), axis=1, keepdims=True)
        hit = blk_f == first
        sel = jnp.where(hit, 1.0, sel)
        score = jnp.where(hit, -jnp.inf, score)
    sel_pen = jnp.where(sel > 0.5, 0.0, NEG_INF).astype(BF16)

    def run_branch(kv_ref, k_lo, kind_of, use_sel):
        m_sc[...] = jnp.full((rows, 1), NEG_INF, F32)
        l_sc[...] = jnp.zeros((rows, 1), F32)
        acc_sc[...] = jnp.zeros((rows, LANES), F32)

        def body(kt, carry):
            k0 = pl.multiple_of(kt * tk, tk)
            kv = kv_ref[0, pl.ds(k0, tk), :]
            s = _dot_nt(q_stack, kv) + bias_t_ref[0, parity, kind_of(kt)]
            if use_sel:
                pen = _dot(sel_pen, expand_ref[:, pl.ds(k0, tk)])
                s = s + jnp.concatenate([pen] * HPG, axis=0)
            m_old = m_sc[...]
            m_new = jnp.maximum(m_old, jnp.max(s, axis=1, keepdims=True))
            p = jnp.exp(s - m_new)
            alpha = jnp.exp(m_old - m_new)
            l_sc[...] = alpha * l_sc[...] + jnp.sum(p, axis=1, keepdims=True)
            acc_sc[...] = alpha * acc_sc[...] + _dot(p.astype(BF16), kv)
            m_sc[...] = m_new
            return carry

        lax.fori_loop(k_lo, kd + 1, body, 0)
        return acc_sc[...] / l_sc[...]

    def kind_slc(kt):
        return jnp.where(kt == kd, _KIND_DIAG, jnp.where(kt == kd - 1, _KIND_PREV, _KIND_FAR))

    def kind_win(kt):
        return kd - kt

    o_slc = run_branch(slc_ref, 0, kind_slc, True)
    o_win = run_branch(win_ref, jnp.maximum(kd - WINDOW // tk, 0), kind_win, False)

    gates = gate_ref[0]
    g = pl.program_id(1)
    outs = []
    for h in range(HPG):
        sl = slice(h * tq, (h + 1) * tq)
        col = g * HPG + h

        def gate(branch):
            onehot = lane == branch * N_HEADS + col
            return jnp.sum(jnp.where(onehot, gates, 0.0), axis=1, keepdims=True)

        outs.append(gate(0) * o_cmp[sl] + gate(1) * o_slc[sl] + gate(2) * o_win[sl])
    for pair_idx in range(HPG // 2):
        even, odd = outs[2 * pair_idx], outs[2 * pair_idx + 1]
        merged = jnp.where(lo_half, pltpu.roll(even, HEAD_DIM, 1), odd)
        o_ref[0, :, pair_idx * LANES:(pair_idx + 1) * LANES] = merged.astype(o_ref.dtype)


def _nsa(q, kcv, slc, win, gates, bias_tiles, bias_c, overlap, expand):
    b, seq, _ = q.shape
    n_c = seq // CMP_STRIDE
    n_blk = seq // SLC_BLOCK
    tq, tk = ATTN_TQ, ATTN_TK
    rows = HPG * tq
    grp = HPG * HEAD_DIM
    return pl.pallas_call(
        functools.partial(_nsa_kernel, seq),
        grid=(b, N_KV, seq // tq),
        in_specs=[
            pl.BlockSpec((1, tq, grp), lambda i, g, t: (i, t, g)),
            pl.BlockSpec((1, 1, n_c, 2 * HEAD_DIM), lambda i, g, t: (i, g, 0, 0)),
            pl.BlockSpec((1, seq, 2 * HEAD_DIM), lambda i, g, t: (i, 0, g)),
            pl.BlockSpec((1, seq, 2 * HEAD_DIM), lambda i, g, t: (i, 0, g)),
            pl.BlockSpec((1, tq, GATE_PAD), lambda i, g, t: (i, t, 0)),
            pl.BlockSpec((HPG, tq, n_c), lambda i, g, t: (g, t, 0)),
            pl.BlockSpec((1, tk // tq, 4, rows, tk), lambda i, g, t: (g, 0, 0, 0, 0)),
            pl.BlockSpec((n_c, n_blk), lambda i, g, t: (0, 0)),
            pl.BlockSpec((n_blk, seq), lambda i, g, t: (0, 0)),
        ],
        out_specs=pl.BlockSpec((1, tq, grp), lambda i, g, t: (i, t, g)),
        out_shape=jax.ShapeDtypeStruct((b, seq, ATTN_WIDTH), BF16),
        scratch_shapes=[
            pltpu.VMEM((rows, 1), F32),
            pltpu.VMEM((rows, 1), F32),
            pltpu.VMEM((rows, LANES), F32),
        ],
        compiler_params=_compiler_params(("parallel", "parallel", "arbitrary")),
        name="nsa",
    )(q, kcv, slc, win, gates, bias_c, bias_tiles, overlap, expand)


def _mixer_inputs(x, rel_bias, w_in, b_in, cmp_pos, cmp_w1, cmp_b1, cmp_w2, cmp_b2):
    b, seq, d = x.shape
    wp, bp = _pack_in_proj(w_in, b_in, d)
    q, slc, win, cmp, gates, u, merge = _in_proj(x.reshape(b * seq, d), wp, bp, d, 512)
    n_c = seq // CMP_STRIDE
    cmp4 = cmp.reshape(b, seq, 2 * N_KV, HEAD_DIM).transpose(0, 2, 1, 3).reshape(
        b, 2 * N_KV, n_c, CMP_STRIDE * HEAD_DIM)
    kcv = _compress(cmp4, cmp_pos, cmp_w1, cmp_b1, cmp_w2, cmp_b2)
    bias_tiles, bias_c = _attention_bias_tables(rel_bias, seq)
    overlap, expand = _nsa_constants(seq)
    o = _nsa(q.reshape(b, seq, -1), kcv, slc.reshape(b, seq, -1), win.reshape(b, seq, -1),
             gates.reshape(b, seq, -1), bias_tiles, bias_c, overlap, expand)
    return o, u, merge


S5_HALF_GROUPS = S5_GROUPS // 2
S5_HALF_IN = S5_HALF_GROUPS * S5_GROUP
S5_HALF_STATE = S5_HALF_GROUPS * S5_STATE
S5_SCAN_LANES = 512
S5_CHUNK = 64
S5_UNROLL = 8


def _s5_params(lam_re, lam_im, log_dt, b_re, b_im, c_re, c_im, nb):
    dt = jnp.exp(log_dt.astype(F32))[:, None]
    lr, li = lam_re.astype(F32), lam_im.astype(F32)
    mag = jnp.exp(lr * dt)
    ab_re, ab_im = mag * jnp.cos(li * dt), mag * jnp.sin(li * dt)
    nr, ni = ab_re - 1.0, ab_im
    den = lr * lr + li * li
    fr, fi = (nr * lr + ni * li) / den, (ni * lr - nr * li) / den
    br, bim = b_re.astype(F32), b_im.astype(F32)
    bb_re = fr[..., None] * br - fi[..., None] * bim
    bb_im = fr[..., None] * bim + fi[..., None] * br
    eye = jnp.eye(S5_HALF_GROUPS, dtype=F32)

    def in_mat(bb):
        t = bb.reshape(2, S5_HALF_GROUPS, S5_STATE, S5_GROUP)
        m = jnp.einsum('kgph,gj->kghjp', t, eye)
        return m.reshape(2, S5_HALF_IN, S5_HALF_STATE)

    def out_mat(c):
        t = c.astype(F32).reshape(2, S5_HALF_GROUPS, S5_GROUP, S5_STATE)
        m = jnp.einsum('kghp,gj->kgpjh', t, eye)
        return m.reshape(2, S5_HALF_STATE, S5_HALF_IN)

    bmat = jnp.concatenate([in_mat(bb_re), in_mat(bb_im)], axis=2).astype(BF16)
    cmat = jnp.concatenate([out_mat(c_re), -out_mat(c_im)], axis=1).astype(BF16)
    a = jnp.concatenate([ab_re.reshape(2, S5_HALF_STATE), ab_im.reshape(2, S5_HALF_STATE)], axis=1)
    a = jnp.broadcast_to(a.reshape(1, 4 * S5_HALF_STATE), (nb, 4 * S5_HALF_STATE))
    return bmat, cmat, a


def _s5_kernel(u_ref, bmat_ref, cmat_ref, a_ref, d_ref, y_ref, ut_sc, x_sc, st_sc):
    nb, t_len, _ = u_ref.shape
    half_w = 2 * S5_HALF_STATE

    @pl.when(pl.program_id(0) == 0)
    def _():
        st_sc[...] = jnp.zeros_like(st_sc)

    n_cb = ut_sc.shape[0]
    for b in range(nb):
        for cb in range(n_cb):
            ut_sc[cb, pl.ds(b, t_len, stride=nb), :] = u_ref[b, :, cb * LANES:(cb + 1) * LANES]
    ut = jnp.concatenate([ut_sc[cb] for cb in range(n_cb)], axis=1)
    ub = ut.astype(BF16)
    for k in range(2):
        x_sc[:, k * half_w:(k + 1) * half_w] = _dot(ub[:, k * S5_HALF_IN:(k + 1) * S5_HALF_IN], bmat_ref[k])

    for k in range(2):
        for j in range(S5_HALF_STATE // S5_SCAN_LANES):
            re0 = k * half_w + j * S5_SCAN_LANES
            im0 = re0 + S5_HALF_STATE
            re_sl, im_sl = pl.ds(re0, S5_SCAN_LANES), pl.ds(im0, S5_SCAN_LANES)
            ar, ai = a_ref[:, re_sl], a_ref[:, im_sl]

            def steps(c, carry):
                xr, xi = carry
                for s in range(S5_UNROLL):
                    rows = pl.ds(pl.multiple_of((c * S5_UNROLL + s) * nb, nb), nb)
                    nxr = ar * xr - ai * xi + x_sc[rows, re_sl]
                    nxi = ar * xi + ai * xr + x_sc[rows, im_sl]
                    x_sc[rows, re_sl] = nxr
                    x_sc[rows, im_sl] = nxi
                    xr, xi = nxr, nxi
                return xr, xi

            xr, xi = lax.fori_loop(0, t_len // S5_UNROLL, steps, (st_sc[:, re_sl], st_sc[:, im_sl]))
            st_sc[:, re_sl] = xr
            st_sc[:, im_sl] = xi

    xs = x_sc[...].astype(BF16)
    y = jnp.concatenate([_dot(xs[:, k * half_w:(k + 1) * half_w], cmat_ref[k]) for k in range(2)], axis=1)
    y = y + d_ref[...] * ut
    for cb in range(n_cb):
        ut_sc[cb] = y[:, cb * LANES:(cb + 1) * LANES]
    for b in range(nb):
        for cb in range(n_cb):
            y_ref[b, :, cb * LANES:(cb + 1) * LANES] = ut_sc[cb, pl.ds(b, t_len, stride=nb), :]


def _s5(u, bmat, cmat, a, d_skip):
    nb, seq, w = u.shape
    t_len = min(S5_CHUNK, seq)
    full = lambda shape: pl.BlockSpec(shape, lambda c: (0,) * len(shape))
    return pl.pallas_call(
        _s5_kernel,
        grid=(seq // t_len,),
        in_specs=[
            pl.BlockSpec((nb, t_len, w), lambda c: (0, c, 0)),
            full(bmat.shape), full(cmat.shape), full(a.shape), full((1, w)),
        ],
        out_specs=pl.BlockSpec((nb, t_len, w), lambda c: (0, c, 0)),
        out_shape=jax.ShapeDtypeStruct((nb, seq, w), F32),
        scratch_shapes=[
            pltpu.VMEM((w // LANES, t_len * nb, LANES), F32),
            pltpu.VMEM((t_len * nb, 4 * S5_HALF_STATE), F32),
            pltpu.VMEM((nb, 4 * S5_HALF_STATE), F32),
        ],
        compiler_params=_compiler_params(("arbitrary",)),
        name="s5",
    )(u, bmat, cmat, a, d_skip.reshape(1, w).astype(F32))


ROUTE_PAD = LANES
_R_E1, _R_E2, _R_W1, _R_W2, _R_RANK1, _R_RANK2 = range(6)


def _layer_norm(t, g, b):
    mu = jnp.mean(t, axis=1, keepdims=True)
    c = t - mu
    var = jnp.mean(c * c, axis=1, keepdims=True)
    return c * lax.rsqrt(var + LN_EPS) * g + b


def _post_kernel(x_ref, o_ref, y_ref, m_ref, wup_ref, wval_ref, wgate_ref, bgate_ref, wout_ref,
                 g1_ref, b1_ref, wr_ref, br_ref, h_ref, route_ref, cnt_ref, run_sc):
    tm, d = x_ref.shape

    @pl.when(pl.program_id(0) == 0)
    def _():
        run_sc[...] = jnp.zeros_like(run_sc)

    y_a = _dot(o_ref[...], wup_ref[...])
    z = _gelu_tanh(y_ref[...]).astype(BF16)
    y_b = _dot(z, wval_ref[...]) * jax.nn.sigmoid(_dot(z, wgate_ref[...]) + bgate_ref[...])
    mixed = m_ref[:, :d] * y_a + m_ref[:, d:] * y_b
    t = DN_ALPHA * x_ref[...] + _dot(mixed.astype(BF16), wout_ref[...])
    h = _layer_norm(t, g1_ref[...], b1_ref[...])
    h_ref[...] = h

    logits = _dot(h.astype(BF16), wr_ref[...]) + br_ref[...]
    lane = lax.broadcasted_iota(jnp.int32, (tm, ROUTE_PAD), 1)
    lane_f = lane.astype(F32)
    is_group = lane < N_EGROUPS

    def first_max(v):
        top = jnp.max(v, axis=1, keepdims=True)
        idx = jnp.min(jnp.where(v == top, lane_f, float(ROUTE_PAD)), axis=1, keepdims=True)
        return top, idx

    g_max, g_top = first_max(jnp.where(is_group, logits, -jnp.inf))
    p_group = 1.0 / jnp.sum(jnp.where(is_group, jnp.exp(logits - g_max), 0.0), axis=1, keepdims=True)
    grp_of_lane = jnp.right_shift(lane - N_EGROUPS, int(math.log2(EXPERTS_PER_GROUP))).astype(F32)
    in_group = (lane >= N_EGROUPS) & (lane < N_EGROUPS + N_EXPERTS) & (grp_of_lane == g_top)
    e_log = jnp.where(in_group, logits, -jnp.inf)
    v1, i1 = first_max(e_log)
    hit1 = lane_f == i1
    v2, i2 = first_max(jnp.where(hit1, -jnp.inf, e_log))
    hit2 = lane_f == i2
    e2 = jnp.exp(v2 - v1)
    w1 = p_group / (1.0 + e2)
    w2 = p_group * e2 / (1.0 + e2)

    hits = jnp.where(hit1 | hit2, 1.0, 0.0)
    row = lax.broadcasted_iota(jnp.int32, (tm, tm), 0)
    col = lax.broadcasted_iota(jnp.int32, (tm, tm), 1)
    earlier = jnp.where(col < row, 1.0, 0.0).astype(BF16)
    before = _dot(earlier, hits.astype(BF16)) + run_sc[...]
    rank1 = jnp.sum(jnp.where(hit1, before, 0.0), axis=1, keepdims=True)
    rank2 = jnp.sum(jnp.where(hit2, before, 0.0), axis=1, keepdims=True)
    run_sc[...] = run_sc[...] + jnp.sum(hits, axis=0, keepdims=True)
    cnt_ref[...] = run_sc[...]

    rec = jnp.zeros((tm, ROUTE_PAD), F32)
    for slot, val in ((_R_E1, i1 - N_EGROUPS), (_R_E2, i2 - N_EGROUPS), (_R_W1, w1), (_R_W2, w2),
                      (_R_RANK1, rank1), (_R_RANK2, rank2)):
        rec = jnp.where(lane == slot, val, rec)
    route_ref[...] = rec


def _post(x2d, o2d, y2d, merge, w_attn_up, s5_w_val, s5_w_gate, s5_b_gate, w_out, ln1_g, ln1_b,
          router_w_group, router_b_group, router_w_expert, router_b_expert, tm):
    n, d = x2d.shape
    rpad = ROUTE_PAD - N_EGROUPS - N_EXPERTS
    wr = jnp.concatenate([router_w_group, router_w_expert, jnp.zeros((d, rpad), F32)], axis=1).astype(BF16)
    br = jnp.concatenate([router_b_group, router_b_expert, jnp.zeros((rpad,), F32)]).reshape(1, -1).astype(F32)
    row = lambda w: pl.BlockSpec((tm, w), lambda i: (i, 0))
    full = lambda a: pl.BlockSpec(a.shape, lambda i: (0,) * a.ndim)
    weights = [w_attn_up.astype(BF16), s5_w_val.astype(BF16), s5_w_gate.astype(BF16),
               s5_b_gate.reshape(1, d).astype(F32), w_out.astype(BF16),
               ln1_g.reshape(1, d).astype(F32), ln1_b.reshape(1, d).astype(F32), wr, br]
    return pl.pallas_call(
        _post_kernel,
        grid=(n // tm,),
        in_specs=[row(d), row(ATTN_WIDTH), row(S5_WIDTH), row(2 * d)] + [full(w) for w in weights],
        out_specs=[row(d), row(ROUTE_PAD), pl.BlockSpec((1, ROUTE_PAD), lambda i: (0, 0))],
        out_shape=[jax.ShapeDtypeStruct((n, d), F32), jax.ShapeDtypeStruct((n, ROUTE_PAD), F32),
                   jax.ShapeDtypeStruct((1, ROUTE_PAD), F32)],
        scratch_shapes=[pltpu.VMEM((1, ROUTE_PAD), F32)],
        compiler_params=_compiler_params(("arbitrary",)),
        name="post_mixer",
    )(x2d, o2d, y2d, merge, *weights)


def _row_gather_start(idx_ref, base, n_rows, src_hbm, dst, sem):
    def body(r, c):
        pltpu.make_async_copy(src_hbm.at[idx_ref[base + r]], dst.at[r], sem).start()
        return c
    lax.fori_loop(0, n_rows, body, 0)


def _row_gather_wait(n_rows, src_hbm, dst, sem):
    def body(r, c):
        pltpu.make_async_copy(src_hbm.at[0], dst.at[r], sem).wait()
        return c
    lax.fori_loop(0, n_rows, body, 0)


def _rows_to_matrix(buf):
    return jnp.concatenate([buf[:, s, :] for s in range(buf.shape[1])], axis=1)


def _expert_kernel(row_tok_ref, blk_exp_ref, h_hbm, wg_ref, wu_ref, wd_ref, y_ref, xbuf, sem):
    i = pl.program_id(0)
    n_blocks = pl.num_programs(0)
    slot = i % 2

    @pl.when(i == 0)
    def _():
        _row_gather_start(row_tok_ref, 0, EXPERT_BLOCK, h_hbm, xbuf.at[0], sem.at[0])

    @pl.when(i + 1 < n_blocks)
    def _():
        _row_gather_start(row_tok_ref, (i + 1) * EXPERT_BLOCK, EXPERT_BLOCK, h_hbm,
                          xbuf.at[1 - slot], sem.at[1 - slot])

    _row_gather_wait(EXPERT_BLOCK, h_hbm, xbuf.at[slot], sem.at[slot])
    xb = _rows_to_matrix(xbuf.at[slot]).astype(BF16)
    h_gate = _dot(xb, wg_ref[0].astype(BF16))
    h_up = _dot(xb, wu_ref[0].astype(BF16))
    hb = (h_gate * jax.nn.sigmoid(h_gate) * h_up).astype(BF16)
    y = _dot(hb, wd_ref[0].astype(BF16))
    for s in range(y_ref.shape[1]):
        y_ref[:, s, :] = y[:, s * LANES:(s + 1) * LANES]


def _experts(row_tok, blk_expert, h3, w_gate, w_up, w_down):
    n_blocks = blk_expert.shape[0]
    _, s_dim, _ = h3.shape
    d = s_dim * LANES
    grid_spec = pltpu.PrefetchScalarGridSpec(
        num_scalar_prefetch=2,
        grid=(n_blocks,),
        in_specs=[
            pl.BlockSpec(memory_space=pl.ANY),
            pl.BlockSpec((1, d, D_EXPERT), lambda i, rt, be: (be[i], 0, 0)),
            pl.BlockSpec((1, d, D_EXPERT), lambda i, rt, be: (be[i], 0, 0)),
            pl.BlockSpec((1, D_EXPERT, d), lambda i, rt, be: (be[i], 0, 0)),
        ],
        out_specs=pl.BlockSpec((EXPERT_BLOCK, s_dim, LANES), lambda i, rt, be: (i, 0, 0)),
        scratch_shapes=[
            pltpu.VMEM((2, EXPERT_BLOCK, s_dim, LANES), F32),
            pltpu.SemaphoreType.DMA((2,)),
        ],
    )
    return pl.pallas_call(
        _expert_kernel,
        grid_spec=grid_spec,
        out_shape=jax.ShapeDtypeStruct((n_blocks * EXPERT_BLOCK, s_dim, LANES), F32),
        compiler_params=_compiler_params(("arbitrary",)),
        name="experts",
    )(row_tok, blk_expert, h3, w_gate, w_up, w_down)


COMBINE_TM = 128


def _combine_kernel(dest1_ref, dest2_ref, yb_hbm, h_ref, route_ref, g2_ref, b2_ref, out_ref, ybuf, sem):
    i = pl.program_id(0)
    n_tiles = pl.num_programs(0)
    tm = h_ref.shape[0]
    slot = i % 2

    def start(tile, s):
        _row_gather_start(dest1_ref, tile * tm, tm, yb_hbm, ybuf.at[s, 0], sem.at[s, 0])
        _row_gather_start(dest2_ref, tile * tm, tm, yb_hbm, ybuf.at[s, 1], sem.at[s, 1])

    @pl.when(i == 0)
    def _():
        start(0, 0)

    @pl.when(i + 1 < n_tiles)
    def _():
        start(i + 1, 1 - slot)

    _row_gather_wait(tm, yb_hbm, ybuf.at[slot, 0], sem.at[slot, 0])
    _row_gather_wait(tm, yb_hbm, ybuf.at[slot, 1], sem.at[slot, 1])
    y1 = _rows_to_matrix(ybuf.at[slot, 0])
    y2 = _rows_to_matrix(ybuf.at[slot, 1])
    route = route_ref[...]
    w1 = route[:, _R_W1:_R_W1 + 1]
    w2 = route[:, _R_W2:_R_W2 + 1]
    t = DN_ALPHA * h_ref[...] + (y1 * w1 + y2 * w2)
    out_ref[...] = _layer_norm(t, g2_ref[...], b2_ref[...])


def _combine(dest1, dest2, yb3, h2d, route, ln2_g, ln2_b):
    n, d = h2d.shape
    tm = COMBINE_TM
    _, s_dim, _ = yb3.shape
    grid_spec = pltpu.PrefetchScalarGridSpec(
        num_scalar_prefetch=2,
        grid=(n // tm,),
        in_specs=[
            pl.BlockSpec(memory_space=pl.ANY),
            pl.BlockSpec((tm, d), lambda i, d1, d2: (i, 0)),
            pl.BlockSpec((tm, ROUTE_PAD), lambda i, d1, d2: (i, 0)),
            pl.BlockSpec((1, d), lambda i, d1, d2: (0, 0)),
            pl.BlockSpec((1, d), lambda i, d1, d2: (0, 0)),
        ],
        out_specs=pl.BlockSpec((tm, d), lambda i, d1, d2: (i, 0)),
        scratch_shapes=[
            pltpu.VMEM((2, 2, tm, s_dim, LANES), F32),
            pltpu.SemaphoreType.DMA((2, 2)),
        ],
    )
    return pl.pallas_call(
        _combine_kernel,
        grid_spec=grid_spec,
        out_shape=jax.ShapeDtypeStruct((n, d), F32),
        compiler_params=_compiler_params(("arbitrary",)),
        name="combine",
    )(dest1, dest2, yb3, h2d, route, ln2_g.reshape(1, d).astype(F32), ln2_b.reshape(1, d).astype(F32))


def _moe(h2d, route, counts_row, w_gate, w_up, w_down, ln2_g, ln2_b):
    n, d = h2d.shape
    e1 = route[:, _R_E1].astype(jnp.int32)
    e2 = route[:, _R_E2].astype(jnp.int32)
    rank1 = route[:, _R_RANK1].astype(jnp.int32)
    rank2 = route[:, _R_RANK2].astype(jnp.int32)
    counts = counts_row[0, N_EGROUPS:N_EGROUPS + N_EXPERTS].astype(jnp.int32)
    padded = (counts + EXPERT_BLOCK - 1) // EXPERT_BLOCK * EXPERT_BLOCK
    pend = jnp.cumsum(padded)
    pstart = pend - padded
    dest1 = pstart[e1] + rank1
    dest2 = pstart[e2] + rank2
    n_blocks = -(-(n * TOP_K_IN_GROUP) // EXPERT_BLOCK) + N_EXPERTS
    tok = jnp.arange(n, dtype=jnp.int32)
    row_tok = jnp.zeros((n_blocks * EXPERT_BLOCK,), jnp.int32).at[dest1].set(tok).at[dest2].set(tok)
    blk_expert = jnp.minimum(
        jnp.searchsorted(pend, jnp.arange(n_blocks, dtype=jnp.int32) * EXPERT_BLOCK, side='right'),
        N_EXPERTS - 1).astype(jnp.int32)
    h3 = h2d.reshape(n, d // LANES, LANES)
    yb3 = _experts(row_tok, blk_expert, h3, w_gate, w_up, w_down)
    return _combine(dest1, dest2, yb3, h2d, route, ln2_g, ln2_b)


def kernel(x, rel_bias, w_in, b_in, cmp_pos, cmp_w1, cmp_b1, cmp_w2, cmp_b2, w_attn_up, s5_lambda_re, s5_lambda_im, s5_log_dt, s5_b_re, s5_b_im, s5_c_re, s5_c_im, s5_d, s5_w_val, s5_w_gate, s5_b_gate, w_out, ln1_g, ln1_b, router_w_group, router_b_group, router_w_expert, router_b_expert, exp_w_gate, exp_w_up, exp_w_down, ln2_g, ln2_b):
    b, seq, d = x.shape
    n = b * seq
    assert w_in.shape[0] == DEPTH
    l = 0
    o, u, merge = _mixer_inputs(x, rel_bias, w_in[l], b_in[l], cmp_pos[l], cmp_w1[l], cmp_b1[l],
                                cmp_w2[l], cmp_b2[l])
    bmat, cmat, a = _s5_params(s5_lambda_re[l], s5_lambda_im[l], s5_log_dt[l], s5_b_re[l], s5_b_im[l],
                               s5_c_re[l], s5_c_im[l], b)
    y_s = _s5(u.reshape(b, seq, S5_WIDTH), bmat, cmat, a, s5_d[l])
    h2d, route, counts = _post(x.reshape(n, d), o.reshape(n, ATTN_WIDTH), y_s.reshape(n, S5_WIDTH), merge,
                               w_attn_up[l], s5_w_val[l], s5_w_gate[l], s5_b_gate[l], w_out[l], ln1_g[l],
                               ln1_b[l], router_w_group[l], router_b_group[l], router_w_expert[l],
                               router_b_expert[l], 256)
    out = _moe(h2d, route, counts, exp_w_gate[l], exp_w_up[l], exp_w_down[l], ln2_g[l], ln2_b[l])
    return out.reshape(b, seq, d)
```

```python
import functools
import math

import jax
import jax.numpy as jnp
from jax import lax
from jax.experimental import pallas as pl
from jax.experimental.pallas import tpu as pltpu

F32 = jnp.float32
BF16 = jnp.bfloat16

N_HEADS = 8
HEAD_DIM = 64
N_KV = 2
HPG = N_HEADS // N_KV
CMP_STRIDE = 16
CMP_BLOCK = 2 * CMP_STRIDE
CMP_HIDDEN = 128
SLC_BLOCK = 64
N_SEL = 16
WINDOW = 512
REL_BUCKETS = 32
REL_MAX_DIST = 128
S5_WIDTH = 512
S5_GROUP = 16
S5_GROUPS = S5_WIDTH // S5_GROUP
S5_STATE = 64
N_EGROUPS = 8
EXPERTS_PER_GROUP = 8
N_EXPERTS = N_EGROUPS * EXPERTS_PER_GROUP
TOP_K_IN_GROUP = 2
D_EXPERT = 256
EXPERT_BLOCK = 128
DEPTH = 1
DN_ALPHA = (2.0 * DEPTH) ** 0.25
LN_EPS = 1e-5
NEG_INF = -1e30
BIG = 1e9

ATTN_WIDTH = N_HEADS * HEAD_DIM
KV_WIDTH = N_KV * HEAD_DIM
KV_OFF = ATTN_WIDTH
NSA_GATE_OFF = KV_OFF + 6 * KV_WIDTH
S5_OFF = NSA_GATE_OFF + 3 * N_HEADS
MERGE_OFF = S5_OFF + S5_WIDTH

LANES = 128
SUBLANES = 8
VMEM_LIMIT_BYTES = 56 * 1024 * 1024

ATTN_TQ = 128
ATTN_TK = 256
GATE_PAD = LANES


def _gelu_tanh(x):
    c = math.sqrt(2.0 / math.pi)
    return x * (0.5 * (1.0 + jnp.tanh(c * (x + 0.044715 * (x * x * x)))))


def _dot(a, b):
    return jnp.dot(a, b, preferred_element_type=F32)


def _dot_nt(a, b):
    return lax.dot_general(a, b, (((1,), (1,)), ((), ())), preferred_element_type=F32)


def _compiler_params(semantics):
    return pltpu.CompilerParams(dimension_semantics=semantics, vmem_limit_bytes=VMEM_LIMIT_BYTES)


def _in_proj_layout(d_model):
    widths = (ATTN_WIDTH, 2 * KV_WIDTH, 2 * KV_WIDTH, 2 * KV_WIDTH, GATE_PAD, S5_WIDTH, 2 * d_model)
    offs = [0]
    for w in widths:
        offs.append(offs[-1] + w)
    return widths, offs


def _pack_in_proj(w_in, b_in, d_model):
    def kv_cols(j):
        return KV_OFF + j * KV_WIDTH

    def pair(jk, jv):
        cols = []
        for g in range(N_KV):
            cols.append(jnp.arange(kv_cols(jk) + g * HEAD_DIM, kv_cols(jk) + (g + 1) * HEAD_DIM))
            cols.append(jnp.arange(kv_cols(jv) + g * HEAD_DIM, kv_cols(jv) + (g + 1) * HEAD_DIM))
        return jnp.concatenate(cols)

    gate_cols = jnp.asarray([NSA_GATE_OFF + h * 3 + j for j in range(3) for h in range(N_HEADS)])
    idx = jnp.concatenate([
        jnp.arange(0, ATTN_WIDTH),
        pair(2, 3),
        pair(4, 5),
        jnp.arange(kv_cols(0), kv_cols(2)),
        gate_cols,
    ])
    idx2 = jnp.concatenate([jnp.arange(S5_OFF, S5_OFF + S5_WIDTH),
                            jnp.arange(MERGE_OFF, MERGE_OFF + 2 * d_model)])
    gpad = GATE_PAD - 3 * N_HEADS
    w = jnp.concatenate([w_in[:, idx], jnp.zeros((d_model, gpad), F32), w_in[:, idx2]], axis=1)
    b = jnp.concatenate([b_in[idx], jnp.zeros((gpad,), F32), b_in[idx2]])
    return w.astype(BF16), b.reshape(1, -1).astype(F32)


def _in_proj_kernel(offs, x_ref, w_ref, b_ref, q_ref, slc_ref, win_ref, cmp_ref, g_ref, u_ref, m_ref):
    xb = x_ref[...].astype(BF16)

    def proj(i):
        c0, c1 = offs[i], offs[i + 1]
        return _dot(xb, w_ref[:, c0:c1]) + b_ref[:, c0:c1]

    q_ref[...] = (proj(0) * (HEAD_DIM ** -0.5)).astype(BF16)
    slc_ref[...] = proj(1).astype(BF16)
    win_ref[...] = proj(2).astype(BF16)
    cmp_ref[...] = proj(3)
    g_ref[...] = jax.nn.sigmoid(proj(4))
    u_ref[...] = proj(5)
    m_ref[...] = jax.nn.sigmoid(proj(6))


def _in_proj(x2d, w_packed, b_packed, d_model, tm):
    n = x2d.shape[0]
    widths, offs = _in_proj_layout(d_model)
    ncols = offs[-1]
    dtypes = (BF16, BF16, BF16, F32, F32, F32, F32)
    return pl.pallas_call(
        functools.partial(_in_proj_kernel, tuple(offs)),
        grid=(n // tm,),
        in_specs=[
            pl.BlockSpec((tm, d_model), lambda i: (i, 0)),
            pl.BlockSpec((d_model, ncols), lambda i: (0, 0)),
            pl.BlockSpec((1, ncols), lambda i: (0, 0)),
        ],
        out_specs=[pl.BlockSpec((tm, w), lambda i: (i, 0)) for w in widths],
        out_shape=[jax.ShapeDtypeStruct((n, w), dt) for w, dt in zip(widths, dtypes)],
        compiler_params=_compiler_params(("parallel",)),
        name="in_proj",
    )(x2d, w_packed, b_packed)


def _compress_kernel(ck_ref, cv_ref, pos_ref, w1_ref, b1_ref, w2_ref, b2_ref, out_ref):
    n_c = ck_ref.shape[2]
    outs = []
    for i, c_ref in enumerate((ck_ref, cv_ref)):
        c = c_ref[0, 0]
        lo = (c + pos_ref[i, 0:1, :]).astype(BF16)
        hi = (c + pos_ref[i, 1:2, :]).astype(BF16)
        p_lo = _dot(lo, w1_ref[i, 0])
        p_hi = _dot(hi, w1_ref[i, 1])
        hid = p_lo + pltpu.roll(p_hi, n_c - 1, 0) + b1_ref[i]
        hid = _gelu_tanh(hid).astype(BF16)
        outs.append(_dot(hid, w2_ref[i]) + b2_ref[i])
    out_ref[0, 0] = jnp.concatenate(outs, axis=1).astype(BF16)


def _compress(cmp4, cmp_pos, cmp_w1, cmp_b1, cmp_w2, cmp_b2):
    b, _, n_c, cw = cmp4.shape
    half = CMP_STRIDE * HEAD_DIM
    pos = cmp_pos.reshape(2, 2, half).astype(F32)
    w1 = cmp_w1.reshape(2, 2, half, CMP_HIDDEN).astype(BF16)
    b1 = cmp_b1.reshape(2, 1, CMP_HIDDEN).astype(F32)
    w2 = cmp_w2.astype(BF16)
    b2 = cmp_b2.reshape(2, 1, HEAD_DIM).astype(F32)
    full = lambda shape: pl.BlockSpec(shape, lambda i, g: (0,) * len(shape))
    return pl.pallas_call(
        _compress_kernel,
        grid=(b, N_KV),
        in_specs=[
            pl.BlockSpec((1, 1, n_c, cw), lambda i, g: (i, g, 0, 0)),
            pl.BlockSpec((1, 1, n_c, cw), lambda i, g: (i, N_KV + g, 0, 0)),
            full((2, 2, half)),
            full((2, 2, half, CMP_HIDDEN)),
            full((2, 1, CMP_HIDDEN)),
            full((2, CMP_HIDDEN, HEAD_DIM)),
            full((2, 1, HEAD_DIM)),
        ],
        out_specs=pl.BlockSpec((1, 1, n_c, 2 * HEAD_DIM), lambda i, g: (i, g, 0, 0)),
        out_shape=jax.ShapeDtypeStruct((b, N_KV, n_c, 2 * HEAD_DIM), BF16),
        compiler_params=_compiler_params(("parallel", "parallel")),
        name="compress",
    )(cmp4, cmp4, pos, w1, b1, w2, b2)


def _t5_bucket(dist):
    n = jnp.maximum(dist, 0)
    max_exact = REL_BUCKETS // 2
    nf = jnp.maximum(n, 1).astype(F32)
    large = max_exact + (jnp.log(nf / max_exact) / math.log(REL_MAX_DIST / max_exact)
                         * (REL_BUCKETS - max_exact)).astype(jnp.int32)
    large = jnp.minimum(large, REL_BUCKETS - 1)
    return jnp.where(n < max_exact, n, large)


_KIND_DIAG, _KIND_PREV, _KIND_WIN_EDGE, _KIND_FAR = 0, 1, 2, 3


def _bucket_thresholds():
    buckets = _t5_bucket(jnp.arange(REL_MAX_DIST + 1))
    return jnp.sum(buckets[None, :] < jnp.arange(REL_BUCKETS)[:, None], axis=1).astype(jnp.int32)


def _bias_of_dist(dist, head, thr_ref, tbl_ref):
    bias = jnp.full(dist.shape, tbl_ref[head], F32)
    for k in range(1, REL_BUCKETS):
        bias = jnp.where(dist >= thr_ref[k], tbl_ref[k * N_HEADS + head], bias)
    return bias


BIAS_ROWS = 32


def _bias_c_kernel(thr_ref, tbl_ref, out_ref):
    _, tr, n_c = out_ref.shape
    r0 = pl.program_id(0) * tr

    def chunk(ci, carry):
        row0 = pl.multiple_of(ci * BIAS_ROWS, BIAS_ROWS)
        pos = r0 + row0 + lax.broadcasted_iota(jnp.int32, (BIAS_ROWS, n_c), 0)
        key_end = lax.broadcasted_iota(jnp.int32, (BIAS_ROWS, n_c), 1) * CMP_STRIDE + (CMP_BLOCK - 1)
        dist = pos - key_end
        for h in range(N_HEADS):
            bias = _bias_of_dist(dist, h, thr_ref, tbl_ref)
            out_ref[h, pl.ds(row0, BIAS_ROWS), :] = jnp.where(dist >= 0, bias, NEG_INF)
        return carry

    lax.fori_loop(0, tr // BIAS_ROWS, chunk, 0)


def _bias_tiles_kernel(thr_ref, tbl_ref, out_ref):
    tq, tk = ATTN_TQ, ATTN_TK
    parity = pl.program_id(0)
    kind = pl.program_id(1)
    off = jnp.where(kind == _KIND_FAR, 4 * REL_MAX_DIST + tk, kind * tk + parity * tq)
    hi = jnp.where(kind == _KIND_WIN_EDGE, WINDOW, 1 << 30)

    def chunk(ci, carry):
        row0 = pl.multiple_of(ci * BIAS_ROWS, BIAS_ROWS)
        dist = (off + row0 + lax.broadcasted_iota(jnp.int32, (BIAS_ROWS, tk), 0)
                - lax.broadcasted_iota(jnp.int32, (BIAS_ROWS, tk), 1))
        visible = (dist >= 0) & (dist < hi)
        for h in range(N_HEADS):
            bias = _bias_of_dist(dist, h, thr_ref, tbl_ref)
            out_ref[h // HPG, 0, 0, pl.ds((h % HPG) * tq + row0, BIAS_ROWS), :] = jnp.where(visible, bias, NEG_INF)
        return carry

    lax.fori_loop(0, tq // BIAS_ROWS, chunk, 0)


def _attention_bias_tables(rel_bias, seq):
    tbl = rel_bias.astype(F32).reshape(REL_BUCKETS * N_HEADS)
    thr = _bucket_thresholds()
    tq, tk = ATTN_TQ, ATTN_TK
    n_c = seq // CMP_STRIDE
    smem = pl.BlockSpec(memory_space=pltpu.SMEM)
    tr = min(512, seq)
    bias_c = pl.pallas_call(
        _bias_c_kernel,
        grid=(seq // tr,),
        in_specs=[smem, smem],
        out_specs=pl.BlockSpec((N_HEADS, tr, n_c), lambda i: (0, i, 0)),
        out_shape=jax.ShapeDtypeStruct((N_HEADS, seq, n_c), F32),
        compiler_params=_compiler_params(("parallel",)),
        name="bias_cmp",
    )(thr, tbl)
    bias_tiles = pl.pallas_call(
        _bias_tiles_kernel,
        grid=(tk // tq, 4),
        in_specs=[smem, smem],
        out_specs=pl.BlockSpec((N_KV, 1, 1, HPG * tq, tk), lambda p, k: (0, p, k, 0, 0)),
        out_shape=jax.ShapeDtypeStruct((N_KV, tk // tq, 4, HPG * tq, tk), F32),
        compiler_params=_compiler_params(("parallel", "parallel")),
        name="bias_tiles",
    )(thr, tbl)
    return bias_tiles, bias_c


def _nsa_constants(seq):
    n_c = seq // CMP_STRIDE
    n_blk = seq // SLC_BLOCK
    cmp_start = jnp.arange(n_c) * CMP_STRIDE
    blk_start = jnp.arange(n_blk) * SLC_BLOCK
    overlap = ((cmp_start[:, None] <= blk_start[None, :] + SLC_BLOCK - 1)
               & (cmp_start[:, None] + CMP_BLOCK - 1 >= blk_start[None, :]))
    overlap = overlap & (cmp_start[:, None] + CMP_BLOCK <= seq)
    expand = (jnp.arange(seq)[None, :] // SLC_BLOCK) == jnp.arange(n_blk)[:, None]
    return overlap.astype(BF16), expand.astype(BF16)


def _nsa_kernel(seq, q_ref, kcv_ref, slc_ref, win_ref, gate_ref, bias_c_ref, bias_t_ref,
                overlap_ref, expand_ref, o_ref, m_sc, l_sc, acc_sc):
    tq, tk = ATTN_TQ, ATTN_TK
    n_blk = seq // SLC_BLOCK
    n_sel = min(N_SEL, n_blk)
    rows = HPG * tq
    qt = pl.program_id(2)
    q0 = qt * tq
    kd = q0 // tk
    parity = qt % (tk // tq)
    lane = lax.broadcasted_iota(jnp.int32, (tq, LANES), 1)
    lo_half = lane < HEAD_DIM

    qs = []
    for pair_idx in range(HPG // 2):
        q2 = q_ref[0, :, pair_idx * LANES:(pair_idx + 1) * LANES].astype(F32)
        qs.append(jnp.where(lo_half, q2, 0.0))
        qs.append(jnp.where(lo_half, pltpu.roll(q2, HEAD_DIM, 1), 0.0))
    q_stack = jnp.concatenate(qs, axis=0).astype(BF16)

    kcv = kcv_ref[0, 0]
    s_c = _dot_nt(q_stack, kcv) + bias_c_ref[...].reshape(rows, -1)
    vis_c = s_c > 0.5 * NEG_INF
    m_c = jnp.max(s_c, axis=1, keepdims=True)
    e_c = jnp.exp(s_c - m_c)
    p_c = jnp.where(vis_c, e_c / jnp.sum(e_c, axis=1, keepdims=True), 0.0)
    p_cb = p_c.astype(BF16)
    o_cmp = _dot(p_cb, kcv)
    imp4 = _dot(p_cb, overlap_ref[...])
    imp = imp4[0:tq]
    for h in range(1, HPG):
        imp = imp + imp4[h * tq:(h + 1) * tq]

    blk = lax.broadcasted_iota(jnp.int32, (tq, n_blk), 1)
    pos = q0 + lax.broadcasted_iota(jnp.int32, (tq, n_blk), 0)
    cur = jnp.right_shift(pos, int(math.log2(SLC_BLOCK)))
    forced = (blk == 0) | (blk == cur) | (blk == cur - 1)
    valid = blk * SLC_BLOCK <= pos
    score = jnp.where(forced, BIG, jnp.where(valid, imp, -BIG))
    blk_f = blk.astype(F32)
    sel = jnp.zeros((tq, n_blk), F32)
    for _ in range(n_sel):
        top = jnp.max(score, axis=1, keepdims=True)
        first = jnp.min(jnp.where(score == top, blk_f, float(n_blk)), axis=1, keepdims=True)
        hit = blk_f == first
        sel = jnp.where(hit, 1.0, sel)
        score = jnp.where(hit, -jnp.inf, score)
    sel_pen = jnp.where(sel > 0.5, 0.0, NEG_INF).astype(BF16)

    def run_branch(kv_ref, k_lo, kind_of, use_sel):
        m_sc[...] = jnp.full((rows, 1), NEG_INF, F32)
        l_sc[...] = jnp.zeros((rows, 1), F32)
        acc_sc[...] = jnp.zeros((rows, LANES), F32)

        def body(kt, carry):
            k0 = pl.multiple_of(kt * tk, tk)
            kv = kv_ref[0, pl.ds(k0, tk), :]
            s = _dot_nt(q_stack, kv) + bias_t_ref[0, parity, kind_of(kt)]
            if use_sel:
                pen = _dot(sel_pen, expand_ref[:, pl.ds(k0, tk)])
                s = s + jnp.concatenate([pen] * HPG, axis=0)
            m_old = m_sc[...]
            m_new = jnp.maximum(m_old, jnp.max(s, axis=1, keepdims=True))
            p = jnp.exp(s - m_new)
            alpha = jnp.exp(m_old - m_new)
            l_sc[...] = alpha * l_sc[...] + jnp.sum(p, axis=1, keepdims=True)
            acc_sc[...] = alpha * acc_sc[...] + _dot(p.astype(BF16), kv)
            m_sc[...] = m_new
            return carry

        lax.fori_loop(k_lo, kd + 1, body, 0)
        return acc_sc[...] / l_sc[...]

    def kind_slc(kt):
        return jnp.where(kt == kd, _KIND_DIAG, jnp.where(kt == kd - 1, _KIND_PREV, _KIND_FAR))

    def kind_win(kt):
        return kd - kt

    o_slc = run_branch(slc_ref, 0, kind_slc, True)
    o_win = run_branch(win_ref, jnp.maximum(kd - WINDOW // tk, 0), kind_win, False)

    gates = gate_ref[0]
    g = pl.program_id(1)
    outs = []
    for h in range(HPG):
        sl = slice(h * tq, (h + 1) * tq)
        col = g * HPG + h

        def gate(branch):
            onehot = lane == branch * N_HEADS + col
            return jnp.sum(jnp.where(onehot, gates, 0.0), axis=1, keepdims=True)

        outs.append(gate(0) * o_cmp[sl] + gate(1) * o_slc[sl] + gate(2) * o_win[sl])
    for pair_idx in range(HPG // 2):
        even, odd = outs[2 * pair_idx], outs[2 * pair_idx + 1]
        merged = jnp.where(lo_half, pltpu.roll(even, HEAD_DIM, 1), odd)
        o_ref[0, :, pair_idx * LANES:(pair_idx + 1) * LANES] = merged.astype(o_ref.dtype)


def _nsa(q, kcv, slc, win, gates, bias_tiles, bias_c, overlap, expand):
    b, seq, _ = q.shape
    n_c = seq // CMP_STRIDE
    n_blk = seq // SLC_BLOCK
    tq, tk = ATTN_TQ, ATTN_TK
    rows = HPG * tq
    grp = HPG * HEAD_DIM
    return pl.pallas_call(
        functools.partial(_nsa_kernel, seq),
        grid=(b, N_KV, seq // tq),
        in_specs=[
            pl.BlockSpec((1, tq, grp), lambda i, g, t: (i, t, g)),
            pl.BlockSpec((1, 1, n_c, 2 * HEAD_DIM), lambda i, g, t: (i, g, 0, 0)),
            pl.BlockSpec((1, seq, 2 * HEAD_DIM), lambda i, g, t: (i, 0, g)),
            pl.BlockSpec((1, seq, 2 * HEAD_DIM), lambda i, g, t: (i, 0, g)),
            pl.BlockSpec((1, tq, GATE_PAD), lambda i, g, t: (i, t, 0)),
            pl.BlockSpec((HPG, tq, n_c), lambda i, g, t: (g, t, 0)),
            pl.BlockSpec((1, tk // tq, 4, rows, tk), lambda i, g, t: (g, 0, 0, 0, 0)),
            pl.BlockSpec((n_c, n_blk), lambda i, g, t: (0, 0)),
            pl.BlockSpec((n_blk, seq), lambda i, g, t: (0, 0)),
        ],
        out_specs=pl.BlockSpec((1, tq, grp), lambda i, g, t: (i, t, g)),
        out_shape=jax.ShapeDtypeStruct((b, seq, ATTN_WIDTH), BF16),
        scratch_shapes=[
            pltpu.VMEM((rows, 1), F32),
            pltpu.VMEM((rows, 1), F32),
            pltpu.VMEM((rows, LANES), F32),
        ],
        compiler_params=_compiler_params(("parallel", "parallel", "arbitrary")),
        name="nsa",
    )(q, kcv, slc, win, gates, bias_c, bias_tiles, overlap, expand)


def _mixer_inputs(x, rel_bias, w_in, b_in, cmp_pos, cmp_w1, cmp_b1, cmp_w2, cmp_b2):
    b, seq, d = x.shape
    wp, bp = _pack_in_proj(w_in, b_in, d)
    q, slc, win, cmp, gates, u, merge = _in_proj(x.reshape(b * seq, d), wp, bp, d, 512)
    n_c = seq // CMP_STRIDE
    cmp4 = cmp.reshape(b, seq, 2 * N_KV, HEAD_DIM).transpose(0, 2, 1, 3).reshape(
        b, 2 * N_KV, n_c, CMP_STRIDE * HEAD_DIM)
    kcv = _compress(cmp4, cmp_pos, cmp_w1, cmp_b1, cmp_w2, cmp_b2)
    bias_tiles, bias_c = _attention_bias_tables(rel_bias, seq)
    overlap, expand = _nsa_constants(seq)
    o = _nsa(q.reshape(b, seq, -1), kcv, slc.reshape(b, seq, -1), win.reshape(b, seq, -1),
             gates.reshape(b, seq, -1), bias_tiles, bias_c, overlap, expand)
    return o, u, merge


S5_HALF_GROUPS = S5_GROUPS // 2
S5_HALF_IN = S5_HALF_GROUPS * S5_GROUP
S5_HALF_STATE = S5_HALF_GROUPS * S5_STATE
S5_SCAN_LANES = 512
S5_CHUNK = 64
S5_UNROLL = 8


def _s5_params(lam_re, lam_im, log_dt, b_re, b_im, c_re, c_im, nb):
    dt = jnp.exp(log_dt.astype(F32))[:, None]
    lr, li = lam_re.astype(F32), lam_im.astype(F32)
    mag = jnp.exp(lr * dt)
    ab_re, ab_im = mag * jnp.cos(li * dt), mag * jnp.sin(li * dt)
    nr, ni = ab_re - 1.0, ab_im
    den = lr * lr + li * li
    fr, fi = (nr * lr + ni * li) / den, (ni * lr - nr * li) / den
    br, bim = b_re.astype(F32), b_im.astype(F32)
    bb_re = fr[..., None] * br - fi[..., None] * bim
    bb_im = fr[..., None] * bim + fi[..., None] * br
    eye = jnp.eye(S5_HALF_GROUPS, dtype=F32)

    def in_mat(bb):
        t = bb.reshape(2, S5_HALF_GROUPS, S5_STATE, S5_GROUP)
        m = jnp.einsum('kgph,gj->kghjp', t, eye)
        return m.reshape(2, S5_HALF_IN, S5_HALF_STATE)

    def out_mat(c):
        t = c.astype(F32).reshape(2, S5_HALF_GROUPS, S5_GROUP, S5_STATE)
        m = jnp.einsum('kghp,gj->kgpjh', t, eye)
        return m.reshape(2, S5_HALF_STATE, S5_HALF_IN)

    bmat = jnp.concatenate([in_mat(bb_re), in_mat(bb_im)], axis=2).astype(BF16)
    cmat = jnp.concatenate([out_mat(c_re), -out_mat(c_im)], axis=1).astype(BF16)
    a = jnp.concatenate([ab_re.reshape(2, S5_HALF_STATE), ab_im.reshape(2, S5_HALF_STATE)], axis=1)
    a = jnp.broadcast_to(a.reshape(1, 4 * S5_HALF_STATE), (nb, 4 * S5_HALF_STATE))
    return bmat, cmat, a


def _s5_kernel(u_ref, bmat_ref, cmat_ref, a_ref, d_ref, y_ref, ut_sc, x_sc, st_sc):
    nb, t_len, _ = u_ref.shape
    half_w = 2 * S5_HALF_STATE

    @pl.when(pl.program_id(0) == 0)
    def _():
        st_sc[...] = jnp.zeros_like(st_sc)

    n_cb = ut_sc.shape[0]
    for b in range(nb):
        for cb in range(n_cb):
            ut_sc[cb, pl.ds(b, t_len, stride=nb), :] = u_ref[b, :, cb * LANES:(cb + 1) * LANES]
    ut = jnp.concatenate([ut_sc[cb] for cb in range(n_cb)], axis=1)
    ub = ut.astype(BF16)
    for k in range(2):
        x_sc[:, k * half_w:(k + 1) * half_w] = _dot(ub[:, k * S5_HALF_IN:(k + 1) * S5_HALF_IN], bmat_ref[k])

    for k in range(2):
        for j in range(S5_HALF_STATE // S5_SCAN_LANES):
            re0 = k * half_w + j * S5_SCAN_LANES
            im0 = re0 + S5_HALF_STATE
            re_sl, im_sl = pl.ds(re0, S5_SCAN_LANES), pl.ds(im0, S5_SCAN_LANES)
            ar, ai = a_ref[:, re_sl], a_ref[:, im_sl]

            def steps(c, carry):
                xr, xi = carry
                for s in range(S5_UNROLL):
                    rows = pl.ds(pl.multiple_of((c * S5_UNROLL + s) * nb, nb), nb)
                    nxr = ar * xr - ai * xi + x_sc[rows, re_sl]
                    nxi = ar * xi + ai * xr + x_sc[rows, im_sl]
                    x_sc[rows, re_sl] = nxr
                    x_sc[rows, im_sl] = nxi
                    xr, xi = nxr, nxi
                return xr, xi

            xr, xi = lax.fori_loop(0, t_len // S5_UNROLL, steps, (st_sc[:, re_sl], st_sc[:, im_sl]))
            st_sc[:, re_sl] = xr
            st_sc[:, im_sl] = xi

    xs = x_sc[...].astype(BF16)
    y = jnp.concatenate([_dot(xs[:, k * half_w:(k + 1) * half_w], cmat_ref[k]) for k in range(2)], axis=1)
    y = y + d_ref[...] * ut
    for cb in range(n_cb):
        ut_sc[cb] = y[:, cb * LANES:(cb + 1) * LANES]
    for b in range(nb):
        for cb in range(n_cb):
            y_ref[b, :, cb * LANES:(cb + 1) * LANES] = ut_sc[cb, pl.ds(b, t_len, stride=nb), :]


def _s5(u, bmat, cmat, a, d_skip):
    nb, seq, w = u.shape
    t_len = min(S5_CHUNK, seq)
    full = lambda shape: pl.BlockSpec(shape, lambda c: (0,) * len(shape))
    return pl.pallas_call(
        _s5_kernel,
        grid=(seq // t_len,),
        in_specs=[
            pl.BlockSpec((nb, t_len, w), lambda c: (0, c, 0)),
            full(bmat.shape), full(cmat.shape), full(a.shape), full((1, w)),
        ],
        out_specs=pl.BlockSpec((nb, t_len, w), lambda c: (0, c, 0)),
        out_shape=jax.ShapeDtypeStruct((nb, seq, w), F32),
        scratch_shapes=[
            pltpu.VMEM((w // LANES, t_len * nb, LANES), F32),
            pltpu.VMEM((t_len * nb, 4 * S5_HALF_STATE), F32),
            pltpu.VMEM((nb, 4 * S5_HALF_STATE), F32),
        ],
        compiler_params=_compiler_params(("arbitrary",)),
        name="s5",
    )(u, bmat, cmat, a, d_skip.reshape(1, w).astype(F32))


ROUTE_PAD = LANES
_R_E1, _R_E2, _R_W1, _R_W2, _R_RANK1, _R_RANK2 = range(6)


def _layer_norm(t, g, b):
    mu = jnp.mean(t, axis=1, keepdims=True)
    c = t - mu
    var = jnp.mean(c * c, axis=1, keepdims=True)
    return c * lax.rsqrt(var + LN_EPS) * g + b


def _post_kernel(x_ref, o_ref, y_ref, m_ref, wup_ref, wval_ref, wgate_ref, bgate_ref, wout_ref,
                 g1_ref, b1_ref, wr_ref, br_ref, h_ref, route_ref, cnt_ref, run_sc):
    tm, d = x_ref.shape

    @pl.when(pl.program_id(0) == 0)
    def _():
        run_sc[...] = jnp.zeros_like(run_sc)

    y_a = _dot(o_ref[...], wup_ref[...])
    z = _gelu_tanh(y_ref[...]).astype(BF16)
    y_b = _dot(z, wval_ref[...]) * jax.nn.sigmoid(_dot(z, wgate_ref[...]) + bgate_ref[...])
    mixed = m_ref[:, :d] * y_a + m_ref[:, d:] * y_b
    t = DN_ALPHA * x_ref[...] + _dot(mixed.astype(BF16), wout_ref[...])
    h = _layer_norm(t, g1_ref[...], b1_ref[...])
    h_ref[...] = h

    logits = _dot(h.astype(BF16), wr_ref[...]) + br_ref[...]
    lane = lax.broadcasted_iota(jnp.int32, (tm, ROUTE_PAD), 1)
    lane_f = lane.astype(F32)
    is_group = lane < N_EGROUPS

    def first_max(v):
        top = jnp.max(v, axis=1, keepdims=True)
        idx = jnp.min(jnp.where(v == top, lane_f, float(ROUTE_PAD)), axis=1, keepdims=True)
        return top, idx

    g_max, g_top = first_max(jnp.where(is_group, logits, -jnp.inf))
    p_group = 1.0 / jnp.sum(jnp.where(is_group, jnp.exp(logits - g_max), 0.0), axis=1, keepdims=True)
    grp_of_lane = jnp.right_shift(lane - N_EGROUPS, int(math.log2(EXPERTS_PER_GROUP))).astype(F32)
    in_group = (lane >= N_EGROUPS) & (lane < N_EGROUPS + N_EXPERTS) & (grp_of_lane == g_top)
    e_log = jnp.where(in_group, logits, -jnp.inf)
    v1, i1 = first_max(e_log)
    hit1 = lane_f == i1
    v2, i2 = first_max(jnp.where(hit1, -jnp.inf, e_log))
    hit2 = lane_f == i2
    e2 = jnp.exp(v2 - v1)
    w1 = p_group / (1.0 + e2)
    w2 = p_group * e2 / (1.0 + e2)

    hits = jnp.where(hit1 | hit2, 1.0, 0.0)
    row = lax.broadcasted_iota(jnp.int32, (tm, tm), 0)
    col = lax.broadcasted_iota(jnp.int32, (tm, tm), 1)
    earlier = jnp.where(col < row, 1.0, 0.0).astype(BF16)
    before = _dot(earlier, hits.astype(BF16)) + run_sc[...]
    rank1 = jnp.sum(jnp.where(hit1, before, 0.0), axis=1, keepdims=True)
    rank2 = jnp.sum(jnp.where(hit2, before, 0.0), axis=1, keepdims=True)
    run_sc[...] = run_sc[...] + jnp.sum(hits, axis=0, keepdims=True)
    cnt_ref[...] = run_sc[...]

    rec = jnp.zeros((tm, ROUTE_PAD), F32)
    for slot, val in ((_R_E1, i1 - N_EGROUPS), (_R_E2, i2 - N_EGROUPS), (_R_W1, w1), (_R_W2, w2),
                      (_R_RANK1, rank1), (_R_RANK2, rank2)):
        rec = jnp.where(lane == slot, val, rec)
    route_ref[...] = rec


def _post(x2d, o2d, y2d, merge, w_attn_up, s5_w_val, s5_w_gate, s5_b_gate, w_out, ln1_g, ln1_b,
          router_w_group, router_b_group, router_w_expert, router_b_expert, tm):
    n, d = x2d.shape
    rpad = ROUTE_PAD - N_EGROUPS - N_EXPERTS
    wr = jnp.concatenate([router_w_group, router_w_expert, jnp.zeros((d, rpad), F32)], axis=1).astype(BF16)
    br = jnp.concatenate([router_b_group, router_b_expert, jnp.zeros((rpad,), F32)]).reshape(1, -1).astype(F32)
    row = lambda w: pl.BlockSpec((tm, w), lambda i: (i, 0))
    full = lambda a: pl.BlockSpec(a.shape, lambda i: (0,) * a.ndim)
    weights = [w_attn_up.astype(BF16), s5_w_val.astype(BF16), s5_w_gate.astype(BF16),
               s5_b_gate.reshape(1, d).astype(F32), w_out.astype(BF16),
               ln1_g.reshape(1, d).astype(F32), ln1_b.reshape(1, d).astype(F32), wr, br]
    return pl.pallas_call(
        _post_kernel,
        grid=(n // tm,),
        in_specs=[row(d), row(ATTN_WIDTH), row(S5_WIDTH), row(2 * d)] + [full(w) for w in weights],
        out_specs=[row(d), row(ROUTE_PAD), pl.BlockSpec((1, ROUTE_PAD), lambda i: (0, 0))],
        out_shape=[jax.ShapeDtypeStruct((n, d), F32), jax.ShapeDtypeStruct((n, ROUTE_PAD), F32),
                   jax.ShapeDtypeStruct((1, ROUTE_PAD), F32)],
        scratch_shapes=[pltpu.VMEM((1, ROUTE_PAD), F32)],
        compiler_params=_compiler_params(("arbitrary",)),
        name="post_mixer",
    )(x2d, o2d, y2d, merge, *weights)


def _row_gather_start(idx_ref, base, n_rows, src_hbm, dst, sem):
    def body(r, c):
        pltpu.make_async_copy(src_hbm.at[idx_ref[base + r]], dst.at[r], sem).start()
        return c
    lax.fori_loop(0, n_rows, body, 0)


def _row_gather_wait(n_rows, src_hbm, dst, sem):
    def body(r, c):
        pltpu.make_async_copy(src_hbm.at[0], dst.at[r], sem).wait()
        return c
    lax.fori_loop(0, n_rows, body, 0)


def _rows_to_matrix(buf):
    return jnp.concatenate([buf[:, s, :] for s in range(buf.shape[1])], axis=1)


def _expert_kernel(row_tok_ref, blk_exp_ref, h_hbm, wg_ref, wu_ref, wd_ref, y_ref, xbuf, sem,
                   wg_sc, wu_sc, wd_sc):
    i = pl.program_id(0)
    n_blocks = pl.num_programs(0)
    slot = i % 2

    @pl.when((i == 0) | (blk_exp_ref[i] != blk_exp_ref[jnp.maximum(i - 1, 0)]))
    def _():
        wg_sc[...] = wg_ref[0].astype(BF16)
        wu_sc[...] = wu_ref[0].astype(BF16)
        wd_sc[...] = wd_ref[0].astype(BF16)

    @pl.when(i == 0)
    def _():
        _row_gather_start(row_tok_ref, 0, EXPERT_BLOCK, h_hbm, xbuf.at[0], sem.at[0])

    @pl.when(i + 1 < n_blocks)
    def _():
        _row_gather_start(row_tok_ref, (i + 1) * EXPERT_BLOCK, EXPERT_BLOCK, h_hbm,
                          xbuf.at[1 - slot], sem.at[1 - slot])

    _row_gather_wait(EXPERT_BLOCK, h_hbm, xbuf.at[slot], sem.at[slot])
    xb = _rows_to_matrix(xbuf.at[slot]).astype(BF16)
    h_gate = _dot(xb, wg_sc[...])
    h_up = _dot(xb, wu_sc[...])
    hb = (h_gate * jax.nn.sigmoid(h_gate) * h_up).astype(BF16)
    y = _dot(hb, wd_sc[...])
    for s in range(y_ref.shape[1]):
        y_ref[:, s, :] = y[:, s * LANES:(s + 1) * LANES]


def _experts(row_tok, blk_expert, h3, w_gate, w_up, w_down):
    n_blocks = blk_expert.shape[0]
    _, s_dim, _ = h3.shape
    d = s_dim * LANES
    grid_spec = pltpu.PrefetchScalarGridSpec(
        num_scalar_prefetch=2,
        grid=(n_blocks,),
        in_specs=[
            pl.BlockSpec(memory_space=pl.ANY),
            pl.BlockSpec((1, d, D_EXPERT), lambda i, rt, be: (be[i], 0, 0)),
            pl.BlockSpec((1, d, D_EXPERT), lambda i, rt, be: (be[i], 0, 0)),
            pl.BlockSpec((1, D_EXPERT, d), lambda i, rt, be: (be[i], 0, 0)),
        ],
        out_specs=pl.BlockSpec((EXPERT_BLOCK, s_dim, LANES), lambda i, rt, be: (i, 0, 0)),
        scratch_shapes=[
            pltpu.VMEM((2, EXPERT_BLOCK, s_dim, LANES), F32),
            pltpu.SemaphoreType.DMA((2,)),
            pltpu.VMEM((d, D_EXPERT), BF16),
            pltpu.VMEM((d, D_EXPERT), BF16),
            pltpu.VMEM((D_EXPERT, d), BF16),
        ],
    )
    return pl.pallas_call(
        _expert_kernel,
        grid_spec=grid_spec,
        out_shape=jax.ShapeDtypeStruct((n_blocks * EXPERT_BLOCK, s_dim, LANES), F32),
        compiler_params=_compiler_params(("arbitrary",)),
        name="experts",
    )(row_tok, blk_expert, h3, w_gate, w_up, w_down)


COMBINE_TM = 128


def _combine_kernel(dest1_ref, dest2_ref, yb_hbm, h_ref, route_ref, g2_ref, b2_ref, out_ref, ybuf, sem):
    i = pl.program_id(0)
    n_tiles = pl.num_programs(0)
    tm = h_ref.shape[0]
    slot = i % 2

    def start(tile, s):
        _row_gather_start(dest1_ref, tile * tm, tm, yb_hbm, ybuf.at[s, 0], sem.at[s, 0])
        _row_gather_start(dest2_ref, tile * tm, tm, yb_hbm, ybuf.at[s, 1], sem.at[s, 1])

    @pl.when(i == 0)
    def _():
        start(0, 0)

    @pl.when(i + 1 < n_tiles)
    def _():
        start(i + 1, 1 - slot)

    _row_gather_wait(tm, yb_hbm, ybuf.at[slot, 0], sem.at[slot, 0])
    _row_gather_wait(tm, yb_hbm, ybuf.at[slot, 1], sem.at[slot, 1])
    y1 = _rows_to_matrix(ybuf.at[slot, 0])
    y2 = _rows_to_matrix(ybuf.at[slot, 1])
    route = route_ref[...]
    w1 = route[:, _R_W1:_R_W1 + 1]
    w2 = route[:, _R_W2:_R_W2 + 1]
    t = DN_ALPHA * h_ref[...] + (y1 * w1 + y2 * w2)
    out_ref[...] = _layer_norm(t, g2_ref[...], b2_ref[...])


def _combine(dest1, dest2, yb3, h2d, route, ln2_g, ln2_b):
    n, d = h2d.shape
    tm = COMBINE_TM
    _, s_dim, _ = yb3.shape
    grid_spec = pltpu.PrefetchScalarGridSpec(
        num_scalar_prefetch=2,
        grid=(n // tm,),
        in_specs=[
            pl.BlockSpec(memory_space=pl.ANY),
            pl.BlockSpec((tm, d), lambda i, d1, d2: (i, 0)),
            pl.BlockSpec((tm, ROUTE_PAD), lambda i, d1, d2: (i, 0)),
            pl.BlockSpec((1, d), lambda i, d1, d2: (0, 0)),
            pl.BlockSpec((1, d), lambda i, d1, d2: (0, 0)),
        ],
        out_specs=pl.BlockSpec((tm, d), lambda i, d1, d2: (i, 0)),
        scratch_shapes=[
            pltpu.VMEM((2, 2, tm, s_dim, LANES), F32),
            pltpu.SemaphoreType.DMA((2, 2)),
        ],
    )
    return pl.pallas_call(
        _combine_kernel,
        grid_spec=grid_spec,
        out_shape=jax.ShapeDtypeStruct((n, d), F32),
        compiler_params=_compiler_params(("arbitrary",)),
        name="combine",
    )(dest1, dest2, yb3, h2d, route, ln2_g.reshape(1, d).astype(F32), ln2_b.reshape(1, d).astype(F32))


def _moe(h2d, route, counts_row, w_gate, w_up, w_down, ln2_g, ln2_b):
    n, d = h2d.shape
    e1 = route[:, _R_E1].astype(jnp.int32)
    e2 = route[:, _R_E2].astype(jnp.int32)
    rank1 = route[:, _R_RANK1].astype(jnp.int32)
    rank2 = route[:, _R_RANK2].astype(jnp.int32)
    counts = counts_row[0, N_EGROUPS:N_EGROUPS + N_EXPERTS].astype(jnp.int32)
    padded = (counts + EXPERT_BLOCK - 1) // EXPERT_BLOCK * EXPERT_BLOCK
    pend = jnp.cumsum(padded)
    pstart = pend - padded
    dest1 = pstart[e1] + rank1
    dest2 = pstart[e2] + rank2
    n_blocks = -(-(n * TOP_K_IN_GROUP) // EXPERT_BLOCK) + N_EXPERTS
    tok = jnp.arange(n, dtype=jnp.int32)
    row_tok = jnp.zeros((n_blocks * EXPERT_BLOCK,), jnp.int32).at[dest1].set(tok).at[dest2].set(tok)
    blk_row0 = jnp.arange(n_blocks, dtype=jnp.int32) * EXPERT_BLOCK
    blk_expert = jnp.minimum(jnp.sum(pend[None, :] <= blk_row0[:, None], axis=1), N_EXPERTS - 1).astype(jnp.int32)
    h3 = h2d.reshape(n, d // LANES, LANES)
    yb3 = _experts(row_tok, blk_expert, h3, w_gate, w_up, w_down)
    return _combine(dest1, dest2, yb3, h2d, route, ln2_g, ln2_b)


def kernel(x, rel_bias, w_in, b_in, cmp_pos, cmp_w1, cmp_b1, cmp_w2, cmp_b2, w_attn_up, s5_lambda_re, s5_lambda_im, s5_log_dt, s5_b_re, s5_b_im, s5_c_re, s5_c_im, s5_d, s5_w_val, s5_w_gate, s5_b_gate, w_out, ln1_g, ln1_b, router_w_group, router_b_group, router_w_expert, router_b_expert, exp_w_gate, exp_w_up, exp_w_down, ln2_g, ln2_b):
    b, seq, d = x.shape
    n = b * seq
    assert w_in.shape[0] == DEPTH
    l = 0
    o, u, merge = _mixer_inputs(x, rel_bias, w_in[l], b_in[l], cmp_pos[l], cmp_w1[l], cmp_b1[l],
                                cmp_w2[l], cmp_b2[l])
    bmat, cmat, a = _s5_params(s5_lambda_re[l], s5_lambda_im[l], s5_log_dt[l], s5_b_re[l], s5_b_im[l],
                               s5_c_re[l], s5_c_im[l], b)
    y_s = _s5(u.reshape(b, seq, S5_WIDTH), bmat, cmat, a, s5_d[l])
    h2d, route, counts = _post(x.reshape(n, d), o.reshape(n, ATTN_WIDTH), y_s.reshape(n, S5_WIDTH), merge,
                               w_attn_up[l], s5_w_val[l], s5_w_gate[l], s5_b_gate[l], w_out[l], ln1_g[l],
                               ln1_b[l], router_w_group[l], router_b_group[l], router_w_expert[l],
                               router_b_expert[l], 256)
    out = _moe(h2d, route, counts, exp_w_gate[l], exp_w_up[l], exp_w_down[l], ln2_g[l], ln2_b[l])
    return out.reshape(b, seq, d)
```

```python
import functools
import math

import jax
import jax.numpy as jnp
from jax import lax
from jax.experimental import pallas as pl
from jax.experimental.pallas import tpu as pltpu

F32 = jnp.float32
BF16 = jnp.bfloat16

N_HEADS = 8
HEAD_DIM = 64
N_KV = 2
HPG = N_HEADS // N_KV
CMP_STRIDE = 16
CMP_BLOCK = 2 * CMP_STRIDE
CMP_HIDDEN = 128
SLC_BLOCK = 64
N_SEL = 16
WINDOW = 512
REL_BUCKETS = 32
REL_MAX_DIST = 128
S5_WIDTH = 512
S5_GROUP = 16
S5_GROUPS = S5_WIDTH // S5_GROUP
S5_STATE = 64
N_EGROUPS = 8
EXPERTS_PER_GROUP = 8
N_EXPERTS = N_EGROUPS * EXPERTS_PER_GROUP
TOP_K_IN_GROUP = 2
D_EXPERT = 256
EXPERT_BLOCK = 128
DEPTH = 1
DN_ALPHA = (2.0 * DEPTH) ** 0.25
LN_EPS = 1e-5
NEG_INF = -1e30
BIG = 1e9

ATTN_WIDTH = N_HEADS * HEAD_DIM
KV_WIDTH = N_KV * HEAD_DIM
KV_OFF = ATTN_WIDTH
NSA_GATE_OFF = KV_OFF + 6 * KV_WIDTH
S5_OFF = NSA_GATE_OFF + 3 * N_HEADS
MERGE_OFF = S5_OFF + S5_WIDTH

LANES = 128
SUBLANES = 8
VMEM_LIMIT_BYTES = 56 * 1024 * 1024

ATTN_TQ = 128
SLC_FAR_TK = 256
SLC_NEAR_BACK = ATTN_TQ
KV_PAD = WINDOW
GATE_PAD = LANES


def _gelu_tanh(x):
    c = math.sqrt(2.0 / math.pi)
    return x * (0.5 * (1.0 + jnp.tanh(c * (x + 0.044715 * (x * x * x)))))


def _dot(a, b):
    return jnp.dot(a, b, preferred_element_type=F32)


def _dot_nt(a, b):
    return lax.dot_general(a, b, (((1,), (1,)), ((), ())), preferred_element_type=F32)


def _compiler_params(semantics):
    return pltpu.CompilerParams(dimension_semantics=semantics, vmem_limit_bytes=VMEM_LIMIT_BYTES)


def _in_proj_layout(d_model):
    widths = (ATTN_WIDTH, 2 * KV_WIDTH, 2 * KV_WIDTH, 2 * KV_WIDTH, N_KV * GATE_PAD, S5_WIDTH, 2 * d_model)
    offs = [0]
    for w in widths:
        offs.append(offs[-1] + w)
    return widths, offs


def _pack_in_proj(w_in, b_in, d_model):
    def kv_cols(j):
        return KV_OFF + j * KV_WIDTH

    def pair(jk, jv):
        cols = []
        for g in range(N_KV):
            cols.append(jnp.arange(kv_cols(jk) + g * HEAD_DIM, kv_cols(jk) + (g + 1) * HEAD_DIM))
            cols.append(jnp.arange(kv_cols(jv) + g * HEAD_DIM, kv_cols(jv) + (g + 1) * HEAD_DIM))
        return jnp.concatenate(cols)

    idx = jnp.concatenate([
        jnp.arange(0, ATTN_WIDTH),
        pair(2, 3),
        pair(4, 5),
        jnp.arange(kv_cols(0), kv_cols(2)),
    ])
    idx2 = jnp.concatenate([jnp.arange(S5_OFF, S5_OFF + S5_WIDTH),
                            jnp.arange(MERGE_OFF, MERGE_OFF + 2 * d_model)])
    gpad = GATE_PAD - 3 * HPG
    w_parts, b_parts = [w_in[:, idx]], [b_in[idx]]
    for g in range(N_KV):
        cols = jnp.asarray([NSA_GATE_OFF + (g * HPG + h) * 3 + j for j in range(3) for h in range(HPG)])
        w_parts += [w_in[:, cols], jnp.zeros((d_model, gpad), F32)]
        b_parts += [b_in[cols], jnp.zeros((gpad,), F32)]
    w = jnp.concatenate(w_parts + [w_in[:, idx2]], axis=1)
    b = jnp.concatenate(b_parts + [b_in[idx2]])
    return w.astype(BF16), b.reshape(1, -1).astype(F32)


def _in_proj_kernel(offs, x_ref, w_ref, b_ref, q_ref, slc_ref, win_ref, cmp_ref, g_ref, u_ref, m_ref):
    xb = x_ref[...].astype(BF16)

    def proj(i):
        c0, c1 = offs[i], offs[i + 1]
        return _dot(xb, w_ref[:, c0:c1]) + b_ref[:, c0:c1]

    q_ref[...] = (proj(0) * (HEAD_DIM ** -0.5)).astype(BF16)
    slc_ref[...] = proj(1).astype(BF16)
    win_ref[...] = proj(2).astype(BF16)
    cmp_ref[...] = proj(3)
    g_ref[...] = jax.nn.sigmoid(proj(4))
    u_ref[...] = proj(5)
    m_ref[...] = jax.nn.sigmoid(proj(6))


def _in_proj(x2d, w_packed, b_packed, d_model, tm):
    n = x2d.shape[0]
    widths, offs = _in_proj_layout(d_model)
    ncols = offs[-1]
    dtypes = (BF16, BF16, BF16, F32, F32, F32, F32)
    return pl.pallas_call(
        functools.partial(_in_proj_kernel, tuple(offs)),
        grid=(n // tm,),
        in_specs=[
            pl.BlockSpec((tm, d_model), lambda i: (i, 0)),
            pl.BlockSpec((d_model, ncols), lambda i: (0, 0)),
            pl.BlockSpec((1, ncols), lambda i: (0, 0)),
        ],
        out_specs=[pl.BlockSpec((tm, w), lambda i: (i, 0)) for w in widths],
        out_shape=[jax.ShapeDtypeStruct((n, w), dt) for w, dt in zip(widths, dtypes)],
        compiler_params=_compiler_params(("parallel",)),
        name="in_proj",
    )(x2d, w_packed, b_packed)


def _compress_kernel(ck_ref, cv_ref, pos_ref, w1_ref, b1_ref, w2_ref, b2_ref, out_ref):
    n_c = ck_ref.shape[2]
    outs = []
    for i, c_ref in enumerate((ck_ref, cv_ref)):
        c = c_ref[0, 0]
        lo = (c + pos_ref[i, 0:1, :]).astype(BF16)
        hi = (c + pos_ref[i, 1:2, :]).astype(BF16)
        p_lo = _dot(lo, w1_ref[i, 0])
        p_hi = _dot(hi, w1_ref[i, 1])
        hid = p_lo + pltpu.roll(p_hi, n_c - 1, 0) + b1_ref[i]
        hid = _gelu_tanh(hid).astype(BF16)
        outs.append(_dot(hid, w2_ref[i]) + b2_ref[i])
    out_ref[0, 0] = jnp.concatenate(outs[::-1], axis=1).astype(BF16)


def _compress(cmp4, cmp_pos, cmp_w1, cmp_b1, cmp_w2, cmp_b2):
    b, _, n_c, cw = cmp4.shape
    half = CMP_STRIDE * HEAD_DIM
    pos = cmp_pos.reshape(2, 2, half).astype(F32)
    w1 = cmp_w1.reshape(2, 2, half, CMP_HIDDEN).astype(BF16)
    b1 = cmp_b1.reshape(2, 1, CMP_HIDDEN).astype(F32)
    w2 = cmp_w2.astype(BF16)
    b2 = cmp_b2.reshape(2, 1, HEAD_DIM).astype(F32)
    full = lambda shape: pl.BlockSpec(shape, lambda i, g: (0,) * len(shape))
    return pl.pallas_call(
        _compress_kernel,
        grid=(b, N_KV),
        in_specs=[
            pl.BlockSpec((1, 1, n_c, cw), lambda i, g: (i, g, 0, 0)),
            pl.BlockSpec((1, 1, n_c, cw), lambda i, g: (i, N_KV + g, 0, 0)),
            full((2, 2, half)),
            full((2, 2, half, CMP_HIDDEN)),
            full((2, 1, CMP_HIDDEN)),
            full((2, CMP_HIDDEN, HEAD_DIM)),
            full((2, 1, HEAD_DIM)),
        ],
        out_specs=pl.BlockSpec((1, 1, n_c, 2 * HEAD_DIM), lambda i, g: (i, g, 0, 0)),
        out_shape=jax.ShapeDtypeStruct((b, N_KV, n_c, 2 * HEAD_DIM), BF16),
        compiler_params=_compiler_params(("parallel", "parallel")),
        name="compress",
    )(cmp4, cmp4, pos, w1, b1, w2, b2)


def _t5_bucket(dist):
    n = jnp.maximum(dist, 0)
    max_exact = REL_BUCKETS // 2
    nf = jnp.maximum(n, 1).astype(F32)
    large = max_exact + (jnp.log(nf / max_exact) / math.log(REL_MAX_DIST / max_exact)
                         * (REL_BUCKETS - max_exact)).astype(jnp.int32)
    large = jnp.minimum(large, REL_BUCKETS - 1)
    return jnp.where(n < max_exact, n, large)


def _bucket_thresholds():
    buckets = _t5_bucket(jnp.arange(REL_MAX_DIST + 1))
    return jnp.sum(buckets[None, :] < jnp.arange(REL_BUCKETS)[:, None], axis=1).astype(jnp.int32)


def _bias_of_dist(dist, head, thr_ref, tbl_ref):
    bias = jnp.full(dist.shape, tbl_ref[head], F32)
    for k in range(1, REL_BUCKETS):
        bias = jnp.where(dist >= thr_ref[k], tbl_ref[k * N_HEADS + head], bias)
    return bias


BIAS_ROWS = 32


def _bias_c_kernel(thr_ref, tbl_ref, out_ref):
    _, tr, n_c = out_ref.shape
    r0 = pl.program_id(0) * tr

    def chunk(ci, carry):
        row0 = pl.multiple_of(ci * BIAS_ROWS, BIAS_ROWS)
        pos = r0 + row0 + lax.broadcasted_iota(jnp.int32, (BIAS_ROWS, n_c), 0)
        key_end = lax.broadcasted_iota(jnp.int32, (BIAS_ROWS, n_c), 1) * CMP_STRIDE + (CMP_BLOCK - 1)
        dist = pos - key_end
        for h in range(N_HEADS):
            bias = _bias_of_dist(dist, h, thr_ref, tbl_ref)
            out_ref[h, pl.ds(row0, BIAS_ROWS), :] = jnp.where(dist >= 0, bias, NEG_INF)
        return carry

    lax.fori_loop(0, tr // BIAS_ROWS, chunk, 0)


def _bias_near_kernel(thr_ref, tbl_ref, near_ref, win_ref, far_ref):
    tq = ATTN_TQ
    h = pl.program_id(0)

    def table(out_ref, lo_keys, window):
        width = out_ref.shape[2]

        def chunk(ci, carry):
            row0 = pl.multiple_of(ci * BIAS_ROWS, BIAS_ROWS)
            dist = (lo_keys + row0 + lax.broadcasted_iota(jnp.int32, (BIAS_ROWS, width), 0)
                    - lax.broadcasted_iota(jnp.int32, (BIAS_ROWS, width), 1))
            visible = (dist >= 0) & (dist < window)
            bias = jnp.full(dist.shape, tbl_ref[h], F32)
            for k in range(1, REL_BUCKETS):
                bias = jnp.where(dist >= thr_ref[k], tbl_ref[k * N_HEADS + h], bias)
            out_ref[0, pl.ds(row0, BIAS_ROWS), :] = jnp.where(visible, bias, NEG_INF)
            return carry

        lax.fori_loop(0, tq // BIAS_ROWS, chunk, 0)

    table(near_ref, SLC_NEAR_BACK, 1 << 30)
    table(win_ref, WINDOW, WINDOW)
    far_ref[0] = jnp.full(far_ref.shape[1:], tbl_ref[(REL_BUCKETS - 1) * N_HEADS + h], F32)


def _attention_bias_tables(rel_bias, seq):
    tbl = rel_bias.astype(F32).reshape(REL_BUCKETS * N_HEADS)
    thr = _bucket_thresholds()
    tq = ATTN_TQ
    n_c = seq // CMP_STRIDE
    smem = pl.BlockSpec(memory_space=pltpu.SMEM)
    tr = min(512, seq)
    bias_c = pl.pallas_call(
        _bias_c_kernel,
        grid=(seq // tr,),
        in_specs=[smem, smem],
        out_specs=pl.BlockSpec((N_HEADS, tr, n_c), lambda i: (0, i, 0)),
        out_shape=jax.ShapeDtypeStruct((N_HEADS, seq, n_c), F32),
        compiler_params=_compiler_params(("parallel",)),
        name="bias_cmp",
    )(thr, tbl)
    head_block = lambda w: pl.BlockSpec((1, tq, w), lambda h: (h, 0, 0))
    widths = (SLC_NEAR_BACK + tq, WINDOW + tq, LANES)
    bias_near, bias_win, bias_far = pl.pallas_call(
        _bias_near_kernel,
        grid=(N_HEADS,),
        in_specs=[smem, smem],
        out_specs=[head_block(w) for w in widths],
        out_shape=[jax.ShapeDtypeStruct((N_HEADS, tq, w), F32) for w in widths],
        compiler_params=_compiler_params(("parallel",)),
        name="bias_near",
    )(thr, tbl)
    return bias_c, bias_near, bias_win, bias_far


def _nsa_constants(seq):
    n_c = seq // CMP_STRIDE
    n_blk = seq // SLC_BLOCK
    cmp_start = jnp.arange(n_c) * CMP_STRIDE
    blk_start = jnp.arange(n_blk) * SLC_BLOCK
    overlap_t = ((cmp_start[None, :] <= blk_start[:, None] + SLC_BLOCK - 1)
                 & (cmp_start[None, :] + CMP_BLOCK - 1 >= blk_start[:, None]))
    overlap_t = overlap_t & (cmp_start[None, :] + CMP_BLOCK <= seq)
    ones_rows = jnp.arange(SUBLANES)[:, None] == 0
    overlap_t = jnp.concatenate([overlap_t, jnp.broadcast_to(ones_rows, (SUBLANES, n_c))], axis=0)
    cmp_ones = jnp.broadcast_to(jnp.arange(LANES)[None, :] == 0, (n_c, LANES))
    return overlap_t.astype(BF16), cmp_ones.astype(BF16)


def _build_kv_scratch(seq, slc_ref, win_ref, ks_sc, vs_sc, kw_sc, vw_sc):
    chunk = min(512, seq)
    lane_p = lax.broadcasted_iota(jnp.int32, (KV_PAD, LANES), 1)
    zeros = jnp.zeros((KV_PAD, LANES), BF16)
    ks_sc[0:KV_PAD] = jnp.where(lane_p >= HEAD_DIM, 1.0, 0.0).astype(BF16)
    kw_sc[0:KV_PAD] = jnp.where(lane_p == HEAD_DIM, NEG_INF, 0.0).astype(BF16)
    vs_sc[0:KV_PAD] = zeros
    vw_sc[0:KV_PAD] = zeros
    lane = lax.broadcasted_iota(jnp.int32, (chunk, LANES), 1)
    row = lax.broadcasted_iota(jnp.int32, (chunk, LANES), 0)
    lo_half = lane < HEAD_DIM
    ones_lane = jnp.where(lane == HEAD_DIM, 1.0, 0.0)
    for c in range(seq // chunk):
        r0 = c * chunk
        dst = slice(KV_PAD + r0, KV_PAD + r0 + chunk)
        blk = jnp.right_shift(r0 + row, int(math.log2(SLC_BLOCK)))
        slab = slc_ref[0, r0:r0 + chunk, :].astype(F32)
        ks_sc[dst] = jnp.where(lo_half, slab, jnp.where(lane - HEAD_DIM == blk, 1.0, 0.0)).astype(BF16)
        vs_sc[dst] = jnp.where(lo_half, pltpu.roll(slab, HEAD_DIM, 1), ones_lane).astype(BF16)
        slab = win_ref[0, r0:r0 + chunk, :].astype(F32)
        kw_sc[dst] = jnp.where(lo_half, slab, 0.0).astype(BF16)
        vw_sc[dst] = jnp.where(lo_half, pltpu.roll(slab, HEAD_DIM, 1), ones_lane).astype(BF16)


def _nsa_kernel(seq, q_ref, vkc_ref, slc_ref, win_ref, gate_ref, bias_c_ref, bias_near_ref, bias_win_ref,
                bias_far_ref, overlap_t_ref, cmp_ones_ref, o_ref,
                ks_sc, vs_sc, kw_sc, vw_sc, s_sc, mrun_sc, acc_sc):
    tq = ATTN_TQ
    n_blk = seq // SLC_BLOCK
    n_sel = min(N_SEL, n_blk)
    rows = HPG * tq
    qt = pl.program_id(2)
    q0 = pl.multiple_of(qt * tq, tq)
    lane = lax.broadcasted_iota(jnp.int32, (tq, LANES), 1)
    lo_half = lane < HEAD_DIM
    lane_r = lax.broadcasted_iota(jnp.int32, (rows, LANES), 1)

    @pl.when(qt == 0)
    def _():
        _build_kv_scratch(seq, slc_ref, win_ref, ks_sc, vs_sc, kw_sc, vw_sc)

    q_lo, q_hi = [], []
    for pair_idx in range(HPG // 2):
        q2 = q_ref[0, :, pair_idx * LANES:(pair_idx + 1) * LANES].astype(F32)
        q2r = pltpu.roll(q2, HEAD_DIM, 1)
        q_lo += [jnp.where(lo_half, q2, 0.0), jnp.where(lo_half, q2r, 0.0)]
        q_hi += [jnp.where(lo_half, 0.0, q2r), jnp.where(lo_half, 0.0, q2)]
    q_lo = jnp.concatenate(q_lo, axis=0)
    q_hi = jnp.concatenate(q_hi, axis=0)

    vkc = vkc_ref[0, 0]
    s_c = _dot_nt(q_hi.astype(BF16), vkc) + bias_c_ref[...].reshape(rows, -1)
    m_c = jnp.max(s_c, axis=1, keepdims=True)
    e_cb = jnp.exp(s_c - m_c).astype(BF16)
    pv_c = _dot(e_cb, jnp.concatenate([vkc, cmp_ones_ref[...]], axis=1))
    row_pos = q0 + (lax.broadcasted_iota(jnp.int32, (rows, LANES), 0) & (tq - 1))
    has_key = row_pos >= CMP_BLOCK - 1
    o_cmp = jnp.where(has_key, pv_c[:, :LANES] / pv_c[:, LANES:LANES + 1], 0.0)

    imp_t4 = _dot_nt(overlap_t_ref[...], e_cb)
    imp_t = None
    for h in range(HPG):
        part = imp_t4[:, h * tq:(h + 1) * tq]
        part = part[:n_blk] / part[n_blk:n_blk + 1]
        imp_t = part if imp_t is None else imp_t + part

    blk = lax.broadcasted_iota(jnp.int32, (n_blk, tq), 0)
    pos = q0 + lax.broadcasted_iota(jnp.int32, (n_blk, tq), 1)
    cur = jnp.right_shift(pos, int(math.log2(SLC_BLOCK)))
    forced = (blk == 0) | (blk == cur) | (blk == cur - 1)
    valid = blk * SLC_BLOCK <= pos
    score = jnp.where(forced, BIG, jnp.where(valid, imp_t, -BIG))
    blk_f = blk.astype(F32)
    pen_t = jnp.full((n_blk, tq), NEG_INF, F32)
    for _ in range(n_sel):
        top = jnp.max(score, axis=0, keepdims=True)
        first = jnp.min(jnp.where(score == top, blk_f, float(n_blk)), axis=0, keepdims=True)
        hit = blk_f == first
        pen_t = jnp.where(hit, 0.0, pen_t)
        score = jnp.where(hit, -jnp.inf, score)
    pieces = [jnp.zeros((tq, HEAD_DIM), F32), jnp.transpose(pen_t)]
    if n_blk < LANES - HEAD_DIM:
        pieces.append(jnp.zeros((tq, LANES - HEAD_DIM - n_blk), F32))
    pen_lanes = jnp.concatenate(pieces, axis=1)
    q_slc = jnp.where(lane_r < HEAD_DIM, q_lo, jnp.concatenate([pen_lanes] * HPG, axis=0)).astype(BF16)

    n_far = qt // (SLC_FAR_TK // tq)
    near0 = pl.multiple_of(q0 + (KV_PAD - SLC_NEAR_BACK), tq)
    near_rows = pl.ds(near0, SLC_NEAR_BACK + tq)

    def far_rows(t):
        return pl.ds(pl.multiple_of(near0 - (t + 1) * SLC_FAR_TK, tq), SLC_FAR_TK)

    def far_cols(t):
        return pl.ds(pl.multiple_of(t * SLC_FAR_TK, SLC_FAR_TK), SLC_FAR_TK)

    mrun_sc[...] = jnp.full((rows, LANES), NEG_INF, F32)

    def pass1(t, carry):
        s = _dot_nt(q_slc, ks_sc[far_rows(t), :])
        s_sc[:, far_cols(t)] = s
        mrun_sc[...] = jnp.maximum(mrun_sc[...], jnp.maximum(s[:, :LANES], s[:, LANES:]))
        return carry

    lax.fori_loop(0, n_far, pass1, 0)
    c_far = bias_far_ref[...].reshape(rows, LANES)
    s_n = _dot_nt(q_slc, ks_sc[near_rows, :]) + bias_near_ref[...].reshape(rows, -1)
    m_s = jnp.maximum(jnp.max(mrun_sc[...], axis=1, keepdims=True) + c_far[:, :1],
                      jnp.max(s_n, axis=1, keepdims=True))
    acc_sc[...] = _dot(jnp.exp(s_n - m_s).astype(BF16), vs_sc[near_rows, :])
    mrun_sc[...] = m_s - c_far

    def pass2(t, carry):
        shift = mrun_sc[...]
        p = jnp.exp(s_sc[:, far_cols(t)] - jnp.concatenate([shift, shift], axis=1))
        acc_sc[...] = acc_sc[...] + _dot(p.astype(BF16), vs_sc[far_rows(t), :])
        return carry

    lax.fori_loop(0, n_far, pass2, 0)
    acc = acc_sc[...]
    o_slc = acc / acc[:, HEAD_DIM:HEAD_DIM + 1]

    q_win = jnp.where(lane_r == HEAD_DIM, 1.0, q_lo).astype(BF16)
    win_rows = pl.ds(q0, WINDOW + tq)
    s_w = _dot_nt(q_win, kw_sc[win_rows, :]) + bias_win_ref[...].reshape(rows, -1)
    p_w = jnp.exp(s_w - jnp.max(s_w, axis=1, keepdims=True)).astype(BF16)
    acc_w = _dot(p_w, vw_sc[win_rows, :])
    o_win = acc_w / acc_w[:, HEAD_DIM:HEAD_DIM + 1]

    gates = gate_ref[0]
    outs = []
    for h in range(HPG):
        sl = slice(h * tq, (h + 1) * tq)
        g_c, g_s, g_w = (gates[:, br * HPG + h:br * HPG + h + 1] for br in range(3))
        outs.append(g_c * o_cmp[sl] + g_s * o_slc[sl] + g_w * o_win[sl])
    for pair_idx in range(HPG // 2):
        even, odd = outs[2 * pair_idx], outs[2 * pair_idx + 1]
        merged = jnp.where(lo_half, even, pltpu.roll(odd, HEAD_DIM, 1))
        o_ref[0, :, pair_idx * LANES:(pair_idx + 1) * LANES] = merged.astype(o_ref.dtype)


def _nsa(q, vkc, slc, win, gates, bias_c, bias_near, bias_win, bias_far, overlap_t, cmp_ones):
    b, seq, _ = q.shape
    n_c = seq // CMP_STRIDE
    n_blk = seq // SLC_BLOCK
    assert n_blk <= LANES - HEAD_DIM and seq % SLC_FAR_TK == 0
    tq = ATTN_TQ
    rows = HPG * tq
    grp = HPG * HEAD_DIM
    const = lambda a: pl.BlockSpec(a.shape, lambda i, g, t: (0,) * a.ndim)
    per_group = lambda a: pl.BlockSpec((HPG,) + a.shape[1:], lambda i, g, t: (g, 0, 0))
    kv_scratch = pltpu.VMEM((KV_PAD + seq, LANES), BF16)
    return pl.pallas_call(
        functools.partial(_nsa_kernel, seq),
        grid=(b, N_KV, seq // tq),
        in_specs=[
            pl.BlockSpec((1, tq, grp), lambda i, g, t: (i, t, g)),
            pl.BlockSpec((1, 1, n_c, 2 * HEAD_DIM), lambda i, g, t: (i, g, 0, 0)),
            pl.BlockSpec((1, seq, 2 * HEAD_DIM), lambda i, g, t: (i, 0, g)),
            pl.BlockSpec((1, seq, 2 * HEAD_DIM), lambda i, g, t: (i, 0, g)),
            pl.BlockSpec((1, tq, GATE_PAD), lambda i, g, t: (i, t, g)),
            pl.BlockSpec((HPG, tq, n_c), lambda i, g, t: (g, t, 0)),
            per_group(bias_near), per_group(bias_win), per_group(bias_far),
            const(overlap_t), const(cmp_ones),
        ],
        out_specs=pl.BlockSpec((1, tq, grp), lambda i, g, t: (i, t, g)),
        out_shape=jax.ShapeDtypeStruct((b, seq, ATTN_WIDTH), BF16),
        scratch_shapes=[
            kv_scratch, kv_scratch, kv_scratch, kv_scratch,
            pltpu.VMEM((rows, seq), F32),
            pltpu.VMEM((rows, LANES), F32),
            pltpu.VMEM((rows, LANES), F32),
        ],
        compiler_params=_compiler_params(("parallel", "parallel", "arbitrary")),
        name="nsa",
    )(q, vkc, slc, win, gates, bias_c, bias_near, bias_win, bias_far, overlap_t, cmp_ones)


def _mixer_inputs(x, rel_bias, w_in, b_in, cmp_pos, cmp_w1, cmp_b1, cmp_w2, cmp_b2):
    b, seq, d = x.shape
    wp, bp = _pack_in_proj(w_in, b_in, d)
    q, slc, win, cmp, gates, u, merge = _in_proj(x.reshape(b * seq, d), wp, bp, d, 512)
    n_c = seq // CMP_STRIDE
    cmp4 = cmp.reshape(b, seq, 2 * N_KV, HEAD_DIM).transpose(0, 2, 1, 3).reshape(
        b, 2 * N_KV, n_c, CMP_STRIDE * HEAD_DIM)
    vkc = _compress(cmp4, cmp_pos, cmp_w1, cmp_b1, cmp_w2, cmp_b2)
    bias_c, bias_near, bias_win, bias_far = _attention_bias_tables(rel_bias, seq)
    overlap_t, cmp_ones = _nsa_constants(seq)
    o = _nsa(q.reshape(b, seq, -1), vkc, slc.reshape(b, seq, -1), win.reshape(b, seq, -1),
             gates.reshape(b, seq, -1), bias_c, bias_near, bias_win, bias_far, overlap_t, cmp_ones)
    return o, u, merge


S5_HALF_GROUPS = S5_GROUPS // 2
S5_HALF_IN = S5_HALF_GROUPS * S5_GROUP
S5_HALF_STATE = S5_HALF_GROUPS * S5_STATE
S5_SCAN_LANES = 512
S5_CHUNK = 64
S5_UNROLL = 8


def _s5_params(lam_re, lam_im, log_dt, b_re, b_im, c_re, c_im, nb):
    dt = jnp.exp(log_dt.astype(F32))[:, None]
    lr, li = lam_re.astype(F32), lam_im.astype(F32)
    mag = jnp.exp(lr * dt)
    ab_re, ab_im = mag * jnp.cos(li * dt), mag * jnp.sin(li * dt)
    nr, ni = ab_re - 1.0, ab_im
    den = lr * lr + li * li
    fr, fi = (nr * lr + ni * li) / den, (ni * lr - nr * li) / den
    br, bim = b_re.astype(F32), b_im.astype(F32)
    bb_re = fr[..., None] * br - fi[..., None] * bim
    bb_im = fr[..., None] * bim + fi[..., None] * br
    eye = jnp.eye(S5_HALF_GROUPS, dtype=F32)

    def in_mat(bb):
        t = bb.reshape(2, S5_HALF_GROUPS, S5_STATE, S5_GROUP)
        m = jnp.einsum('kgph,gj->kghjp', t, eye)
        return m.reshape(2, S5_HALF_IN, S5_HALF_STATE)

    def out_mat(c):
        t = c.astype(F32).reshape(2, S5_HALF_GROUPS, S5_GROUP, S5_STATE)
        m = jnp.einsum('kghp,gj->kgpjh', t, eye)
        return m.reshape(2, S5_HALF_STATE, S5_HALF_IN)

    bmat = jnp.concatenate([in_mat(bb_re), in_mat(bb_im)], axis=2).astype(BF16)
    cmat = jnp.concatenate([out_mat(c_re), -out_mat(c_im)], axis=1).astype(BF16)
    a = jnp.concatenate([ab_re.reshape(2, S5_HALF_STATE), ab_im.reshape(2, S5_HALF_STATE)], axis=1)
    a = jnp.broadcast_to(a.reshape(1, 4 * S5_HALF_STATE), (nb, 4 * S5_HALF_STATE))
    return bmat, cmat, a


def _s5_kernel(u_ref, bmat_ref, cmat_ref, a_ref, d_ref, y_ref, ut_sc, x_sc, st_sc):
    nb, t_len, _ = u_ref.shape
    half_w = 2 * S5_HALF_STATE

    @pl.when(pl.program_id(0) == 0)
    def _():
        st_sc[...] = jnp.zeros_like(st_sc)

    n_cb = ut_sc.shape[0]
    for b in range(nb):
        for cb in range(n_cb):
            ut_sc[cb, pl.ds(b, t_len, stride=nb), :] = u_ref[b, :, cb * LANES:(cb + 1) * LANES]
    ut = jnp.concatenate([ut_sc[cb] for cb in range(n_cb)], axis=1)
    ub = ut.astype(BF16)
    for k in range(2):
        x_sc[:, k * half_w:(k + 1) * half_w] = _dot(ub[:, k * S5_HALF_IN:(k + 1) * S5_HALF_IN], bmat_ref[k])

    for k in range(2):
        for j in range(S5_HALF_STATE // S5_SCAN_LANES):
            re0 = k * half_w + j * S5_SCAN_LANES
            im0 = re0 + S5_HALF_STATE
            re_sl, im_sl = pl.ds(re0, S5_SCAN_LANES), pl.ds(im0, S5_SCAN_LANES)
            ar, ai = a_ref[:, re_sl], a_ref[:, im_sl]

            def steps(c, carry):
                xr, xi = carry
                for s in range(S5_UNROLL):
                    rows = pl.ds(pl.multiple_of((c * S5_UNROLL + s) * nb, nb), nb)
                    nxr = ar * xr - ai * xi + x_sc[rows, re_sl]
                    nxi = ar * xi + ai * xr + x_sc[rows, im_sl]
                    x_sc[rows, re_sl] = nxr
                    x_sc[rows, im_sl] = nxi
                    xr, xi = nxr, nxi
                return xr, xi

            xr, xi = lax.fori_loop(0, t_len // S5_UNROLL, steps, (st_sc[:, re_sl], st_sc[:, im_sl]))
            st_sc[:, re_sl] = xr
            st_sc[:, im_sl] = xi

    xs = x_sc[...].astype(BF16)
    y = jnp.concatenate([_dot(xs[:, k * half_w:(k + 1) * half_w], cmat_ref[k]) for k in range(2)], axis=1)
    y = y + d_ref[...] * ut
    for cb in range(n_cb):
        ut_sc[cb] = y[:, cb * LANES:(cb + 1) * LANES]
    for b in range(nb):
        for cb in range(n_cb):
            y_ref[b, :, cb * LANES:(cb + 1) * LANES] = ut_sc[cb, pl.ds(b, t_len, stride=nb), :]


def _s5(u, bmat, cmat, a, d_skip):
    nb, seq, w = u.shape
    t_len = min(S5_CHUNK, seq)
    full = lambda shape: pl.BlockSpec(shape, lambda c: (0,) * len(shape))
    return pl.pallas_call(
        _s5_kernel,
        grid=(seq // t_len,),
        in_specs=[
            pl.BlockSpec((nb, t_len, w), lambda c: (0, c, 0)),
            full(bmat.shape), full(cmat.shape), full(a.shape), full((1, w)),
        ],
        out_specs=pl.BlockSpec((nb, t_len, w), lambda c: (0, c, 0)),
        out_shape=jax.ShapeDtypeStruct((nb, seq, w), F32),
        scratch_shapes=[
            pltpu.VMEM((w // LANES, t_len * nb, LANES), F32),
            pltpu.VMEM((t_len * nb, 4 * S5_HALF_STATE), F32),
            pltpu.VMEM((nb, 4 * S5_HALF_STATE), F32),
        ],
        compiler_params=_compiler_params(("arbitrary",)),
        name="s5",
    )(u, bmat, cmat, a, d_skip.reshape(1, w).astype(F32))


ROUTE_PAD = LANES
_R_E1, _R_E2, _R_W1, _R_W2, _R_RANK1, _R_RANK2 = range(6)


def _layer_norm(t, g, b):
    mu = jnp.mean(t, axis=1, keepdims=True)
    c = t - mu
    var = jnp.mean(c * c, axis=1, keepdims=True)
    return c * lax.rsqrt(var + LN_EPS) * g + b


def _post_kernel(x_ref, o_ref, y_ref, m_ref, wup_ref, wval_ref, wgate_ref, bgate_ref, wout_ref,
                 g1_ref, b1_ref, wr_ref, br_ref, h_ref, route_ref, cnt_ref, run_sc):
    tm, d = x_ref.shape

    @pl.when(pl.program_id(0) == 0)
    def _():
        run_sc[...] = jnp.zeros_like(run_sc)

    y_a = _dot(o_ref[...], wup_ref[...])
    z = _gelu_tanh(y_ref[...]).astype(BF16)
    y_b = _dot(z, wval_ref[...]) * jax.nn.sigmoid(_dot(z, wgate_ref[...]) + bgate_ref[...])
    mixed = m_ref[:, :d] * y_a + m_ref[:, d:] * y_b
    t = DN_ALPHA * x_ref[...] + _dot(mixed.astype(BF16), wout_ref[...])
    h = _layer_norm(t, g1_ref[...], b1_ref[...])
    h_ref[...] = h

    logits = _dot(h.astype(BF16), wr_ref[...]) + br_ref[...]
    lane = lax.broadcasted_iota(jnp.int32, (tm, ROUTE_PAD), 1)
    lane_f = lane.astype(F32)
    is_group = lane < N_EGROUPS

    def first_max(v):
        top = jnp.max(v, axis=1, keepdims=True)
        idx = jnp.min(jnp.where(v == top, lane_f, float(ROUTE_PAD)), axis=1, keepdims=True)
        return top, idx

    g_max, g_top = first_max(jnp.where(is_group, logits, -jnp.inf))
    p_group = 1.0 / jnp.sum(jnp.where(is_group, jnp.exp(logits - g_max), 0.0), axis=1, keepdims=True)
    grp_of_lane = jnp.right_shift(lane - N_EGROUPS, int(math.log2(EXPERTS_PER_GROUP))).astype(F32)
    in_group = (lane >= N_EGROUPS) & (lane < N_EGROUPS + N_EXPERTS) & (grp_of_lane == g_top)
    e_log = jnp.where(in_group, logits, -jnp.inf)
    v1, i1 = first_max(e_log)
    hit1 = lane_f == i1
    v2, i2 = first_max(jnp.where(hit1, -jnp.inf, e_log))
    hit2 = lane_f == i2
    e2 = jnp.exp(v2 - v1)
    w1 = p_group / (1.0 + e2)
    w2 = p_group * e2 / (1.0 + e2)

    hits = jnp.where(hit1 | hit2, 1.0, 0.0)
    row = lax.broadcasted_iota(jnp.int32, (tm, tm), 0)
    col = lax.broadcasted_iota(jnp.int32, (tm, tm), 1)
    earlier = jnp.where(col < row, 1.0, 0.0).astype(BF16)
    before = _dot(earlier, hits.astype(BF16)) + run_sc[...]
    rank1 = jnp.sum(jnp.where(hit1, before, 0.0), axis=1, keepdims=True)
    rank2 = jnp.sum(jnp.where(hit2, before, 0.0), axis=1, keepdims=True)
    run_sc[...] = run_sc[...] + jnp.sum(hits, axis=0, keepdims=True)
    cnt_ref[...] = run_sc[...]

    rec = jnp.zeros((tm, ROUTE_PAD), F32)
    for slot, val in ((_R_E1, i1 - N_EGROUPS), (_R_E2, i2 - N_EGROUPS), (_R_W1, w1), (_R_W2, w2),
                      (_R_RANK1, rank1), (_R_RANK2, rank2)):
        rec = jnp.where(lane == slot, val, rec)
    route_ref[...] = rec


def _post(x2d, o2d, y2d, merge, w_attn_up, s5_w_val, s5_w_gate, s5_b_gate, w_out, ln1_g, ln1_b,
          router_w_group, router_b_group, router_w_expert, router_b_expert, tm):
    n, d = x2d.shape
    rpad = ROUTE_PAD - N_EGROUPS - N_EXPERTS
    wr = jnp.concatenate([router_w_group, router_w_expert, jnp.zeros((d, rpad), F32)], axis=1).astype(BF16)
    br = jnp.concatenate([router_b_group, router_b_expert, jnp.zeros((rpad,), F32)]).reshape(1, -1).astype(F32)
    row = lambda w: pl.BlockSpec((tm, w), lambda i: (i, 0))
    full = lambda a: pl.BlockSpec(a.shape, lambda i: (0,) * a.ndim)
    weights = [w_attn_up.astype(BF16), s5_w_val.astype(BF16), s5_w_gate.astype(BF16),
               s5_b_gate.reshape(1, d).astype(F32), w_out.astype(BF16),
               ln1_g.reshape(1, d).astype(F32), ln1_b.reshape(1, d).astype(F32), wr, br]
    return pl.pallas_call(
        _post_kernel,
        grid=(n // tm,),
        in_specs=[row(d), row(ATTN_WIDTH), row(S5_WIDTH), row(2 * d)] + [full(w) for w in weights],
        out_specs=[row(d), row(ROUTE_PAD), pl.BlockSpec((1, ROUTE_PAD), lambda i: (0, 0))],
        out_shape=[jax.ShapeDtypeStruct((n, d), F32), jax.ShapeDtypeStruct((n, ROUTE_PAD), F32),
                   jax.ShapeDtypeStruct((1, ROUTE_PAD), F32)],
        scratch_shapes=[pltpu.VMEM((1, ROUTE_PAD), F32)],
        compiler_params=_compiler_params(("arbitrary",)),
        name="post_mixer",
    )(x2d, o2d, y2d, merge, *weights)


def _row_gather_start(idx_ref, base, n_rows, src_hbm, dst, sem):
    def body(r, c):
        pltpu.make_async_copy(src_hbm.at[idx_ref[base + r]], dst.at[r], sem).start()
        return c
    lax.fori_loop(0, n_rows, body, 0)


def _row_gather_wait(n_rows, src_hbm, dst, sem):
    def body(r, c):
        pltpu.make_async_copy(src_hbm.at[0], dst.at[r], sem).wait()
        return c
    lax.fori_loop(0, n_rows, body, 0)


def _rows_to_matrix(buf):
    return jnp.concatenate([buf[:, s, :] for s in range(buf.shape[1])], axis=1)


def _expert_kernel(row_tok_ref, blk_exp_ref, h_hbm, wg_ref, wu_ref, wd_ref, y_ref, xbuf, sem,
                   wg_sc, wu_sc, wd_sc):
    i = pl.program_id(0)
    n_blocks = pl.num_programs(0)
    slot = i % 2

    @pl.when((i == 0) | (blk_exp_ref[i] != blk_exp_ref[jnp.maximum(i - 1, 0)]))
    def _():
        wg_sc[...] = wg_ref[0].astype(BF16)
        wu_sc[...] = wu_ref[0].astype(BF16)
        wd_sc[...] = wd_ref[0].astype(BF16)

    @pl.when(i == 0)
    def _():
        _row_gather_start(row_tok_ref, 0, EXPERT_BLOCK, h_hbm, xbuf.at[0], sem.at[0])

    @pl.when(i + 1 < n_blocks)
    def _():
        _row_gather_start(row_tok_ref, (i + 1) * EXPERT_BLOCK, EXPERT_BLOCK, h_hbm,
                          xbuf.at[1 - slot], sem.at[1 - slot])

    _row_gather_wait(EXPERT_BLOCK, h_hbm, xbuf.at[slot], sem.at[slot])
    xb = _rows_to_matrix(xbuf.at[slot]).astype(BF16)
    h_gate = _dot(xb, wg_sc[...])
    h_up = _dot(xb, wu_sc[...])
    hb = (h_gate * jax.nn.sigmoid(h_gate) * h_up).astype(BF16)
    y = _dot(hb, wd_sc[...])
    for s in range(y_ref.shape[1]):
        y_ref[:, s, :] = y[:, s * LANES:(s + 1) * LANES]


def _experts(row_tok, blk_expert, h3, w_gate, w_up, w_down):
    n_blocks = blk_expert.shape[0]
    _, s_dim, _ = h3.shape
    d = s_dim * LANES
    grid_spec = pltpu.PrefetchScalarGridSpec(
        num_scalar_prefetch=2,
        grid=(n_blocks,),
        in_specs=[
            pl.BlockSpec(memory_space=pl.ANY),
            pl.BlockSpec((1, d, D_EXPERT), lambda i, rt, be: (be[i], 0, 0)),
            pl.BlockSpec((1, d, D_EXPERT), lambda i, rt, be: (be[i], 0, 0)),
            pl.BlockSpec((1, D_EXPERT, d), lambda i, rt, be: (be[i], 0, 0)),
        ],
        out_specs=pl.BlockSpec((EXPERT_BLOCK, s_dim, LANES), lambda i, rt, be: (i, 0, 0)),
        scratch_shapes=[
            pltpu.VMEM((2, EXPERT_BLOCK, s_dim, LANES), F32),
            pltpu.SemaphoreType.DMA((2,)),
            pltpu.VMEM((d, D_EXPERT), BF16),
            pltpu.VMEM((d, D_EXPERT), BF16),
            pltpu.VMEM((D_EXPERT, d), BF16),
        ],
    )
    return pl.pallas_call(
        _expert_kernel,
        grid_spec=grid_spec,
        out_shape=jax.ShapeDtypeStruct((n_blocks * EXPERT_BLOCK, s_dim, LANES), F32),
        compiler_params=_compiler_params(("arbitrary",)),
        name="experts",
    )(row_tok, blk_expert, h3, w_gate, w_up, w_down)


COMBINE_TM = 128


def _combine_kernel(dest1_ref, dest2_ref, yb_hbm, h_ref, route_ref, g2_ref, b2_ref, out_ref, ybuf, sem):
    i = pl.program_id(0)
    n_tiles = pl.num_programs(0)
    tm = h_ref.shape[0]
    slot = i % 2

    def start(tile, s):
        _row_gather_start(dest1_ref, tile * tm, tm, yb_hbm, ybuf.at[s, 0], sem.at[s, 0])
        _row_gather_start(dest2_ref, tile * tm, tm, yb_hbm, ybuf.at[s, 1], sem.at[s, 1])

    @pl.when(i == 0)
    def _():
        start(0, 0)

    @pl.when(i + 1 < n_tiles)
    def _():
        start(i + 1, 1 - slot)

    _row_gather_wait(tm, yb_hbm, ybuf.at[slot, 0], sem.at[slot, 0])
    _row_gather_wait(tm, yb_hbm, ybuf.at[slot, 1], sem.at[slot, 1])
    y1 = _rows_to_matrix(ybuf.at[slot, 0])
    y2 = _rows_to_matrix(ybuf.at[slot, 1])
    route = route_ref[...]
    w1 = route[:, _R_W1:_R_W1 + 1]
    w2 = route[:, _R_W2:_R_W2 + 1]
    t = DN_ALPHA * h_ref[...] + (y1 * w1 + y2 * w2)
    out_ref[...] = _layer_norm(t, g2_ref[...], b2_ref[...])


def _combine(dest1, dest2, yb3, h2d, route, ln2_g, ln2_b):
    n, d = h2d.shape
    tm = COMBINE_TM
    _, s_dim, _ = yb3.shape
    grid_spec = pltpu.PrefetchScalarGridSpec(
        num_scalar_prefetch=2,
        grid=(n // tm,),
        in_specs=[
            pl.BlockSpec(memory_space=pl.ANY),
            pl.BlockSpec((tm, d), lambda i, d1, d2: (i, 0)),
            pl.BlockSpec((tm, ROUTE_PAD), lambda i, d1, d2: (i, 0)),
            pl.BlockSpec((1, d), lambda i, d1, d2: (0, 0)),
            pl.BlockSpec((1, d), lambda i, d1, d2: (0, 0)),
        ],
        out_specs=pl.BlockSpec((tm, d), lambda i, d1, d2: (i, 0)),
        scratch_shapes=[
            pltpu.VMEM((2, 2, tm, s_dim, LANES), F32),
            pltpu.SemaphoreType.DMA((2, 2)),
        ],
    )
    return pl.pallas_call(
        _combine_kernel,
        grid_spec=grid_spec,
        out_shape=jax.ShapeDtypeStruct((n, d), F32),
        compiler_params=_compiler_params(("arbitrary",)),
        name="combine",
    )(dest1, dest2, yb3, h2d, route, ln2_g.reshape(1, d).astype(F32), ln2_b.reshape(1, d).astype(F32))


def _moe(h2d, route, counts_row, w_gate, w_up, w_down, ln2_g, ln2_b):
    n, d = h2d.shape
    e1 = route[:, _R_E1].astype(jnp.int32)
    e2 = route[:, _R_E2].astype(jnp.int32)
    rank1 = route[:, _R_RANK1].astype(jnp.int32)
    rank2 = route[:, _R_RANK2].astype(jnp.int32)
    counts = counts_row[0, N_EGROUPS:N_EGROUPS + N_EXPERTS].astype(jnp.int32)
    padded = (counts + EXPERT_BLOCK - 1) // EXPERT_BLOCK * EXPERT_BLOCK
    pend = jnp.cumsum(padded)
    pstart = pend - padded
    dest1 = pstart[e1] + rank1
    dest2 = pstart[e2] + rank2
    n_blocks = -(-(n * TOP_K_IN_GROUP) // EXPERT_BLOCK) + N_EXPERTS
    tok = jnp.arange(n, dtype=jnp.int32)
    row_tok = jnp.zeros((n_blocks * EXPERT_BLOCK,), jnp.int32).at[dest1].set(tok).at[dest2].set(tok)
    blk_row0 = jnp.arange(n_blocks, dtype=jnp.int32) * EXPERT_BLOCK
    blk_expert = jnp.minimum(jnp.sum(pend[None, :] <= blk_row0[:, None], axis=1), N_EXPERTS - 1).astype(jnp.int32)
    h3 = h2d.reshape(n, d // LANES, LANES)
    yb3 = _experts(row_tok, blk_expert, h3, w_gate, w_up, w_down)
    return _combine(dest1, dest2, yb3, h2d, route, ln2_g, ln2_b)


def kernel(x, rel_bias, w_in, b_in, cmp_pos, cmp_w1, cmp_b1, cmp_w2, cmp_b2, w_attn_up, s5_lambda_re, s5_lambda_im, s5_log_dt, s5_b_re, s5_b_im, s5_c_re, s5_c_im, s5_d, s5_w_val, s5_w_gate, s5_b_gate, w_out, ln1_g, ln1_b, router_w_group, router_b_group, router_w_expert, router_b_expert, exp_w_gate, exp_w_up, exp_w_down, ln2_g, ln2_b):
    b, seq, d = x.shape
    n = b * seq
    assert w_in.shape[0] == DEPTH
    l = 0
    o, u, merge = _mixer_inputs(x, rel_bias, w_in[l], b_in[l], cmp_pos[l], cmp_w1[l], cmp_b1[l],
                                cmp_w2[l], cmp_b2[l])
    bmat, cmat, a = _s5_params(s5_lambda_re[l], s5_lambda_im[l], s5_log_dt[l], s5_b_re[l], s5_b_im[l],
                               s5_c_re[l], s5_c_im[l], b)
    y_s = _s5(u.reshape(b, seq, S5_WIDTH), bmat, cmat, a, s5_d[l])
    h2d, route, counts = _post(x.reshape(n, d), o.reshape(n, ATTN_WIDTH), y_s.reshape(n, S5_WIDTH), merge,
                               w_attn_up[l], s5_w_val[l], s5_w_gate[l], s5_b_gate[l], w_out[l], ln1_g[l],
                               ln1_b[l], router_w_group[l], router_b_group[l], router_w_expert[l],
                               router_b_expert[l], 256)
    out = _moe(h2d, route, counts, exp_w_gate[l], exp_w_up[l], exp_w_down[l], ln2_g[l], ln2_b[l])
    return out.reshape(b, seq, d)
```

```python
import functools
import math

import jax
import jax.numpy as jnp
from jax import lax
from jax.experimental import pallas as pl
from jax.experimental.pallas import tpu as pltpu
from jax.experimental.pallas import tpu_sc as plsc

F32 = jnp.float32
BF16 = jnp.bfloat16

N_HEADS = 8
HEAD_DIM = 64
N_KV = 2
HPG = N_HEADS // N_KV
CMP_STRIDE = 16
CMP_BLOCK = 2 * CMP_STRIDE
CMP_HIDDEN = 128
SLC_BLOCK = 64
N_SEL = 16
WINDOW = 512
REL_BUCKETS = 32
REL_MAX_DIST = 128
S5_WIDTH = 512
S5_GROUP = 16
S5_GROUPS = S5_WIDTH // S5_GROUP
S5_STATE = 64
N_EGROUPS = 8
EXPERTS_PER_GROUP = 8
N_EXPERTS = N_EGROUPS * EXPERTS_PER_GROUP
TOP_K_IN_GROUP = 2
D_EXPERT = 256
EXPERT_BLOCK = 128
DEPTH = 1
DN_ALPHA = (2.0 * DEPTH) ** 0.25
LN_EPS = 1e-5
NEG_INF = -1e30
BIG = 1e9

ATTN_WIDTH = N_HEADS * HEAD_DIM
KV_WIDTH = N_KV * HEAD_DIM
KV_OFF = ATTN_WIDTH
NSA_GATE_OFF = KV_OFF + 6 * KV_WIDTH
S5_OFF = NSA_GATE_OFF + 3 * N_HEADS
MERGE_OFF = S5_OFF + S5_WIDTH

LANES = 128
SUBLANES = 8
VMEM_LIMIT_BYTES = 56 * 1024 * 1024

ATTN_TQ = 128
SLC_FAR_TK = 256
SLC_NEAR_BACK = ATTN_TQ
KV_PAD = WINDOW
GATE_PAD = LANES


def _gelu_tanh(x):
    c = math.sqrt(2.0 / math.pi)
    return x * (0.5 * (1.0 + jnp.tanh(c * (x + 0.044715 * (x * x * x)))))


def _dot(a, b):
    return jnp.dot(a, b, preferred_element_type=F32)


def _dot_nt(a, b):
    return lax.dot_general(a, b, (((1,), (1,)), ((), ())), preferred_element_type=F32)


def _compiler_params(semantics):
    return pltpu.CompilerParams(dimension_semantics=semantics, vmem_limit_bytes=VMEM_LIMIT_BYTES)


def _in_proj_layout(d_model):
    widths = (ATTN_WIDTH, 2 * KV_WIDTH, 2 * KV_WIDTH, 2 * KV_WIDTH, N_KV * GATE_PAD, S5_WIDTH, 2 * d_model)
    offs = [0]
    for w in widths:
        offs.append(offs[-1] + w)
    return widths, offs


def _pack_in_proj(w_in, b_in, d_model):
    def kv_cols(j):
        return KV_OFF + j * KV_WIDTH

    def pair(jk, jv):
        cols = []
        for g in range(N_KV):
            cols.append(jnp.arange(kv_cols(jk) + g * HEAD_DIM, kv_cols(jk) + (g + 1) * HEAD_DIM))
            cols.append(jnp.arange(kv_cols(jv) + g * HEAD_DIM, kv_cols(jv) + (g + 1) * HEAD_DIM))
        return jnp.concatenate(cols)

    idx = jnp.concatenate([
        jnp.arange(0, ATTN_WIDTH),
        pair(2, 3),
        pair(4, 5),
        jnp.arange(kv_cols(0), kv_cols(2)),
    ])
    idx2 = jnp.concatenate([jnp.arange(S5_OFF, S5_OFF + S5_WIDTH),
                            jnp.arange(MERGE_OFF, MERGE_OFF + 2 * d_model)])
    gpad = GATE_PAD - 3 * HPG
    w_parts, b_parts = [w_in[:, idx]], [b_in[idx]]
    for g in range(N_KV):
        cols = jnp.asarray([NSA_GATE_OFF + (g * HPG + h) * 3 + j for j in range(3) for h in range(HPG)])
        w_parts += [w_in[:, cols], jnp.zeros((d_model, gpad), F32)]
        b_parts += [b_in[cols], jnp.zeros((gpad,), F32)]
    w = jnp.concatenate(w_parts + [w_in[:, idx2]], axis=1)
    b = jnp.concatenate(b_parts + [b_in[idx2]])
    return w.astype(BF16), b.reshape(1, -1).astype(F32)


def _in_proj_kernel(offs, x_ref, w_ref, b_ref, q_ref, slc_ref, win_ref, cmp_ref, g_ref, u_ref, m_ref):
    xb = x_ref[...].astype(BF16)

    def proj(i):
        c0, c1 = offs[i], offs[i + 1]
        return _dot(xb, w_ref[:, c0:c1]) + b_ref[:, c0:c1]

    q_ref[...] = (proj(0) * (HEAD_DIM ** -0.5)).astype(BF16)
    slc_ref[...] = proj(1).astype(BF16)
    win_ref[...] = proj(2).astype(BF16)
    cmp_ref[...] = proj(3)
    g_ref[...] = jax.nn.sigmoid(proj(4))
    u_ref[...] = proj(5)
    m_ref[...] = jax.nn.sigmoid(proj(6))


def _in_proj(x2d, w_packed, b_packed, d_model, tm):
    n = x2d.shape[0]
    widths, offs = _in_proj_layout(d_model)
    ncols = offs[-1]
    dtypes = (BF16, BF16, BF16, F32, F32, F32, F32)
    return pl.pallas_call(
        functools.partial(_in_proj_kernel, tuple(offs)),
        grid=(n // tm,),
        in_specs=[
            pl.BlockSpec((tm, d_model), lambda i: (i, 0)),
            pl.BlockSpec((d_model, ncols), lambda i: (0, 0)),
            pl.BlockSpec((1, ncols), lambda i: (0, 0)),
        ],
        out_specs=[pl.BlockSpec((tm, w), lambda i: (i, 0)) for w in widths],
        out_shape=[jax.ShapeDtypeStruct((n, w), dt) for w, dt in zip(widths, dtypes)],
        compiler_params=_compiler_params(("parallel",)),
        name="in_proj",
    )(x2d, w_packed, b_packed)


def _compress_kernel(ck_ref, cv_ref, pos_ref, w1_ref, b1_ref, w2_ref, b2_ref, out_ref):
    n_c = ck_ref.shape[2]
    outs = []
    for i, c_ref in enumerate((ck_ref, cv_ref)):
        c = c_ref[0, 0]
        lo = (c + pos_ref[i, 0:1, :]).astype(BF16)
        hi = (c + pos_ref[i, 1:2, :]).astype(BF16)
        p_lo = _dot(lo, w1_ref[i, 0])
        p_hi = _dot(hi, w1_ref[i, 1])
        hid = p_lo + pltpu.roll(p_hi, n_c - 1, 0) + b1_ref[i]
        hid = _gelu_tanh(hid).astype(BF16)
        outs.append(_dot(hid, w2_ref[i]) + b2_ref[i])
    out_ref[0, 0] = jnp.concatenate(outs[::-1], axis=1).astype(BF16)


def _compress(cmp4, cmp_pos, cmp_w1, cmp_b1, cmp_w2, cmp_b2):
    b, _, n_c, cw = cmp4.shape
    half = CMP_STRIDE * HEAD_DIM
    pos = cmp_pos.reshape(2, 2, half).astype(F32)
    w1 = cmp_w1.reshape(2, 2, half, CMP_HIDDEN).astype(BF16)
    b1 = cmp_b1.reshape(2, 1, CMP_HIDDEN).astype(F32)
    w2 = cmp_w2.astype(BF16)
    b2 = cmp_b2.reshape(2, 1, HEAD_DIM).astype(F32)
    full = lambda shape: pl.BlockSpec(shape, lambda i, g: (0,) * len(shape))
    return pl.pallas_call(
        _compress_kernel,
        grid=(b, N_KV),
        in_specs=[
            pl.BlockSpec((1, 1, n_c, cw), lambda i, g: (i, g, 0, 0)),
            pl.BlockSpec((1, 1, n_c, cw), lambda i, g: (i, N_KV + g, 0, 0)),
            full((2, 2, half)),
            full((2, 2, half, CMP_HIDDEN)),
            full((2, 1, CMP_HIDDEN)),
            full((2, CMP_HIDDEN, HEAD_DIM)),
            full((2, 1, HEAD_DIM)),
        ],
        out_specs=pl.BlockSpec((1, 1, n_c, 2 * HEAD_DIM), lambda i, g: (i, g, 0, 0)),
        out_shape=jax.ShapeDtypeStruct((b, N_KV, n_c, 2 * HEAD_DIM), BF16),
        compiler_params=_compiler_params(("parallel", "parallel")),
        name="compress",
    )(cmp4, cmp4, pos, w1, b1, w2, b2)


def _t5_bucket(dist):
    n = jnp.maximum(dist, 0)
    max_exact = REL_BUCKETS // 2
    nf = jnp.maximum(n, 1).astype(F32)
    large = max_exact + (jnp.log(nf / max_exact) / math.log(REL_MAX_DIST / max_exact)
                         * (REL_BUCKETS - max_exact)).astype(jnp.int32)
    large = jnp.minimum(large, REL_BUCKETS - 1)
    return jnp.where(n < max_exact, n, large)


def _bucket_thresholds():
    buckets = _t5_bucket(jnp.arange(REL_MAX_DIST + 1))
    return jnp.sum(buckets[None, :] < jnp.arange(REL_BUCKETS)[:, None], axis=1).astype(jnp.int32)


def _bias_of_dist(dist, head, thr_ref, tbl_ref):
    bias = jnp.full(dist.shape, tbl_ref[head], F32)
    for k in range(1, REL_BUCKETS):
        bias = jnp.where(dist >= thr_ref[k], tbl_ref[k * N_HEADS + head], bias)
    return bias


BIAS_ROWS = 32


def _bias_c_kernel(thr_ref, tbl_ref, out_ref):
    _, tr, n_c = out_ref.shape
    r0 = pl.program_id(0) * tr

    def chunk(ci, carry):
        row0 = pl.multiple_of(ci * BIAS_ROWS, BIAS_ROWS)
        pos = r0 + row0 + lax.broadcasted_iota(jnp.int32, (BIAS_ROWS, n_c), 0)
        key_end = lax.broadcasted_iota(jnp.int32, (BIAS_ROWS, n_c), 1) * CMP_STRIDE + (CMP_BLOCK - 1)
        dist = pos - key_end
        for h in range(N_HEADS):
            bias = _bias_of_dist(dist, h, thr_ref, tbl_ref)
            out_ref[h, pl.ds(row0, BIAS_ROWS), :] = jnp.where(dist >= 0, bias, NEG_INF)
        return carry

    lax.fori_loop(0, tr // BIAS_ROWS, chunk, 0)


def _bias_near_kernel(thr_ref, tbl_ref, near_ref, win_ref, far_ref):
    tq = ATTN_TQ
    h = pl.program_id(0)

    def table(out_ref, lo_keys, window):
        width = out_ref.shape[2]

        def chunk(ci, carry):
            row0 = pl.multiple_of(ci * BIAS_ROWS, BIAS_ROWS)
            dist = (lo_keys + row0 + lax.broadcasted_iota(jnp.int32, (BIAS_ROWS, width), 0)
                    - lax.broadcasted_iota(jnp.int32, (BIAS_ROWS, width), 1))
            visible = (dist >= 0) & (dist < window)
            bias = jnp.full(dist.shape, tbl_ref[h], F32)
            for k in range(1, REL_BUCKETS):
                bias = jnp.where(dist >= thr_ref[k], tbl_ref[k * N_HEADS + h], bias)
            out_ref[0, pl.ds(row0, BIAS_ROWS), :] = jnp.where(visible, bias, NEG_INF)
            return carry

        lax.fori_loop(0, tq // BIAS_ROWS, chunk, 0)

    table(near_ref, SLC_NEAR_BACK, 1 << 30)
    table(win_ref, WINDOW, WINDOW)
    far_ref[0] = jnp.full(far_ref.shape[1:], tbl_ref[(REL_BUCKETS - 1) * N_HEADS + h], F32)


def _attention_bias_tables(rel_bias, seq):
    tbl = rel_bias.astype(F32).reshape(REL_BUCKETS * N_HEADS)
    thr = _bucket_thresholds()
    tq = ATTN_TQ
    n_c = seq // CMP_STRIDE
    smem = pl.BlockSpec(memory_space=pltpu.SMEM)
    tr = min(512, seq)
    bias_c = pl.pallas_call(
        _bias_c_kernel,
        grid=(seq // tr,),
        in_specs=[smem, smem],
        out_specs=pl.BlockSpec((N_HEADS, tr, n_c), lambda i: (0, i, 0)),
        out_shape=jax.ShapeDtypeStruct((N_HEADS, seq, n_c), F32),
        compiler_params=_compiler_params(("parallel",)),
        name="bias_cmp",
    )(thr, tbl)
    head_block = lambda w: pl.BlockSpec((1, tq, w), lambda h: (h, 0, 0))
    widths = (SLC_NEAR_BACK + tq, WINDOW + tq, LANES)
    bias_near, bias_win, bias_far = pl.pallas_call(
        _bias_near_kernel,
        grid=(N_HEADS,),
        in_specs=[smem, smem],
        out_specs=[head_block(w) for w in widths],
        out_shape=[jax.ShapeDtypeStruct((N_HEADS, tq, w), F32) for w in widths],
        compiler_params=_compiler_params(("parallel",)),
        name="bias_near",
    )(thr, tbl)
    return bias_c, bias_near, bias_win, bias_far


def _nsa_constants(seq):
    n_c = seq // CMP_STRIDE
    n_blk = seq // SLC_BLOCK
    cmp_start = jnp.arange(n_c) * CMP_STRIDE
    blk_start = jnp.arange(n_blk) * SLC_BLOCK
    overlap_t = ((cmp_start[None, :] <= blk_start[:, None] + SLC_BLOCK - 1)
                 & (cmp_start[None, :] + CMP_BLOCK - 1 >= blk_start[:, None]))
    overlap_t = overlap_t & (cmp_start[None, :] + CMP_BLOCK <= seq)
    ones_rows = jnp.arange(SUBLANES)[:, None] == 0
    overlap_t = jnp.concatenate([overlap_t, jnp.broadcast_to(ones_rows, (SUBLANES, n_c))], axis=0)
    cmp_ones = jnp.broadcast_to(jnp.arange(LANES)[None, :] == 0, (n_c, LANES))
    return overlap_t.astype(BF16), cmp_ones.astype(BF16)


def _build_kv_scratch(seq, slc_ref, win_ref, ks_sc, vs_sc, kw_sc, vw_sc):
    chunk = min(512, seq)
    lane_p = lax.broadcasted_iota(jnp.int32, (KV_PAD, LANES), 1)
    zeros = jnp.zeros((KV_PAD, LANES), BF16)
    ks_sc[0:KV_PAD] = jnp.where(lane_p >= HEAD_DIM, 1.0, 0.0).astype(BF16)
    kw_sc[0:KV_PAD] = jnp.where(lane_p == HEAD_DIM, NEG_INF, 0.0).astype(BF16)
    vs_sc[0:KV_PAD] = zeros
    vw_sc[0:KV_PAD] = zeros
    lane = lax.broadcasted_iota(jnp.int32, (chunk, LANES), 1)
    row = lax.broadcasted_iota(jnp.int32, (chunk, LANES), 0)
    lo_half = lane < HEAD_DIM
    ones_lane = jnp.where(lane == HEAD_DIM, 1.0, 0.0)
    for c in range(seq // chunk):
        r0 = c * chunk
        dst = slice(KV_PAD + r0, KV_PAD + r0 + chunk)
        blk = jnp.right_shift(r0 + row, int(math.log2(SLC_BLOCK)))
        slab = slc_ref[0, r0:r0 + chunk, :].astype(F32)
        ks_sc[dst] = jnp.where(lo_half, slab, jnp.where(lane - HEAD_DIM == blk, 1.0, 0.0)).astype(BF16)
        vs_sc[dst] = jnp.where(lo_half, pltpu.roll(slab, HEAD_DIM, 1), ones_lane).astype(BF16)
        slab = win_ref[0, r0:r0 + chunk, :].astype(F32)
        kw_sc[dst] = jnp.where(lo_half, slab, 0.0).astype(BF16)
        vw_sc[dst] = jnp.where(lo_half, pltpu.roll(slab, HEAD_DIM, 1), ones_lane).astype(BF16)


def _nsa_kernel(seq, q_ref, vkc_ref, slc_ref, win_ref, gate_ref, bias_c_ref, bias_near_ref, bias_win_ref,
                bias_far_ref, overlap_t_ref, cmp_ones_ref, o_ref,
                ks_sc, vs_sc, kw_sc, vw_sc, s_sc, mrun_sc, acc_sc):
    tq = ATTN_TQ
    n_blk = seq // SLC_BLOCK
    n_sel = min(N_SEL, n_blk)
    rows = HPG * tq
    qt = pl.program_id(2)
    q0 = pl.multiple_of(qt * tq, tq)
    lane = lax.broadcasted_iota(jnp.int32, (tq, LANES), 1)
    lo_half = lane < HEAD_DIM
    lane_r = lax.broadcasted_iota(jnp.int32, (rows, LANES), 1)

    @pl.when(qt == 0)
    def _():
        _build_kv_scratch(seq, slc_ref, win_ref, ks_sc, vs_sc, kw_sc, vw_sc)

    q_lo, q_hi = [], []
    for pair_idx in range(HPG // 2):
        q2 = q_ref[0, :, pair_idx * LANES:(pair_idx + 1) * LANES].astype(F32)
        q2r = pltpu.roll(q2, HEAD_DIM, 1)
        q_lo += [jnp.where(lo_half, q2, 0.0), jnp.where(lo_half, q2r, 0.0)]
        q_hi += [jnp.where(lo_half, 0.0, q2r), jnp.where(lo_half, 0.0, q2)]
    q_lo = jnp.concatenate(q_lo, axis=0)
    q_hi = jnp.concatenate(q_hi, axis=0)

    vkc = vkc_ref[0, 0]
    s_c = _dot_nt(q_hi.astype(BF16), vkc) + bias_c_ref[...].reshape(rows, -1)
    m_c = jnp.max(s_c, axis=1, keepdims=True)
    e_cb = jnp.exp(s_c - m_c).astype(BF16)
    pv_c = _dot(e_cb, jnp.concatenate([vkc, cmp_ones_ref[...]], axis=1))
    row_pos = q0 + (lax.broadcasted_iota(jnp.int32, (rows, LANES), 0) & (tq - 1))
    has_key = row_pos >= CMP_BLOCK - 1
    o_cmp = jnp.where(has_key, pv_c[:, :LANES] / pv_c[:, LANES:LANES + 1], 0.0)

    imp_t4 = _dot_nt(overlap_t_ref[...], e_cb)
    imp_t = None
    for h in range(HPG):
        part = imp_t4[:, h * tq:(h + 1) * tq]
        part = part[:n_blk] / part[n_blk:n_blk + 1]
        imp_t = part if imp_t is None else imp_t + part

    blk = lax.broadcasted_iota(jnp.int32, (n_blk, tq), 0)
    pos = q0 + lax.broadcasted_iota(jnp.int32, (n_blk, tq), 1)
    cur = jnp.right_shift(pos, int(math.log2(SLC_BLOCK)))
    forced = (blk == 0) | (blk == cur) | (blk == cur - 1)
    valid = blk * SLC_BLOCK <= pos
    score = jnp.where(forced, BIG, jnp.where(valid, imp_t, -BIG))
    blk_f = blk.astype(F32)
    pen_t = jnp.full((n_blk, tq), NEG_INF, F32)
    for _ in range(n_sel):
        top = jnp.max(score, axis=0, keepdims=True)
        first = jnp.min(jnp.where(score == top, blk_f, float(n_blk)), axis=0, keepdims=True)
        hit = blk_f == first
        pen_t = jnp.where(hit, 0.0, pen_t)
        score = jnp.where(hit, -jnp.inf, score)
    pieces = [jnp.zeros((tq, HEAD_DIM), F32), jnp.transpose(pen_t)]
    if n_blk < LANES - HEAD_DIM:
        pieces.append(jnp.zeros((tq, LANES - HEAD_DIM - n_blk), F32))
    pen_lanes = jnp.concatenate(pieces, axis=1)
    q_slc = jnp.where(lane_r < HEAD_DIM, q_lo, jnp.concatenate([pen_lanes] * HPG, axis=0)).astype(BF16)

    n_far = qt // (SLC_FAR_TK // tq)
    near0 = pl.multiple_of(q0 + (KV_PAD - SLC_NEAR_BACK), tq)
    near_rows = pl.ds(near0, SLC_NEAR_BACK + tq)

    def far_rows(t):
        return pl.ds(pl.multiple_of(near0 - (t + 1) * SLC_FAR_TK, tq), SLC_FAR_TK)

    def far_cols(t):
        return pl.ds(pl.multiple_of(t * SLC_FAR_TK, SLC_FAR_TK), SLC_FAR_TK)

    mrun_sc[...] = jnp.full((rows, LANES), NEG_INF, F32)

    def pass1(t, carry):
        s = _dot_nt(q_slc, ks_sc[far_rows(t), :])
        s_sc[:, far_cols(t)] = s
        mrun_sc[...] = jnp.maximum(mrun_sc[...], jnp.maximum(s[:, :LANES], s[:, LANES:]))
        return carry

    lax.fori_loop(0, n_far, pass1, 0)
    c_far = bias_far_ref[...].reshape(rows, LANES)
    s_n = _dot_nt(q_slc, ks_sc[near_rows, :]) + bias_near_ref[...].reshape(rows, -1)
    m_s = jnp.maximum(jnp.max(mrun_sc[...], axis=1, keepdims=True) + c_far[:, :1],
                      jnp.max(s_n, axis=1, keepdims=True))
    acc_sc[...] = _dot(jnp.exp(s_n - m_s).astype(BF16), vs_sc[near_rows, :])
    mrun_sc[...] = m_s - c_far

    def pass2(t, carry):
        shift = mrun_sc[...]
        p = jnp.exp(s_sc[:, far_cols(t)] - jnp.concatenate([shift, shift], axis=1))
        acc_sc[...] = acc_sc[...] + _dot(p.astype(BF16), vs_sc[far_rows(t), :])
        return carry

    lax.fori_loop(0, n_far, pass2, 0)
    acc = acc_sc[...]
    o_slc = acc / acc[:, HEAD_DIM:HEAD_DIM + 1]

    q_win = jnp.where(lane_r == HEAD_DIM, 1.0, q_lo).astype(BF16)
    win_rows = pl.ds(q0, WINDOW + tq)
    s_w = _dot_nt(q_win, kw_sc[win_rows, :]) + bias_win_ref[...].reshape(rows, -1)
    p_w = jnp.exp(s_w - jnp.max(s_w, axis=1, keepdims=True)).astype(BF16)
    acc_w = _dot(p_w, vw_sc[win_rows, :])
    o_win = acc_w / acc_w[:, HEAD_DIM:HEAD_DIM + 1]

    gates = gate_ref[0]
    outs = []
    for h in range(HPG):
        sl = slice(h * tq, (h + 1) * tq)
        g_c, g_s, g_w = (gates[:, br * HPG + h:br * HPG + h + 1] for br in range(3))
        outs.append(g_c * o_cmp[sl] + g_s * o_slc[sl] + g_w * o_win[sl])
    for pair_idx in range(HPG // 2):
        even, odd = outs[2 * pair_idx], outs[2 * pair_idx + 1]
        merged = jnp.where(lo_half, even, pltpu.roll(odd, HEAD_DIM, 1))
        o_ref[0, :, pair_idx * LANES:(pair_idx + 1) * LANES] = merged.astype(o_ref.dtype)


def _nsa(q, vkc, slc, win, gates, bias_c, bias_near, bias_win, bias_far, overlap_t, cmp_ones):
    b, seq, _ = q.shape
    n_c = seq // CMP_STRIDE
    n_blk = seq // SLC_BLOCK
    assert n_blk <= LANES - HEAD_DIM and seq % SLC_FAR_TK == 0
    tq = ATTN_TQ
    rows = HPG * tq
    grp = HPG * HEAD_DIM
    const = lambda a: pl.BlockSpec(a.shape, lambda i, g, t: (0,) * a.ndim)
    per_group = lambda a: pl.BlockSpec((HPG,) + a.shape[1:], lambda i, g, t: (g, 0, 0))
    kv_scratch = pltpu.VMEM((KV_PAD + seq, LANES), BF16)
    return pl.pallas_call(
        functools.partial(_nsa_kernel, seq),
        grid=(b, N_KV, seq // tq),
        in_specs=[
            pl.BlockSpec((1, tq, grp), lambda i, g, t: (i, t, g)),
            pl.BlockSpec((1, 1, n_c, 2 * HEAD_DIM), lambda i, g, t: (i, g, 0, 0)),
            pl.BlockSpec((1, seq, 2 * HEAD_DIM), lambda i, g, t: (i, 0, g)),
            pl.BlockSpec((1, seq, 2 * HEAD_DIM), lambda i, g, t: (i, 0, g)),
            pl.BlockSpec((1, tq, GATE_PAD), lambda i, g, t: (i, t, g)),
            pl.BlockSpec((HPG, tq, n_c), lambda i, g, t: (g, t, 0)),
            per_group(bias_near), per_group(bias_win), per_group(bias_far),
            const(overlap_t), const(cmp_ones),
        ],
        out_specs=pl.BlockSpec((1, tq, grp), lambda i, g, t: (i, t, g)),
        out_shape=jax.ShapeDtypeStruct((b, seq, ATTN_WIDTH), BF16),
        scratch_shapes=[
            kv_scratch, kv_scratch, kv_scratch, kv_scratch,
            pltpu.VMEM((rows, seq), F32),
            pltpu.VMEM((rows, LANES), F32),
            pltpu.VMEM((rows, LANES), F32),
        ],
        compiler_params=_compiler_params(("parallel", "parallel", "arbitrary")),
        name="nsa",
    )(q, vkc, slc, win, gates, bias_c, bias_near, bias_win, bias_far, overlap_t, cmp_ones)


def _mixer_inputs(x, rel_bias, w_in, b_in, cmp_pos, cmp_w1, cmp_b1, cmp_w2, cmp_b2):
    b, seq, d = x.shape
    wp, bp = _pack_in_proj(w_in, b_in, d)
    q, slc, win, cmp, gates, u, merge = _in_proj(x.reshape(b * seq, d), wp, bp, d, 512)
    n_c = seq // CMP_STRIDE
    cmp4 = cmp.reshape(b, seq, 2 * N_KV, HEAD_DIM).transpose(0, 2, 1, 3).reshape(
        b, 2 * N_KV, n_c, CMP_STRIDE * HEAD_DIM)
    vkc = _compress(cmp4, cmp_pos, cmp_w1, cmp_b1, cmp_w2, cmp_b2)
    bias_c, bias_near, bias_win, bias_far = _attention_bias_tables(rel_bias, seq)
    overlap_t, cmp_ones = _nsa_constants(seq)
    o = _nsa(q.reshape(b, seq, -1), vkc, slc.reshape(b, seq, -1), win.reshape(b, seq, -1),
             gates.reshape(b, seq, -1), bias_c, bias_near, bias_win, bias_far, overlap_t, cmp_ones)
    return o, u, merge


S5_HALF_GROUPS = S5_GROUPS // 2
S5_HALF_IN = S5_HALF_GROUPS * S5_GROUP
S5_HALF_STATE = S5_HALF_GROUPS * S5_STATE
S5_SCAN_LANES = 512
S5_CHUNK = 64
S5_UNROLL = 8


def _s5_params(lam_re, lam_im, log_dt, b_re, b_im, c_re, c_im, nb):
    dt = jnp.exp(log_dt.astype(F32))[:, None]
    lr, li = lam_re.astype(F32), lam_im.astype(F32)
    mag = jnp.exp(lr * dt)
    ab_re, ab_im = mag * jnp.cos(li * dt), mag * jnp.sin(li * dt)
    nr, ni = ab_re - 1.0, ab_im
    den = lr * lr + li * li
    fr, fi = (nr * lr + ni * li) / den, (ni * lr - nr * li) / den
    br, bim = b_re.astype(F32), b_im.astype(F32)
    bb_re = fr[..., None] * br - fi[..., None] * bim
    bb_im = fr[..., None] * bim + fi[..., None] * br
    eye = jnp.eye(S5_HALF_GROUPS, dtype=F32)

    def in_mat(bb):
        t = bb.reshape(2, S5_HALF_GROUPS, S5_STATE, S5_GROUP)
        m = jnp.einsum('kgph,gj->kghjp', t, eye)
        return m.reshape(2, S5_HALF_IN, S5_HALF_STATE)

    def out_mat(c):
        t = c.astype(F32).reshape(2, S5_HALF_GROUPS, S5_GROUP, S5_STATE)
        m = jnp.einsum('kghp,gj->kgpjh', t, eye)
        return m.reshape(2, S5_HALF_STATE, S5_HALF_IN)

    bmat = jnp.concatenate([in_mat(bb_re), in_mat(bb_im)], axis=2).astype(BF16)
    cmat = jnp.concatenate([out_mat(c_re), -out_mat(c_im)], axis=1).astype(BF16)
    a = jnp.concatenate([ab_re.reshape(2, S5_HALF_STATE), ab_im.reshape(2, S5_HALF_STATE)], axis=1)
    a = jnp.broadcast_to(a.reshape(1, 4 * S5_HALF_STATE), (nb, 4 * S5_HALF_STATE))
    return bmat, cmat, a


def _s5_kernel(u_ref, bmat_ref, cmat_ref, a_ref, d_ref, y_ref, ut_sc, x_sc, st_sc):
    nb, t_len, _ = u_ref.shape
    half_w = 2 * S5_HALF_STATE

    @pl.when(pl.program_id(0) == 0)
    def _():
        st_sc[...] = jnp.zeros_like(st_sc)

    n_cb = ut_sc.shape[0]
    for b in range(nb):
        for cb in range(n_cb):
            ut_sc[cb, pl.ds(b, t_len, stride=nb), :] = u_ref[b, :, cb * LANES:(cb + 1) * LANES]
    ut = jnp.concatenate([ut_sc[cb] for cb in range(n_cb)], axis=1)
    ub = ut.astype(BF16)
    for k in range(2):
        x_sc[:, k * half_w:(k + 1) * half_w] = _dot(ub[:, k * S5_HALF_IN:(k + 1) * S5_HALF_IN], bmat_ref[k])

    for k in range(2):
        for j in range(S5_HALF_STATE // S5_SCAN_LANES):
            re0 = k * half_w + j * S5_SCAN_LANES
            im0 = re0 + S5_HALF_STATE
            re_sl, im_sl = pl.ds(re0, S5_SCAN_LANES), pl.ds(im0, S5_SCAN_LANES)
            ar, ai = a_ref[:, re_sl], a_ref[:, im_sl]

            def steps(c, carry):
                xr, xi = carry
                for s in range(S5_UNROLL):
                    rows = pl.ds(pl.multiple_of((c * S5_UNROLL + s) * nb, nb), nb)
                    nxr = ar * xr - ai * xi + x_sc[rows, re_sl]
                    nxi = ar * xi + ai * xr + x_sc[rows, im_sl]
                    x_sc[rows, re_sl] = nxr
                    x_sc[rows, im_sl] = nxi
                    xr, xi = nxr, nxi
                return xr, xi

            xr, xi = lax.fori_loop(0, t_len // S5_UNROLL, steps, (st_sc[:, re_sl], st_sc[:, im_sl]))
            st_sc[:, re_sl] = xr
            st_sc[:, im_sl] = xi

    xs = x_sc[...].astype(BF16)
    y = jnp.concatenate([_dot(xs[:, k * half_w:(k + 1) * half_w], cmat_ref[k]) for k in range(2)], axis=1)
    y = y + d_ref[...] * ut
    for cb in range(n_cb):
        ut_sc[cb] = y[:, cb * LANES:(cb + 1) * LANES]
    for b in range(nb):
        for cb in range(n_cb):
            y_ref[b, :, cb * LANES:(cb + 1) * LANES] = ut_sc[cb, pl.ds(b, t_len, stride=nb), :]


def _s5(u, bmat, cmat, a, d_skip):
    nb, seq, w = u.shape
    t_len = min(S5_CHUNK, seq)
    full = lambda shape: pl.BlockSpec(shape, lambda c: (0,) * len(shape))
    return pl.pallas_call(
        _s5_kernel,
        grid=(seq // t_len,),
        in_specs=[
            pl.BlockSpec((nb, t_len, w), lambda c: (0, c, 0)),
            full(bmat.shape), full(cmat.shape), full(a.shape), full((1, w)),
        ],
        out_specs=pl.BlockSpec((nb, t_len, w), lambda c: (0, c, 0)),
        out_shape=jax.ShapeDtypeStruct((nb, seq, w), F32),
        scratch_shapes=[
            pltpu.VMEM((w // LANES, t_len * nb, LANES), F32),
            pltpu.VMEM((t_len * nb, 4 * S5_HALF_STATE), F32),
            pltpu.VMEM((nb, 4 * S5_HALF_STATE), F32),
        ],
        compiler_params=_compiler_params(("arbitrary",)),
        name="s5",
    )(u, bmat, cmat, a, d_skip.reshape(1, w).astype(F32))


ROUTE_PAD = LANES
_R_E1, _R_E2, _R_W1, _R_W2, _R_RANK1, _R_RANK2 = range(6)


def _layer_norm(t, g, b):
    mu = jnp.mean(t, axis=1, keepdims=True)
    c = t - mu
    var = jnp.mean(c * c, axis=1, keepdims=True)
    return c * lax.rsqrt(var + LN_EPS) * g + b


def _post_kernel(x_ref, o_ref, y_ref, m_ref, wup_ref, wval_ref, wgate_ref, bgate_ref, wout_ref,
                 g1_ref, b1_ref, wr_ref, br_ref, h_ref, route_ref, cnt_ref, run_sc):
    tm, d = x_ref.shape

    @pl.when(pl.program_id(0) == 0)
    def _():
        run_sc[...] = jnp.zeros_like(run_sc)

    y_a = _dot(o_ref[...], wup_ref[...])
    z = _gelu_tanh(y_ref[...]).astype(BF16)
    y_b = _dot(z, wval_ref[...]) * jax.nn.sigmoid(_dot(z, wgate_ref[...]) + bgate_ref[...])
    mixed = m_ref[:, :d] * y_a + m_ref[:, d:] * y_b
    t = DN_ALPHA * x_ref[...] + _dot(mixed.astype(BF16), wout_ref[...])
    h = _layer_norm(t, g1_ref[...], b1_ref[...])
    h_ref[...] = h

    logits = _dot(h.astype(BF16), wr_ref[...]) + br_ref[...]
    lane = lax.broadcasted_iota(jnp.int32, (tm, ROUTE_PAD), 1)
    lane_f = lane.astype(F32)
    is_group = lane < N_EGROUPS

    def first_max(v):
        top = jnp.max(v, axis=1, keepdims=True)
        idx = jnp.min(jnp.where(v == top, lane_f, float(ROUTE_PAD)), axis=1, keepdims=True)
        return top, idx

    g_max, g_top = first_max(jnp.where(is_group, logits, -jnp.inf))
    p_group = 1.0 / jnp.sum(jnp.where(is_group, jnp.exp(logits - g_max), 0.0), axis=1, keepdims=True)
    grp_of_lane = jnp.right_shift(lane - N_EGROUPS, int(math.log2(EXPERTS_PER_GROUP))).astype(F32)
    in_group = (lane >= N_EGROUPS) & (lane < N_EGROUPS + N_EXPERTS) & (grp_of_lane == g_top)
    e_log = jnp.where(in_group, logits, -jnp.inf)
    v1, i1 = first_max(e_log)
    hit1 = lane_f == i1
    v2, i2 = first_max(jnp.where(hit1, -jnp.inf, e_log))
    hit2 = lane_f == i2
    e2 = jnp.exp(v2 - v1)
    w1 = p_group / (1.0 + e2)
    w2 = p_group * e2 / (1.0 + e2)

    hits = jnp.where(hit1 | hit2, 1.0, 0.0)
    row = lax.broadcasted_iota(jnp.int32, (tm, tm), 0)
    col = lax.broadcasted_iota(jnp.int32, (tm, tm), 1)
    earlier = jnp.where(col < row, 1.0, 0.0).astype(BF16)
    before = _dot(earlier, hits.astype(BF16)) + run_sc[...]
    rank1 = jnp.sum(jnp.where(hit1, before, 0.0), axis=1, keepdims=True)
    rank2 = jnp.sum(jnp.where(hit2, before, 0.0), axis=1, keepdims=True)
    run_sc[...] = run_sc[...] + jnp.sum(hits, axis=0, keepdims=True)
    cnt_ref[...] = run_sc[...]

    rec = jnp.zeros((tm, ROUTE_PAD), F32)
    for slot, val in ((_R_E1, i1 - N_EGROUPS), (_R_E2, i2 - N_EGROUPS), (_R_W1, w1), (_R_W2, w2),
                      (_R_RANK1, rank1), (_R_RANK2, rank2)):
        rec = jnp.where(lane == slot, val, rec)
    route_ref[...] = rec


def _post(x2d, o2d, y2d, merge, w_attn_up, s5_w_val, s5_w_gate, s5_b_gate, w_out, ln1_g, ln1_b,
          router_w_group, router_b_group, router_w_expert, router_b_expert, tm):
    n, d = x2d.shape
    rpad = ROUTE_PAD - N_EGROUPS - N_EXPERTS
    wr = jnp.concatenate([router_w_group, router_w_expert, jnp.zeros((d, rpad), F32)], axis=1).astype(BF16)
    br = jnp.concatenate([router_b_group, router_b_expert, jnp.zeros((rpad,), F32)]).reshape(1, -1).astype(F32)
    row = lambda w: pl.BlockSpec((tm, w), lambda i: (i, 0))
    full = lambda a: pl.BlockSpec(a.shape, lambda i: (0,) * a.ndim)
    weights = [w_attn_up.astype(BF16), s5_w_val.astype(BF16), s5_w_gate.astype(BF16),
               s5_b_gate.reshape(1, d).astype(F32), w_out.astype(BF16),
               ln1_g.reshape(1, d).astype(F32), ln1_b.reshape(1, d).astype(F32), wr, br]
    return pl.pallas_call(
        _post_kernel,
        grid=(n // tm,),
        in_specs=[row(d), row(ATTN_WIDTH), row(S5_WIDTH), row(2 * d)] + [full(w) for w in weights],
        out_specs=[row(d), row(ROUTE_PAD), pl.BlockSpec((1, ROUTE_PAD), lambda i: (0, 0))],
        out_shape=[jax.ShapeDtypeStruct((n, d), F32), jax.ShapeDtypeStruct((n, ROUTE_PAD), F32),
                   jax.ShapeDtypeStruct((1, ROUTE_PAD), F32)],
        scratch_shapes=[pltpu.VMEM((1, ROUTE_PAD), F32)],
        compiler_params=_compiler_params(("arbitrary",)),
        name="post_mixer",
    )(x2d, o2d, y2d, merge, *weights)


def _plan_kernel(route_ref, cnt_ref, dest_ref):
    tm = route_ref.shape[0]
    lane8 = lax.broadcasted_iota(jnp.int32, (SUBLANES, ROUTE_PAD), 1)
    counts = jnp.broadcast_to(cnt_ref[...], (SUBLANES, ROUTE_PAD)).astype(jnp.int32)
    shift = int(math.log2(EXPERT_BLOCK))
    padded = jnp.left_shift(jnp.right_shift(counts + (EXPERT_BLOCK - 1), shift), shift)
    incl = padded
    step = 1
    while step < ROUTE_PAD:
        incl = incl + jnp.where(lane8 >= step, pltpu.roll(incl, step, 1), 0)
        step *= 2
    pstart = (incl - padded)[0:1].astype(F32)
    route = route_ref[...]
    expert_of_lane = (lax.broadcasted_iota(jnp.int32, (tm, ROUTE_PAD), 1) - N_EGROUPS).astype(F32)
    lane = lax.broadcasted_iota(jnp.int32, (tm, ROUTE_PAD), 1)

    def dest(e_slot, rank_slot):
        hit = expert_of_lane == route[:, e_slot:e_slot + 1]
        return jnp.sum(jnp.where(hit, pstart, 0.0), axis=1, keepdims=True) + route[:, rank_slot:rank_slot + 1]

    d1 = dest(_R_E1, _R_RANK1)
    d2 = dest(_R_E2, _R_RANK2)
    dest_ref[...] = jnp.where(lane == 0, d1, jnp.where(lane == 1, d2, 0.0)).astype(jnp.int32)


def _plan(route, counts_row):
    n = route.shape[0]
    tm = min(1024, n)
    return pl.pallas_call(
        _plan_kernel,
        grid=(n // tm,),
        in_specs=[pl.BlockSpec((tm, ROUTE_PAD), lambda i: (i, 0)),
                  pl.BlockSpec((1, ROUTE_PAD), lambda i: (0, 0))],
        out_specs=pl.BlockSpec((tm, ROUTE_PAD), lambda i: (i, 0)),
        out_shape=jax.ShapeDtypeStruct((n, ROUTE_PAD), jnp.int32),
        compiler_params=_compiler_params(("parallel",)),
        name="moe_plan",
    )(route, counts_row)


SC_GATHER_ROWS = 32


def _sc_row_gather(table, idx):
    n_idx = idx.shape[0]
    d = table.shape[1]
    info = plsc.get_sparse_core_info()
    n_workers = info.num_cores * info.num_subcores
    per_worker = n_idx // n_workers
    assert n_idx % (n_workers * SC_GATHER_ROWS) == 0
    mesh = plsc.VectorSubcoreMesh(core_axis_name="c", subcore_axis_name="s")

    @functools.partial(
        pl.kernel, mesh=mesh,
        out_type=jax.ShapeDtypeStruct((n_idx, d), table.dtype),
        scratch_types=[
            pltpu.VMEM((SC_GATHER_ROWS,), jnp.int32),
            pltpu.VMEM((SC_GATHER_ROWS, d), table.dtype),
            pltpu.SemaphoreType.DMA,
        ],
    )
    def gather(table_hbm, idx_hbm, out_hbm, idx_v, rows_v, sem):
        worker = lax.axis_index("s") * info.num_cores + lax.axis_index("c")
        base = worker * per_worker

        @pl.loop(0, per_worker // SC_GATHER_ROWS)
        def _(j):
            off = base + j * SC_GATHER_ROWS
            pltpu.sync_copy(idx_hbm.at[pl.ds(off, SC_GATHER_ROWS)], idx_v)
            pltpu.async_copy(table_hbm.at[idx_v], rows_v, sem).wait()
            pltpu.sync_copy(rows_v, out_hbm.at[pl.ds(off, SC_GATHER_ROWS)])

    return gather(table, idx)


def _expert_kernel(blk_exp_ref, x_ref, wg_ref, wu_ref, wd_ref, y_ref, wg_sc, wu_sc, wd_sc):
    i = pl.program_id(0)

    @pl.when((i == 0) | (blk_exp_ref[i] != blk_exp_ref[jnp.maximum(i - 1, 0)]))
    def _():
        wg_sc[...] = wg_ref[0].astype(BF16)
        wu_sc[...] = wu_ref[0].astype(BF16)
        wd_sc[...] = wd_ref[0].astype(BF16)

    xb = x_ref[...].astype(BF16)
    h_gate = _dot(xb, wg_sc[...])
    h_up = _dot(xb, wu_sc[...])
    hb = (h_gate * jax.nn.sigmoid(h_gate) * h_up).astype(BF16)
    y_ref[...] = _dot(hb, wd_sc[...])


def _experts(blk_expert, xs, w_gate, w_up, w_down):
    n_blocks = blk_expert.shape[0]
    d = xs.shape[1]
    grid_spec = pltpu.PrefetchScalarGridSpec(
        num_scalar_prefetch=1,
        grid=(n_blocks,),
        in_specs=[
            pl.BlockSpec((EXPERT_BLOCK, d), lambda i, be: (i, 0)),
            pl.BlockSpec((1, d, D_EXPERT), lambda i, be: (be[i], 0, 0)),
            pl.BlockSpec((1, d, D_EXPERT), lambda i, be: (be[i], 0, 0)),
            pl.BlockSpec((1, D_EXPERT, d), lambda i, be: (be[i], 0, 0)),
        ],
        out_specs=pl.BlockSpec((EXPERT_BLOCK, d), lambda i, be: (i, 0)),
        scratch_shapes=[
            pltpu.VMEM((d, D_EXPERT), BF16),
            pltpu.VMEM((d, D_EXPERT), BF16),
            pltpu.VMEM((D_EXPERT, d), BF16),
        ],
    )
    return pl.pallas_call(
        _expert_kernel,
        grid_spec=grid_spec,
        out_shape=jax.ShapeDtypeStruct((n_blocks * EXPERT_BLOCK, d), F32),
        compiler_params=_compiler_params(("arbitrary",)),
        name="experts",
    )(blk_expert, xs, w_gate, w_up, w_down)


COMBINE_TM = 512


def _combine_kernel(h_ref, yg_ref, route_ref, g2_ref, b2_ref, out_ref):
    d = h_ref.shape[1]
    route = route_ref[...]
    w1 = route[:, _R_W1:_R_W1 + 1]
    w2 = route[:, _R_W2:_R_W2 + 1]
    t = DN_ALPHA * h_ref[...] + (yg_ref[:, :d] * w1 + yg_ref[:, d:] * w2)
    out_ref[...] = _layer_norm(t, g2_ref[...], b2_ref[...])


def _combine(yg, h2d, route, ln2_g, ln2_b):
    n, d = h2d.shape
    tm = min(COMBINE_TM, n)
    row = lambda w: pl.BlockSpec((tm, w), lambda i: (i, 0))
    vec = pl.BlockSpec((1, d), lambda i: (0, 0))
    return pl.pallas_call(
        _combine_kernel,
        grid=(n // tm,),
        in_specs=[row(d), row(TOP_K_IN_GROUP * d), row(ROUTE_PAD), vec, vec],
        out_specs=row(d),
        out_shape=jax.ShapeDtypeStruct((n, d), F32),
        compiler_params=_compiler_params(("parallel",)),
        name="combine",
    )(h2d, yg, route, ln2_g.reshape(1, d).astype(F32), ln2_b.reshape(1, d).astype(F32))


def _moe(h2d, route, counts_row, w_gate, w_up, w_down, ln2_g, ln2_b):
    n, d = h2d.shape
    dest = _plan(route, counts_row)[:, :TOP_K_IN_GROUP]
    counts = counts_row[0, N_EGROUPS:N_EGROUPS + N_EXPERTS].astype(jnp.int32)
    padded = (counts + EXPERT_BLOCK - 1) // EXPERT_BLOCK * EXPERT_BLOCK
    pend = jnp.cumsum(padded)
    n_blocks = -(-(n * TOP_K_IN_GROUP) // EXPERT_BLOCK) + N_EXPERTS
    blk_row0 = jnp.arange(n_blocks, dtype=jnp.int32) * EXPERT_BLOCK
    blk_expert = jnp.minimum(jnp.sum(pend[None, :] <= blk_row0[:, None], axis=1), N_EXPERTS - 1).astype(jnp.int32)
    tok = jnp.broadcast_to(jnp.arange(n, dtype=jnp.int32)[:, None], dest.shape)
    dest_flat = dest.reshape(-1)
    row_tok = jnp.zeros((n_blocks * EXPERT_BLOCK,), jnp.int32).at[dest_flat].set(
        tok.reshape(-1), unique_indices=True)
    xs = _sc_row_gather(h2d, row_tok)
    yb = _experts(blk_expert, xs, w_gate, w_up, w_down)
    yg = _sc_row_gather(yb, dest_flat).reshape(n, TOP_K_IN_GROUP * d)
    return _combine(yg, h2d, route, ln2_g, ln2_b)


def kernel(x, rel_bias, w_in, b_in, cmp_pos, cmp_w1, cmp_b1, cmp_w2, cmp_b2, w_attn_up, s5_lambda_re, s5_lambda_im, s5_log_dt, s5_b_re, s5_b_im, s5_c_re, s5_c_im, s5_d, s5_w_val, s5_w_gate, s5_b_gate, w_out, ln1_g, ln1_b, router_w_group, router_b_group, router_w_expert, router_b_expert, exp_w_gate, exp_w_up, exp_w_down, ln2_g, ln2_b):
    b, seq, d = x.shape
    n = b * seq
    assert w_in.shape[0] == DEPTH
    l = 0
    o, u, merge = _mixer_inputs(x, rel_bias, w_in[l], b_in[l], cmp_pos[l], cmp_w1[l], cmp_b1[l],
                                cmp_w2[l], cmp_b2[l])
    bmat, cmat, a = _s5_params(s5_lambda_re[l], s5_lambda_im[l], s5_log_dt[l], s5_b_re[l], s5_b_im[l],
                               s5_c_re[l], s5_c_im[l], b)
    y_s = _s5(u.reshape(b, seq, S5_WIDTH), bmat, cmat, a, s5_d[l])
    h2d, route, counts = _post(x.reshape(n, d), o.reshape(n, ATTN_WIDTH), y_s.reshape(n, S5_WIDTH), merge,
                               w_attn_up[l], s5_w_val[l], s5_w_gate[l], s5_b_gate[l], w_out[l], ln1_g[l],
                               ln1_b[l], router_w_group[l], router_b_group[l], router_w_expert[l],
                               router_b_expert[l], 256)
    out = _moe(h2d, route, counts, exp_w_gate[l], exp_w_up[l], exp_w_down[l], ln2_g[l], ln2_b[l])
    return out.reshape(b, seq, d)
```

```python
import functools
import math

import jax
import jax.numpy as jnp
from jax import lax
from jax.experimental import pallas as pl
from jax.experimental.pallas import tpu as pltpu
from jax.experimental.pallas import tpu_sc as plsc

F32 = jnp.float32
BF16 = jnp.bfloat16

N_HEADS = 8
HEAD_DIM = 64
N_KV = 2
HPG = N_HEADS // N_KV
CMP_STRIDE = 16
CMP_BLOCK = 2 * CMP_STRIDE
CMP_HIDDEN = 128
SLC_BLOCK = 64
N_SEL = 16
WINDOW = 512
REL_BUCKETS = 32
REL_MAX_DIST = 128
S5_WIDTH = 512
S5_GROUP = 16
S5_GROUPS = S5_WIDTH // S5_GROUP
S5_STATE = 64
N_EGROUPS = 8
EXPERTS_PER_GROUP = 8
N_EXPERTS = N_EGROUPS * EXPERTS_PER_GROUP
TOP_K_IN_GROUP = 2
D_EXPERT = 256
EXPERT_BLOCK = 128
DEPTH = 1
DN_ALPHA = (2.0 * DEPTH) ** 0.25
LN_EPS = 1e-5
NEG_INF = -1e30
BIG = 1e9

ATTN_WIDTH = N_HEADS * HEAD_DIM
KV_WIDTH = N_KV * HEAD_DIM
KV_OFF = ATTN_WIDTH
NSA_GATE_OFF = KV_OFF + 6 * KV_WIDTH
S5_OFF = NSA_GATE_OFF + 3 * N_HEADS
MERGE_OFF = S5_OFF + S5_WIDTH

LANES = 128
SUBLANES = 8
VMEM_LIMIT_BYTES = 56 * 1024 * 1024

ATTN_TQ = 128
SLC_FAR_TK = 256
SLC_NEAR_BACK = ATTN_TQ
KV_PAD = WINDOW
GATE_PAD = LANES


def _gelu_tanh(x):
    c = math.sqrt(2.0 / math.pi)
    return x * (0.5 * (1.0 + jnp.tanh(c * (x + 0.044715 * (x * x * x)))))


def _dot(a, b):
    return jnp.dot(a, b, preferred_element_type=F32)


def _dot_nt(a, b):
    return lax.dot_general(a, b, (((1,), (1,)), ((), ())), preferred_element_type=F32)


def _compiler_params(semantics):
    return pltpu.CompilerParams(dimension_semantics=semantics, vmem_limit_bytes=VMEM_LIMIT_BYTES)


def _in_proj_layout(d_model):
    widths = (ATTN_WIDTH, 2 * KV_WIDTH, 2 * KV_WIDTH, 2 * KV_WIDTH, N_KV * GATE_PAD, S5_WIDTH, 2 * d_model)
    offs = [0]
    for w in widths:
        offs.append(offs[-1] + w)
    return widths, offs


def _pack_in_proj(w_in, b_in, d_model):
    def kv_cols(j):
        return KV_OFF + j * KV_WIDTH

    def pair(jk, jv):
        cols = []
        for g in range(N_KV):
            cols.append(jnp.arange(kv_cols(jk) + g * HEAD_DIM, kv_cols(jk) + (g + 1) * HEAD_DIM))
            cols.append(jnp.arange(kv_cols(jv) + g * HEAD_DIM, kv_cols(jv) + (g + 1) * HEAD_DIM))
        return jnp.concatenate(cols)

    idx = jnp.concatenate([
        jnp.arange(0, ATTN_WIDTH),
        pair(2, 3),
        pair(4, 5),
        jnp.arange(kv_cols(0), kv_cols(2)),
    ])
    idx2 = jnp.concatenate([jnp.arange(S5_OFF, S5_OFF + S5_WIDTH),
                            jnp.arange(MERGE_OFF, MERGE_OFF + 2 * d_model)])
    gpad = GATE_PAD - 3 * HPG
    w_parts, b_parts = [w_in[:, idx]], [b_in[idx]]
    for g in range(N_KV):
        cols = jnp.asarray([NSA_GATE_OFF + (g * HPG + h) * 3 + j for j in range(3) for h in range(HPG)])
        w_parts += [w_in[:, cols], jnp.zeros((d_model, gpad), F32)]
        b_parts += [b_in[cols], jnp.zeros((gpad,), F32)]
    w = jnp.concatenate(w_parts + [w_in[:, idx2]], axis=1)
    b = jnp.concatenate(b_parts + [b_in[idx2]])
    return w.astype(BF16), b.reshape(1, -1).astype(F32)


def _in_proj_kernel(offs, x_ref, w_ref, b_ref, q_ref, slc_ref, win_ref, cmp_ref, g_ref, u_ref, m_ref):
    xb = x_ref[...].astype(BF16)

    def proj(i):
        c0, c1 = offs[i], offs[i + 1]
        return _dot(xb, w_ref[:, c0:c1]) + b_ref[:, c0:c1]

    q_ref[...] = (proj(0) * (HEAD_DIM ** -0.5)).astype(BF16)
    slc_ref[...] = proj(1).astype(BF16)
    win_ref[...] = proj(2).astype(BF16)
    cmp_ref[...] = proj(3)
    g_ref[...] = jax.nn.sigmoid(proj(4))
    u_ref[...] = proj(5)
    m_ref[...] = jax.nn.sigmoid(proj(6))


def _in_proj(x2d, w_packed, b_packed, d_model, tm):
    n = x2d.shape[0]
    widths, offs = _in_proj_layout(d_model)
    ncols = offs[-1]
    dtypes = (BF16, BF16, BF16, F32, F32, F32, F32)
    return pl.pallas_call(
        functools.partial(_in_proj_kernel, tuple(offs)),
        grid=(n // tm,),
        in_specs=[
            pl.BlockSpec((tm, d_model), lambda i: (i, 0)),
            pl.BlockSpec((d_model, ncols), lambda i: (0, 0)),
            pl.BlockSpec((1, ncols), lambda i: (0, 0)),
        ],
        out_specs=[pl.BlockSpec((tm, w), lambda i: (i, 0)) for w in widths],
        out_shape=[jax.ShapeDtypeStruct((n, w), dt) for w, dt in zip(widths, dtypes)],
        compiler_params=_compiler_params(("parallel",)),
        name="in_proj",
    )(x2d, w_packed, b_packed)


def _compress_kernel(ck_ref, cv_ref, pos_ref, w1_ref, b1_ref, w2_ref, b2_ref, out_ref):
    n_c = ck_ref.shape[2]
    outs = []
    for i, c_ref in enumerate((ck_ref, cv_ref)):
        c = c_ref[0, 0]
        lo = (c + pos_ref[i, 0:1, :]).astype(BF16)
        hi = (c + pos_ref[i, 1:2, :]).astype(BF16)
        p_lo = _dot(lo, w1_ref[i, 0])
        p_hi = _dot(hi, w1_ref[i, 1])
        hid = p_lo + pltpu.roll(p_hi, n_c - 1, 0) + b1_ref[i]
        hid = _gelu_tanh(hid).astype(BF16)
        outs.append(_dot(hid, w2_ref[i]) + b2_ref[i])
    out_ref[0, 0] = jnp.concatenate(outs[::-1], axis=1).astype(BF16)


def _compress(cmp4, cmp_pos, cmp_w1, cmp_b1, cmp_w2, cmp_b2):
    b, _, n_c, cw = cmp4.shape
    half = CMP_STRIDE * HEAD_DIM
    pos = cmp_pos.reshape(2, 2, half).astype(F32)
    w1 = cmp_w1.reshape(2, 2, half, CMP_HIDDEN).astype(BF16)
    b1 = cmp_b1.reshape(2, 1, CMP_HIDDEN).astype(F32)
    w2 = cmp_w2.astype(BF16)
    b2 = cmp_b2.reshape(2, 1, HEAD_DIM).astype(F32)
    full = lambda shape: pl.BlockSpec(shape, lambda i, g: (0,) * len(shape))
    return pl.pallas_call(
        _compress_kernel,
        grid=(b, N_KV),
        in_specs=[
            pl.BlockSpec((1, 1, n_c, cw), lambda i, g: (i, g, 0, 0)),
            pl.BlockSpec((1, 1, n_c, cw), lambda i, g: (i, N_KV + g, 0, 0)),
            full((2, 2, half)),
            full((2, 2, half, CMP_HIDDEN)),
            full((2, 1, CMP_HIDDEN)),
            full((2, CMP_HIDDEN, HEAD_DIM)),
            full((2, 1, HEAD_DIM)),
        ],
        out_specs=pl.BlockSpec((1, 1, n_c, 2 * HEAD_DIM), lambda i, g: (i, g, 0, 0)),
        out_shape=jax.ShapeDtypeStruct((b, N_KV, n_c, 2 * HEAD_DIM), BF16),
        compiler_params=_compiler_params(("parallel", "parallel")),
        name="compress",
    )(cmp4, cmp4, pos, w1, b1, w2, b2)


def _t5_bucket(dist):
    n = jnp.maximum(dist, 0)
    max_exact = REL_BUCKETS // 2
    nf = jnp.maximum(n, 1).astype(F32)
    large = max_exact + (jnp.log(nf / max_exact) / math.log(REL_MAX_DIST / max_exact)
                         * (REL_BUCKETS - max_exact)).astype(jnp.int32)
    large = jnp.minimum(large, REL_BUCKETS - 1)
    return jnp.where(n < max_exact, n, large)


def _bucket_thresholds():
    buckets = _t5_bucket(jnp.arange(REL_MAX_DIST + 1))
    return jnp.sum(buckets[None, :] < jnp.arange(REL_BUCKETS)[:, None], axis=1).astype(jnp.int32)


def _bias_of_dist(dist, head, thr_ref, tbl_ref):
    bias = jnp.full(dist.shape, tbl_ref[head], F32)
    for k in range(1, REL_BUCKETS):
        bias = jnp.where(dist >= thr_ref[k], tbl_ref[k * N_HEADS + head], bias)
    return bias


BIAS_ROWS = 32


def _bias_c_kernel(thr_ref, tbl_ref, out_ref):
    _, tr, n_c = out_ref.shape
    r0 = pl.program_id(0) * tr

    def chunk(ci, carry):
        row0 = pl.multiple_of(ci * BIAS_ROWS, BIAS_ROWS)
        pos = r0 + row0 + lax.broadcasted_iota(jnp.int32, (BIAS_ROWS, n_c), 0)
        key_end = lax.broadcasted_iota(jnp.int32, (BIAS_ROWS, n_c), 1) * CMP_STRIDE + (CMP_BLOCK - 1)
        dist = pos - key_end
        for h in range(N_HEADS):
            bias = _bias_of_dist(dist, h, thr_ref, tbl_ref)
            out_ref[h, pl.ds(row0, BIAS_ROWS), :] = jnp.where(dist >= 0, bias, NEG_INF)
        return carry

    lax.fori_loop(0, tr // BIAS_ROWS, chunk, 0)


def _bias_near_kernel(thr_ref, tbl_ref, near_ref, win_ref, far_ref):
    tq = ATTN_TQ
    h = pl.program_id(0)

    def table(out_ref, lo_keys, window):
        width = out_ref.shape[2]

        def chunk(ci, carry):
            row0 = pl.multiple_of(ci * BIAS_ROWS, BIAS_ROWS)
            dist = (lo_keys + row0 + lax.broadcasted_iota(jnp.int32, (BIAS_ROWS, width), 0)
                    - lax.broadcasted_iota(jnp.int32, (BIAS_ROWS, width), 1))
            visible = (dist >= 0) & (dist < window)
            bias = jnp.full(dist.shape, tbl_ref[h], F32)
            for k in range(1, REL_BUCKETS):
                bias = jnp.where(dist >= thr_ref[k], tbl_ref[k * N_HEADS + h], bias)
            out_ref[0, pl.ds(row0, BIAS_ROWS), :] = jnp.where(visible, bias, NEG_INF)
            return carry

        lax.fori_loop(0, tq // BIAS_ROWS, chunk, 0)

    table(near_ref, SLC_NEAR_BACK, 1 << 30)
    table(win_ref, WINDOW, WINDOW)
    far_ref[0] = jnp.full(far_ref.shape[1:], tbl_ref[(REL_BUCKETS - 1) * N_HEADS + h], F32)


def _attention_bias_tables(rel_bias, seq):
    tbl = rel_bias.astype(F32).reshape(REL_BUCKETS * N_HEADS)
    thr = _bucket_thresholds()
    tq = ATTN_TQ
    n_c = seq // CMP_STRIDE
    smem = pl.BlockSpec(memory_space=pltpu.SMEM)
    tr = min(512, seq)
    bias_c = pl.pallas_call(
        _bias_c_kernel,
        grid=(seq // tr,),
        in_specs=[smem, smem],
        out_specs=pl.BlockSpec((N_HEADS, tr, n_c), lambda i: (0, i, 0)),
        out_shape=jax.ShapeDtypeStruct((N_HEADS, seq, n_c), F32),
        compiler_params=_compiler_params(("parallel",)),
        name="bias_cmp",
    )(thr, tbl)
    head_block = lambda w: pl.BlockSpec((1, tq, w), lambda h: (h, 0, 0))
    widths = (SLC_NEAR_BACK + tq, WINDOW + tq, LANES)
    bias_near, bias_win, bias_far = pl.pallas_call(
        _bias_near_kernel,
        grid=(N_HEADS,),
        in_specs=[smem, smem],
        out_specs=[head_block(w) for w in widths],
        out_shape=[jax.ShapeDtypeStruct((N_HEADS, tq, w), F32) for w in widths],
        compiler_params=_compiler_params(("parallel",)),
        name="bias_near",
    )(thr, tbl)
    return bias_c, bias_near, bias_win, bias_far


def _nsa_constants(seq):
    n_c = seq // CMP_STRIDE
    n_blk = seq // SLC_BLOCK
    cmp_start = jnp.arange(n_c) * CMP_STRIDE
    blk_start = jnp.arange(n_blk) * SLC_BLOCK
    overlap_t = ((cmp_start[None, :] <= blk_start[:, None] + SLC_BLOCK - 1)
                 & (cmp_start[None, :] + CMP_BLOCK - 1 >= blk_start[:, None]))
    overlap_t = overlap_t & (cmp_start[None, :] + CMP_BLOCK <= seq)
    ones_rows = jnp.arange(SUBLANES)[:, None] == 0
    overlap_t = jnp.concatenate([overlap_t, jnp.broadcast_to(ones_rows, (SUBLANES, n_c))], axis=0)
    cmp_ones = jnp.broadcast_to(jnp.arange(LANES)[None, :] == 0, (n_c, LANES))
    return overlap_t.astype(BF16), cmp_ones.astype(BF16)


def _build_kv_scratch(seq, slc_ref, win_ref, ks_sc, vs_sc, kw_sc, vw_sc):
    chunk = min(512, seq)
    lane_p = lax.broadcasted_iota(jnp.int32, (KV_PAD, LANES), 1)
    zeros = jnp.zeros((KV_PAD, LANES), BF16)
    ks_sc[0:KV_PAD] = jnp.where(lane_p >= HEAD_DIM, 1.0, 0.0).astype(BF16)
    kw_sc[0:KV_PAD] = jnp.where(lane_p == HEAD_DIM, NEG_INF, 0.0).astype(BF16)
    vs_sc[0:KV_PAD] = zeros
    vw_sc[0:KV_PAD] = zeros
    lane = lax.broadcasted_iota(jnp.int32, (chunk, LANES), 1)
    row = lax.broadcasted_iota(jnp.int32, (chunk, LANES), 0)
    lo_half = lane < HEAD_DIM
    ones_lane = jnp.where(lane == HEAD_DIM, 1.0, 0.0)
    for c in range(seq // chunk):
        r0 = c * chunk
        dst = slice(KV_PAD + r0, KV_PAD + r0 + chunk)
        blk = jnp.right_shift(r0 + row, int(math.log2(SLC_BLOCK)))
        slab = slc_ref[0, r0:r0 + chunk, :].astype(F32)
        ks_sc[dst] = jnp.where(lo_half, slab, jnp.where(lane - HEAD_DIM == blk, 1.0, 0.0)).astype(BF16)
        vs_sc[dst] = jnp.where(lo_half, pltpu.roll(slab, HEAD_DIM, 1), ones_lane).astype(BF16)
        slab = win_ref[0, r0:r0 + chunk, :].astype(F32)
        kw_sc[dst] = jnp.where(lo_half, slab, 0.0).astype(BF16)
        vw_sc[dst] = jnp.where(lo_half, pltpu.roll(slab, HEAD_DIM, 1), ones_lane).astype(BF16)


def _nsa_kernel(seq, q_ref, vkc_ref, slc_ref, win_ref, gate_ref, bias_c_ref, bias_near_ref, bias_win_ref,
                bias_far_ref, overlap_t_ref, cmp_ones_ref, o_ref,
                ks_sc, vs_sc, kw_sc, vw_sc, s_sc, mrun_sc, acc_sc):
    tq = ATTN_TQ
    n_blk = seq // SLC_BLOCK
    n_sel = min(N_SEL, n_blk)
    rows = HPG * tq
    qt = pl.program_id(2)
    q0 = pl.multiple_of(qt * tq, tq)
    lane = lax.broadcasted_iota(jnp.int32, (tq, LANES), 1)
    lo_half = lane < HEAD_DIM
    lane_r = lax.broadcasted_iota(jnp.int32, (rows, LANES), 1)

    @pl.when(qt == 0)
    def _():
        _build_kv_scratch(seq, slc_ref, win_ref, ks_sc, vs_sc, kw_sc, vw_sc)

    q_lo, q_hi = [], []
    for pair_idx in range(HPG // 2):
        q2 = q_ref[0, :, pair_idx * LANES:(pair_idx + 1) * LANES].astype(F32)
        q2r = pltpu.roll(q2, HEAD_DIM, 1)
        q_lo += [jnp.where(lo_half, q2, 0.0), jnp.where(lo_half, q2r, 0.0)]
        q_hi += [jnp.where(lo_half, 0.0, q2r), jnp.where(lo_half, 0.0, q2)]
    q_lo = jnp.concatenate(q_lo, axis=0)
    q_hi = jnp.concatenate(q_hi, axis=0)

    vkc = vkc_ref[0, 0]
    s_c = _dot_nt(q_hi.astype(BF16), vkc) + bias_c_ref[...].reshape(rows, -1)
    m_c = jnp.max(s_c, axis=1, keepdims=True)
    e_cb = jnp.exp(s_c - m_c).astype(BF16)
    pv_c = _dot(e_cb, jnp.concatenate([vkc, cmp_ones_ref[...]], axis=1))
    row_pos = q0 + (lax.broadcasted_iota(jnp.int32, (rows, LANES), 0) & (tq - 1))
    has_key = row_pos >= CMP_BLOCK - 1
    o_cmp = jnp.where(has_key, pv_c[:, :LANES] / pv_c[:, LANES:LANES + 1], 0.0)

    imp_t4 = _dot_nt(overlap_t_ref[...], e_cb)
    imp_t = None
    for h in range(HPG):
        part = imp_t4[:, h * tq:(h + 1) * tq]
        part = part[:n_blk] / part[n_blk:n_blk + 1]
        imp_t = part if imp_t is None else imp_t + part

    blk = lax.broadcasted_iota(jnp.int32, (n_blk, tq), 0)
    pos = q0 + lax.broadcasted_iota(jnp.int32, (n_blk, tq), 1)
    cur = jnp.right_shift(pos, int(math.log2(SLC_BLOCK)))
    forced = (blk == 0) | (blk == cur) | (blk == cur - 1)
    valid = blk * SLC_BLOCK <= pos
    score = jnp.where(forced, BIG, jnp.where(valid, imp_t, -BIG))
    blk_f = blk.astype(F32)
    pen_t = jnp.full((n_blk, tq), NEG_INF, F32)
    for _ in range(n_sel):
        top = jnp.max(score, axis=0, keepdims=True)
        first = jnp.min(jnp.where(score == top, blk_f, float(n_blk)), axis=0, keepdims=True)
        hit = blk_f == first
        pen_t = jnp.where(hit, 0.0, pen_t)
        score = jnp.where(hit, -jnp.inf, score)
    pieces = [jnp.zeros((tq, HEAD_DIM), F32), jnp.transpose(pen_t)]
    if n_blk < LANES - HEAD_DIM:
        pieces.append(jnp.zeros((tq, LANES - HEAD_DIM - n_blk), F32))
    pen_lanes = jnp.concatenate(pieces, axis=1)
    q_slc = jnp.where(lane_r < HEAD_DIM, q_lo, jnp.concatenate([pen_lanes] * HPG, axis=0)).astype(BF16)

    n_far = qt // (SLC_FAR_TK // tq)
    near0 = pl.multiple_of(q0 + (KV_PAD - SLC_NEAR_BACK), tq)
    near_rows = pl.ds(near0, SLC_NEAR_BACK + tq)

    def far_rows(t):
        return pl.ds(pl.multiple_of(near0 - (t + 1) * SLC_FAR_TK, tq), SLC_FAR_TK)

    def far_cols(t):
        return pl.ds(pl.multiple_of(t * SLC_FAR_TK, SLC_FAR_TK), SLC_FAR_TK)

    mrun_sc[...] = jnp.full((rows, LANES), NEG_INF, F32)

    def pass1(t, carry):
        s = _dot_nt(q_slc, ks_sc[far_rows(t), :])
        s_sc[:, far_cols(t)] = s
        mrun_sc[...] = jnp.maximum(mrun_sc[...], jnp.maximum(s[:, :LANES], s[:, LANES:]))
        return carry

    lax.fori_loop(0, n_far, pass1, 0)
    c_far = bias_far_ref[...].reshape(rows, LANES)
    s_n = _dot_nt(q_slc, ks_sc[near_rows, :]) + bias_near_ref[...].reshape(rows, -1)
    m_s = jnp.maximum(jnp.max(mrun_sc[...], axis=1, keepdims=True) + c_far[:, :1],
                      jnp.max(s_n, axis=1, keepdims=True))
    acc_sc[...] = _dot(jnp.exp(s_n - m_s).astype(BF16), vs_sc[near_rows, :])
    mrun_sc[...] = m_s - c_far

    def pass2(t, carry):
        shift = mrun_sc[...]
        p = jnp.exp(s_sc[:, far_cols(t)] - jnp.concatenate([shift, shift], axis=1))
        acc_sc[...] = acc_sc[...] + _dot(p.astype(BF16), vs_sc[far_rows(t), :])
        return carry

    lax.fori_loop(0, n_far, pass2, 0)
    acc = acc_sc[...]
    o_slc = acc / acc[:, HEAD_DIM:HEAD_DIM + 1]

    q_win = jnp.where(lane_r == HEAD_DIM, 1.0, q_lo).astype(BF16)
    win_rows = pl.ds(q0, WINDOW + tq)
    s_w = _dot_nt(q_win, kw_sc[win_rows, :]) + bias_win_ref[...].reshape(rows, -1)
    p_w = jnp.exp(s_w - jnp.max(s_w, axis=1, keepdims=True)).astype(BF16)
    acc_w = _dot(p_w, vw_sc[win_rows, :])
    o_win = acc_w / acc_w[:, HEAD_DIM:HEAD_DIM + 1]

    gates = gate_ref[0]
    outs = []
    for h in range(HPG):
        sl = slice(h * tq, (h + 1) * tq)
        g_c, g_s, g_w = (gates[:, br * HPG + h:br * HPG + h + 1] for br in range(3))
        outs.append(g_c * o_cmp[sl] + g_s * o_slc[sl] + g_w * o_win[sl])
    for pair_idx in range(HPG // 2):
        even, odd = outs[2 * pair_idx], outs[2 * pair_idx + 1]
        merged = jnp.where(lo_half, even, pltpu.roll(odd, HEAD_DIM, 1))
        o_ref[0, :, pair_idx * LANES:(pair_idx + 1) * LANES] = merged.astype(o_ref.dtype)


def _nsa(q, vkc, slc, win, gates, bias_c, bias_near, bias_win, bias_far, overlap_t, cmp_ones):
    b, seq, _ = q.shape
    n_c = seq // CMP_STRIDE
    n_blk = seq // SLC_BLOCK
    assert n_blk <= LANES - HEAD_DIM and seq % SLC_FAR_TK == 0
    tq = ATTN_TQ
    rows = HPG * tq
    grp = HPG * HEAD_DIM
    const = lambda a: pl.BlockSpec(a.shape, lambda i, g, t: (0,) * a.ndim)
    per_group = lambda a: pl.BlockSpec((HPG,) + a.shape[1:], lambda i, g, t: (g, 0, 0))
    kv_scratch = pltpu.VMEM((KV_PAD + seq, LANES), BF16)
    return pl.pallas_call(
        functools.partial(_nsa_kernel, seq),
        grid=(b, N_KV, seq // tq),
        in_specs=[
            pl.BlockSpec((1, tq, grp), lambda i, g, t: (i, t, g)),
            pl.BlockSpec((1, 1, n_c, 2 * HEAD_DIM), lambda i, g, t: (i, g, 0, 0)),
            pl.BlockSpec((1, seq, 2 * HEAD_DIM), lambda i, g, t: (i, 0, g)),
            pl.BlockSpec((1, seq, 2 * HEAD_DIM), lambda i, g, t: (i, 0, g)),
            pl.BlockSpec((1, tq, GATE_PAD), lambda i, g, t: (i, t, g)),
            pl.BlockSpec((HPG, tq, n_c), lambda i, g, t: (g, t, 0)),
            per_group(bias_near), per_group(bias_win), per_group(bias_far),
            const(overlap_t), const(cmp_ones),
        ],
        out_specs=pl.BlockSpec((1, tq, grp), lambda i, g, t: (i, t, g)),
        out_shape=jax.ShapeDtypeStruct((b, seq, ATTN_WIDTH), BF16),
        scratch_shapes=[
            kv_scratch, kv_scratch, kv_scratch, kv_scratch,
            pltpu.VMEM((rows, seq), F32),
            pltpu.VMEM((rows, LANES), F32),
            pltpu.VMEM((rows, LANES), F32),
        ],
        compiler_params=_compiler_params(("parallel", "parallel", "arbitrary")),
        name="nsa",
    )(q, vkc, slc, win, gates, bias_c, bias_near, bias_win, bias_far, overlap_t, cmp_ones)


def _mixer_inputs(x, rel_bias, w_in, b_in, cmp_pos, cmp_w1, cmp_b1, cmp_w2, cmp_b2):
    b, seq, d = x.shape
    wp, bp = _pack_in_proj(w_in, b_in, d)
    q, slc, win, cmp, gates, u, merge = _in_proj(x.reshape(b * seq, d), wp, bp, d, 512)
    n_c = seq // CMP_STRIDE
    cmp4 = cmp.reshape(b, seq, 2 * N_KV, HEAD_DIM).transpose(0, 2, 1, 3).reshape(
        b, 2 * N_KV, n_c, CMP_STRIDE * HEAD_DIM)
    vkc = _compress(cmp4, cmp_pos, cmp_w1, cmp_b1, cmp_w2, cmp_b2)
    bias_c, bias_near, bias_win, bias_far = _attention_bias_tables(rel_bias, seq)
    overlap_t, cmp_ones = _nsa_constants(seq)
    o = _nsa(q.reshape(b, seq, -1), vkc, slc.reshape(b, seq, -1), win.reshape(b, seq, -1),
             gates.reshape(b, seq, -1), bias_c, bias_near, bias_win, bias_far, overlap_t, cmp_ones)
    return o, u, merge


S5_HALF_GROUPS = S5_GROUPS // 2
S5_HALF_IN = S5_HALF_GROUPS * S5_GROUP
S5_HALF_STATE = S5_HALF_GROUPS * S5_STATE
S5_SCAN_LANES = 512
S5_CHUNK = 64
S5_UNROLL = 8


def _s5_params(lam_re, lam_im, log_dt, b_re, b_im, c_re, c_im, nb):
    dt = jnp.exp(log_dt.astype(F32))[:, None]
    lr, li = lam_re.astype(F32), lam_im.astype(F32)
    mag = jnp.exp(lr * dt)
    ab_re, ab_im = mag * jnp.cos(li * dt), mag * jnp.sin(li * dt)
    nr, ni = ab_re - 1.0, ab_im
    den = lr * lr + li * li
    fr, fi = (nr * lr + ni * li) / den, (ni * lr - nr * li) / den
    br, bim = b_re.astype(F32), b_im.astype(F32)
    bb_re = fr[..., None] * br - fi[..., None] * bim
    bb_im = fr[..., None] * bim + fi[..., None] * br
    eye = jnp.eye(S5_HALF_GROUPS, dtype=F32)

    def in_mat(bb):
        t = bb.reshape(2, S5_HALF_GROUPS, S5_STATE, S5_GROUP)
        m = jnp.einsum('kgph,gj->kghjp', t, eye)
        return m.reshape(2, S5_HALF_IN, S5_HALF_STATE)

    def out_mat(c):
        t = c.astype(F32).reshape(2, S5_HALF_GROUPS, S5_GROUP, S5_STATE)
        m = jnp.einsum('kghp,gj->kgpjh', t, eye)
        return m.reshape(2, S5_HALF_STATE, S5_HALF_IN)

    bmat = jnp.concatenate([in_mat(bb_re), in_mat(bb_im)], axis=2).astype(BF16)
    cmat = jnp.concatenate([out_mat(c_re), -out_mat(c_im)], axis=1).astype(BF16)
    a = jnp.concatenate([ab_re.reshape(2, S5_HALF_STATE), ab_im.reshape(2, S5_HALF_STATE)], axis=1)
    a = jnp.broadcast_to(a.reshape(1, 4 * S5_HALF_STATE), (nb, 4 * S5_HALF_STATE))
    return bmat, cmat, a


def _s5_kernel(u_ref, bmat_ref, cmat_ref, a_ref, d_ref, y_ref, ut_sc, x_sc, st_sc):
    nb, t_len, _ = u_ref.shape
    half_w = 2 * S5_HALF_STATE

    @pl.when(pl.program_id(0) == 0)
    def _():
        st_sc[...] = jnp.zeros_like(st_sc)

    n_cb = ut_sc.shape[0]
    for b in range(nb):
        for cb in range(n_cb):
            ut_sc[cb, pl.ds(b, t_len, stride=nb), :] = u_ref[b, :, cb * LANES:(cb + 1) * LANES]
    ut = jnp.concatenate([ut_sc[cb] for cb in range(n_cb)], axis=1)
    ub = ut.astype(BF16)
    for k in range(2):
        x_sc[:, k * half_w:(k + 1) * half_w] = _dot(ub[:, k * S5_HALF_IN:(k + 1) * S5_HALF_IN], bmat_ref[k])

    for k in range(2):
        for j in range(S5_HALF_STATE // S5_SCAN_LANES):
            re0 = k * half_w + j * S5_SCAN_LANES
            im0 = re0 + S5_HALF_STATE
            re_sl, im_sl = pl.ds(re0, S5_SCAN_LANES), pl.ds(im0, S5_SCAN_LANES)
            ar, ai = a_ref[:, re_sl], a_ref[:, im_sl]

            def steps(c, carry):
                xr, xi = carry
                for s in range(S5_UNROLL):
                    rows = pl.ds(pl.multiple_of((c * S5_UNROLL + s) * nb, nb), nb)
                    nxr = ar * xr - ai * xi + x_sc[rows, re_sl]
                    nxi = ar * xi + ai * xr + x_sc[rows, im_sl]
                    x_sc[rows, re_sl] = nxr
                    x_sc[rows, im_sl] = nxi
                    xr, xi = nxr, nxi
                return xr, xi

            xr, xi = lax.fori_loop(0, t_len // S5_UNROLL, steps, (st_sc[:, re_sl], st_sc[:, im_sl]))
            st_sc[:, re_sl] = xr
            st_sc[:, im_sl] = xi

    xs = x_sc[...].astype(BF16)
    y = jnp.concatenate([_dot(xs[:, k * half_w:(k + 1) * half_w], cmat_ref[k]) for k in range(2)], axis=1)
    y = y + d_ref[...] * ut
    for cb in range(n_cb):
        ut_sc[cb] = y[:, cb * LANES:(cb + 1) * LANES]
    for b in range(nb):
        for cb in range(n_cb):
            y_ref[b, :, cb * LANES:(cb + 1) * LANES] = ut_sc[cb, pl.ds(b, t_len, stride=nb), :]


def _s5(u, bmat, cmat, a, d_skip):
    nb, seq, w = u.shape
    t_len = min(S5_CHUNK, seq)
    full = lambda shape: pl.BlockSpec(shape, lambda c: (0,) * len(shape))
    return pl.pallas_call(
        _s5_kernel,
        grid=(seq // t_len,),
        in_specs=[
            pl.BlockSpec((nb, t_len, w), lambda c: (0, c, 0)),
            full(bmat.shape), full(cmat.shape), full(a.shape), full((1, w)),
        ],
        out_specs=pl.BlockSpec((nb, t_len, w), lambda c: (0, c, 0)),
        out_shape=jax.ShapeDtypeStruct((nb, seq, w), F32),
        scratch_shapes=[
            pltpu.VMEM((w // LANES, t_len * nb, LANES), F32),
            pltpu.VMEM((t_len * nb, 4 * S5_HALF_STATE), F32),
            pltpu.VMEM((nb, 4 * S5_HALF_STATE), F32),
        ],
        compiler_params=_compiler_params(("arbitrary",)),
        name="s5",
    )(u, bmat, cmat, a, d_skip.reshape(1, w).astype(F32))


ROUTE_PAD = LANES
_R_E1, _R_E2, _R_W1, _R_W2, _R_RANK1, _R_RANK2 = range(6)


def _layer_norm(t, g, b):
    mu = jnp.mean(t, axis=1, keepdims=True)
    c = t - mu
    var = jnp.mean(c * c, axis=1, keepdims=True)
    return c * lax.rsqrt(var + LN_EPS) * g + b


def _post_kernel(x_ref, o_ref, y_ref, m_ref, wup_ref, wval_ref, wgate_ref, bgate_ref, wout_ref,
                 g1_ref, b1_ref, wr_ref, br_ref, h_ref, route_ref, cnt_ref, run_sc):
    tm, d = x_ref.shape

    @pl.when(pl.program_id(0) == 0)
    def _():
        run_sc[...] = jnp.zeros_like(run_sc)

    y_a = _dot(o_ref[...], wup_ref[...])
    z = _gelu_tanh(y_ref[...]).astype(BF16)
    y_b = _dot(z, wval_ref[...]) * jax.nn.sigmoid(_dot(z, wgate_ref[...]) + bgate_ref[...])
    mixed = m_ref[:, :d] * y_a + m_ref[:, d:] * y_b
    t = DN_ALPHA * x_ref[...] + _dot(mixed.astype(BF16), wout_ref[...])
    h = _layer_norm(t, g1_ref[...], b1_ref[...])
    h_ref[...] = h

    logits = _dot(h.astype(BF16), wr_ref[...]) + br_ref[...]
    lane = lax.broadcasted_iota(jnp.int32, (tm, ROUTE_PAD), 1)
    lane_f = lane.astype(F32)
    is_group = lane < N_EGROUPS

    def first_max(v):
        top = jnp.max(v, axis=1, keepdims=True)
        idx = jnp.min(jnp.where(v == top, lane_f, float(ROUTE_PAD)), axis=1, keepdims=True)
        return top, idx

    g_max, g_top = first_max(jnp.where(is_group, logits, -jnp.inf))
    p_group = 1.0 / jnp.sum(jnp.where(is_group, jnp.exp(logits - g_max), 0.0), axis=1, keepdims=True)
    grp_of_lane = jnp.right_shift(lane - N_EGROUPS, int(math.log2(EXPERTS_PER_GROUP))).astype(F32)
    in_group = (lane >= N_EGROUPS) & (lane < N_EGROUPS + N_EXPERTS) & (grp_of_lane == g_top)
    e_log = jnp.where(in_group, logits, -jnp.inf)
    v1, i1 = first_max(e_log)
    hit1 = lane_f == i1
    v2, i2 = first_max(jnp.where(hit1, -jnp.inf, e_log))
    hit2 = lane_f == i2
    e2 = jnp.exp(v2 - v1)
    w1 = p_group / (1.0 + e2)
    w2 = p_group * e2 / (1.0 + e2)

    hits = jnp.where(hit1 | hit2, 1.0, 0.0)
    row = lax.broadcasted_iota(jnp.int32, (tm, tm), 0)
    col = lax.broadcasted_iota(jnp.int32, (tm, tm), 1)
    earlier = jnp.where(col < row, 1.0, 0.0).astype(BF16)
    before = _dot(earlier, hits.astype(BF16)) + run_sc[...]
    rank1 = jnp.sum(jnp.where(hit1, before, 0.0), axis=1, keepdims=True)
    rank2 = jnp.sum(jnp.where(hit2, before, 0.0), axis=1, keepdims=True)
    run_sc[...] = run_sc[...] + jnp.sum(hits, axis=0, keepdims=True)
    cnt_ref[...] = run_sc[...]

    rec = jnp.zeros((tm, ROUTE_PAD), F32)
    for slot, val in ((_R_E1, i1 - N_EGROUPS), (_R_E2, i2 - N_EGROUPS), (_R_W1, w1), (_R_W2, w2),
                      (_R_RANK1, rank1), (_R_RANK2, rank2)):
        rec = jnp.where(lane == slot, val, rec)
    route_ref[...] = rec


def _post(x2d, o2d, y2d, merge, w_attn_up, s5_w_val, s5_w_gate, s5_b_gate, w_out, ln1_g, ln1_b,
          router_w_group, router_b_group, router_w_expert, router_b_expert, tm):
    n, d = x2d.shape
    rpad = ROUTE_PAD - N_EGROUPS - N_EXPERTS
    wr = jnp.concatenate([router_w_group, router_w_expert, jnp.zeros((d, rpad), F32)], axis=1).astype(BF16)
    br = jnp.concatenate([router_b_group, router_b_expert, jnp.zeros((rpad,), F32)]).reshape(1, -1).astype(F32)
    row = lambda w: pl.BlockSpec((tm, w), lambda i: (i, 0))
    full = lambda a: pl.BlockSpec(a.shape, lambda i: (0,) * a.ndim)
    weights = [w_attn_up.astype(BF16), s5_w_val.astype(BF16), s5_w_gate.astype(BF16),
               s5_b_gate.reshape(1, d).astype(F32), w_out.astype(BF16),
               ln1_g.reshape(1, d).astype(F32), ln1_b.reshape(1, d).astype(F32), wr, br]
    return pl.pallas_call(
        _post_kernel,
        grid=(n // tm,),
        in_specs=[row(d), row(ATTN_WIDTH), row(S5_WIDTH), row(2 * d)] + [full(w) for w in weights],
        out_specs=[row(d), row(ROUTE_PAD), pl.BlockSpec((1, ROUTE_PAD), lambda i: (0, 0))],
        out_shape=[jax.ShapeDtypeStruct((n, d), F32), jax.ShapeDtypeStruct((n, ROUTE_PAD), F32),
                   jax.ShapeDtypeStruct((1, ROUTE_PAD), F32)],
        scratch_shapes=[pltpu.VMEM((1, ROUTE_PAD), F32)],
        compiler_params=_compiler_params(("arbitrary",)),
        name="post_mixer",
    )(x2d, o2d, y2d, merge, *weights)


def _plan_kernel(route_ref, cnt_ref, dest_ref):
    tm = route_ref.shape[0]
    lane8 = lax.broadcasted_iota(jnp.int32, (SUBLANES, ROUTE_PAD), 1)
    counts = jnp.broadcast_to(cnt_ref[...], (SUBLANES, ROUTE_PAD)).astype(jnp.int32)
    shift = int(math.log2(EXPERT_BLOCK))
    padded = jnp.left_shift(jnp.right_shift(counts + (EXPERT_BLOCK - 1), shift), shift)
    incl = padded
    step = 1
    while step < ROUTE_PAD:
        incl = incl + jnp.where(lane8 >= step, pltpu.roll(incl, step, 1), 0)
        step *= 2
    pstart = (incl - padded)[0:1].astype(F32)
    route = route_ref[...]
    expert_of_lane = (lax.broadcasted_iota(jnp.int32, (tm, ROUTE_PAD), 1) - N_EGROUPS).astype(F32)
    lane = lax.broadcasted_iota(jnp.int32, (tm, ROUTE_PAD), 1)

    def dest(e_slot, rank_slot):
        hit = expert_of_lane == route[:, e_slot:e_slot + 1]
        return jnp.sum(jnp.where(hit, pstart, 0.0), axis=1, keepdims=True) + route[:, rank_slot:rank_slot + 1]

    d1 = dest(_R_E1, _R_RANK1)
    d2 = dest(_R_E2, _R_RANK2)
    dest_ref[...] = jnp.where(lane == 0, d1, jnp.where(lane == 1, d2, 0.0)).astype(jnp.int32)


def _plan(route, counts_row):
    n = route.shape[0]
    tm = min(1024, n)
    return pl.pallas_call(
        _plan_kernel,
        grid=(n // tm,),
        in_specs=[pl.BlockSpec((tm, ROUTE_PAD), lambda i: (i, 0)),
                  pl.BlockSpec((1, ROUTE_PAD), lambda i: (0, 0))],
        out_specs=pl.BlockSpec((tm, ROUTE_PAD), lambda i: (i, 0)),
        out_shape=jax.ShapeDtypeStruct((n, ROUTE_PAD), jnp.int32),
        compiler_params=_compiler_params(("parallel",)),
        name="moe_plan",
    )(route, counts_row)


SC_GATHER_ROWS = 32


def _sc_row_gather(table, idx):
    n_idx = idx.shape[0]
    d = table.shape[1]
    info = plsc.get_sparse_core_info()
    n_workers = info.num_cores * info.num_subcores
    per_worker = n_idx // n_workers
    assert n_idx % (n_workers * SC_GATHER_ROWS) == 0
    mesh = plsc.VectorSubcoreMesh(core_axis_name="c", subcore_axis_name="s")

    @functools.partial(
        pl.kernel, mesh=mesh,
        out_type=jax.ShapeDtypeStruct((n_idx, d), table.dtype),
        scratch_types=[
            pltpu.VMEM((SC_GATHER_ROWS,), jnp.int32),
            pltpu.VMEM((SC_GATHER_ROWS, d), table.dtype),
            pltpu.SemaphoreType.DMA,
        ],
    )
    def gather(table_hbm, idx_hbm, out_hbm, idx_v, rows_v, sem):
        worker = lax.axis_index("s") * info.num_cores + lax.axis_index("c")
        base = worker * per_worker

        @pl.loop(0, per_worker // SC_GATHER_ROWS)
        def _(j):
            off = base + j * SC_GATHER_ROWS
            pltpu.sync_copy(idx_hbm.at[pl.ds(off, SC_GATHER_ROWS)], idx_v)
            pltpu.async_copy(table_hbm.at[idx_v], rows_v, sem).wait()
            pltpu.sync_copy(rows_v, out_hbm.at[pl.ds(off, SC_GATHER_ROWS)])

    return gather(table, idx)


def _sc_row_scatter(rows, idx_a, idx_b, n_out):
    n, d = rows.shape
    info = plsc.get_sparse_core_info()
    n_workers = info.num_cores * info.num_subcores
    per_worker = n // n_workers
    assert n % (n_workers * SC_GATHER_ROWS) == 0
    mesh = plsc.VectorSubcoreMesh(core_axis_name="c", subcore_axis_name="s")

    @functools.partial(
        pl.kernel, mesh=mesh,
        out_type=jax.ShapeDtypeStruct((n_out, d), rows.dtype),
        scratch_types=[
            pltpu.VMEM((SC_GATHER_ROWS,), jnp.int32),
            pltpu.VMEM((SC_GATHER_ROWS,), jnp.int32),
            pltpu.VMEM((SC_GATHER_ROWS, d), rows.dtype),
        ],
    )
    def scatter(rows_hbm, idx_a_hbm, idx_b_hbm, out_hbm, idx_a_v, idx_b_v, rows_v):
        worker = lax.axis_index("s") * info.num_cores + lax.axis_index("c")
        base = worker * per_worker

        @pl.loop(0, per_worker // SC_GATHER_ROWS)
        def _(j):
            src = pl.ds(base + j * SC_GATHER_ROWS, SC_GATHER_ROWS)
            pltpu.sync_copy(rows_hbm.at[src], rows_v)
            pltpu.sync_copy(idx_a_hbm.at[src], idx_a_v)
            pltpu.sync_copy(idx_b_hbm.at[src], idx_b_v)
            pltpu.sync_copy(rows_v, out_hbm.at[idx_a_v])
            pltpu.sync_copy(rows_v, out_hbm.at[idx_b_v])

    return scatter(rows, idx_a, idx_b)


def _expert_kernel(blk_exp_ref, blk_valid_ref, x_ref, wg_ref, wu_ref, wd_ref, y_ref, wg_sc, wu_sc, wd_sc):
    i = pl.program_id(0)
    n_valid = blk_valid_ref[i]

    @pl.when((i == 0) | (blk_exp_ref[i] != blk_exp_ref[jnp.maximum(i - 1, 0)]))
    def _():
        wg_sc[...] = wg_ref[0].astype(BF16)
        wu_sc[...] = wu_ref[0].astype(BF16)
        wd_sc[...] = wd_ref[0].astype(BF16)

    @pl.when(n_valid > 0)
    def _():
        row = lax.broadcasted_iota(jnp.int32, x_ref.shape, 0)
        xb = jnp.where(row < n_valid, x_ref[...], 0.0).astype(BF16)
        h_gate = _dot(xb, wg_sc[...])
        h_up = _dot(xb, wu_sc[...])
        hb = (h_gate * jax.nn.sigmoid(h_gate) * h_up).astype(BF16)
        y_ref[...] = _dot(hb, wd_sc[...])

    @pl.when(n_valid == 0)
    def _():
        y_ref[...] = jnp.zeros_like(y_ref)


def _experts(blk_expert, blk_valid, xs, w_gate, w_up, w_down):
    n_blocks = blk_expert.shape[0]
    d = xs.shape[1]
    grid_spec = pltpu.PrefetchScalarGridSpec(
        num_scalar_prefetch=2,
        grid=(n_blocks,),
        in_specs=[
            pl.BlockSpec((EXPERT_BLOCK, d), lambda i, be, bv: (i, 0)),
            pl.BlockSpec((1, d, D_EXPERT), lambda i, be, bv: (be[i], 0, 0)),
            pl.BlockSpec((1, d, D_EXPERT), lambda i, be, bv: (be[i], 0, 0)),
            pl.BlockSpec((1, D_EXPERT, d), lambda i, be, bv: (be[i], 0, 0)),
        ],
        out_specs=pl.BlockSpec((EXPERT_BLOCK, d), lambda i, be, bv: (i, 0)),
        scratch_shapes=[
            pltpu.VMEM((d, D_EXPERT), BF16),
            pltpu.VMEM((d, D_EXPERT), BF16),
            pltpu.VMEM((D_EXPERT, d), BF16),
        ],
    )
    return pl.pallas_call(
        _expert_kernel,
        grid_spec=grid_spec,
        out_shape=jax.ShapeDtypeStruct((n_blocks * EXPERT_BLOCK, d), F32),
        compiler_params=_compiler_params(("arbitrary",)),
        name="experts",
    )(blk_expert, blk_valid, xs, w_gate, w_up, w_down)


COMBINE_TM = 512


def _combine_kernel(h_ref, y1_ref, y2_ref, route_ref, g2_ref, b2_ref, out_ref):
    route = route_ref[...]
    w1 = route[:, _R_W1:_R_W1 + 1]
    w2 = route[:, _R_W2:_R_W2 + 1]
    t = DN_ALPHA * h_ref[...] + (y1_ref[...] * w1 + y2_ref[...] * w2)
    out_ref[...] = _layer_norm(t, g2_ref[...], b2_ref[...])


def _combine(yg, h2d, route, ln2_g, ln2_b):
    n, d = h2d.shape
    tm = min(COMBINE_TM, n)
    n_tiles = n // tm
    row = lambda w: pl.BlockSpec((tm, w), lambda i: (i, 0))
    vec = pl.BlockSpec((1, d), lambda i: (0, 0))
    return pl.pallas_call(
        _combine_kernel,
        grid=(n_tiles,),
        in_specs=[row(d), row(d), pl.BlockSpec((tm, d), lambda i: (i + n_tiles, 0)), row(ROUTE_PAD), vec, vec],
        out_specs=row(d),
        out_shape=jax.ShapeDtypeStruct((n, d), F32),
        compiler_params=_compiler_params(("parallel",)),
        name="combine",
    )(h2d, yg, yg, route, ln2_g.reshape(1, d).astype(F32), ln2_b.reshape(1, d).astype(F32))


def _moe(h2d, route, counts_row, w_gate, w_up, w_down, ln2_g, ln2_b):
    n, d = h2d.shape
    dest = _plan(route, counts_row)
    dest1, dest2 = dest[:, 0], dest[:, 1]
    counts = counts_row[0, N_EGROUPS:N_EGROUPS + N_EXPERTS].astype(jnp.int32)
    padded = (counts + EXPERT_BLOCK - 1) // EXPERT_BLOCK * EXPERT_BLOCK
    pend = jnp.cumsum(padded)
    n_blocks = -(-(n * TOP_K_IN_GROUP) // EXPERT_BLOCK) + N_EXPERTS
    blk_row0 = jnp.arange(n_blocks, dtype=jnp.int32) * EXPERT_BLOCK
    blk_expert = jnp.minimum(jnp.sum(pend[None, :] <= blk_row0[:, None], axis=1), N_EXPERTS - 1).astype(jnp.int32)
    blk_valid = jnp.clip((pend - padded + counts)[blk_expert] - blk_row0, 0, EXPERT_BLOCK).astype(jnp.int32)
    xs = _sc_row_scatter(h2d, dest1, dest2, n_blocks * EXPERT_BLOCK)
    yb = _experts(blk_expert, blk_valid, xs, w_gate, w_up, w_down)
    yg = _sc_row_gather(yb, jnp.concatenate([dest1, dest2]))
    return _combine(yg, h2d, route, ln2_g, ln2_b)


def kernel(x, rel_bias, w_in, b_in, cmp_pos, cmp_w1, cmp_b1, cmp_w2, cmp_b2, w_attn_up, s5_lambda_re, s5_lambda_im, s5_log_dt, s5_b_re, s5_b_im, s5_c_re, s5_c_im, s5_d, s5_w_val, s5_w_gate, s5_b_gate, w_out, ln1_g, ln1_b, router_w_group, router_b_group, router_w_expert, router_b_expert, exp_w_gate, exp_w_up, exp_w_down, ln2_g, ln2_b):
    b, seq, d = x.shape
    n = b * seq
    assert w_in.shape[0] == DEPTH
    l = 0
    o, u, merge = _mixer_inputs(x, rel_bias, w_in[l], b_in[l], cmp_pos[l], cmp_w1[l], cmp_b1[l],
                                cmp_w2[l], cmp_b2[l])
    bmat, cmat, a = _s5_params(s5_lambda_re[l], s5_lambda_im[l], s5_log_dt[l], s5_b_re[l], s5_b_im[l],
                               s5_c_re[l], s5_c_im[l], b)
    y_s = _s5(u.reshape(b, seq, S5_WIDTH), bmat, cmat, a, s5_d[l])
    h2d, route, counts = _post(x.reshape(n, d), o.reshape(n, ATTN_WIDTH), y_s.reshape(n, S5_WIDTH), merge,
                               w_attn_up[l], s5_w_val[l], s5_w_gate[l], s5_b_gate[l], w_out[l], ln1_g[l],
                               ln1_b[l], router_w_group[l], router_b_group[l], router_w_expert[l],
                               router_b_expert[l], 256)
    out = _moe(h2d, route, counts, exp_w_gate[l], exp_w_up[l], exp_w_down[l], ln2_g[l], ln2_b[l])
    return out.reshape(b, seq, d)
```

```python
import functools
import math

import jax
import jax.numpy as jnp
from jax import lax
from jax.experimental import pallas as pl
from jax.experimental.pallas import tpu as pltpu
from jax.experimental.pallas import tpu_sc as plsc

F32 = jnp.float32
BF16 = jnp.bfloat16

N_HEADS = 8
HEAD_DIM = 64
N_KV = 2
HPG = N_HEADS // N_KV
CMP_STRIDE = 16
CMP_BLOCK = 2 * CMP_STRIDE
CMP_HIDDEN = 128
SLC_BLOCK = 64
N_SEL = 16
WINDOW = 512
REL_BUCKETS = 32
REL_MAX_DIST = 128
S5_WIDTH = 512
S5_GROUP = 16
S5_GROUPS = S5_WIDTH // S5_GROUP
S5_STATE = 64
N_EGROUPS = 8
EXPERTS_PER_GROUP = 8
N_EXPERTS = N_EGROUPS * EXPERTS_PER_GROUP
TOP_K_IN_GROUP = 2
D_EXPERT = 256
EXPERT_BLOCK = 128
DEPTH = 1
DN_ALPHA = (2.0 * DEPTH) ** 0.25
LN_EPS = 1e-5
NEG_INF = -1e30
BIG = 1e9

ATTN_WIDTH = N_HEADS * HEAD_DIM
KV_WIDTH = N_KV * HEAD_DIM
KV_OFF = ATTN_WIDTH
NSA_GATE_OFF = KV_OFF + 6 * KV_WIDTH
S5_OFF = NSA_GATE_OFF + 3 * N_HEADS
MERGE_OFF = S5_OFF + S5_WIDTH

LANES = 128
SUBLANES = 8
VMEM_LIMIT_BYTES = 56 * 1024 * 1024

ATTN_TQ = 128
SLC_FAR_TK = 256
SLC_NEAR_BACK = ATTN_TQ
KV_PAD = WINDOW
GATE_PAD = LANES


def _gelu_tanh(x):
    c = math.sqrt(2.0 / math.pi)
    return x * (0.5 * (1.0 + jnp.tanh(c * (x + 0.044715 * (x * x * x)))))


def _dot(a, b):
    return jnp.dot(a, b, preferred_element_type=F32)


def _dot_nt(a, b):
    return lax.dot_general(a, b, (((1,), (1,)), ((), ())), preferred_element_type=F32)


def _compiler_params(semantics):
    return pltpu.CompilerParams(dimension_semantics=semantics, vmem_limit_bytes=VMEM_LIMIT_BYTES)


def _in_proj_layout(d_model):
    widths = (ATTN_WIDTH, 2 * KV_WIDTH, 2 * KV_WIDTH, 2 * KV_WIDTH, N_KV * GATE_PAD, S5_WIDTH, 2 * d_model)
    offs = [0]
    for w in widths:
        offs.append(offs[-1] + w)
    return widths, offs


def _pack_in_proj(w_in, b_in, d_model):
    def kv_cols(j):
        return KV_OFF + j * KV_WIDTH

    def pair(jk, jv):
        cols = []
        for g in range(N_KV):
            cols.append(jnp.arange(kv_cols(jk) + g * HEAD_DIM, kv_cols(jk) + (g + 1) * HEAD_DIM))
            cols.append(jnp.arange(kv_cols(jv) + g * HEAD_DIM, kv_cols(jv) + (g + 1) * HEAD_DIM))
        return jnp.concatenate(cols)

    idx = jnp.concatenate([
        jnp.arange(0, ATTN_WIDTH),
        pair(2, 3),
        pair(4, 5),
        jnp.arange(kv_cols(0), kv_cols(2)),
    ])
    idx2 = jnp.concatenate([jnp.arange(S5_OFF, S5_OFF + S5_WIDTH),
                            jnp.arange(MERGE_OFF, MERGE_OFF + 2 * d_model)])
    gpad = GATE_PAD - 3 * HPG
    w_parts, b_parts = [w_in[:, idx]], [b_in[idx]]
    for g in range(N_KV):
        cols = jnp.asarray([NSA_GATE_OFF + (g * HPG + h) * 3 + j for j in range(3) for h in range(HPG)])
        w_parts += [w_in[:, cols], jnp.zeros((d_model, gpad), F32)]
        b_parts += [b_in[cols], jnp.zeros((gpad,), F32)]
    w = jnp.concatenate(w_parts + [w_in[:, idx2]], axis=1)
    b = jnp.concatenate(b_parts + [b_in[idx2]])
    return w.astype(BF16), b.reshape(1, -1).astype(F32)


def _in_proj_kernel(offs, x_ref, w_ref, b_ref, q_ref, slc_ref, win_ref, cmp_ref, g_ref, u_ref, m_ref):
    xb = x_ref[...].astype(BF16)

    def proj(i):
        c0, c1 = offs[i], offs[i + 1]
        return _dot(xb, w_ref[:, c0:c1]) + b_ref[:, c0:c1]

    q_ref[...] = (proj(0) * (HEAD_DIM ** -0.5)).astype(BF16)
    slc_ref[...] = proj(1).astype(BF16)
    win_ref[...] = proj(2).astype(BF16)
    cmp_ref[...] = proj(3)
    g_ref[...] = jax.nn.sigmoid(proj(4))
    u_ref[...] = proj(5)
    m_ref[...] = jax.nn.sigmoid(proj(6))


def _in_proj(x2d, w_packed, b_packed, d_model, tm):
    n = x2d.shape[0]
    widths, offs = _in_proj_layout(d_model)
    ncols = offs[-1]
    dtypes = (BF16, BF16, BF16, F32, F32, F32, F32)
    return pl.pallas_call(
        functools.partial(_in_proj_kernel, tuple(offs)),
        grid=(n // tm,),
        in_specs=[
            pl.BlockSpec((tm, d_model), lambda i: (i, 0)),
            pl.BlockSpec((d_model, ncols), lambda i: (0, 0)),
            pl.BlockSpec((1, ncols), lambda i: (0, 0)),
        ],
        out_specs=[pl.BlockSpec((tm, w), lambda i: (i, 0)) for w in widths],
        out_shape=[jax.ShapeDtypeStruct((n, w), dt) for w, dt in zip(widths, dtypes)],
        compiler_params=_compiler_params(("parallel",)),
        name="in_proj",
    )(x2d, w_packed, b_packed)


def _compress_kernel(ck_ref, cv_ref, pos_ref, w1_ref, b1_ref, w2_ref, b2_ref, out_ref):
    n_c = ck_ref.shape[2]
    outs = []
    for i, c_ref in enumerate((ck_ref, cv_ref)):
        c = c_ref[0, 0]
        lo = (c + pos_ref[i, 0:1, :]).astype(BF16)
        hi = (c + pos_ref[i, 1:2, :]).astype(BF16)
        p_lo = _dot(lo, w1_ref[i, 0])
        p_hi = _dot(hi, w1_ref[i, 1])
        hid = p_lo + pltpu.roll(p_hi, n_c - 1, 0) + b1_ref[i]
        hid = _gelu_tanh(hid).astype(BF16)
        outs.append(_dot(hid, w2_ref[i]) + b2_ref[i])
    out_ref[0, 0] = jnp.concatenate(outs[::-1], axis=1).astype(BF16)


def _compress(cmp4, cmp_pos, cmp_w1, cmp_b1, cmp_w2, cmp_b2):
    b, _, n_c, cw = cmp4.shape
    half = CMP_STRIDE * HEAD_DIM
    pos = cmp_pos.reshape(2, 2, half).astype(F32)
    w1 = cmp_w1.reshape(2, 2, half, CMP_HIDDEN).astype(BF16)
    b1 = cmp_b1.reshape(2, 1, CMP_HIDDEN).astype(F32)
    w2 = cmp_w2.astype(BF16)
    b2 = cmp_b2.reshape(2, 1, HEAD_DIM).astype(F32)
    full = lambda shape: pl.BlockSpec(shape, lambda i, g: (0,) * len(shape))
    return pl.pallas_call(
        _compress_kernel,
        grid=(b, N_KV),
        in_specs=[
            pl.BlockSpec((1, 1, n_c, cw), lambda i, g: (i, g, 0, 0)),
            pl.BlockSpec((1, 1, n_c, cw), lambda i, g: (i, N_KV + g, 0, 0)),
            full((2, 2, half)),
            full((2, 2, half, CMP_HIDDEN)),
            full((2, 1, CMP_HIDDEN)),
            full((2, CMP_HIDDEN, HEAD_DIM)),
            full((2, 1, HEAD_DIM)),
        ],
        out_specs=pl.BlockSpec((1, 1, n_c, 2 * HEAD_DIM), lambda i, g: (i, g, 0, 0)),
        out_shape=jax.ShapeDtypeStruct((b, N_KV, n_c, 2 * HEAD_DIM), BF16),
        compiler_params=_compiler_params(("parallel", "parallel")),
        name="compress",
    )(cmp4, cmp4, pos, w1, b1, w2, b2)


def _t5_bucket(dist):
    n = jnp.maximum(dist, 0)
    max_exact = REL_BUCKETS // 2
    nf = jnp.maximum(n, 1).astype(F32)
    large = max_exact + (jnp.log(nf / max_exact) / math.log(REL_MAX_DIST / max_exact)
                         * (REL_BUCKETS - max_exact)).astype(jnp.int32)
    large = jnp.minimum(large, REL_BUCKETS - 1)
    return jnp.where(n < max_exact, n, large)


def _bucket_thresholds():
    buckets = _t5_bucket(jnp.arange(REL_MAX_DIST + 1))
    return jnp.sum(buckets[None, :] < jnp.arange(REL_BUCKETS)[:, None], axis=1).astype(jnp.int32)


def _bias_of_dist(dist, head, thr_ref, tbl_ref):
    bias = jnp.full(dist.shape, tbl_ref[head], F32)
    for k in range(1, REL_BUCKETS):
        bias = jnp.where(dist >= thr_ref[k], tbl_ref[k * N_HEADS + head], bias)
    return bias


BIAS_ROWS = 32


def _bias_c_kernel(thr_ref, tbl_ref, out_ref):
    _, tr, n_c = out_ref.shape
    r0 = pl.program_id(0) * tr

    def chunk(ci, carry):
        row0 = pl.multiple_of(ci * BIAS_ROWS, BIAS_ROWS)
        pos = r0 + row0 + lax.broadcasted_iota(jnp.int32, (BIAS_ROWS, n_c), 0)
        key_end = lax.broadcasted_iota(jnp.int32, (BIAS_ROWS, n_c), 1) * CMP_STRIDE + (CMP_BLOCK - 1)
        dist = pos - key_end
        for h in range(N_HEADS):
            bias = _bias_of_dist(dist, h, thr_ref, tbl_ref)
            out_ref[h, pl.ds(row0, BIAS_ROWS), :] = jnp.where(dist >= 0, bias, NEG_INF)
        return carry

    lax.fori_loop(0, tr // BIAS_ROWS, chunk, 0)


def _bias_near_kernel(thr_ref, tbl_ref, near_ref, win_ref, far_ref):
    tq = ATTN_TQ
    h = pl.program_id(0)

    def table(out_ref, lo_keys, window):
        width = out_ref.shape[2]

        def chunk(ci, carry):
            row0 = pl.multiple_of(ci * BIAS_ROWS, BIAS_ROWS)
            dist = (lo_keys + row0 + lax.broadcasted_iota(jnp.int32, (BIAS_ROWS, width), 0)
                    - lax.broadcasted_iota(jnp.int32, (BIAS_ROWS, width), 1))
            visible = (dist >= 0) & (dist < window)
            bias = jnp.full(dist.shape, tbl_ref[h], F32)
            for k in range(1, REL_BUCKETS):
                bias = jnp.where(dist >= thr_ref[k], tbl_ref[k * N_HEADS + h], bias)
            out_ref[0, pl.ds(row0, BIAS_ROWS), :] = jnp.where(visible, bias, NEG_INF)
            return carry

        lax.fori_loop(0, tq // BIAS_ROWS, chunk, 0)

    table(near_ref, SLC_NEAR_BACK, 1 << 30)
    table(win_ref, WINDOW, WINDOW)
    far_ref[0] = jnp.full(far_ref.shape[1:], tbl_ref[(REL_BUCKETS - 1) * N_HEADS + h], F32)


def _attention_bias_tables(rel_bias, seq):
    tbl = rel_bias.astype(F32).reshape(REL_BUCKETS * N_HEADS)
    thr = _bucket_thresholds()
    tq = ATTN_TQ
    n_c = seq // CMP_STRIDE
    smem = pl.BlockSpec(memory_space=pltpu.SMEM)
    tr = min(512, seq)
    bias_c = pl.pallas_call(
        _bias_c_kernel,
        grid=(seq // tr,),
        in_specs=[smem, smem],
        out_specs=pl.BlockSpec((N_HEADS, tr, n_c), lambda i: (0, i, 0)),
        out_shape=jax.ShapeDtypeStruct((N_HEADS, seq, n_c), F32),
        compiler_params=_compiler_params(("parallel",)),
        name="bias_cmp",
    )(thr, tbl)
    head_block = lambda w: pl.BlockSpec((1, tq, w), lambda h: (h, 0, 0))
    widths = (SLC_NEAR_BACK + tq, WINDOW + tq, LANES)
    bias_near, bias_win, bias_far = pl.pallas_call(
        _bias_near_kernel,
        grid=(N_HEADS,),
        in_specs=[smem, smem],
        out_specs=[head_block(w) for w in widths],
        out_shape=[jax.ShapeDtypeStruct((N_HEADS, tq, w), F32) for w in widths],
        compiler_params=_compiler_params(("parallel",)),
        name="bias_near",
    )(thr, tbl)
    return bias_c, bias_near, bias_win, bias_far


def _nsa_constants(seq):
    n_c = seq // CMP_STRIDE
    n_blk = seq // SLC_BLOCK
    cmp_start = jnp.arange(n_c) * CMP_STRIDE
    blk_start = jnp.arange(n_blk) * SLC_BLOCK
    overlap_t = ((cmp_start[None, :] <= blk_start[:, None] + SLC_BLOCK - 1)
                 & (cmp_start[None, :] + CMP_BLOCK - 1 >= blk_start[:, None]))
    overlap_t = overlap_t & (cmp_start[None, :] + CMP_BLOCK <= seq)
    ones_rows = jnp.arange(SUBLANES)[:, None] == 0
    overlap_t = jnp.concatenate([overlap_t, jnp.broadcast_to(ones_rows, (SUBLANES, n_c))], axis=0)
    cmp_ones = jnp.broadcast_to(jnp.arange(LANES)[None, :] == 0, (n_c, LANES))
    return overlap_t.astype(BF16), cmp_ones.astype(BF16)


def _build_kv_scratch(seq, slc_ref, win_ref, g, ks_sc, vs_sc, kw_sc, vw_sc):
    chunk = min(512, seq)
    lane_p = lax.broadcasted_iota(jnp.int32, (KV_PAD, LANES), 1)
    zeros = jnp.zeros((KV_PAD, LANES), BF16)
    ks_sc[0:KV_PAD] = jnp.where(lane_p >= HEAD_DIM, 1.0, 0.0).astype(BF16)
    kw_sc[0:KV_PAD] = jnp.where(lane_p == HEAD_DIM, NEG_INF, 0.0).astype(BF16)
    vs_sc[0:KV_PAD] = zeros
    vw_sc[0:KV_PAD] = zeros
    lane = lax.broadcasted_iota(jnp.int32, (chunk, LANES), 1)
    row = lax.broadcasted_iota(jnp.int32, (chunk, LANES), 0)
    lo_half = lane < HEAD_DIM
    ones_lane = jnp.where(lane == HEAD_DIM, 1.0, 0.0)
    for c in range(seq // chunk):
        r0 = c * chunk
        dst = slice(KV_PAD + r0, KV_PAD + r0 + chunk)
        blk = jnp.right_shift(r0 + row, int(math.log2(SLC_BLOCK)))
        lanes_g = slice(g * LANES, (g + 1) * LANES)
        slab = slc_ref[0, r0:r0 + chunk, lanes_g].astype(F32)
        ks_sc[dst] = jnp.where(lo_half, slab, jnp.where(lane - HEAD_DIM == blk, 1.0, 0.0)).astype(BF16)
        vs_sc[dst] = jnp.where(lo_half, pltpu.roll(slab, HEAD_DIM, 1), ones_lane).astype(BF16)
        slab = win_ref[0, r0:r0 + chunk, lanes_g].astype(F32)
        kw_sc[dst] = jnp.where(lo_half, slab, 0.0).astype(BF16)
        vw_sc[dst] = jnp.where(lo_half, pltpu.roll(slab, HEAD_DIM, 1), ones_lane).astype(BF16)


def _nsa_select(seq, g, q0, q_ref, vkc_ref, bias_c_ref, overlap_t_ref, cmp_ones_ref):
    tq = ATTN_TQ
    n_blk = seq // SLC_BLOCK
    n_sel = min(N_SEL, n_blk)
    rows = HPG * tq
    lane = lax.broadcasted_iota(jnp.int32, (tq, LANES), 1)
    lo_half = lane < HEAD_DIM
    lane_r = lax.broadcasted_iota(jnp.int32, (rows, LANES), 1)

    q_lo, q_hi = [], []
    for pair_idx in range(HPG // 2):
        lanes_p = slice((g * (HPG // 2) + pair_idx) * LANES, (g * (HPG // 2) + pair_idx + 1) * LANES)
        q2 = q_ref[0, :, lanes_p].astype(F32)
        q2r = pltpu.roll(q2, HEAD_DIM, 1)
        q_lo += [jnp.where(lo_half, q2, 0.0), jnp.where(lo_half, q2r, 0.0)]
        q_hi += [jnp.where(lo_half, 0.0, q2r), jnp.where(lo_half, 0.0, q2)]
    q_lo = jnp.concatenate(q_lo, axis=0)
    q_hi = jnp.concatenate(q_hi, axis=0)

    vkc = vkc_ref[0, g]
    s_c = _dot_nt(q_hi.astype(BF16), vkc) + bias_c_ref[g * HPG:(g + 1) * HPG].reshape(rows, -1)
    m_c = jnp.max(s_c, axis=1, keepdims=True)
    e_cb = jnp.exp(s_c - m_c).astype(BF16)
    pv_c = _dot(e_cb, jnp.concatenate([vkc, cmp_ones_ref[...]], axis=1))
    row_pos = q0 + (lax.broadcasted_iota(jnp.int32, (rows, LANES), 0) & (tq - 1))
    has_key = row_pos >= CMP_BLOCK - 1
    o_cmp = jnp.where(has_key, pv_c[:, :LANES] / pv_c[:, LANES:LANES + 1], 0.0)

    imp_t4 = _dot_nt(overlap_t_ref[...], e_cb)
    imp_t = None
    for h in range(HPG):
        part = imp_t4[:, h * tq:(h + 1) * tq]
        part = part[:n_blk] / part[n_blk:n_blk + 1]
        imp_t = part if imp_t is None else imp_t + part

    blk = lax.broadcasted_iota(jnp.int32, (n_blk, tq), 0)
    pos = q0 + lax.broadcasted_iota(jnp.int32, (n_blk, tq), 1)
    cur = jnp.right_shift(pos, int(math.log2(SLC_BLOCK)))
    forced = (blk == 0) | (blk == cur) | (blk == cur - 1)
    valid = blk * SLC_BLOCK <= pos
    score = jnp.where(forced, BIG, jnp.where(valid, imp_t, -BIG))
    blk_f = blk.astype(F32)
    pen_t = jnp.full((n_blk, tq), NEG_INF, F32)
    for _ in range(n_sel):
        top = jnp.max(score, axis=0, keepdims=True)
        first = jnp.min(jnp.where(score == top, blk_f, float(n_blk)), axis=0, keepdims=True)
        hit = blk_f == first
        pen_t = jnp.where(hit, 0.0, pen_t)
        score = jnp.where(hit, -jnp.inf, score)
    pieces = [jnp.zeros((tq, HEAD_DIM), F32), jnp.transpose(pen_t)]
    if n_blk < LANES - HEAD_DIM:
        pieces.append(jnp.zeros((tq, LANES - HEAD_DIM - n_blk), F32))
    pen_lanes = jnp.concatenate(pieces, axis=1)
    q_slc = jnp.where(lane_r < HEAD_DIM, q_lo, jnp.concatenate([pen_lanes] * HPG, axis=0)).astype(BF16)
    return o_cmp, q_lo, q_slc


def _nsa_kernel(seq, q_ref, vkc_ref, slc_ref, win_ref, gate_ref, bias_c_ref, bias_near_ref, bias_win_ref,
                bias_far_ref, overlap_t_ref, cmp_ones_ref, o_ref, *scratch):
    tq = ATTN_TQ
    rows = HPG * tq
    groups = range(N_KV)
    kv_sc = [scratch[4 * g:4 * g + 4] for g in groups]
    s_sc, mrun_sc, acc_sc = (scratch[4 * N_KV + i * N_KV:4 * N_KV + (i + 1) * N_KV] for i in range(3))
    qt = pl.program_id(1)
    q0 = pl.multiple_of(qt * tq, tq)
    lane = lax.broadcasted_iota(jnp.int32, (tq, LANES), 1)
    lo_half = lane < HEAD_DIM
    lane_r = lax.broadcasted_iota(jnp.int32, (rows, LANES), 1)

    @pl.when(qt == 0)
    def _():
        for g in groups:
            _build_kv_scratch(seq, slc_ref, win_ref, g, *kv_sc[g])

    sel = [_nsa_select(seq, g, q0, q_ref, vkc_ref, bias_c_ref, overlap_t_ref, cmp_ones_ref) for g in groups]
    o_cmp = [s[0] for s in sel]
    q_lo = [s[1] for s in sel]
    q_slc = [s[2] for s in sel]
    head_rows = lambda ref, g: ref[g * HPG:(g + 1) * HPG].reshape(rows, -1)

    n_far = qt // (SLC_FAR_TK // tq)
    near0 = pl.multiple_of(q0 + (KV_PAD - SLC_NEAR_BACK), tq)
    near_rows = pl.ds(near0, SLC_NEAR_BACK + tq)

    def far_rows(t):
        return pl.ds(pl.multiple_of(near0 - (t + 1) * SLC_FAR_TK, tq), SLC_FAR_TK)

    def far_cols(t):
        return pl.ds(pl.multiple_of(t * SLC_FAR_TK, SLC_FAR_TK), SLC_FAR_TK)

    for g in groups:
        mrun_sc[g][...] = jnp.full((rows, LANES), NEG_INF, F32)

    def pass1(t, carry):
        for g in groups:
            s = _dot_nt(q_slc[g], kv_sc[g][0][far_rows(t), :])
            s_sc[g][:, far_cols(t)] = s
            mrun_sc[g][...] = jnp.maximum(mrun_sc[g][...], jnp.maximum(s[:, :LANES], s[:, LANES:]))
        return carry

    lax.fori_loop(0, n_far, pass1, 0)
    for g in groups:
        ks_sc, vs_sc = kv_sc[g][0], kv_sc[g][1]
        c_far = head_rows(bias_far_ref, g)
        s_n = _dot_nt(q_slc[g], ks_sc[near_rows, :]) + head_rows(bias_near_ref, g)
        m_s = jnp.maximum(jnp.max(mrun_sc[g][...], axis=1, keepdims=True) + c_far[:, :1],
                          jnp.max(s_n, axis=1, keepdims=True))
        acc_sc[g][...] = _dot(jnp.exp(s_n - m_s).astype(BF16), vs_sc[near_rows, :])
        mrun_sc[g][...] = m_s - c_far

    def pass2(t, carry):
        for g in groups:
            shift = mrun_sc[g][...]
            p = jnp.exp(s_sc[g][:, far_cols(t)] - jnp.concatenate([shift, shift], axis=1))
            acc_sc[g][...] = acc_sc[g][...] + _dot(p.astype(BF16), kv_sc[g][1][far_rows(t), :])
        return carry

    lax.fori_loop(0, n_far, pass2, 0)

    for g in groups:
        acc = acc_sc[g][...]
        o_slc = acc / acc[:, HEAD_DIM:HEAD_DIM + 1]

        kw_sc, vw_sc = kv_sc[g][2], kv_sc[g][3]
        q_win = jnp.where(lane_r == HEAD_DIM, 1.0, q_lo[g]).astype(BF16)
        win_rows = pl.ds(q0, WINDOW + tq)
        s_w = _dot_nt(q_win, kw_sc[win_rows, :]) + head_rows(bias_win_ref, g)
        p_w = jnp.exp(s_w - jnp.max(s_w, axis=1, keepdims=True)).astype(BF16)
        acc_w = _dot(p_w, vw_sc[win_rows, :])
        o_win = acc_w / acc_w[:, HEAD_DIM:HEAD_DIM + 1]

        gates = gate_ref[0, :, g * GATE_PAD:(g + 1) * GATE_PAD]
        outs = []
        for h in range(HPG):
            sl = slice(h * tq, (h + 1) * tq)
            g_c, g_s, g_w = (gates[:, br * HPG + h:br * HPG + h + 1] for br in range(3))
            outs.append(g_c * o_cmp[g][sl] + g_s * o_slc[sl] + g_w * o_win[sl])
        for pair_idx in range(HPG // 2):
            even, odd = outs[2 * pair_idx], outs[2 * pair_idx + 1]
            merged = jnp.where(lo_half, even, pltpu.roll(odd, HEAD_DIM, 1))
            lanes_p = slice((g * (HPG // 2) + pair_idx) * LANES, (g * (HPG // 2) + pair_idx + 1) * LANES)
            o_ref[0, :, lanes_p] = merged.astype(o_ref.dtype)


def _nsa(q, vkc, slc, win, gates, bias_c, bias_near, bias_win, bias_far, overlap_t, cmp_ones):
    b, seq, _ = q.shape
    n_c = seq // CMP_STRIDE
    n_blk = seq // SLC_BLOCK
    assert n_blk <= LANES - HEAD_DIM and seq % SLC_FAR_TK == 0
    tq = ATTN_TQ
    rows = HPG * tq
    const = lambda a: pl.BlockSpec(a.shape, lambda i, t: (0,) * a.ndim)
    tile = lambda w: pl.BlockSpec((1, tq, w), lambda i, t: (i, t, 0))
    whole_seq = lambda w: pl.BlockSpec((1, seq, w), lambda i, t: (i, 0, 0))
    per_group = lambda shape: [pltpu.VMEM(shape, F32)] * N_KV
    return pl.pallas_call(
        functools.partial(_nsa_kernel, seq),
        grid=(b, seq // tq),
        in_specs=[
            tile(ATTN_WIDTH),
            pl.BlockSpec((1, N_KV, n_c, 2 * HEAD_DIM), lambda i, t: (i, 0, 0, 0)),
            whole_seq(N_KV * 2 * HEAD_DIM), whole_seq(N_KV * 2 * HEAD_DIM),
            tile(N_KV * GATE_PAD),
            pl.BlockSpec((N_HEADS, tq, n_c), lambda i, t: (0, t, 0)),
            const(bias_near), const(bias_win), const(bias_far),
            const(overlap_t), const(cmp_ones),
        ],
        out_specs=tile(ATTN_WIDTH),
        out_shape=jax.ShapeDtypeStruct((b, seq, ATTN_WIDTH), BF16),
        scratch_shapes=(
            [pltpu.VMEM((KV_PAD + seq, LANES), BF16)] * (4 * N_KV)
            + per_group((rows, seq)) + per_group((rows, LANES)) + per_group((rows, LANES))
        ),
        compiler_params=_compiler_params(("parallel", "arbitrary")),
        name="nsa",
    )(q, vkc, slc, win, gates, bias_c, bias_near, bias_win, bias_far, overlap_t, cmp_ones)


def _mixer_inputs(x, rel_bias, w_in, b_in, cmp_pos, cmp_w1, cmp_b1, cmp_w2, cmp_b2):
    b, seq, d = x.shape
    wp, bp = _pack_in_proj(w_in, b_in, d)
    q, slc, win, cmp, gates, u, merge = _in_proj(x.reshape(b * seq, d), wp, bp, d, 512)
    n_c = seq // CMP_STRIDE
    cmp4 = cmp.reshape(b, seq, 2 * N_KV, HEAD_DIM).transpose(0, 2, 1, 3).reshape(
        b, 2 * N_KV, n_c, CMP_STRIDE * HEAD_DIM)
    vkc = _compress(cmp4, cmp_pos, cmp_w1, cmp_b1, cmp_w2, cmp_b2)
    bias_c, bias_near, bias_win, bias_far = _attention_bias_tables(rel_bias, seq)
    overlap_t, cmp_ones = _nsa_constants(seq)
    o = _nsa(q.reshape(b, seq, -1), vkc, slc.reshape(b, seq, -1), win.reshape(b, seq, -1),
             gates.reshape(b, seq, -1), bias_c, bias_near, bias_win, bias_far, overlap_t, cmp_ones)
    return o, u, merge


S5_HALF_GROUPS = S5_GROUPS // 2
S5_HALF_IN = S5_HALF_GROUPS * S5_GROUP
S5_HALF_STATE = S5_HALF_GROUPS * S5_STATE
S5_SCAN_LANES = 512
S5_CHUNK = 64
S5_UNROLL = 8


def _s5_params(lam_re, lam_im, log_dt, b_re, b_im, c_re, c_im, nb):
    dt = jnp.exp(log_dt.astype(F32))[:, None]
    lr, li = lam_re.astype(F32), lam_im.astype(F32)
    mag = jnp.exp(lr * dt)
    ab_re, ab_im = mag * jnp.cos(li * dt), mag * jnp.sin(li * dt)
    nr, ni = ab_re - 1.0, ab_im
    den = lr * lr + li * li
    fr, fi = (nr * lr + ni * li) / den, (ni * lr - nr * li) / den
    br, bim = b_re.astype(F32), b_im.astype(F32)
    bb_re = fr[..., None] * br - fi[..., None] * bim
    bb_im = fr[..., None] * bim + fi[..., None] * br
    eye = jnp.eye(S5_HALF_GROUPS, dtype=F32)

    def in_mat(bb):
        t = bb.reshape(2, S5_HALF_GROUPS, S5_STATE, S5_GROUP)
        m = jnp.einsum('kgph,gj->kghjp', t, eye)
        return m.reshape(2, S5_HALF_IN, S5_HALF_STATE)

    def out_mat(c):
        t = c.astype(F32).reshape(2, S5_HALF_GROUPS, S5_GROUP, S5_STATE)
        m = jnp.einsum('kghp,gj->kgpjh', t, eye)
        return m.reshape(2, S5_HALF_STATE, S5_HALF_IN)

    bmat = jnp.concatenate([in_mat(bb_re), in_mat(bb_im)], axis=2).astype(BF16)
    cmat = jnp.concatenate([out_mat(c_re), -out_mat(c_im)], axis=1).astype(BF16)
    a = jnp.concatenate([ab_re.reshape(2, S5_HALF_STATE), ab_im.reshape(2, S5_HALF_STATE)], axis=1)
    a = jnp.broadcast_to(a.reshape(1, 4 * S5_HALF_STATE), (nb, 4 * S5_HALF_STATE))
    return bmat, cmat, a


def _s5_kernel(u_ref, bmat_ref, cmat_ref, a_ref, d_ref, y_ref, ut_sc, x_sc, st_sc):
    nb, t_len, _ = u_ref.shape
    half_w = 2 * S5_HALF_STATE

    @pl.when(pl.program_id(0) == 0)
    def _():
        st_sc[...] = jnp.zeros_like(st_sc)

    n_cb = ut_sc.shape[0]
    for b in range(nb):
        for cb in range(n_cb):
            ut_sc[cb, pl.ds(b, t_len, stride=nb), :] = u_ref[b, :, cb * LANES:(cb + 1) * LANES]
    ut = jnp.concatenate([ut_sc[cb] for cb in range(n_cb)], axis=1)
    ub = ut.astype(BF16)
    for k in range(2):
        x_sc[:, k * half_w:(k + 1) * half_w] = _dot(ub[:, k * S5_HALF_IN:(k + 1) * S5_HALF_IN], bmat_ref[k])

    for k in range(2):
        for j in range(S5_HALF_STATE // S5_SCAN_LANES):
            re0 = k * half_w + j * S5_SCAN_LANES
            im0 = re0 + S5_HALF_STATE
            re_sl, im_sl = pl.ds(re0, S5_SCAN_LANES), pl.ds(im0, S5_SCAN_LANES)
            ar, ai = a_ref[:, re_sl], a_ref[:, im_sl]

            def steps(c, carry):
                xr, xi = carry
                for s in range(S5_UNROLL):
                    rows = pl.ds(pl.multiple_of((c * S5_UNROLL + s) * nb, nb), nb)
                    nxr = ar * xr - ai * xi + x_sc[rows, re_sl]
                    nxi = ar * xi + ai * xr + x_sc[rows, im_sl]
                    x_sc[rows, re_sl] = nxr
                    x_sc[rows, im_sl] = nxi
                    xr, xi = nxr, nxi
                return xr, xi

            xr, xi = lax.fori_loop(0, t_len // S5_UNROLL, steps, (st_sc[:, re_sl], st_sc[:, im_sl]))
            st_sc[:, re_sl] = xr
            st_sc[:, im_sl] = xi

    xs = x_sc[...].astype(BF16)
    y = jnp.concatenate([_dot(xs[:, k * half_w:(k + 1) * half_w], cmat_ref[k]) for k in range(2)], axis=1)
    y = y + d_ref[...] * ut
    for cb in range(n_cb):
        ut_sc[cb] = y[:, cb * LANES:(cb + 1) * LANES]
    for b in range(nb):
        for cb in range(n_cb):
            y_ref[b, :, cb * LANES:(cb + 1) * LANES] = ut_sc[cb, pl.ds(b, t_len, stride=nb), :]


def _s5(u, bmat, cmat, a, d_skip):
    nb, seq, w = u.shape
    t_len = min(S5_CHUNK, seq)
    full = lambda shape: pl.BlockSpec(shape, lambda c: (0,) * len(shape))
    return pl.pallas_call(
        _s5_kernel,
        grid=(seq // t_len,),
        in_specs=[
            pl.BlockSpec((nb, t_len, w), lambda c: (0, c, 0)),
            full(bmat.shape), full(cmat.shape), full(a.shape), full((1, w)),
        ],
        out_specs=pl.BlockSpec((nb, t_len, w), lambda c: (0, c, 0)),
        out_shape=jax.ShapeDtypeStruct((nb, seq, w), F32),
        scratch_shapes=[
            pltpu.VMEM((w // LANES, t_len * nb, LANES), F32),
            pltpu.VMEM((t_len * nb, 4 * S5_HALF_STATE), F32),
            pltpu.VMEM((nb, 4 * S5_HALF_STATE), F32),
        ],
        compiler_params=_compiler_params(("arbitrary",)),
        name="s5",
    )(u, bmat, cmat, a, d_skip.reshape(1, w).astype(F32))


ROUTE_PAD = LANES
_R_E1, _R_E2, _R_W1, _R_W2, _R_RANK1, _R_RANK2 = range(6)


def _layer_norm(t, g, b):
    mu = jnp.mean(t, axis=1, keepdims=True)
    c = t - mu
    var = jnp.mean(c * c, axis=1, keepdims=True)
    return c * lax.rsqrt(var + LN_EPS) * g + b


def _post_kernel(x_ref, o_ref, y_ref, m_ref, wup_ref, wval_ref, wgate_ref, bgate_ref, wout_ref,
                 g1_ref, b1_ref, wr_ref, br_ref, h_ref, route_ref, cnt_ref, run_sc):
    tm, d = x_ref.shape

    @pl.when(pl.program_id(0) == 0)
    def _():
        run_sc[...] = jnp.zeros_like(run_sc)

    y_a = _dot(o_ref[...], wup_ref[...])
    z = _gelu_tanh(y_ref[...]).astype(BF16)
    y_b = _dot(z, wval_ref[...]) * jax.nn.sigmoid(_dot(z, wgate_ref[...]) + bgate_ref[...])
    mixed = m_ref[:, :d] * y_a + m_ref[:, d:] * y_b
    t = DN_ALPHA * x_ref[...] + _dot(mixed.astype(BF16), wout_ref[...])
    h = _layer_norm(t, g1_ref[...], b1_ref[...])
    h_ref[...] = h

    logits = _dot(h.astype(BF16), wr_ref[...]) + br_ref[...]
    lane = lax.broadcasted_iota(jnp.int32, (tm, ROUTE_PAD), 1)
    lane_f = lane.astype(F32)
    is_group = lane < N_EGROUPS

    def first_max(v):
        top = jnp.max(v, axis=1, keepdims=True)
        idx = jnp.min(jnp.where(v == top, lane_f, float(ROUTE_PAD)), axis=1, keepdims=True)
        return top, idx

    g_max, g_top = first_max(jnp.where(is_group, logits, -jnp.inf))
    p_group = 1.0 / jnp.sum(jnp.where(is_group, jnp.exp(logits - g_max), 0.0), axis=1, keepdims=True)
    grp_of_lane = jnp.right_shift(lane - N_EGROUPS, int(math.log2(EXPERTS_PER_GROUP))).astype(F32)
    in_group = (lane >= N_EGROUPS) & (lane < N_EGROUPS + N_EXPERTS) & (grp_of_lane == g_top)
    e_log = jnp.where(in_group, logits, -jnp.inf)
    v1, i1 = first_max(e_log)
    hit1 = lane_f == i1
    v2, i2 = first_max(jnp.where(hit1, -jnp.inf, e_log))
    hit2 = lane_f == i2
    e2 = jnp.exp(v2 - v1)
    w1 = p_group / (1.0 + e2)
    w2 = p_group * e2 / (1.0 + e2)

    hits = jnp.where(hit1 | hit2, 1.0, 0.0)
    row = lax.broadcasted_iota(jnp.int32, (tm, tm), 0)
    col = lax.broadcasted_iota(jnp.int32, (tm, tm), 1)
    earlier = jnp.where(col < row, 1.0, 0.0).astype(BF16)
    before = _dot(earlier, hits.astype(BF16)) + run_sc[...]
    rank1 = jnp.sum(jnp.where(hit1, before, 0.0), axis=1, keepdims=True)
    rank2 = jnp.sum(jnp.where(hit2, before, 0.0), axis=1, keepdims=True)
    run_sc[...] = run_sc[...] + jnp.sum(hits, axis=0, keepdims=True)
    cnt_ref[...] = run_sc[...]

    rec = jnp.zeros((tm, ROUTE_PAD), F32)
    for slot, val in ((_R_E1, i1 - N_EGROUPS), (_R_E2, i2 - N_EGROUPS), (_R_W1, w1), (_R_W2, w2),
                      (_R_RANK1, rank1), (_R_RANK2, rank2)):
        rec = jnp.where(lane == slot, val, rec)
    route_ref[...] = rec


def _post(x2d, o2d, y2d, merge, w_attn_up, s5_w_val, s5_w_gate, s5_b_gate, w_out, ln1_g, ln1_b,
          router_w_group, router_b_group, router_w_expert, router_b_expert, tm):
    n, d = x2d.shape
    rpad = ROUTE_PAD - N_EGROUPS - N_EXPERTS
    wr = jnp.concatenate([router_w_group, router_w_expert, jnp.zeros((d, rpad), F32)], axis=1).astype(BF16)
    br = jnp.concatenate([router_b_group, router_b_expert, jnp.zeros((rpad,), F32)]).reshape(1, -1).astype(F32)
    row = lambda w: pl.BlockSpec((tm, w), lambda i: (i, 0))
    full = lambda a: pl.BlockSpec(a.shape, lambda i: (0,) * a.ndim)
    weights = [w_attn_up.astype(BF16), s5_w_val.astype(BF16), s5_w_gate.astype(BF16),
               s5_b_gate.reshape(1, d).astype(F32), w_out.astype(BF16),
               ln1_g.reshape(1, d).astype(F32), ln1_b.reshape(1, d).astype(F32), wr, br]
    return pl.pallas_call(
        _post_kernel,
        grid=(n // tm,),
        in_specs=[row(d), row(ATTN_WIDTH), row(S5_WIDTH), row(2 * d)] + [full(w) for w in weights],
        out_specs=[row(d), row(ROUTE_PAD), pl.BlockSpec((1, ROUTE_PAD), lambda i: (0, 0))],
        out_shape=[jax.ShapeDtypeStruct((n, d), F32), jax.ShapeDtypeStruct((n, ROUTE_PAD), F32),
                   jax.ShapeDtypeStruct((1, ROUTE_PAD), F32)],
        scratch_shapes=[pltpu.VMEM((1, ROUTE_PAD), F32)],
        compiler_params=_compiler_params(("arbitrary",)),
        name="post_mixer",
    )(x2d, o2d, y2d, merge, *weights)


def _plan_kernel(route_ref, cnt_ref, dest_ref):
    tm = route_ref.shape[0]
    lane8 = lax.broadcasted_iota(jnp.int32, (SUBLANES, ROUTE_PAD), 1)
    counts = jnp.broadcast_to(cnt_ref[...], (SUBLANES, ROUTE_PAD)).astype(jnp.int32)
    shift = int(math.log2(EXPERT_BLOCK))
    padded = jnp.left_shift(jnp.right_shift(counts + (EXPERT_BLOCK - 1), shift), shift)
    incl = padded
    step = 1
    while step < ROUTE_PAD:
        incl = incl + jnp.where(lane8 >= step, pltpu.roll(incl, step, 1), 0)
        step *= 2
    pstart = (incl - padded)[0:1].astype(F32)
    route = route_ref[...]
    expert_of_lane = (lax.broadcasted_iota(jnp.int32, (tm, ROUTE_PAD), 1) - N_EGROUPS).astype(F32)
    lane = lax.broadcasted_iota(jnp.int32, (tm, ROUTE_PAD), 1)

    def dest(e_slot, rank_slot):
        hit = expert_of_lane == route[:, e_slot:e_slot + 1]
        return jnp.sum(jnp.where(hit, pstart, 0.0), axis=1, keepdims=True) + route[:, rank_slot:rank_slot + 1]

    d1 = dest(_R_E1, _R_RANK1)
    d2 = dest(_R_E2, _R_RANK2)
    dest_ref[...] = jnp.where(lane == 0, d1, jnp.where(lane == 1, d2, 0.0)).astype(jnp.int32)


def _plan(route, counts_row):
    n = route.shape[0]
    tm = min(1024, n)
    return pl.pallas_call(
        _plan_kernel,
        grid=(n // tm,),
        in_specs=[pl.BlockSpec((tm, ROUTE_PAD), lambda i: (i, 0)),
                  pl.BlockSpec((1, ROUTE_PAD), lambda i: (0, 0))],
        out_specs=pl.BlockSpec((tm, ROUTE_PAD), lambda i: (i, 0)),
        out_shape=jax.ShapeDtypeStruct((n, ROUTE_PAD), jnp.int32),
        compiler_params=_compiler_params(("parallel",)),
        name="moe_plan",
    )(route, counts_row)


SC_GATHER_ROWS = 32


def _sc_row_gather(table, idx):
    n_idx = idx.shape[0]
    d = table.shape[1]
    info = plsc.get_sparse_core_info()
    n_workers = info.num_cores * info.num_subcores
    per_worker = n_idx // n_workers
    assert n_idx % (n_workers * SC_GATHER_ROWS) == 0
    mesh = plsc.VectorSubcoreMesh(core_axis_name="c", subcore_axis_name="s")

    @functools.partial(
        pl.kernel, mesh=mesh,
        out_type=jax.ShapeDtypeStruct((n_idx, d), table.dtype),
        scratch_types=[
            pltpu.VMEM((SC_GATHER_ROWS,), jnp.int32),
            pltpu.VMEM((SC_GATHER_ROWS, d), table.dtype),
            pltpu.SemaphoreType.DMA,
        ],
    )
    def gather(table_hbm, idx_hbm, out_hbm, idx_v, rows_v, sem):
        worker = lax.axis_index("s") * info.num_cores + lax.axis_index("c")
        base = worker * per_worker

        @pl.loop(0, per_worker // SC_GATHER_ROWS)
        def _(j):
            off = base + j * SC_GATHER_ROWS
            pltpu.sync_copy(idx_hbm.at[pl.ds(off, SC_GATHER_ROWS)], idx_v)
            pltpu.async_copy(table_hbm.at[idx_v], rows_v, sem).wait()
            pltpu.sync_copy(rows_v, out_hbm.at[pl.ds(off, SC_GATHER_ROWS)])

    return gather(table, idx)


def _sc_row_scatter(rows, idx_a, idx_b, n_out):
    n, d = rows.shape
    info = plsc.get_sparse_core_info()
    n_workers = info.num_cores * info.num_subcores
    per_worker = n // n_workers
    assert n % (n_workers * SC_GATHER_ROWS) == 0
    mesh = plsc.VectorSubcoreMesh(core_axis_name="c", subcore_axis_name="s")

    @functools.partial(
        pl.kernel, mesh=mesh,
        out_type=jax.ShapeDtypeStruct((n_out, d), rows.dtype),
        scratch_types=[
            pltpu.VMEM((SC_GATHER_ROWS,), jnp.int32),
            pltpu.VMEM((SC_GATHER_ROWS,), jnp.int32),
            pltpu.VMEM((SC_GATHER_ROWS, d), rows.dtype),
        ],
    )
    def scatter(rows_hbm, idx_a_hbm, idx_b_hbm, out_hbm, idx_a_v, idx_b_v, rows_v):
        worker = lax.axis_index("s") * info.num_cores + lax.axis_index("c")
        base = worker * per_worker

        @pl.loop(0, per_worker // SC_GATHER_ROWS)
        def _(j):
            src = pl.ds(base + j * SC_GATHER_ROWS, SC_GATHER_ROWS)
            pltpu.sync_copy(rows_hbm.at[src], rows_v)
            pltpu.sync_copy(idx_a_hbm.at[src], idx_a_v)
            pltpu.sync_copy(idx_b_hbm.at[src], idx_b_v)
            pltpu.sync_copy(rows_v, out_hbm.at[idx_a_v])
            pltpu.sync_copy(rows_v, out_hbm.at[idx_b_v])

    return scatter(rows, idx_a, idx_b)


def _expert_kernel(blk_exp_ref, blk_valid_ref, x_ref, wg_ref, wu_ref, wd_ref, y_ref, wg_sc, wu_sc, wd_sc):
    i = pl.program_id(0)
    n_valid = blk_valid_ref[i]

    @pl.when((i == 0) | (blk_exp_ref[i] != blk_exp_ref[jnp.maximum(i - 1, 0)]))
    def _():
        wg_sc[...] = wg_ref[0].astype(BF16)
        wu_sc[...] = wu_ref[0].astype(BF16)
        wd_sc[...] = wd_ref[0].astype(BF16)

    @pl.when(n_valid > 0)
    def _():
        row = lax.broadcasted_iota(jnp.int32, x_ref.shape, 0)
        xb = jnp.where(row < n_valid, x_ref[...], 0.0).astype(BF16)
        h_gate = _dot(xb, wg_sc[...])
        h_up = _dot(xb, wu_sc[...])
        hb = (h_gate * jax.nn.sigmoid(h_gate) * h_up).astype(BF16)
        y_ref[...] = _dot(hb, wd_sc[...])

    @pl.when(n_valid == 0)
    def _():
        y_ref[...] = jnp.zeros_like(y_ref)


def _experts(blk_expert, blk_valid, xs, w_gate, w_up, w_down):
    n_blocks = blk_expert.shape[0]
    d = xs.shape[1]
    grid_spec = pltpu.PrefetchScalarGridSpec(
        num_scalar_prefetch=2,
        grid=(n_blocks,),
        in_specs=[
            pl.BlockSpec((EXPERT_BLOCK, d), lambda i, be, bv: (i, 0)),
            pl.BlockSpec((1, d, D_EXPERT), lambda i, be, bv: (be[i], 0, 0)),
            pl.BlockSpec((1, d, D_EXPERT), lambda i, be, bv: (be[i], 0, 0)),
            pl.BlockSpec((1, D_EXPERT, d), lambda i, be, bv: (be[i], 0, 0)),
        ],
        out_specs=pl.BlockSpec((EXPERT_BLOCK, d), lambda i, be, bv: (i, 0)),
        scratch_shapes=[
            pltpu.VMEM((d, D_EXPERT), BF16),
            pltpu.VMEM((d, D_EXPERT), BF16),
            pltpu.VMEM((D_EXPERT, d), BF16),
        ],
    )
    return pl.pallas_call(
        _expert_kernel,
        grid_spec=grid_spec,
        out_shape=jax.ShapeDtypeStruct((n_blocks * EXPERT_BLOCK, d), F32),
        compiler_params=_compiler_params(("arbitrary",)),
        name="experts",
    )(blk_expert, blk_valid, xs, w_gate, w_up, w_down)


COMBINE_TM = 512


def _combine_kernel(h_ref, y1_ref, y2_ref, route_ref, g2_ref, b2_ref, out_ref):
    route = route_ref[...]
    w1 = route[:, _R_W1:_R_W1 + 1]
    w2 = route[:, _R_W2:_R_W2 + 1]
    t = DN_ALPHA * h_ref[...] + (y1_ref[...] * w1 + y2_ref[...] * w2)
    out_ref[...] = _layer_norm(t, g2_ref[...], b2_ref[...])


def _combine(yg, h2d, route, ln2_g, ln2_b):
    n, d = h2d.shape
    tm = min(COMBINE_TM, n)
    n_tiles = n // tm
    row = lambda w: pl.BlockSpec((tm, w), lambda i: (i, 0))
    vec = pl.BlockSpec((1, d), lambda i: (0, 0))
    return pl.pallas_call(
        _combine_kernel,
        grid=(n_tiles,),
        in_specs=[row(d), row(d), pl.BlockSpec((tm, d), lambda i: (i + n_tiles, 0)), row(ROUTE_PAD), vec, vec],
        out_specs=row(d),
        out_shape=jax.ShapeDtypeStruct((n, d), F32),
        compiler_params=_compiler_params(("parallel",)),
        name="combine",
    )(h2d, yg, yg, route, ln2_g.reshape(1, d).astype(F32), ln2_b.reshape(1, d).astype(F32))


def _moe(h2d, route, counts_row, w_gate, w_up, w_down, ln2_g, ln2_b):
    n, d = h2d.shape
    dest = _plan(route, counts_row)
    dest1, dest2 = dest[:, 0], dest[:, 1]
    counts = counts_row[0, N_EGROUPS:N_EGROUPS + N_EXPERTS].astype(jnp.int32)
    padded = (counts + EXPERT_BLOCK - 1) // EXPERT_BLOCK * EXPERT_BLOCK
    pend = jnp.cumsum(padded)
    n_blocks = -(-(n * TOP_K_IN_GROUP) // EXPERT_BLOCK) + N_EXPERTS
    blk_row0 = jnp.arange(n_blocks, dtype=jnp.int32) * EXPERT_BLOCK
    blk_expert = jnp.minimum(jnp.sum(pend[None, :] <= blk_row0[:, None], axis=1), N_EXPERTS - 1).astype(jnp.int32)
    blk_valid = jnp.clip((pend - padded + counts)[blk_expert] - blk_row0, 0, EXPERT_BLOCK).astype(jnp.int32)
    xs = _sc_row_scatter(h2d, dest1, dest2, n_blocks * EXPERT_BLOCK)
    yb = _experts(blk_expert, blk_valid, xs, w_gate, w_up, w_down)
    yg = _sc_row_gather(yb, jnp.concatenate([dest1, dest2]))
    return _combine(yg, h2d, route, ln2_g, ln2_b)


def kernel(x, rel_bias, w_in, b_in, cmp_pos, cmp_w1, cmp_b1, cmp_w2, cmp_b2, w_attn_up, s5_lambda_re, s5_lambda_im, s5_log_dt, s5_b_re, s5_b_im, s5_c_re, s5_c_im, s5_d, s5_w_val, s5_w_gate, s5_b_gate, w_out, ln1_g, ln1_b, router_w_group, router_b_group, router_w_expert, router_b_expert, exp_w_gate, exp_w_up, exp_w_down, ln2_g, ln2_b):
    b, seq, d = x.shape
    n = b * seq
    assert w_in.shape[0] == DEPTH
    l = 0
    o, u, merge = _mixer_inputs(x, rel_bias, w_in[l], b_in[l], cmp_pos[l], cmp_w1[l], cmp_b1[l],
                                cmp_w2[l], cmp_b2[l])
    bmat, cmat, a = _s5_params(s5_lambda_re[l], s5_lambda_im[l], s5_log_dt[l], s5_b_re[l], s5_b_im[l],
                               s5_c_re[l], s5_c_im[l], b)
    y_s = _s5(u.reshape(b, seq, S5_WIDTH), bmat, cmat, a, s5_d[l])
    h2d, route, counts = _post(x.reshape(n, d), o.reshape(n, ATTN_WIDTH), y_s.reshape(n, S5_WIDTH), merge,
                               w_attn_up[l], s5_w_val[l], s5_w_gate[l], s5_b_gate[l], w_out[l], ln1_g[l],
                               ln1_b[l], router_w_group[l], router_b_group[l], router_w_expert[l],
                               router_b_expert[l], 256)
    out = _moe(h2d, route, counts, exp_w_gate[l], exp_w_up[l], exp_w_down[l], ln2_g[l], ln2_b[l])
    return out.reshape(b, seq, d)
```

```python
import functools
import math

import jax
import jax.numpy as jnp
from jax import lax
from jax.experimental import pallas as pl
from jax.experimental.pallas import tpu as pltpu
from jax.experimental.pallas import tpu_sc as plsc

F32 = jnp.float32
BF16 = jnp.bfloat16

N_HEADS = 8
HEAD_DIM = 64
N_KV = 2
HPG = N_HEADS // N_KV
CMP_STRIDE = 16
CMP_BLOCK = 2 * CMP_STRIDE
CMP_HIDDEN = 128
SLC_BLOCK = 64
N_SEL = 16
WINDOW = 512
REL_BUCKETS = 32
REL_MAX_DIST = 128
S5_WIDTH = 512
S5_GROUP = 16
S5_GROUPS = S5_WIDTH // S5_GROUP
S5_STATE = 64
N_EGROUPS = 8
EXPERTS_PER_GROUP = 8
N_EXPERTS = N_EGROUPS * EXPERTS_PER_GROUP
TOP_K_IN_GROUP = 2
D_EXPERT = 256
EXPERT_BLOCK = 128
DEPTH = 1
DN_ALPHA = (2.0 * DEPTH) ** 0.25
LN_EPS = 1e-5
NEG_INF = -1e30
BIG = 1e9
LOG2_E = math.log2(math.e)
MOE_BLOCK = 256

ATTN_WIDTH = N_HEADS * HEAD_DIM
KV_WIDTH = N_KV * HEAD_DIM
KV_OFF = ATTN_WIDTH
NSA_GATE_OFF = KV_OFF + 6 * KV_WIDTH
S5_OFF = NSA_GATE_OFF + 3 * N_HEADS
MERGE_OFF = S5_OFF + S5_WIDTH

LANES = 128
SUBLANES = 8
VMEM_LIMIT_BYTES = 56 * 1024 * 1024

ATTN_TQ = 128
SLC_FAR_TK = 256
SLC_NEAR_BACK = ATTN_TQ
KV_PAD = WINDOW
GATE_PAD = LANES


def _gelu_tanh(x):
    c = math.sqrt(2.0 / math.pi)
    return x * (0.5 * (1.0 + jnp.tanh(c * (x + 0.044715 * (x * x * x)))))


def _dot(a, b):
    return jnp.dot(a, b, preferred_element_type=F32)


def _dot_nt(a, b):
    return lax.dot_general(a, b, (((1,), (1,)), ((), ())), preferred_element_type=F32)


def _compiler_params(semantics):
    return pltpu.CompilerParams(dimension_semantics=semantics, vmem_limit_bytes=VMEM_LIMIT_BYTES)


def _in_proj_layout(d_model):
    widths = (ATTN_WIDTH, 2 * KV_WIDTH, 2 * KV_WIDTH, 2 * KV_WIDTH, N_KV * GATE_PAD, S5_WIDTH, 2 * d_model)
    offs = [0]
    for w in widths:
        offs.append(offs[-1] + w)
    return widths, offs


def _pack_in_proj(w_in, b_in, d_model):
    def kv_cols(j):
        return KV_OFF + j * KV_WIDTH

    def pair(jk, jv):
        cols = []
        for g in range(N_KV):
            cols.append(jnp.arange(kv_cols(jk) + g * HEAD_DIM, kv_cols(jk) + (g + 1) * HEAD_DIM))
            cols.append(jnp.arange(kv_cols(jv) + g * HEAD_DIM, kv_cols(jv) + (g + 1) * HEAD_DIM))
        return jnp.concatenate(cols)

    idx = jnp.concatenate([
        jnp.arange(0, ATTN_WIDTH),
        pair(2, 3),
        pair(4, 5),
        jnp.arange(kv_cols(0), kv_cols(2)),
    ])
    idx2 = jnp.concatenate([jnp.arange(S5_OFF, S5_OFF + S5_WIDTH),
                            jnp.arange(MERGE_OFF, MERGE_OFF + 2 * d_model)])
    gpad = GATE_PAD - 3 * HPG
    w_parts, b_parts = [w_in[:, idx]], [b_in[idx]]
    for g in range(N_KV):
        cols = jnp.asarray([NSA_GATE_OFF + (g * HPG + h) * 3 + j for j in range(3) for h in range(HPG)])
        w_parts += [w_in[:, cols], jnp.zeros((d_model, gpad), F32)]
        b_parts += [b_in[cols], jnp.zeros((gpad,), F32)]
    w = jnp.concatenate(w_parts + [w_in[:, idx2]], axis=1)
    b = jnp.concatenate(b_parts + [b_in[idx2]])
    return w.astype(BF16), b.reshape(1, -1).astype(F32)


def _in_proj_kernel(offs, x_ref, w_ref, b_ref, q_ref, slc_ref, win_ref, cmp_ref, g_ref, u_ref, m_ref):
    xb = x_ref[...].astype(BF16)

    def proj(i):
        c0, c1 = offs[i], offs[i + 1]
        return _dot(xb, w_ref[:, c0:c1]) + b_ref[:, c0:c1]

    q_ref[...] = (proj(0) * (HEAD_DIM ** -0.5 * LOG2_E)).astype(BF16)
    slc_ref[...] = proj(1).astype(BF16)
    win_ref[...] = proj(2).astype(BF16)
    cmp_ref[...] = proj(3)
    g_ref[...] = jax.nn.sigmoid(proj(4))
    u_ref[...] = proj(5)
    m_ref[...] = jax.nn.sigmoid(proj(6))


def _in_proj(x2d, w_packed, b_packed, d_model, tm):
    n = x2d.shape[0]
    widths, offs = _in_proj_layout(d_model)
    ncols = offs[-1]
    dtypes = (BF16, BF16, BF16, F32, F32, F32, F32)
    return pl.pallas_call(
        functools.partial(_in_proj_kernel, tuple(offs)),
        grid=(n // tm,),
        in_specs=[
            pl.BlockSpec((tm, d_model), lambda i: (i, 0)),
            pl.BlockSpec((d_model, ncols), lambda i: (0, 0)),
            pl.BlockSpec((1, ncols), lambda i: (0, 0)),
        ],
        out_specs=[pl.BlockSpec((tm, w), lambda i: (i, 0)) for w in widths],
        out_shape=[jax.ShapeDtypeStruct((n, w), dt) for w, dt in zip(widths, dtypes)],
        compiler_params=_compiler_params(("parallel",)),
        name="in_proj",
    )(x2d, w_packed, b_packed)


def _compress_kernel(ck_ref, cv_ref, pos_ref, w1_ref, b1_ref, w2_ref, b2_ref, out_ref):
    n_c = ck_ref.shape[2]
    outs = []
    for i, c_ref in enumerate((ck_ref, cv_ref)):
        c = c_ref[0, 0]
        lo = (c + pos_ref[i, 0:1, :]).astype(BF16)
        hi = (c + pos_ref[i, 1:2, :]).astype(BF16)
        p_lo = _dot(lo, w1_ref[i, 0])
        p_hi = _dot(hi, w1_ref[i, 1])
        hid = p_lo + pltpu.roll(p_hi, n_c - 1, 0) + b1_ref[i]
        hid = _gelu_tanh(hid).astype(BF16)
        outs.append(_dot(hid, w2_ref[i]) + b2_ref[i])
    out_ref[0, 0] = jnp.concatenate(outs[::-1], axis=1).astype(BF16)


def _compress(cmp4, cmp_pos, cmp_w1, cmp_b1, cmp_w2, cmp_b2):
    b, _, n_c, cw = cmp4.shape
    half = CMP_STRIDE * HEAD_DIM
    pos = cmp_pos.reshape(2, 2, half).astype(F32)
    w1 = cmp_w1.reshape(2, 2, half, CMP_HIDDEN).astype(BF16)
    b1 = cmp_b1.reshape(2, 1, CMP_HIDDEN).astype(F32)
    w2 = cmp_w2.astype(BF16)
    b2 = cmp_b2.reshape(2, 1, HEAD_DIM).astype(F32)
    full = lambda shape: pl.BlockSpec(shape, lambda i, g: (0,) * len(shape))
    return pl.pallas_call(
        _compress_kernel,
        grid=(b, N_KV),
        in_specs=[
            pl.BlockSpec((1, 1, n_c, cw), lambda i, g: (i, g, 0, 0)),
            pl.BlockSpec((1, 1, n_c, cw), lambda i, g: (i, N_KV + g, 0, 0)),
            full((2, 2, half)),
            full((2, 2, half, CMP_HIDDEN)),
            full((2, 1, CMP_HIDDEN)),
            full((2, CMP_HIDDEN, HEAD_DIM)),
            full((2, 1, HEAD_DIM)),
        ],
        out_specs=pl.BlockSpec((1, 1, n_c, 2 * HEAD_DIM), lambda i, g: (i, g, 0, 0)),
        out_shape=jax.ShapeDtypeStruct((b, N_KV, n_c, 2 * HEAD_DIM), BF16),
        compiler_params=_compiler_params(("parallel", "parallel")),
        name="compress",
    )(cmp4, cmp4, pos, w1, b1, w2, b2)


def _t5_bucket(dist):
    n = jnp.maximum(dist, 0)
    max_exact = REL_BUCKETS // 2
    nf = jnp.maximum(n, 1).astype(F32)
    large = max_exact + (jnp.log(nf / max_exact) / math.log(REL_MAX_DIST / max_exact)
                         * (REL_BUCKETS - max_exact)).astype(jnp.int32)
    large = jnp.minimum(large, REL_BUCKETS - 1)
    return jnp.where(n < max_exact, n, large)


def _bucket_thresholds():
    buckets = _t5_bucket(jnp.arange(REL_MAX_DIST + 1))
    return jnp.sum(buckets[None, :] < jnp.arange(REL_BUCKETS)[:, None], axis=1).astype(jnp.int32)


def _bias_of_dist(dist, head, thr_ref, tbl_ref):
    bias = jnp.full(dist.shape, tbl_ref[head], F32)
    for k in range(1, REL_BUCKETS):
        bias = jnp.where(dist >= thr_ref[k], tbl_ref[k * N_HEADS + head], bias)
    return bias


BIAS_ROWS = 32


def _bias_c_kernel(thr_ref, tbl_ref, out_ref):
    _, tr, n_c = out_ref.shape
    r0 = pl.program_id(0) * tr

    def chunk(ci, carry):
        row0 = pl.multiple_of(ci * BIAS_ROWS, BIAS_ROWS)
        pos = r0 + row0 + lax.broadcasted_iota(jnp.int32, (BIAS_ROWS, n_c), 0)
        key_end = lax.broadcasted_iota(jnp.int32, (BIAS_ROWS, n_c), 1) * CMP_STRIDE + (CMP_BLOCK - 1)
        dist = pos - key_end
        for h in range(N_HEADS):
            bias = _bias_of_dist(dist, h, thr_ref, tbl_ref)
            out_ref[h, pl.ds(row0, BIAS_ROWS), :] = jnp.where(dist >= 0, bias, NEG_INF)
        return carry

    lax.fori_loop(0, tr // BIAS_ROWS, chunk, 0)


def _bias_near_kernel(thr_ref, tbl_ref, near_ref, win_ref, far_ref):
    tq = ATTN_TQ
    h = pl.program_id(0)

    def table(out_ref, lo_keys, window):
        width = out_ref.shape[2]

        def chunk(ci, carry):
            row0 = pl.multiple_of(ci * BIAS_ROWS, BIAS_ROWS)
            dist = (lo_keys + row0 + lax.broadcasted_iota(jnp.int32, (BIAS_ROWS, width), 0)
                    - lax.broadcasted_iota(jnp.int32, (BIAS_ROWS, width), 1))
            visible = (dist >= 0) & (dist < window)
            bias = jnp.full(dist.shape, tbl_ref[h], F32)
            for k in range(1, REL_BUCKETS):
                bias = jnp.where(dist >= thr_ref[k], tbl_ref[k * N_HEADS + h], bias)
            out_ref[0, pl.ds(row0, BIAS_ROWS), :] = jnp.where(visible, bias, NEG_INF)
            return carry

        lax.fori_loop(0, tq // BIAS_ROWS, chunk, 0)

    table(near_ref, SLC_NEAR_BACK, 1 << 30)
    table(win_ref, WINDOW, WINDOW)
    far_ref[0] = jnp.full(far_ref.shape[1:], tbl_ref[(REL_BUCKETS - 1) * N_HEADS + h], F32)


def _attention_bias_tables(rel_bias, seq):
    tbl = (rel_bias.astype(F32) * LOG2_E).reshape(REL_BUCKETS * N_HEADS)
    thr = _bucket_thresholds()
    tq = ATTN_TQ
    n_c = seq // CMP_STRIDE
    smem = pl.BlockSpec(memory_space=pltpu.SMEM)
    tr = min(512, seq)
    bias_c = pl.pallas_call(
        _bias_c_kernel,
        grid=(seq // tr,),
        in_specs=[smem, smem],
        out_specs=pl.BlockSpec((N_HEADS, tr, n_c), lambda i: (0, i, 0)),
        out_shape=jax.ShapeDtypeStruct((N_HEADS, seq, n_c), F32),
        compiler_params=_compiler_params(("parallel",)),
        name="bias_cmp",
    )(thr, tbl)
    head_block = lambda w: pl.BlockSpec((1, tq, w), lambda h: (h, 0, 0))
    widths = (SLC_NEAR_BACK + tq, WINDOW + tq, LANES)
    bias_near, bias_win, bias_far = pl.pallas_call(
        _bias_near_kernel,
        grid=(N_HEADS,),
        in_specs=[smem, smem],
        out_specs=[head_block(w) for w in widths],
        out_shape=[jax.ShapeDtypeStruct((N_HEADS, tq, w), F32) for w in widths],
        compiler_params=_compiler_params(("parallel",)),
        name="bias_near",
    )(thr, tbl)
    return bias_c, bias_near, bias_win, bias_far


def _nsa_constants(seq):
    n_c = seq // CMP_STRIDE
    n_blk = seq // SLC_BLOCK
    cmp_start = jnp.arange(n_c) * CMP_STRIDE
    blk_start = jnp.arange(n_blk) * SLC_BLOCK
    overlap_t = ((cmp_start[None, :] <= blk_start[:, None] + SLC_BLOCK - 1)
                 & (cmp_start[None, :] + CMP_BLOCK - 1 >= blk_start[:, None]))
    overlap_t = overlap_t & (cmp_start[None, :] + CMP_BLOCK <= seq)
    ones_rows = jnp.arange(SUBLANES)[:, None] == 0
    overlap_t = jnp.concatenate([overlap_t, jnp.broadcast_to(ones_rows, (SUBLANES, n_c))], axis=0)
    cmp_ones = jnp.broadcast_to(jnp.arange(LANES)[None, :] == 0, (n_c, LANES))
    return overlap_t.astype(BF16), cmp_ones.astype(BF16)


def _build_kv_scratch(seq, slc_ref, win_ref, g, ks_sc, vs_sc, kw_sc, vw_sc):
    chunk = min(512, seq)
    lane_p = lax.broadcasted_iota(jnp.int32, (KV_PAD, LANES), 1)
    zeros = jnp.zeros((KV_PAD, LANES), BF16)
    ks_sc[0:KV_PAD] = jnp.where(lane_p >= HEAD_DIM, 1.0, 0.0).astype(BF16)
    kw_sc[0:KV_PAD] = jnp.where(lane_p == HEAD_DIM, NEG_INF, 0.0).astype(BF16)
    vs_sc[0:KV_PAD] = zeros
    vw_sc[0:KV_PAD] = zeros
    lane = lax.broadcasted_iota(jnp.int32, (chunk, LANES), 1)
    row = lax.broadcasted_iota(jnp.int32, (chunk, LANES), 0)
    lo_half = lane < HEAD_DIM
    ones_lane = jnp.where(lane == HEAD_DIM, 1.0, 0.0)
    for c in range(seq // chunk):
        r0 = c * chunk
        dst = slice(KV_PAD + r0, KV_PAD + r0 + chunk)
        blk = jnp.right_shift(r0 + row, int(math.log2(SLC_BLOCK)))
        lanes_g = slice(g * LANES, (g + 1) * LANES)
        slab = slc_ref[0, r0:r0 + chunk, lanes_g].astype(F32)
        ks_sc[dst] = jnp.where(lo_half, slab, jnp.where(lane - HEAD_DIM == blk, 1.0, 0.0)).astype(BF16)
        vs_sc[dst] = jnp.where(lo_half, pltpu.roll(slab, HEAD_DIM, 1), ones_lane).astype(BF16)
        slab = win_ref[0, r0:r0 + chunk, lanes_g].astype(F32)
        kw_sc[dst] = jnp.where(lo_half, slab, 0.0).astype(BF16)
        vw_sc[dst] = jnp.where(lo_half, pltpu.roll(slab, HEAD_DIM, 1), ones_lane).astype(BF16)


def _nsa_select(seq, g, q0, q_ref, vkc_ref, bias_c_ref, overlap_t_ref, cmp_ones_ref):
    tq = ATTN_TQ
    n_blk = seq // SLC_BLOCK
    n_sel = min(N_SEL, n_blk)
    rows = HPG * tq
    lane = lax.broadcasted_iota(jnp.int32, (tq, LANES), 1)
    lo_half = lane < HEAD_DIM
    lane_r = lax.broadcasted_iota(jnp.int32, (rows, LANES), 1)

    q_lo, q_hi = [], []
    for pair_idx in range(HPG // 2):
        lanes_p = slice((g * (HPG // 2) + pair_idx) * LANES, (g * (HPG // 2) + pair_idx + 1) * LANES)
        q2 = q_ref[0, :, lanes_p].astype(F32)
        q2r = pltpu.roll(q2, HEAD_DIM, 1)
        q_lo += [jnp.where(lo_half, q2, 0.0), jnp.where(lo_half, q2r, 0.0)]
        q_hi += [jnp.where(lo_half, 0.0, q2r), jnp.where(lo_half, 0.0, q2)]
    q_lo = jnp.concatenate(q_lo, axis=0)
    q_hi = jnp.concatenate(q_hi, axis=0)

    vkc = vkc_ref[0, g]
    s_c = _dot_nt(q_hi.astype(BF16), vkc) + bias_c_ref[g * HPG:(g + 1) * HPG].reshape(rows, -1)
    m_c = jnp.max(s_c, axis=1, keepdims=True)
    e_cb = jnp.exp2(s_c - m_c).astype(BF16)
    pv_c = _dot(e_cb, jnp.concatenate([vkc, cmp_ones_ref[...]], axis=1))
    row_pos = q0 + (lax.broadcasted_iota(jnp.int32, (rows, LANES), 0) & (tq - 1))
    has_key = row_pos >= CMP_BLOCK - 1
    o_cmp = jnp.where(has_key, pv_c[:, :LANES] / pv_c[:, LANES:LANES + 1], 0.0)

    imp_t4 = _dot_nt(overlap_t_ref[...], e_cb)
    imp_t = None
    for h in range(HPG):
        part = imp_t4[:, h * tq:(h + 1) * tq]
        part = part[:n_blk] / part[n_blk:n_blk + 1]
        imp_t = part if imp_t is None else imp_t + part

    blk = lax.broadcasted_iota(jnp.int32, (n_blk, tq), 0)
    pos = q0 + lax.broadcasted_iota(jnp.int32, (n_blk, tq), 1)
    cur = jnp.right_shift(pos, int(math.log2(SLC_BLOCK)))
    forced = (blk == 0) | (blk == cur) | (blk == cur - 1)
    valid = blk * SLC_BLOCK <= pos
    score = jnp.where(forced, BIG, jnp.where(valid, imp_t, -BIG))
    blk_f = blk.astype(F32)
    pen_t = jnp.full((n_blk, tq), NEG_INF, F32)
    for _ in range(n_sel):
        top = jnp.max(score, axis=0, keepdims=True)
        first = jnp.min(jnp.where(score == top, blk_f, float(n_blk)), axis=0, keepdims=True)
        hit = blk_f == first
        pen_t = jnp.where(hit, 0.0, pen_t)
        score = jnp.where(hit, -jnp.inf, score)
    pieces = [jnp.zeros((tq, HEAD_DIM), F32), jnp.transpose(pen_t)]
    if n_blk < LANES - HEAD_DIM:
        pieces.append(jnp.zeros((tq, LANES - HEAD_DIM - n_blk), F32))
    pen_lanes = jnp.concatenate(pieces, axis=1)
    q_slc = jnp.where(lane_r < HEAD_DIM, q_lo, jnp.concatenate([pen_lanes] * HPG, axis=0)).astype(BF16)
    return o_cmp, q_lo, q_slc


def _nsa_kernel(seq, q_ref, vkc_ref, slc_ref, win_ref, gate_ref, bias_c_ref, bias_near_ref, bias_win_ref,
                bias_far_ref, overlap_t_ref, cmp_ones_ref, o_ref, *scratch):
    tq = ATTN_TQ
    rows = HPG * tq
    groups = range(N_KV)
    kv_sc = [scratch[4 * g:4 * g + 4] for g in groups]
    s_sc, mrun_sc, acc_sc = (scratch[4 * N_KV + i * N_KV:4 * N_KV + (i + 1) * N_KV] for i in range(3))
    qt = pl.program_id(1)
    q0 = pl.multiple_of(qt * tq, tq)
    lane = lax.broadcasted_iota(jnp.int32, (tq, LANES), 1)
    lo_half = lane < HEAD_DIM
    lane_r = lax.broadcasted_iota(jnp.int32, (rows, LANES), 1)

    @pl.when(qt == 0)
    def _():
        for g in groups:
            _build_kv_scratch(seq, slc_ref, win_ref, g, *kv_sc[g])

    sel = [_nsa_select(seq, g, q0, q_ref, vkc_ref, bias_c_ref, overlap_t_ref, cmp_ones_ref) for g in groups]
    o_cmp = [s[0] for s in sel]
    q_lo = [s[1] for s in sel]
    q_slc = [s[2] for s in sel]
    head_rows = lambda ref, g: ref[g * HPG:(g + 1) * HPG].reshape(rows, -1)

    n_far = qt // (SLC_FAR_TK // tq)
    near0 = pl.multiple_of(q0 + (KV_PAD - SLC_NEAR_BACK), tq)
    near_rows = pl.ds(near0, SLC_NEAR_BACK + tq)

    def far_rows(t):
        return pl.ds(pl.multiple_of(near0 - (t + 1) * SLC_FAR_TK, tq), SLC_FAR_TK)

    def far_cols(t):
        return pl.ds(pl.multiple_of(t * SLC_FAR_TK, SLC_FAR_TK), SLC_FAR_TK)

    for g in groups:
        mrun_sc[g][...] = jnp.full((rows, LANES), NEG_INF, F32)

    def pass1(t, carry):
        for g in groups:
            s = _dot_nt(q_slc[g], kv_sc[g][0][far_rows(t), :])
            s_sc[g][:, far_cols(t)] = s
            mrun_sc[g][...] = jnp.maximum(mrun_sc[g][...], jnp.maximum(s[:, :LANES], s[:, LANES:]))
        return carry

    lax.fori_loop(0, n_far, pass1, 0)
    for g in groups:
        ks_sc, vs_sc = kv_sc[g][0], kv_sc[g][1]
        c_far = head_rows(bias_far_ref, g)
        s_n = _dot_nt(q_slc[g], ks_sc[near_rows, :]) + head_rows(bias_near_ref, g)
        m_s = jnp.maximum(jnp.max(mrun_sc[g][...], axis=1, keepdims=True) + c_far[:, :1],
                          jnp.max(s_n, axis=1, keepdims=True))
        acc_sc[g][...] = _dot(jnp.exp2(s_n - m_s).astype(BF16), vs_sc[near_rows, :])
        mrun_sc[g][...] = m_s - c_far

    def pass2(t, carry):
        for g in groups:
            shift = mrun_sc[g][...]
            p = jnp.exp2(s_sc[g][:, far_cols(t)] - jnp.concatenate([shift, shift], axis=1))
            acc_sc[g][...] = acc_sc[g][...] + _dot(p.astype(BF16), kv_sc[g][1][far_rows(t), :])
        return carry

    lax.fori_loop(0, n_far, pass2, 0)

    for g in groups:
        acc = acc_sc[g][...]
        o_slc = acc / acc[:, HEAD_DIM:HEAD_DIM + 1]

        kw_sc, vw_sc = kv_sc[g][2], kv_sc[g][3]
        q_win = jnp.where(lane_r == HEAD_DIM, 1.0, q_lo[g]).astype(BF16)
        win_rows = pl.ds(q0, WINDOW + tq)
        s_w = _dot_nt(q_win, kw_sc[win_rows, :]) + head_rows(bias_win_ref, g)
        p_w = jnp.exp2(s_w - jnp.max(s_w, axis=1, keepdims=True)).astype(BF16)
        acc_w = _dot(p_w, vw_sc[win_rows, :])
        o_win = acc_w / acc_w[:, HEAD_DIM:HEAD_DIM + 1]

        gates = gate_ref[0, :, g * GATE_PAD:(g + 1) * GATE_PAD]
        outs = []
        for h in range(HPG):
            sl = slice(h * tq, (h + 1) * tq)
            g_c, g_s, g_w = (gates[:, br * HPG + h:br * HPG + h + 1] for br in range(3))
            outs.append(g_c * o_cmp[g][sl] + g_s * o_slc[sl] + g_w * o_win[sl])
        for pair_idx in range(HPG // 2):
            even, odd = outs[2 * pair_idx], outs[2 * pair_idx + 1]
            merged = jnp.where(lo_half, even, pltpu.roll(odd, HEAD_DIM, 1))
            lanes_p = slice((g * (HPG // 2) + pair_idx) * LANES, (g * (HPG // 2) + pair_idx + 1) * LANES)
            o_ref[0, :, lanes_p] = merged.astype(o_ref.dtype)


def _nsa(q, vkc, slc, win, gates, bias_c, bias_near, bias_win, bias_far, overlap_t, cmp_ones):
    b, seq, _ = q.shape
    n_c = seq // CMP_STRIDE
    n_blk = seq // SLC_BLOCK
    assert n_blk <= LANES - HEAD_DIM and seq % SLC_FAR_TK == 0
    tq = ATTN_TQ
    rows = HPG * tq
    const = lambda a: pl.BlockSpec(a.shape, lambda i, t: (0,) * a.ndim)
    tile = lambda w: pl.BlockSpec((1, tq, w), lambda i, t: (i, t, 0))
    whole_seq = lambda w: pl.BlockSpec((1, seq, w), lambda i, t: (i, 0, 0))
    per_group = lambda shape: [pltpu.VMEM(shape, F32)] * N_KV
    return pl.pallas_call(
        functools.partial(_nsa_kernel, seq),
        grid=(b, seq // tq),
        in_specs=[
            tile(ATTN_WIDTH),
            pl.BlockSpec((1, N_KV, n_c, 2 * HEAD_DIM), lambda i, t: (i, 0, 0, 0)),
            whole_seq(N_KV * 2 * HEAD_DIM), whole_seq(N_KV * 2 * HEAD_DIM),
            tile(N_KV * GATE_PAD),
            pl.BlockSpec((N_HEADS, tq, n_c), lambda i, t: (0, t, 0)),
            const(bias_near), const(bias_win), const(bias_far),
            const(overlap_t), const(cmp_ones),
        ],
        out_specs=tile(ATTN_WIDTH),
        out_shape=jax.ShapeDtypeStruct((b, seq, ATTN_WIDTH), BF16),
        scratch_shapes=(
            [pltpu.VMEM((KV_PAD + seq, LANES), BF16)] * (4 * N_KV)
            + per_group((rows, seq)) + per_group((rows, LANES)) + per_group((rows, LANES))
        ),
        compiler_params=_compiler_params(("parallel", "arbitrary")),
        name="nsa",
    )(q, vkc, slc, win, gates, bias_c, bias_near, bias_win, bias_far, overlap_t, cmp_ones)


def _mixer_inputs(x, rel_bias, w_in, b_in, cmp_pos, cmp_w1, cmp_b1, cmp_w2, cmp_b2):
    b, seq, d = x.shape
    wp, bp = _pack_in_proj(w_in, b_in, d)
    q, slc, win, cmp, gates, u, merge = _in_proj(x.reshape(b * seq, d), wp, bp, d, 512)
    n_c = seq // CMP_STRIDE
    cmp4 = cmp.reshape(b, seq, 2 * N_KV, HEAD_DIM).transpose(0, 2, 1, 3).reshape(
        b, 2 * N_KV, n_c, CMP_STRIDE * HEAD_DIM)
    vkc = _compress(cmp4, cmp_pos, cmp_w1, cmp_b1, cmp_w2, cmp_b2)
    bias_c, bias_near, bias_win, bias_far = _attention_bias_tables(rel_bias, seq)
    overlap_t, cmp_ones = _nsa_constants(seq)
    o = _nsa(q.reshape(b, seq, -1), vkc, slc.reshape(b, seq, -1), win.reshape(b, seq, -1),
             gates.reshape(b, seq, -1), bias_c, bias_near, bias_win, bias_far, overlap_t, cmp_ones)
    return o, u, merge


S5_HALF_GROUPS = S5_GROUPS // 2
S5_HALF_IN = S5_HALF_GROUPS * S5_GROUP
S5_HALF_STATE = S5_HALF_GROUPS * S5_STATE
S5_SCAN_LANES = 512
S5_CHUNK = 64
S5_UNROLL = 8


def _s5_params(lam_re, lam_im, log_dt, b_re, b_im, c_re, c_im, nb):
    dt = jnp.exp(log_dt.astype(F32))[:, None]
    lr, li = lam_re.astype(F32), lam_im.astype(F32)
    mag = jnp.exp(lr * dt)
    ab_re, ab_im = mag * jnp.cos(li * dt), mag * jnp.sin(li * dt)
    nr, ni = ab_re - 1.0, ab_im
    den = lr * lr + li * li
    fr, fi = (nr * lr + ni * li) / den, (ni * lr - nr * li) / den
    br, bim = b_re.astype(F32), b_im.astype(F32)
    bb_re = fr[..., None] * br - fi[..., None] * bim
    bb_im = fr[..., None] * bim + fi[..., None] * br
    eye = jnp.eye(S5_HALF_GROUPS, dtype=F32)

    def in_mat(bb):
        t = bb.reshape(2, S5_HALF_GROUPS, S5_STATE, S5_GROUP)
        m = jnp.einsum('kgph,gj->kghjp', t, eye)
        return m.reshape(2, S5_HALF_IN, S5_HALF_STATE)

    def out_mat(c):
        t = c.astype(F32).reshape(2, S5_HALF_GROUPS, S5_GROUP, S5_STATE)
        m = jnp.einsum('kghp,gj->kgpjh', t, eye)
        return m.reshape(2, S5_HALF_STATE, S5_HALF_IN)

    bmat = jnp.concatenate([in_mat(bb_re), in_mat(bb_im)], axis=2).astype(BF16)
    cmat = jnp.concatenate([out_mat(c_re), -out_mat(c_im)], axis=1).astype(BF16)
    a = jnp.concatenate([ab_re.reshape(2, S5_HALF_STATE), ab_im.reshape(2, S5_HALF_STATE)], axis=1)
    a = jnp.broadcast_to(a.reshape(1, 4 * S5_HALF_STATE), (nb, 4 * S5_HALF_STATE))
    return bmat, cmat, a


def _s5_kernel(u_ref, bmat_ref, cmat_ref, a_ref, d_ref, y_ref, ut_sc, x_sc, st_sc):
    nb, t_len, _ = u_ref.shape
    half_w = 2 * S5_HALF_STATE

    @pl.when(pl.program_id(0) == 0)
    def _():
        st_sc[...] = jnp.zeros_like(st_sc)

    n_cb = ut_sc.shape[0]
    for b in range(nb):
        for cb in range(n_cb):
            ut_sc[cb, pl.ds(b, t_len, stride=nb), :] = u_ref[b, :, cb * LANES:(cb + 1) * LANES]
    ut = jnp.concatenate([ut_sc[cb] for cb in range(n_cb)], axis=1)
    ub = ut.astype(BF16)
    for k in range(2):
        x_sc[:, k * half_w:(k + 1) * half_w] = _dot(ub[:, k * S5_HALF_IN:(k + 1) * S5_HALF_IN], bmat_ref[k])

    for k in range(2):
        for j in range(S5_HALF_STATE // S5_SCAN_LANES):
            re0 = k * half_w + j * S5_SCAN_LANES
            im0 = re0 + S5_HALF_STATE
            re_sl, im_sl = pl.ds(re0, S5_SCAN_LANES), pl.ds(im0, S5_SCAN_LANES)
            ar, ai = a_ref[:, re_sl], a_ref[:, im_sl]

            def steps(c, carry):
                xr, xi = carry
                for s in range(S5_UNROLL):
                    rows = pl.ds(pl.multiple_of((c * S5_UNROLL + s) * nb, nb), nb)
                    nxr = ar * xr - ai * xi + x_sc[rows, re_sl]
                    nxi = ar * xi + ai * xr + x_sc[rows, im_sl]
                    x_sc[rows, re_sl] = nxr
                    x_sc[rows, im_sl] = nxi
                    xr, xi = nxr, nxi
                return xr, xi

            xr, xi = lax.fori_loop(0, t_len // S5_UNROLL, steps, (st_sc[:, re_sl], st_sc[:, im_sl]))
            st_sc[:, re_sl] = xr
            st_sc[:, im_sl] = xi

    xs = x_sc[...].astype(BF16)
    y = jnp.concatenate([_dot(xs[:, k * half_w:(k + 1) * half_w], cmat_ref[k]) for k in range(2)], axis=1)
    y = y + d_ref[...] * ut
    for cb in range(n_cb):
        ut_sc[cb] = y[:, cb * LANES:(cb + 1) * LANES]
    for b in range(nb):
        for cb in range(n_cb):
            y_ref[b, :, cb * LANES:(cb + 1) * LANES] = ut_sc[cb, pl.ds(b, t_len, stride=nb), :]


def _s5(u, bmat, cmat, a, d_skip):
    nb, seq, w = u.shape
    t_len = min(S5_CHUNK, seq)
    full = lambda shape: pl.BlockSpec(shape, lambda c: (0,) * len(shape))
    return pl.pallas_call(
        _s5_kernel,
        grid=(seq // t_len,),
        in_specs=[
            pl.BlockSpec((nb, t_len, w), lambda c: (0, c, 0)),
            full(bmat.shape), full(cmat.shape), full(a.shape), full((1, w)),
        ],
        out_specs=pl.BlockSpec((nb, t_len, w), lambda c: (0, c, 0)),
        out_shape=jax.ShapeDtypeStruct((nb, seq, w), F32),
        scratch_shapes=[
            pltpu.VMEM((w // LANES, t_len * nb, LANES), F32),
            pltpu.VMEM((t_len * nb, 4 * S5_HALF_STATE), F32),
            pltpu.VMEM((nb, 4 * S5_HALF_STATE), F32),
        ],
        compiler_params=_compiler_params(("arbitrary",)),
        name="s5",
    )(u, bmat, cmat, a, d_skip.reshape(1, w).astype(F32))


ROUTE_PAD = LANES
_R_E1, _R_E2, _R_W1, _R_W2, _R_RANK1, _R_RANK2 = range(6)


def _layer_norm(t, g, b):
    mu = jnp.mean(t, axis=1, keepdims=True)
    c = t - mu
    var = jnp.mean(c * c, axis=1, keepdims=True)
    return c * lax.rsqrt(var + LN_EPS) * g + b


def _post_kernel(x_ref, o_ref, y_ref, m_ref, wup_ref, wval_ref, wgate_ref, bgate_ref, wout_ref,
                 g1_ref, b1_ref, wr_ref, br_ref, h_ref, route_ref, cnt_ref, run_sc):
    tm, d = x_ref.shape

    @pl.when(pl.program_id(0) == 0)
    def _():
        run_sc[...] = jnp.zeros_like(run_sc)

    y_a = _dot(o_ref[...], wup_ref[...])
    z = _gelu_tanh(y_ref[...]).astype(BF16)
    y_b = _dot(z, wval_ref[...]) * jax.nn.sigmoid(_dot(z, wgate_ref[...]) + bgate_ref[...])
    mixed = m_ref[:, :d] * y_a + m_ref[:, d:] * y_b
    t = DN_ALPHA * x_ref[...] + _dot(mixed.astype(BF16), wout_ref[...])
    h = _layer_norm(t, g1_ref[...], b1_ref[...])
    h_ref[...] = h

    logits = _dot(h.astype(BF16), wr_ref[...]) + br_ref[...]
    lane = lax.broadcasted_iota(jnp.int32, (tm, ROUTE_PAD), 1)
    lane_f = lane.astype(F32)
    is_group = lane < N_EGROUPS

    def first_max(v):
        top = jnp.max(v, axis=1, keepdims=True)
        idx = jnp.min(jnp.where(v == top, lane_f, float(ROUTE_PAD)), axis=1, keepdims=True)
        return top, idx

    g_max, g_top = first_max(jnp.where(is_group, logits, -jnp.inf))
    p_group = 1.0 / jnp.sum(jnp.where(is_group, jnp.exp(logits - g_max), 0.0), axis=1, keepdims=True)
    grp_of_lane = jnp.right_shift(lane - N_EGROUPS, int(math.log2(EXPERTS_PER_GROUP))).astype(F32)
    in_group = (lane >= N_EGROUPS) & (lane < N_EGROUPS + N_EXPERTS) & (grp_of_lane == g_top)
    e_log = jnp.where(in_group, logits, -jnp.inf)
    v1, i1 = first_max(e_log)
    hit1 = lane_f == i1
    v2, i2 = first_max(jnp.where(hit1, -jnp.inf, e_log))
    hit2 = lane_f == i2
    e2 = jnp.exp(v2 - v1)
    w1 = p_group / (1.0 + e2)
    w2 = p_group * e2 / (1.0 + e2)

    hits = jnp.where(hit1 | hit2, 1.0, 0.0)
    row = lax.broadcasted_iota(jnp.int32, (tm, tm), 0)
    col = lax.broadcasted_iota(jnp.int32, (tm, tm), 1)
    earlier = jnp.where(col < row, 1.0, 0.0).astype(BF16)
    before = _dot(earlier, hits.astype(BF16)) + run_sc[...]
    rank1 = jnp.sum(jnp.where(hit1, before, 0.0), axis=1, keepdims=True)
    rank2 = jnp.sum(jnp.where(hit2, before, 0.0), axis=1, keepdims=True)
    run_sc[...] = run_sc[...] + jnp.sum(hits, axis=0, keepdims=True)
    cnt_ref[...] = run_sc[...]

    rec = jnp.zeros((tm, ROUTE_PAD), F32)
    for slot, val in ((_R_E1, i1 - N_EGROUPS), (_R_E2, i2 - N_EGROUPS), (_R_W1, w1), (_R_W2, w2),
                      (_R_RANK1, rank1), (_R_RANK2, rank2)):
        rec = jnp.where(lane == slot, val, rec)
    route_ref[...] = rec


def _post(x2d, o2d, y2d, merge, w_attn_up, s5_w_val, s5_w_gate, s5_b_gate, w_out, ln1_g, ln1_b,
          router_w_group, router_b_group, router_w_expert, router_b_expert, tm):
    n, d = x2d.shape
    rpad = ROUTE_PAD - N_EGROUPS - N_EXPERTS
    wr = jnp.concatenate([router_w_group, router_w_expert, jnp.zeros((d, rpad), F32)], axis=1).astype(BF16)
    br = jnp.concatenate([router_b_group, router_b_expert, jnp.zeros((rpad,), F32)]).reshape(1, -1).astype(F32)
    row = lambda w: pl.BlockSpec((tm, w), lambda i: (i, 0))
    full = lambda a: pl.BlockSpec(a.shape, lambda i: (0,) * a.ndim)
    weights = [w_attn_up.astype(BF16), s5_w_val.astype(BF16), s5_w_gate.astype(BF16),
               s5_b_gate.reshape(1, d).astype(F32), w_out.astype(BF16),
               ln1_g.reshape(1, d).astype(F32), ln1_b.reshape(1, d).astype(F32), wr, br]
    return pl.pallas_call(
        _post_kernel,
        grid=(n // tm,),
        in_specs=[row(d), row(ATTN_WIDTH), row(S5_WIDTH), row(2 * d)] + [full(w) for w in weights],
        out_specs=[row(d), row(ROUTE_PAD), pl.BlockSpec((1, ROUTE_PAD), lambda i: (0, 0))],
        out_shape=[jax.ShapeDtypeStruct((n, d), F32), jax.ShapeDtypeStruct((n, ROUTE_PAD), F32),
                   jax.ShapeDtypeStruct((1, ROUTE_PAD), F32)],
        scratch_shapes=[pltpu.VMEM((1, ROUTE_PAD), F32)],
        compiler_params=_compiler_params(("arbitrary",)),
        name="post_mixer",
    )(x2d, o2d, y2d, merge, *weights)


def _plan_kernel(route_ref, cnt_ref, dest_ref):
    tm = route_ref.shape[0]
    lane8 = lax.broadcasted_iota(jnp.int32, (SUBLANES, ROUTE_PAD), 1)
    counts = jnp.broadcast_to(cnt_ref[...], (SUBLANES, ROUTE_PAD)).astype(jnp.int32)
    shift = int(math.log2(MOE_BLOCK))
    padded = jnp.left_shift(jnp.right_shift(counts + (MOE_BLOCK - 1), shift), shift)
    incl = padded
    step = 1
    while step < ROUTE_PAD:
        incl = incl + jnp.where(lane8 >= step, pltpu.roll(incl, step, 1), 0)
        step *= 2
    pstart = (incl - padded)[0:1].astype(F32)
    route = route_ref[...]
    expert_of_lane = (lax.broadcasted_iota(jnp.int32, (tm, ROUTE_PAD), 1) - N_EGROUPS).astype(F32)
    lane = lax.broadcasted_iota(jnp.int32, (tm, ROUTE_PAD), 1)

    def dest(e_slot, rank_slot):
        hit = expert_of_lane == route[:, e_slot:e_slot + 1]
        return jnp.sum(jnp.where(hit, pstart, 0.0), axis=1, keepdims=True) + route[:, rank_slot:rank_slot + 1]

    d1 = dest(_R_E1, _R_RANK1)
    d2 = dest(_R_E2, _R_RANK2)
    dest_ref[...] = jnp.where(lane == 0, d1, jnp.where(lane == 1, d2, 0.0)).astype(jnp.int32)


def _plan(route, counts_row):
    n = route.shape[0]
    tm = min(1024, n)
    return pl.pallas_call(
        _plan_kernel,
        grid=(n // tm,),
        in_specs=[pl.BlockSpec((tm, ROUTE_PAD), lambda i: (i, 0)),
                  pl.BlockSpec((1, ROUTE_PAD), lambda i: (0, 0))],
        out_specs=pl.BlockSpec((tm, ROUTE_PAD), lambda i: (i, 0)),
        out_shape=jax.ShapeDtypeStruct((n, ROUTE_PAD), jnp.int32),
        compiler_params=_compiler_params(("parallel",)),
        name="moe_plan",
    )(route, counts_row)


SC_GATHER_ROWS = 32


def _sc_row_gather(table, idx):
    n_idx = idx.shape[0]
    d = table.shape[1]
    info = plsc.get_sparse_core_info()
    n_workers = info.num_cores * info.num_subcores
    per_worker = n_idx // n_workers
    assert n_idx % (n_workers * SC_GATHER_ROWS) == 0
    mesh = plsc.VectorSubcoreMesh(core_axis_name="c", subcore_axis_name="s")

    @functools.partial(
        pl.kernel, mesh=mesh,
        out_type=jax.ShapeDtypeStruct((n_idx, d), table.dtype),
        scratch_types=[
            pltpu.VMEM((SC_GATHER_ROWS,), jnp.int32),
            pltpu.VMEM((SC_GATHER_ROWS, d), table.dtype),
            pltpu.SemaphoreType.DMA,
        ],
    )
    def gather(table_hbm, idx_hbm, out_hbm, idx_v, rows_v, sem):
        worker = lax.axis_index("s") * info.num_cores + lax.axis_index("c")
        base = worker * per_worker

        @pl.loop(0, per_worker // SC_GATHER_ROWS)
        def _(j):
            off = base + j * SC_GATHER_ROWS
            pltpu.sync_copy(idx_hbm.at[pl.ds(off, SC_GATHER_ROWS)], idx_v)
            pltpu.async_copy(table_hbm.at[idx_v], rows_v, sem).wait()
            pltpu.sync_copy(rows_v, out_hbm.at[pl.ds(off, SC_GATHER_ROWS)])

    return gather(table, idx)


def _sc_row_scatter(rows, idx_a, idx_b, n_out):
    n, d = rows.shape
    info = plsc.get_sparse_core_info()
    n_workers = info.num_cores * info.num_subcores
    per_worker = n // n_workers
    assert n % (n_workers * SC_GATHER_ROWS) == 0
    mesh = plsc.VectorSubcoreMesh(core_axis_name="c", subcore_axis_name="s")

    @functools.partial(
        pl.kernel, mesh=mesh,
        out_type=jax.ShapeDtypeStruct((n_out, d), rows.dtype),
        scratch_types=[
            pltpu.VMEM((SC_GATHER_ROWS,), jnp.int32),
            pltpu.VMEM((SC_GATHER_ROWS,), jnp.int32),
            pltpu.VMEM((SC_GATHER_ROWS, d), rows.dtype),
        ],
    )
    def scatter(rows_hbm, idx_a_hbm, idx_b_hbm, out_hbm, idx_a_v, idx_b_v, rows_v):
        worker = lax.axis_index("s") * info.num_cores + lax.axis_index("c")
        base = worker * per_worker

        @pl.loop(0, per_worker // SC_GATHER_ROWS)
        def _(j):
            src = pl.ds(base + j * SC_GATHER_ROWS, SC_GATHER_ROWS)
            pltpu.sync_copy(rows_hbm.at[src], rows_v)
            pltpu.sync_copy(idx_a_hbm.at[src], idx_a_v)
            pltpu.sync_copy(idx_b_hbm.at[src], idx_b_v)
            pltpu.sync_copy(rows_v, out_hbm.at[idx_a_v])
            pltpu.sync_copy(rows_v, out_hbm.at[idx_b_v])

    return scatter(rows, idx_a, idx_b)


def _expert_kernel(blk_exp_ref, blk_valid_ref, x_ref, wg_ref, wu_ref, wd_ref, y_ref, wg_sc, wu_sc, wd_sc):
    i = pl.program_id(0)
    n_valid = blk_valid_ref[i]

    @pl.when((i == 0) | (blk_exp_ref[i] != blk_exp_ref[jnp.maximum(i - 1, 0)]))
    def _():
        wg_sc[...] = wg_ref[0].astype(BF16)
        wu_sc[...] = wu_ref[0].astype(BF16)
        wd_sc[...] = wd_ref[0].astype(BF16)

    @pl.when(n_valid > 0)
    def _():
        row = lax.broadcasted_iota(jnp.int32, x_ref.shape, 0)
        xb = jnp.where(row < n_valid, x_ref[...], 0.0).astype(BF16)
        h_gate = _dot(xb, wg_sc[...])
        h_up = _dot(xb, wu_sc[...])
        hb = (h_gate * jax.nn.sigmoid(h_gate) * h_up).astype(BF16)
        y_ref[...] = _dot(hb, wd_sc[...])

    @pl.when(n_valid == 0)
    def _():
        y_ref[...] = jnp.zeros_like(y_ref)


def _experts(blk_expert, blk_valid, xs, w_gate, w_up, w_down):
    n_blocks = blk_expert.shape[0]
    d = xs.shape[1]
    grid_spec = pltpu.PrefetchScalarGridSpec(
        num_scalar_prefetch=2,
        grid=(n_blocks,),
        in_specs=[
            pl.BlockSpec((MOE_BLOCK, d), lambda i, be, bv: (i, 0)),
            pl.BlockSpec((1, d, D_EXPERT), lambda i, be, bv: (be[i], 0, 0)),
            pl.BlockSpec((1, d, D_EXPERT), lambda i, be, bv: (be[i], 0, 0)),
            pl.BlockSpec((1, D_EXPERT, d), lambda i, be, bv: (be[i], 0, 0)),
        ],
        out_specs=pl.BlockSpec((MOE_BLOCK, d), lambda i, be, bv: (i, 0)),
        scratch_shapes=[
            pltpu.VMEM((d, D_EXPERT), BF16),
            pltpu.VMEM((d, D_EXPERT), BF16),
            pltpu.VMEM((D_EXPERT, d), BF16),
        ],
    )
    return pl.pallas_call(
        _expert_kernel,
        grid_spec=grid_spec,
        out_shape=jax.ShapeDtypeStruct((n_blocks * MOE_BLOCK, d), F32),
        compiler_params=_compiler_params(("arbitrary",)),
        name="experts",
    )(blk_expert, blk_valid, xs, w_gate, w_up, w_down)


COMBINE_TM = 512


def _combine_kernel(h_ref, y1_ref, y2_ref, route_ref, g2_ref, b2_ref, out_ref):
    route = route_ref[...]
    w1 = route[:, _R_W1:_R_W1 + 1]
    w2 = route[:, _R_W2:_R_W2 + 1]
    t = DN_ALPHA * h_ref[...] + (y1_ref[...] * w1 + y2_ref[...] * w2)
    out_ref[...] = _layer_norm(t, g2_ref[...], b2_ref[...])


def _combine(yg, h2d, route, ln2_g, ln2_b):
    n, d = h2d.shape
    tm = min(COMBINE_TM, n)
    n_tiles = n // tm
    row = lambda w: pl.BlockSpec((tm, w), lambda i: (i, 0))
    vec = pl.BlockSpec((1, d), lambda i: (0, 0))
    return pl.pallas_call(
        _combine_kernel,
        grid=(n_tiles,),
        in_specs=[row(d), row(d), pl.BlockSpec((tm, d), lambda i: (i + n_tiles, 0)), row(ROUTE_PAD), vec, vec],
        out_specs=row(d),
        out_shape=jax.ShapeDtypeStruct((n, d), F32),
        compiler_params=_compiler_params(("parallel",)),
        name="combine",
    )(h2d, yg, yg, route, ln2_g.reshape(1, d).astype(F32), ln2_b.reshape(1, d).astype(F32))


def _moe(h2d, route, counts_row, w_gate, w_up, w_down, ln2_g, ln2_b):
    n, d = h2d.shape
    dest = _plan(route, counts_row)
    dest1, dest2 = dest[:, 0], dest[:, 1]
    counts = counts_row[0, N_EGROUPS:N_EGROUPS + N_EXPERTS].astype(jnp.int32)
    padded = (counts + MOE_BLOCK - 1) // MOE_BLOCK * MOE_BLOCK
    pend = jnp.cumsum(padded)
    n_blocks = -(-(n * TOP_K_IN_GROUP) // MOE_BLOCK) + N_EXPERTS
    blk_row0 = jnp.arange(n_blocks, dtype=jnp.int32) * MOE_BLOCK
    blk_expert = jnp.minimum(jnp.sum(pend[None, :] <= blk_row0[:, None], axis=1), N_EXPERTS - 1).astype(jnp.int32)
    blk_valid = jnp.clip((pend - padded + counts)[blk_expert] - blk_row0, 0, MOE_BLOCK).astype(jnp.int32)
    xs = _sc_row_scatter(h2d, dest1, dest2, n_blocks * MOE_BLOCK)
    yb = _experts(blk_expert, blk_valid, xs, w_gate, w_up, w_down)
    yg = _sc_row_gather(yb, jnp.concatenate([dest1, dest2]))
    return _combine(yg, h2d, route, ln2_g, ln2_b)


def kernel(x, rel_bias, w_in, b_in, cmp_pos, cmp_w1, cmp_b1, cmp_w2, cmp_b2, w_attn_up, s5_lambda_re, s5_lambda_im, s5_log_dt, s5_b_re, s5_b_im, s5_c_re, s5_c_im, s5_d, s5_w_val, s5_w_gate, s5_b_gate, w_out, ln1_g, ln1_b, router_w_group, router_b_group, router_w_expert, router_b_expert, exp_w_gate, exp_w_up, exp_w_down, ln2_g, ln2_b):
    b, seq, d = x.shape
    n = b * seq
    assert w_in.shape[0] == DEPTH
    l = 0
    o, u, merge = _mixer_inputs(x, rel_bias, w_in[l], b_in[l], cmp_pos[l], cmp_w1[l], cmp_b1[l],
                                cmp_w2[l], cmp_b2[l])
    bmat, cmat, a = _s5_params(s5_lambda_re[l], s5_lambda_im[l], s5_log_dt[l], s5_b_re[l], s5_b_im[l],
                               s5_c_re[l], s5_c_im[l], b)
    y_s = _s5(u.reshape(b, seq, S5_WIDTH), bmat, cmat, a, s5_d[l])
    h2d, route, counts = _post(x.reshape(n, d), o.reshape(n, ATTN_WIDTH), y_s.reshape(n, S5_WIDTH), merge,
                               w_attn_up[l], s5_w_val[l], s5_w_gate[l], s5_b_gate[l], w_out[l], ln1_g[l],
                               ln1_b[l], router_w_group[l], router_b_group[l], router_w_expert[l],
                               router_b_expert[l], 512)
    out = _moe(h2d, route, counts, exp_w_gate[l], exp_w_up[l], exp_w_down[l], ln2_g[l], ln2_b[l])
    return out.reshape(b, seq, d)
```

```python
import functools
import math

import jax
import jax.numpy as jnp
from jax import lax
from jax.experimental import pallas as pl
from jax.experimental.pallas import tpu as pltpu
from jax.experimental.pallas import tpu_sc as plsc

F32 = jnp.float32
BF16 = jnp.bfloat16

N_HEADS = 8
HEAD_DIM = 64
N_KV = 2
HPG = N_HEADS // N_KV
CMP_STRIDE = 16
CMP_BLOCK = 2 * CMP_STRIDE
CMP_HIDDEN = 128
SLC_BLOCK = 64
N_SEL = 16
WINDOW = 512
REL_BUCKETS = 32
REL_MAX_DIST = 128
S5_WIDTH = 512
S5_GROUP = 16
S5_GROUPS = S5_WIDTH // S5_GROUP
S5_STATE = 64
N_EGROUPS = 8
EXPERTS_PER_GROUP = 8
N_EXPERTS = N_EGROUPS * EXPERTS_PER_GROUP
TOP_K_IN_GROUP = 2
D_EXPERT = 256
EXPERT_BLOCK = 128
DEPTH = 1
DN_ALPHA = (2.0 * DEPTH) ** 0.25
LN_EPS = 1e-5
NEG_INF = -1e30
BIG = 1e9
LOG2_E = math.log2(math.e)
MOE_BLOCK = 256

ATTN_WIDTH = N_HEADS * HEAD_DIM
KV_WIDTH = N_KV * HEAD_DIM
KV_OFF = ATTN_WIDTH
NSA_GATE_OFF = KV_OFF + 6 * KV_WIDTH
S5_OFF = NSA_GATE_OFF + 3 * N_HEADS
MERGE_OFF = S5_OFF + S5_WIDTH

LANES = 128
SUBLANES = 8
VMEM_LIMIT_BYTES = 56 * 1024 * 1024

ATTN_TQ = 128
SLC_TK = 512
SLC_NEAR_BACK = SLC_TK - ATTN_TQ
KV_PAD = WINDOW
GATE_PAD = LANES


def _gelu_tanh(x):
    c = math.sqrt(2.0 / math.pi)
    return x * (0.5 * (1.0 + jnp.tanh(c * (x + 0.044715 * (x * x * x)))))


def _dot(a, b):
    return jnp.dot(a, b, preferred_element_type=F32)


def _dot_nt(a, b):
    return lax.dot_general(a, b, (((1,), (1,)), ((), ())), preferred_element_type=F32)


def _compiler_params(semantics):
    return pltpu.CompilerParams(dimension_semantics=semantics, vmem_limit_bytes=VMEM_LIMIT_BYTES)


def _in_proj_layout(d_model):
    widths = (ATTN_WIDTH, 2 * KV_WIDTH, 2 * KV_WIDTH, 2 * KV_WIDTH, N_KV * GATE_PAD, S5_WIDTH, 2 * d_model)
    offs = [0]
    for w in widths:
        offs.append(offs[-1] + w)
    return widths, offs


def _pack_in_proj(w_in, b_in, d_model):
    def kv_cols(j):
        return KV_OFF + j * KV_WIDTH

    def pair(jk, jv):
        cols = []
        for g in range(N_KV):
            cols.append(jnp.arange(kv_cols(jk) + g * HEAD_DIM, kv_cols(jk) + (g + 1) * HEAD_DIM))
            cols.append(jnp.arange(kv_cols(jv) + g * HEAD_DIM, kv_cols(jv) + (g + 1) * HEAD_DIM))
        return jnp.concatenate(cols)

    idx = jnp.concatenate([
        jnp.arange(0, ATTN_WIDTH),
        pair(2, 3),
        pair(4, 5),
        jnp.arange(kv_cols(0), kv_cols(2)),
    ])
    idx2 = jnp.concatenate([jnp.arange(S5_OFF, S5_OFF + S5_WIDTH),
                            jnp.arange(MERGE_OFF, MERGE_OFF + 2 * d_model)])
    gpad = GATE_PAD - 3 * HPG
    w_parts, b_parts = [w_in[:, idx]], [b_in[idx]]
    for g in range(N_KV):
        cols = jnp.asarray([NSA_GATE_OFF + (g * HPG + h) * 3 + j for j in range(3) for h in range(HPG)])
        w_parts += [w_in[:, cols], jnp.zeros((d_model, gpad), F32)]
        b_parts += [b_in[cols], jnp.zeros((gpad,), F32)]
    w = jnp.concatenate(w_parts + [w_in[:, idx2]], axis=1)
    b = jnp.concatenate(b_parts + [b_in[idx2]])
    return w.astype(BF16), b.reshape(1, -1).astype(F32)


def _in_proj_kernel(offs, x_ref, w_ref, b_ref, q_ref, slc_ref, win_ref, cmp_ref, g_ref, u_ref, m_ref):
    xb = x_ref[...].astype(BF16)

    def proj(i):
        c0, c1 = offs[i], offs[i + 1]
        return _dot(xb, w_ref[:, c0:c1]) + b_ref[:, c0:c1]

    q_ref[...] = (proj(0) * (HEAD_DIM ** -0.5 * LOG2_E)).astype(BF16)
    slc_ref[...] = proj(1).astype(BF16)
    win_ref[...] = proj(2).astype(BF16)
    cmp_ref[...] = proj(3)
    g_ref[...] = jax.nn.sigmoid(proj(4))
    u_ref[...] = proj(5)
    m_ref[...] = jax.nn.sigmoid(proj(6))


def _in_proj(x2d, w_packed, b_packed, d_model, tm):
    n = x2d.shape[0]
    widths, offs = _in_proj_layout(d_model)
    ncols = offs[-1]
    dtypes = (BF16, BF16, BF16, F32, F32, F32, F32)
    return pl.pallas_call(
        functools.partial(_in_proj_kernel, tuple(offs)),
        grid=(n // tm,),
        in_specs=[
            pl.BlockSpec((tm, d_model), lambda i: (i, 0)),
            pl.BlockSpec((d_model, ncols), lambda i: (0, 0)),
            pl.BlockSpec((1, ncols), lambda i: (0, 0)),
        ],
        out_specs=[pl.BlockSpec((tm, w), lambda i: (i, 0)) for w in widths],
        out_shape=[jax.ShapeDtypeStruct((n, w), dt) for w, dt in zip(widths, dtypes)],
        compiler_params=_compiler_params(("parallel",)),
        name="in_proj",
    )(x2d, w_packed, b_packed)


def _compress_kernel(ck_ref, cv_ref, pos_ref, w1_ref, b1_ref, w2_ref, b2_ref, out_ref):
    n_c = ck_ref.shape[2]
    outs = []
    for i, c_ref in enumerate((ck_ref, cv_ref)):
        c = c_ref[0, 0]
        lo = (c + pos_ref[i, 0:1, :]).astype(BF16)
        hi = (c + pos_ref[i, 1:2, :]).astype(BF16)
        p_lo = _dot(lo, w1_ref[i, 0])
        p_hi = _dot(hi, w1_ref[i, 1])
        hid = p_lo + pltpu.roll(p_hi, n_c - 1, 0) + b1_ref[i]
        hid = _gelu_tanh(hid).astype(BF16)
        outs.append(_dot(hid, w2_ref[i]) + b2_ref[i])
    out_ref[0, 0] = jnp.concatenate(outs[::-1], axis=1).astype(BF16)


def _compress(cmp4, cmp_pos, cmp_w1, cmp_b1, cmp_w2, cmp_b2):
    b, _, n_c, cw = cmp4.shape
    half = CMP_STRIDE * HEAD_DIM
    pos = cmp_pos.reshape(2, 2, half).astype(F32)
    w1 = cmp_w1.reshape(2, 2, half, CMP_HIDDEN).astype(BF16)
    b1 = cmp_b1.reshape(2, 1, CMP_HIDDEN).astype(F32)
    w2 = cmp_w2.astype(BF16)
    b2 = cmp_b2.reshape(2, 1, HEAD_DIM).astype(F32)
    full = lambda shape: pl.BlockSpec(shape, lambda i, g: (0,) * len(shape))
    return pl.pallas_call(
        _compress_kernel,
        grid=(b, N_KV),
        in_specs=[
            pl.BlockSpec((1, 1, n_c, cw), lambda i, g: (i, g, 0, 0)),
            pl.BlockSpec((1, 1, n_c, cw), lambda i, g: (i, N_KV + g, 0, 0)),
            full((2, 2, half)),
            full((2, 2, half, CMP_HIDDEN)),
            full((2, 1, CMP_HIDDEN)),
            full((2, CMP_HIDDEN, HEAD_DIM)),
            full((2, 1, HEAD_DIM)),
        ],
        out_specs=pl.BlockSpec((1, 1, n_c, 2 * HEAD_DIM), lambda i, g: (i, g, 0, 0)),
        out_shape=jax.ShapeDtypeStruct((b, N_KV, n_c, 2 * HEAD_DIM), BF16),
        compiler_params=_compiler_params(("parallel", "parallel")),
        name="compress",
    )(cmp4, cmp4, pos, w1, b1, w2, b2)


def _t5_bucket(dist):
    n = jnp.maximum(dist, 0)
    max_exact = REL_BUCKETS // 2
    nf = jnp.maximum(n, 1).astype(F32)
    large = max_exact + (jnp.log(nf / max_exact) / math.log(REL_MAX_DIST / max_exact)
                         * (REL_BUCKETS - max_exact)).astype(jnp.int32)
    large = jnp.minimum(large, REL_BUCKETS - 1)
    return jnp.where(n < max_exact, n, large)


def _bucket_thresholds():
    buckets = _t5_bucket(jnp.arange(REL_MAX_DIST + 1))
    return jnp.sum(buckets[None, :] < jnp.arange(REL_BUCKETS)[:, None], axis=1).astype(jnp.int32)


def _bias_of_dist(dist, head, thr_ref, tbl_ref):
    bias = jnp.full(dist.shape, tbl_ref[head], F32)
    for k in range(1, REL_BUCKETS):
        bias = jnp.where(dist >= thr_ref[k], tbl_ref[k * N_HEADS + head], bias)
    return bias


BIAS_ROWS = 32


def _bias_c_kernel(thr_ref, tbl_ref, out_ref):
    _, tr, n_c = out_ref.shape
    r0 = pl.program_id(0) * tr

    def chunk(ci, carry):
        row0 = pl.multiple_of(ci * BIAS_ROWS, BIAS_ROWS)
        pos = r0 + row0 + lax.broadcasted_iota(jnp.int32, (BIAS_ROWS, n_c), 0)
        key_end = lax.broadcasted_iota(jnp.int32, (BIAS_ROWS, n_c), 1) * CMP_STRIDE + (CMP_BLOCK - 1)
        dist = pos - key_end
        for h in range(N_HEADS):
            bias = _bias_of_dist(dist, h, thr_ref, tbl_ref)
            out_ref[h, pl.ds(row0, BIAS_ROWS), :] = jnp.where(dist >= 0, bias, NEG_INF)
        return carry

    lax.fori_loop(0, tr // BIAS_ROWS, chunk, 0)


def _bias_near_kernel(thr_ref, tbl_ref, near_ref, win_ref):
    tq = ATTN_TQ
    h = pl.program_id(0)
    far_bias = tbl_ref[(REL_BUCKETS - 1) * N_HEADS + h]

    def table(out_ref, lo_keys, window, offset):
        width = out_ref.shape[2]

        def chunk(ci, carry):
            row0 = pl.multiple_of(ci * BIAS_ROWS, BIAS_ROWS)
            dist = (lo_keys + row0 + lax.broadcasted_iota(jnp.int32, (BIAS_ROWS, width), 0)
                    - lax.broadcasted_iota(jnp.int32, (BIAS_ROWS, width), 1))
            visible = (dist >= 0) & (dist < window)
            bias = jnp.full(dist.shape, tbl_ref[h], F32)
            for k in range(1, REL_BUCKETS):
                bias = jnp.where(dist >= thr_ref[k], tbl_ref[k * N_HEADS + h], bias)
            out_ref[0, pl.ds(row0, BIAS_ROWS), :] = jnp.where(visible, bias - offset, NEG_INF)
            return carry

        lax.fori_loop(0, tq // BIAS_ROWS, chunk, 0)

    table(near_ref, SLC_NEAR_BACK, 1 << 30, far_bias)
    table(win_ref, WINDOW, WINDOW, 0.0)


def _attention_bias_tables(rel_bias, seq):
    tbl = (rel_bias.astype(F32) * LOG2_E).reshape(REL_BUCKETS * N_HEADS)
    thr = _bucket_thresholds()
    tq = ATTN_TQ
    n_c = seq // CMP_STRIDE
    smem = pl.BlockSpec(memory_space=pltpu.SMEM)
    tr = min(512, seq)
    bias_c = pl.pallas_call(
        _bias_c_kernel,
        grid=(seq // tr,),
        in_specs=[smem, smem],
        out_specs=pl.BlockSpec((N_HEADS, tr, n_c), lambda i: (0, i, 0)),
        out_shape=jax.ShapeDtypeStruct((N_HEADS, seq, n_c), F32),
        compiler_params=_compiler_params(("parallel",)),
        name="bias_cmp",
    )(thr, tbl)
    head_block = lambda w: pl.BlockSpec((1, tq, w), lambda h: (h, 0, 0))
    widths = (SLC_NEAR_BACK + tq, WINDOW + tq)
    bias_near, bias_win = pl.pallas_call(
        _bias_near_kernel,
        grid=(N_HEADS,),
        in_specs=[smem, smem],
        out_specs=[head_block(w) for w in widths],
        out_shape=[jax.ShapeDtypeStruct((N_HEADS, tq, w), F32) for w in widths],
        compiler_params=_compiler_params(("parallel",)),
        name="bias_near",
    )(thr, tbl)
    return bias_c, bias_near, bias_win


def _nsa_constants(seq):
    n_c = seq // CMP_STRIDE
    n_blk = seq // SLC_BLOCK
    cmp_start = jnp.arange(n_c) * CMP_STRIDE
    blk_start = jnp.arange(n_blk) * SLC_BLOCK
    overlap_t = ((cmp_start[None, :] <= blk_start[:, None] + SLC_BLOCK - 1)
                 & (cmp_start[None, :] + CMP_BLOCK - 1 >= blk_start[:, None]))
    overlap_t = overlap_t & (cmp_start[None, :] + CMP_BLOCK <= seq)
    ones_rows = jnp.arange(SUBLANES)[:, None] == 0
    overlap_t = jnp.concatenate([overlap_t, jnp.broadcast_to(ones_rows, (SUBLANES, n_c))], axis=0)
    cmp_ones = jnp.broadcast_to(jnp.arange(LANES)[None, :] == 0, (n_c, LANES))
    return overlap_t.astype(BF16), cmp_ones.astype(BF16)


def _build_kv_scratch(seq, slc_ref, win_ref, g, ks_sc, vs_sc, kw_sc, vw_sc):
    chunk = min(512, seq)
    lane_p = lax.broadcasted_iota(jnp.int32, (KV_PAD, LANES), 1)
    zeros = jnp.zeros((KV_PAD, LANES), BF16)
    ks_sc[0:KV_PAD] = jnp.where(lane_p >= HEAD_DIM, 1.0, 0.0).astype(BF16)
    kw_sc[0:KV_PAD] = jnp.where(lane_p == HEAD_DIM, NEG_INF, 0.0).astype(BF16)
    vs_sc[0:KV_PAD] = zeros
    vw_sc[0:KV_PAD] = zeros
    lane = lax.broadcasted_iota(jnp.int32, (chunk, LANES), 1)
    row = lax.broadcasted_iota(jnp.int32, (chunk, LANES), 0)
    lo_half = lane < HEAD_DIM
    ones_lane = jnp.where(lane == HEAD_DIM, 1.0, 0.0)
    for c in range(seq // chunk):
        r0 = c * chunk
        dst = slice(KV_PAD + r0, KV_PAD + r0 + chunk)
        blk = jnp.right_shift(r0 + row, int(math.log2(SLC_BLOCK)))
        lanes_g = slice(g * LANES, (g + 1) * LANES)
        slab = slc_ref[0, r0:r0 + chunk, lanes_g].astype(F32)
        ks_sc[dst] = jnp.where(lo_half, slab, jnp.where(lane - HEAD_DIM == blk, 1.0, 0.0)).astype(BF16)
        vs_sc[dst] = jnp.where(lo_half, pltpu.roll(slab, HEAD_DIM, 1), ones_lane).astype(BF16)
        slab = win_ref[0, r0:r0 + chunk, lanes_g].astype(F32)
        kw_sc[dst] = jnp.where(lo_half, slab, 0.0).astype(BF16)
        vw_sc[dst] = jnp.where(lo_half, pltpu.roll(slab, HEAD_DIM, 1), ones_lane).astype(BF16)


def _nsa_select(seq, g, q0, q_ref, vkc_ref, bias_c_ref, overlap_t_ref, cmp_ones_ref):
    tq = ATTN_TQ
    n_blk = seq // SLC_BLOCK
    n_sel = min(N_SEL, n_blk)
    rows = HPG * tq
    lane = lax.broadcasted_iota(jnp.int32, (tq, LANES), 1)
    lo_half = lane < HEAD_DIM
    lane_r = lax.broadcasted_iota(jnp.int32, (rows, LANES), 1)

    q_lo, q_hi = [], []
    for pair_idx in range(HPG // 2):
        lanes_p = slice((g * (HPG // 2) + pair_idx) * LANES, (g * (HPG // 2) + pair_idx + 1) * LANES)
        q2 = q_ref[0, :, lanes_p].astype(F32)
        q2r = pltpu.roll(q2, HEAD_DIM, 1)
        q_lo += [jnp.where(lo_half, q2, 0.0), jnp.where(lo_half, q2r, 0.0)]
        q_hi += [jnp.where(lo_half, 0.0, q2r), jnp.where(lo_half, 0.0, q2)]
    q_lo = jnp.concatenate(q_lo, axis=0)
    q_hi = jnp.concatenate(q_hi, axis=0)

    vkc = vkc_ref[0, g]
    s_c = _dot_nt(q_hi.astype(BF16), vkc) + bias_c_ref[g * HPG:(g + 1) * HPG].reshape(rows, -1)
    m_c = jnp.max(s_c, axis=1, keepdims=True)
    e_cb = jnp.exp2(s_c - m_c).astype(BF16)
    pv_c = _dot(e_cb, jnp.concatenate([vkc, cmp_ones_ref[...]], axis=1))
    row_pos = q0 + (lax.broadcasted_iota(jnp.int32, (rows, LANES), 0) & (tq - 1))
    has_key = row_pos >= CMP_BLOCK - 1
    o_cmp = jnp.where(has_key, pv_c[:, :LANES] / pv_c[:, LANES:LANES + 1], 0.0)

    imp_t4 = _dot_nt(overlap_t_ref[...], e_cb)
    imp_t = None
    for h in range(HPG):
        part = imp_t4[:, h * tq:(h + 1) * tq]
        part = part[:n_blk] / part[n_blk:n_blk + 1]
        imp_t = part if imp_t is None else imp_t + part

    blk = lax.broadcasted_iota(jnp.int32, (n_blk, tq), 0)
    pos = q0 + lax.broadcasted_iota(jnp.int32, (n_blk, tq), 1)
    cur = jnp.right_shift(pos, int(math.log2(SLC_BLOCK)))
    forced = (blk == 0) | (blk == cur) | (blk == cur - 1)
    valid = blk * SLC_BLOCK <= pos
    score = jnp.where(forced, BIG, jnp.where(valid, imp_t, -BIG))
    blk_f = blk.astype(F32)
    pen_t = jnp.full((n_blk, tq), NEG_INF, F32)
    for _ in range(n_sel):
        top = jnp.max(score, axis=0, keepdims=True)
        first = jnp.min(jnp.where(score == top, blk_f, float(n_blk)), axis=0, keepdims=True)
        hit = blk_f == first
        pen_t = jnp.where(hit, 0.0, pen_t)
        score = jnp.where(hit, -jnp.inf, score)
    pieces = [jnp.zeros((tq, HEAD_DIM), F32), jnp.transpose(pen_t)]
    if n_blk < LANES - HEAD_DIM:
        pieces.append(jnp.zeros((tq, LANES - HEAD_DIM - n_blk), F32))
    pen_lanes = jnp.concatenate(pieces, axis=1)
    q_slc = jnp.where(lane_r < HEAD_DIM, q_lo, jnp.concatenate([pen_lanes] * HPG, axis=0)).astype(BF16)
    return o_cmp, q_lo, q_slc


def _nsa_kernel(seq, q_ref, vkc_ref, slc_ref, win_ref, gate_ref, bias_c_ref, bias_near_ref, bias_win_ref,
                overlap_t_ref, cmp_ones_ref, o_ref, *scratch):
    tq = ATTN_TQ
    rows = HPG * tq
    groups = range(N_KV)
    kv_sc = [scratch[4 * g:4 * g + 4] for g in groups]
    s_sc, mrun_sc, acc_sc = (scratch[4 * N_KV + i * N_KV:4 * N_KV + (i + 1) * N_KV] for i in range(3))
    qt = pl.program_id(1)
    q0 = pl.multiple_of(qt * tq, tq)
    lane = lax.broadcasted_iota(jnp.int32, (tq, LANES), 1)
    lo_half = lane < HEAD_DIM
    lane_r = lax.broadcasted_iota(jnp.int32, (rows, LANES), 1)

    @pl.when(qt == 0)
    def _():
        for g in groups:
            _build_kv_scratch(seq, slc_ref, win_ref, g, *kv_sc[g])

    sel = [_nsa_select(seq, g, q0, q_ref, vkc_ref, bias_c_ref, overlap_t_ref, cmp_ones_ref) for g in groups]
    o_cmp = [s[0] for s in sel]
    q_lo = [s[1] for s in sel]
    q_slc = [s[2] for s in sel]
    head_rows = lambda ref, g: ref[g * HPG:(g + 1) * HPG].reshape(rows, -1)

    o_win = []
    for g in groups:
        kw_sc, vw_sc = kv_sc[g][2], kv_sc[g][3]
        q_win = jnp.where(lane_r == HEAD_DIM, 1.0, q_lo[g]).astype(BF16)
        win_rows = pl.ds(q0, WINDOW + tq)
        s_w = _dot_nt(q_win, kw_sc[win_rows, :]) + head_rows(bias_win_ref, g)
        p_w = jnp.exp2(s_w - jnp.max(s_w, axis=1, keepdims=True)).astype(BF16)
        acc_w = _dot(p_w, vw_sc[win_rows, :])
        o_win.append(acc_w / acc_w[:, HEAD_DIM:HEAD_DIM + 1])

    tk = SLC_TK
    n_far = qt // (tk // tq)
    lane_tiles = lambda a: [a[:, j * LANES:(j + 1) * LANES] for j in range(a.shape[1] // LANES)]
    near0 = pl.multiple_of(q0 + (KV_PAD - SLC_NEAR_BACK), tq)

    def tile_rows(u):
        return pl.ds(pl.multiple_of(near0 - u * tk, tq), tk)

    def tile_cols(u):
        return pl.ds(pl.multiple_of(u * tk, tk), tk)

    def score_tile(u, g, bias):
        s = _dot_nt(q_slc[g], kv_sc[g][0][tile_rows(u), :])
        if bias is not None:
            s = s + bias
        s_sc[g][:, tile_cols(u)] = s
        return functools.reduce(jnp.maximum, lane_tiles(s))

    for g in groups:
        mrun_sc[g][...] = score_tile(0, g, head_rows(bias_near_ref, g))

    def pass1(u, carry):
        for g in groups:
            mrun_sc[g][...] = jnp.maximum(mrun_sc[g][...], score_tile(u, g, None))
        return carry

    lax.fori_loop(1, n_far + 1, pass1, 0)
    for g in groups:
        m_s = jnp.max(mrun_sc[g][...], axis=1, keepdims=True)
        mrun_sc[g][...] = jnp.broadcast_to(m_s, (rows, LANES))
        acc_sc[g][...] = jnp.zeros((rows, LANES), F32)

    def pass2(u, carry):
        for g in groups:
            shift = mrun_sc[g][...]
            p = jnp.exp2(s_sc[g][:, tile_cols(u)] - jnp.concatenate([shift] * (tk // LANES), axis=1))
            acc_sc[g][...] = acc_sc[g][...] + _dot(p.astype(BF16), kv_sc[g][1][tile_rows(u), :])
        return carry

    lax.fori_loop(0, n_far + 1, pass2, 0)

    for g in groups:
        acc = acc_sc[g][...]
        o_slc = acc / acc[:, HEAD_DIM:HEAD_DIM + 1]

        gates = gate_ref[0, :, g * GATE_PAD:(g + 1) * GATE_PAD]
        outs = []
        for h in range(HPG):
            sl = slice(h * tq, (h + 1) * tq)
            g_c, g_s, g_w = (gates[:, br * HPG + h:br * HPG + h + 1] for br in range(3))
            outs.append(g_c * o_cmp[g][sl] + g_s * o_slc[sl] + g_w * o_win[g][sl])
        for pair_idx in range(HPG // 2):
            even, odd = outs[2 * pair_idx], outs[2 * pair_idx + 1]
            merged = jnp.where(lo_half, even, pltpu.roll(odd, HEAD_DIM, 1))
            lanes_p = slice((g * (HPG // 2) + pair_idx) * LANES, (g * (HPG // 2) + pair_idx + 1) * LANES)
            o_ref[0, :, lanes_p] = merged.astype(o_ref.dtype)


def _nsa(q, vkc, slc, win, gates, bias_c, bias_near, bias_win, overlap_t, cmp_ones):
    b, seq, _ = q.shape
    n_c = seq // CMP_STRIDE
    n_blk = seq // SLC_BLOCK
    assert n_blk <= LANES - HEAD_DIM and seq % SLC_TK == 0 and KV_PAD >= SLC_TK
    tq = ATTN_TQ
    rows = HPG * tq
    const = lambda a: pl.BlockSpec(a.shape, lambda i, t: (0,) * a.ndim)
    tile = lambda w: pl.BlockSpec((1, tq, w), lambda i, t: (i, t, 0))
    whole_seq = lambda w: pl.BlockSpec((1, seq, w), lambda i, t: (i, 0, 0))
    per_group = lambda shape: [pltpu.VMEM(shape, F32)] * N_KV
    return pl.pallas_call(
        functools.partial(_nsa_kernel, seq),
        grid=(b, seq // tq),
        in_specs=[
            tile(ATTN_WIDTH),
            pl.BlockSpec((1, N_KV, n_c, 2 * HEAD_DIM), lambda i, t: (i, 0, 0, 0)),
            whole_seq(N_KV * 2 * HEAD_DIM), whole_seq(N_KV * 2 * HEAD_DIM),
            tile(N_KV * GATE_PAD),
            pl.BlockSpec((N_HEADS, tq, n_c), lambda i, t: (0, t, 0)),
            const(bias_near), const(bias_win),
            const(overlap_t), const(cmp_ones),
        ],
        out_specs=tile(ATTN_WIDTH),
        out_shape=jax.ShapeDtypeStruct((b, seq, ATTN_WIDTH), BF16),
        scratch_shapes=(
            [pltpu.VMEM((KV_PAD + seq, LANES), BF16)] * (4 * N_KV)
            + per_group((rows, seq)) + per_group((rows, LANES)) + per_group((rows, LANES))
        ),
        compiler_params=_compiler_params(("parallel", "arbitrary")),
        name="nsa",
    )(q, vkc, slc, win, gates, bias_c, bias_near, bias_win, overlap_t, cmp_ones)


def _mixer_inputs(x, rel_bias, w_in, b_in, cmp_pos, cmp_w1, cmp_b1, cmp_w2, cmp_b2):
    b, seq, d = x.shape
    wp, bp = _pack_in_proj(w_in, b_in, d)
    q, slc, win, cmp, gates, u, merge = _in_proj(x.reshape(b * seq, d), wp, bp, d, 512)
    n_c = seq // CMP_STRIDE
    cmp4 = cmp.reshape(b, seq, 2 * N_KV, HEAD_DIM).transpose(0, 2, 1, 3).reshape(
        b, 2 * N_KV, n_c, CMP_STRIDE * HEAD_DIM)
    vkc = _compress(cmp4, cmp_pos, cmp_w1, cmp_b1, cmp_w2, cmp_b2)
    bias_c, bias_near, bias_win = _attention_bias_tables(rel_bias, seq)
    overlap_t, cmp_ones = _nsa_constants(seq)
    o = _nsa(q.reshape(b, seq, -1), vkc, slc.reshape(b, seq, -1), win.reshape(b, seq, -1),
             gates.reshape(b, seq, -1), bias_c, bias_near, bias_win, overlap_t, cmp_ones)
    return o, u, merge


S5_HALF_GROUPS = S5_GROUPS // 2
S5_HALF_IN = S5_HALF_GROUPS * S5_GROUP
S5_HALF_STATE = S5_HALF_GROUPS * S5_STATE
S5_SCAN_LANES = 512
S5_CHUNK = 64
S5_UNROLL = 8


def _s5_params(lam_re, lam_im, log_dt, b_re, b_im, c_re, c_im, nb):
    dt = jnp.exp(log_dt.astype(F32))[:, None]
    lr, li = lam_re.astype(F32), lam_im.astype(F32)
    mag = jnp.exp(lr * dt)
    ab_re, ab_im = mag * jnp.cos(li * dt), mag * jnp.sin(li * dt)
    nr, ni = ab_re - 1.0, ab_im
    den = lr * lr + li * li
    fr, fi = (nr * lr + ni * li) / den, (ni * lr - nr * li) / den
    br, bim = b_re.astype(F32), b_im.astype(F32)
    bb_re = fr[..., None] * br - fi[..., None] * bim
    bb_im = fr[..., None] * bim + fi[..., None] * br
    eye = jnp.eye(S5_HALF_GROUPS, dtype=F32)

    def in_mat(bb):
        t = bb.reshape(2, S5_HALF_GROUPS, S5_STATE, S5_GROUP)
        m = jnp.einsum('kgph,gj->kghjp', t, eye)
        return m.reshape(2, S5_HALF_IN, S5_HALF_STATE)

    def out_mat(c):
        t = c.astype(F32).reshape(2, S5_HALF_GROUPS, S5_GROUP, S5_STATE)
        m = jnp.einsum('kghp,gj->kgpjh', t, eye)
        return m.reshape(2, S5_HALF_STATE, S5_HALF_IN)

    bmat = jnp.concatenate([in_mat(bb_re), in_mat(bb_im)], axis=2).astype(BF16)
    cmat = jnp.concatenate([out_mat(c_re), -out_mat(c_im)], axis=1).astype(BF16)
    a = jnp.concatenate([ab_re.reshape(2, S5_HALF_STATE), ab_im.reshape(2, S5_HALF_STATE)], axis=1)
    a = jnp.broadcast_to(a.reshape(1, 4 * S5_HALF_STATE), (nb, 4 * S5_HALF_STATE))
    return bmat, cmat, a


def _s5_kernel(u_ref, bmat_ref, cmat_ref, a_ref, d_ref, y_ref, ut_sc, x_sc, st_sc):
    nb, t_len, _ = u_ref.shape
    half_w = 2 * S5_HALF_STATE

    @pl.when(pl.program_id(0) == 0)
    def _():
        st_sc[...] = jnp.zeros_like(st_sc)

    n_cb = ut_sc.shape[0]
    for b in range(nb):
        for cb in range(n_cb):
            ut_sc[cb, pl.ds(b, t_len, stride=nb), :] = u_ref[b, :, cb * LANES:(cb + 1) * LANES]
    ut = jnp.concatenate([ut_sc[cb] for cb in range(n_cb)], axis=1)
    ub = ut.astype(BF16)
    for k in range(2):
        x_sc[:, k * half_w:(k + 1) * half_w] = _dot(ub[:, k * S5_HALF_IN:(k + 1) * S5_HALF_IN], bmat_ref[k])

    for k in range(2):
        for j in range(S5_HALF_STATE // S5_SCAN_LANES):
            re0 = k * half_w + j * S5_SCAN_LANES
            im0 = re0 + S5_HALF_STATE
            re_sl, im_sl = pl.ds(re0, S5_SCAN_LANES), pl.ds(im0, S5_SCAN_LANES)
            ar, ai = a_ref[:, re_sl], a_ref[:, im_sl]

            def steps(c, carry):
                xr, xi = carry
                for s in range(S5_UNROLL):
                    rows = pl.ds(pl.multiple_of((c * S5_UNROLL + s) * nb, nb), nb)
                    nxr = ar * xr - ai * xi + x_sc[rows, re_sl]
                    nxi = ar * xi + ai * xr + x_sc[rows, im_sl]
                    x_sc[rows, re_sl] = nxr
                    x_sc[rows, im_sl] = nxi
                    xr, xi = nxr, nxi
                return xr, xi

            xr, xi = lax.fori_loop(0, t_len // S5_UNROLL, steps, (st_sc[:, re_sl], st_sc[:, im_sl]))
            st_sc[:, re_sl] = xr
            st_sc[:, im_sl] = xi

    xs = x_sc[...].astype(BF16)
    y = jnp.concatenate([_dot(xs[:, k * half_w:(k + 1) * half_w], cmat_ref[k]) for k in range(2)], axis=1)
    y = y + d_ref[...] * ut
    for cb in range(n_cb):
        ut_sc[cb] = y[:, cb * LANES:(cb + 1) * LANES]
    for b in range(nb):
        for cb in range(n_cb):
            y_ref[b, :, cb * LANES:(cb + 1) * LANES] = ut_sc[cb, pl.ds(b, t_len, stride=nb), :]


def _s5(u, bmat, cmat, a, d_skip):
    nb, seq, w = u.shape
    t_len = min(S5_CHUNK, seq)
    full = lambda shape: pl.BlockSpec(shape, lambda c: (0,) * len(shape))
    return pl.pallas_call(
        _s5_kernel,
        grid=(seq // t_len,),
        in_specs=[
            pl.BlockSpec((nb, t_len, w), lambda c: (0, c, 0)),
            full(bmat.shape), full(cmat.shape), full(a.shape), full((1, w)),
        ],
        out_specs=pl.BlockSpec((nb, t_len, w), lambda c: (0, c, 0)),
        out_shape=jax.ShapeDtypeStruct((nb, seq, w), F32),
        scratch_shapes=[
            pltpu.VMEM((w // LANES, t_len * nb, LANES), F32),
            pltpu.VMEM((t_len * nb, 4 * S5_HALF_STATE), F32),
            pltpu.VMEM((nb, 4 * S5_HALF_STATE), F32),
        ],
        compiler_params=_compiler_params(("arbitrary",)),
        name="s5",
    )(u, bmat, cmat, a, d_skip.reshape(1, w).astype(F32))


ROUTE_PAD = LANES
_R_E1, _R_E2, _R_W1, _R_W2, _R_RANK1, _R_RANK2 = range(6)


def _layer_norm(t, g, b):
    mu = jnp.mean(t, axis=1, keepdims=True)
    c = t - mu
    var = jnp.mean(c * c, axis=1, keepdims=True)
    return c * lax.rsqrt(var + LN_EPS) * g + b


def _post_kernel(x_ref, o_ref, y_ref, m_ref, wup_ref, wval_ref, wgate_ref, bgate_ref, wout_ref,
                 g1_ref, b1_ref, wr_ref, br_ref, h_ref, route_ref, cnt_ref, run_sc):
    tm, d = x_ref.shape

    @pl.when(pl.program_id(0) == 0)
    def _():
        run_sc[...] = jnp.zeros_like(run_sc)

    y_a = _dot(o_ref[...], wup_ref[...])
    z = _gelu_tanh(y_ref[...]).astype(BF16)
    y_b = _dot(z, wval_ref[...]) * jax.nn.sigmoid(_dot(z, wgate_ref[...]) + bgate_ref[...])
    mixed = m_ref[:, :d] * y_a + m_ref[:, d:] * y_b
    t = DN_ALPHA * x_ref[...] + _dot(mixed.astype(BF16), wout_ref[...])
    h = _layer_norm(t, g1_ref[...], b1_ref[...])
    h_ref[...] = h

    logits = _dot(h.astype(BF16), wr_ref[...]) + br_ref[...]
    lane = lax.broadcasted_iota(jnp.int32, (tm, ROUTE_PAD), 1)
    lane_f = lane.astype(F32)
    is_group = lane < N_EGROUPS

    def first_max(v):
        top = jnp.max(v, axis=1, keepdims=True)
        idx = jnp.min(jnp.where(v == top, lane_f, float(ROUTE_PAD)), axis=1, keepdims=True)
        return top, idx

    g_max, g_top = first_max(jnp.where(is_group, logits, -jnp.inf))
    p_group = 1.0 / jnp.sum(jnp.where(is_group, jnp.exp(logits - g_max), 0.0), axis=1, keepdims=True)
    grp_of_lane = jnp.right_shift(lane - N_EGROUPS, int(math.log2(EXPERTS_PER_GROUP))).astype(F32)
    in_group = (lane >= N_EGROUPS) & (lane < N_EGROUPS + N_EXPERTS) & (grp_of_lane == g_top)
    e_log = jnp.where(in_group, logits, -jnp.inf)
    v1, i1 = first_max(e_log)
    hit1 = lane_f == i1
    v2, i2 = first_max(jnp.where(hit1, -jnp.inf, e_log))
    hit2 = lane_f == i2
    e2 = jnp.exp(v2 - v1)
    w1 = p_group / (1.0 + e2)
    w2 = p_group * e2 / (1.0 + e2)

    hits = jnp.where(hit1 | hit2, 1.0, 0.0)
    row = lax.broadcasted_iota(jnp.int32, (tm, tm), 0)
    col = lax.broadcasted_iota(jnp.int32, (tm, tm), 1)
    earlier = jnp.where(col < row, 1.0, 0.0).astype(BF16)
    before = _dot(earlier, hits.astype(BF16)) + run_sc[...]
    rank1 = jnp.sum(jnp.where(hit1, before, 0.0), axis=1, keepdims=True)
    rank2 = jnp.sum(jnp.where(hit2, before, 0.0), axis=1, keepdims=True)
    run_sc[...] = run_sc[...] + jnp.sum(hits, axis=0, keepdims=True)
    cnt_ref[...] = run_sc[...]

    rec = jnp.zeros((tm, ROUTE_PAD), F32)
    for slot, val in ((_R_E1, i1 - N_EGROUPS), (_R_E2, i2 - N_EGROUPS), (_R_W1, w1), (_R_W2, w2),
                      (_R_RANK1, rank1), (_R_RANK2, rank2)):
        rec = jnp.where(lane == slot, val, rec)
    route_ref[...] = rec


def _post(x2d, o2d, y2d, merge, w_attn_up, s5_w_val, s5_w_gate, s5_b_gate, w_out, ln1_g, ln1_b,
          router_w_group, router_b_group, router_w_expert, router_b_expert, tm):
    n, d = x2d.shape
    rpad = ROUTE_PAD - N_EGROUPS - N_EXPERTS
    wr = jnp.concatenate([router_w_group, router_w_expert, jnp.zeros((d, rpad), F32)], axis=1).astype(BF16)
    br = jnp.concatenate([router_b_group, router_b_expert, jnp.zeros((rpad,), F32)]).reshape(1, -1).astype(F32)
    row = lambda w: pl.BlockSpec((tm, w), lambda i: (i, 0))
    full = lambda a: pl.BlockSpec(a.shape, lambda i: (0,) * a.ndim)
    weights = [w_attn_up.astype(BF16), s5_w_val.astype(BF16), s5_w_gate.astype(BF16),
               s5_b_gate.reshape(1, d).astype(F32), w_out.astype(BF16),
               ln1_g.reshape(1, d).astype(F32), ln1_b.reshape(1, d).astype(F32), wr, br]
    return pl.pallas_call(
        _post_kernel,
        grid=(n // tm,),
        in_specs=[row(d), row(ATTN_WIDTH), row(S5_WIDTH), row(2 * d)] + [full(w) for w in weights],
        out_specs=[row(d), row(ROUTE_PAD), pl.BlockSpec((1, ROUTE_PAD), lambda i: (0, 0))],
        out_shape=[jax.ShapeDtypeStruct((n, d), F32), jax.ShapeDtypeStruct((n, ROUTE_PAD), F32),
                   jax.ShapeDtypeStruct((1, ROUTE_PAD), F32)],
        scratch_shapes=[pltpu.VMEM((1, ROUTE_PAD), F32)],
        compiler_params=_compiler_params(("arbitrary",)),
        name="post_mixer",
    )(x2d, o2d, y2d, merge, *weights)


def _plan_kernel(route_ref, cnt_ref, dest_ref):
    tm = route_ref.shape[0]
    lane8 = lax.broadcasted_iota(jnp.int32, (SUBLANES, ROUTE_PAD), 1)
    counts = jnp.broadcast_to(cnt_ref[...], (SUBLANES, ROUTE_PAD)).astype(jnp.int32)
    shift = int(math.log2(MOE_BLOCK))
    padded = jnp.left_shift(jnp.right_shift(counts + (MOE_BLOCK - 1), shift), shift)
    incl = padded
    step = 1
    while step < ROUTE_PAD:
        incl = incl + jnp.where(lane8 >= step, pltpu.roll(incl, step, 1), 0)
        step *= 2
    pstart = (incl - padded)[0:1].astype(F32)
    route = route_ref[...]
    expert_of_lane = (lax.broadcasted_iota(jnp.int32, (tm, ROUTE_PAD), 1) - N_EGROUPS).astype(F32)
    lane = lax.broadcasted_iota(jnp.int32, (tm, ROUTE_PAD), 1)

    def dest(e_slot, rank_slot):
        hit = expert_of_lane == route[:, e_slot:e_slot + 1]
        return jnp.sum(jnp.where(hit, pstart, 0.0), axis=1, keepdims=True) + route[:, rank_slot:rank_slot + 1]

    d1 = dest(_R_E1, _R_RANK1)
    d2 = dest(_R_E2, _R_RANK2)
    dest_ref[...] = jnp.where(lane == 0, d1, jnp.where(lane == 1, d2, 0.0)).astype(jnp.int32)


def _plan(route, counts_row):
    n = route.shape[0]
    tm = min(1024, n)
    return pl.pallas_call(
        _plan_kernel,
        grid=(n // tm,),
        in_specs=[pl.BlockSpec((tm, ROUTE_PAD), lambda i: (i, 0)),
                  pl.BlockSpec((1, ROUTE_PAD), lambda i: (0, 0))],
        out_specs=pl.BlockSpec((tm, ROUTE_PAD), lambda i: (i, 0)),
        out_shape=jax.ShapeDtypeStruct((n, ROUTE_PAD), jnp.int32),
        compiler_params=_compiler_params(("parallel",)),
        name="moe_plan",
    )(route, counts_row)


SC_GATHER_ROWS = 32


def _sc_row_gather(table, idx):
    n_idx = idx.shape[0]
    d = table.shape[1]
    info = plsc.get_sparse_core_info()
    n_workers = info.num_cores * info.num_subcores
    per_worker = n_idx // n_workers
    assert n_idx % (n_workers * SC_GATHER_ROWS) == 0
    mesh = plsc.VectorSubcoreMesh(core_axis_name="c", subcore_axis_name="s")

    @functools.partial(
        pl.kernel, mesh=mesh,
        out_type=jax.ShapeDtypeStruct((n_idx, d), table.dtype),
        scratch_types=[
            pltpu.VMEM((SC_GATHER_ROWS,), jnp.int32),
            pltpu.VMEM((SC_GATHER_ROWS, d), table.dtype),
            pltpu.SemaphoreType.DMA,
        ],
    )
    def gather(table_hbm, idx_hbm, out_hbm, idx_v, rows_v, sem):
        worker = lax.axis_index("s") * info.num_cores + lax.axis_index("c")
        base = worker * per_worker

        @pl.loop(0, per_worker // SC_GATHER_ROWS)
        def _(j):
            off = base + j * SC_GATHER_ROWS
            pltpu.sync_copy(idx_hbm.at[pl.ds(off, SC_GATHER_ROWS)], idx_v)
            pltpu.async_copy(table_hbm.at[idx_v], rows_v, sem).wait()
            pltpu.sync_copy(rows_v, out_hbm.at[pl.ds(off, SC_GATHER_ROWS)])

    return gather(table, idx)


def _sc_row_scatter(rows, idx_a, idx_b, n_out):
    n, d = rows.shape
    info = plsc.get_sparse_core_info()
    n_workers = info.num_cores * info.num_subcores
    per_worker = n // n_workers
    assert n % (n_workers * SC_GATHER_ROWS) == 0
    mesh = plsc.VectorSubcoreMesh(core_axis_name="c", subcore_axis_name="s")

    @functools.partial(
        pl.kernel, mesh=mesh,
        out_type=jax.ShapeDtypeStruct((n_out, d), rows.dtype),
        scratch_types=[
            pltpu.VMEM((SC_GATHER_ROWS,), jnp.int32),
            pltpu.VMEM((SC_GATHER_ROWS,), jnp.int32),
            pltpu.VMEM((SC_GATHER_ROWS, d), rows.dtype),
        ],
    )
    def scatter(rows_hbm, idx_a_hbm, idx_b_hbm, out_hbm, idx_a_v, idx_b_v, rows_v):
        worker = lax.axis_index("s") * info.num_cores + lax.axis_index("c")
        base = worker * per_worker

        @pl.loop(0, per_worker // SC_GATHER_ROWS)
        def _(j):
            src = pl.ds(base + j * SC_GATHER_ROWS, SC_GATHER_ROWS)
            pltpu.sync_copy(rows_hbm.at[src], rows_v)
            pltpu.sync_copy(idx_a_hbm.at[src], idx_a_v)
            pltpu.sync_copy(idx_b_hbm.at[src], idx_b_v)
            pltpu.sync_copy(rows_v, out_hbm.at[idx_a_v])
            pltpu.sync_copy(rows_v, out_hbm.at[idx_b_v])

    return scatter(rows, idx_a, idx_b)


def _expert_kernel(blk_exp_ref, blk_valid_ref, x_ref, wg_ref, wu_ref, wd_ref, y_ref, wg_sc, wu_sc, wd_sc):
    i = pl.program_id(0)
    n_valid = blk_valid_ref[i]

    @pl.when((i == 0) | (blk_exp_ref[i] != blk_exp_ref[jnp.maximum(i - 1, 0)]))
    def _():
        wg_sc[...] = wg_ref[0].astype(BF16)
        wu_sc[...] = wu_ref[0].astype(BF16)
        wd_sc[...] = wd_ref[0].astype(BF16)

    @pl.when(n_valid > 0)
    def _():
        row = lax.broadcasted_iota(jnp.int32, x_ref.shape, 0)
        xb = jnp.where(row < n_valid, x_ref[...], 0.0).astype(BF16)
        h_gate = _dot(xb, wg_sc[...])
        h_up = _dot(xb, wu_sc[...])
        hb = (h_gate * jax.nn.sigmoid(h_gate) * h_up).astype(BF16)
        y_ref[...] = _dot(hb, wd_sc[...])

    @pl.when(n_valid == 0)
    def _():
        y_ref[...] = jnp.zeros_like(y_ref)


def _experts(blk_expert, blk_valid, xs, w_gate, w_up, w_down):
    n_blocks = blk_expert.shape[0]
    d = xs.shape[1]
    grid_spec = pltpu.PrefetchScalarGridSpec(
        num_scalar_prefetch=2,
        grid=(n_blocks,),
        in_specs=[
            pl.BlockSpec((MOE_BLOCK, d), lambda i, be, bv: (i, 0)),
            pl.BlockSpec((1, d, D_EXPERT), lambda i, be, bv: (be[i], 0, 0)),
            pl.BlockSpec((1, d, D_EXPERT), lambda i, be, bv: (be[i], 0, 0)),
            pl.BlockSpec((1, D_EXPERT, d), lambda i, be, bv: (be[i], 0, 0)),
        ],
        out_specs=pl.BlockSpec((MOE_BLOCK, d), lambda i, be, bv: (i, 0)),
        scratch_shapes=[
            pltpu.VMEM((d, D_EXPERT), BF16),
            pltpu.VMEM((d, D_EXPERT), BF16),
            pltpu.VMEM((D_EXPERT, d), BF16),
        ],
    )
    return pl.pallas_call(
        _expert_kernel,
        grid_spec=grid_spec,
        out_shape=jax.ShapeDtypeStruct((n_blocks * MOE_BLOCK, d), F32),
        compiler_params=_compiler_params(("arbitrary",)),
        name="experts",
    )(blk_expert, blk_valid, xs, w_gate, w_up, w_down)


COMBINE_TM = 512


def _combine_kernel(h_ref, y1_ref, y2_ref, route_ref, g2_ref, b2_ref, out_ref):
    route = route_ref[...]
    w1 = route[:, _R_W1:_R_W1 + 1]
    w2 = route[:, _R_W2:_R_W2 + 1]
    t = DN_ALPHA * h_ref[...] + (y1_ref[...] * w1 + y2_ref[...] * w2)
    out_ref[...] = _layer_norm(t, g2_ref[...], b2_ref[...])


def _combine(yg, h2d, route, ln2_g, ln2_b):
    n, d = h2d.shape
    tm = min(COMBINE_TM, n)
    n_tiles = n // tm
    row = lambda w: pl.BlockSpec((tm, w), lambda i: (i, 0))
    vec = pl.BlockSpec((1, d), lambda i: (0, 0))
    return pl.pallas_call(
        _combine_kernel,
        grid=(n_tiles,),
        in_specs=[row(d), row(d), pl.BlockSpec((tm, d), lambda i: (i + n_tiles, 0)), row(ROUTE_PAD), vec, vec],
        out_specs=row(d),
        out_shape=jax.ShapeDtypeStruct((n, d), F32),
        compiler_params=_compiler_params(("parallel",)),
        name="combine",
    )(h2d, yg, yg, route, ln2_g.reshape(1, d).astype(F32), ln2_b.reshape(1, d).astype(F32))


def _moe(h2d, route, counts_row, w_gate, w_up, w_down, ln2_g, ln2_b):
    n, d = h2d.shape
    dest = _plan(route, counts_row)
    dest1, dest2 = dest[:, 0], dest[:, 1]
    counts = counts_row[0, N_EGROUPS:N_EGROUPS + N_EXPERTS].astype(jnp.int32)
    padded = (counts + MOE_BLOCK - 1) // MOE_BLOCK * MOE_BLOCK
    pend = jnp.cumsum(padded)
    n_blocks = -(-(n * TOP_K_IN_GROUP) // MOE_BLOCK) + N_EXPERTS
    blk_row0 = jnp.arange(n_blocks, dtype=jnp.int32) * MOE_BLOCK
    blk_expert = jnp.minimum(jnp.sum(pend[None, :] <= blk_row0[:, None], axis=1), N_EXPERTS - 1).astype(jnp.int32)
    blk_valid = jnp.clip((pend - padded + counts)[blk_expert] - blk_row0, 0, MOE_BLOCK).astype(jnp.int32)
    xs = _sc_row_scatter(h2d, dest1, dest2, n_blocks * MOE_BLOCK)
    yb = _experts(blk_expert, blk_valid, xs, w_gate, w_up, w_down)
    yg = _sc_row_gather(yb, jnp.concatenate([dest1, dest2]))
    return _combine(yg, h2d, route, ln2_g, ln2_b)


def kernel(x, rel_bias, w_in, b_in, cmp_pos, cmp_w1, cmp_b1, cmp_w2, cmp_b2, w_attn_up, s5_lambda_re, s5_lambda_im, s5_log_dt, s5_b_re, s5_b_im, s5_c_re, s5_c_im, s5_d, s5_w_val, s5_w_gate, s5_b_gate, w_out, ln1_g, ln1_b, router_w_group, router_b_group, router_w_expert, router_b_expert, exp_w_gate, exp_w_up, exp_w_down, ln2_g, ln2_b):
    b, seq, d = x.shape
    n = b * seq
    assert w_in.shape[0] == DEPTH
    l = 0
    o, u, merge = _mixer_inputs(x, rel_bias, w_in[l], b_in[l], cmp_pos[l], cmp_w1[l], cmp_b1[l],
                                cmp_w2[l], cmp_b2[l])
    bmat, cmat, a = _s5_params(s5_lambda_re[l], s5_lambda_im[l], s5_log_dt[l], s5_b_re[l], s5_b_im[l],
                               s5_c_re[l], s5_c_im[l], b)
    y_s = _s5(u.reshape(b, seq, S5_WIDTH), bmat, cmat, a, s5_d[l])
    h2d, route, counts = _post(x.reshape(n, d), o.reshape(n, ATTN_WIDTH), y_s.reshape(n, S5_WIDTH), merge,
                               w_attn_up[l], s5_w_val[l], s5_w_gate[l], s5_b_gate[l], w_out[l], ln1_g[l],
                               ln1_b[l], router_w_group[l], router_b_group[l], router_w_expert[l],
                               router_b_expert[l], 512)
    out = _moe(h2d, route, counts, exp_w_gate[l], exp_w_up[l], exp_w_down[l], ln2_g[l], ln2_b[l])
    return out.reshape(b, seq, d)
```

```python
import functools
import math

import jax
import jax.numpy as jnp
from jax import lax
from jax.experimental import pallas as pl
from jax.experimental.pallas import tpu as pltpu
from jax.experimental.pallas import tpu_sc as plsc

F32 = jnp.float32
BF16 = jnp.bfloat16

N_HEADS = 8
HEAD_DIM = 64
N_KV = 2
HPG = N_HEADS // N_KV
CMP_STRIDE = 16
CMP_BLOCK = 2 * CMP_STRIDE
CMP_HIDDEN = 128
SLC_BLOCK = 64
N_SEL = 16
WINDOW = 512
REL_BUCKETS = 32
REL_MAX_DIST = 128
S5_WIDTH = 512
S5_GROUP = 16
S5_GROUPS = S5_WIDTH // S5_GROUP
S5_STATE = 64
N_EGROUPS = 8
EXPERTS_PER_GROUP = 8
N_EXPERTS = N_EGROUPS * EXPERTS_PER_GROUP
TOP_K_IN_GROUP = 2
D_EXPERT = 256
EXPERT_BLOCK = 128
DEPTH = 1
DN_ALPHA = (2.0 * DEPTH) ** 0.25
LN_EPS = 1e-5
NEG_INF = -1e30
BIG = 1e9
LOG2_E = math.log2(math.e)
MOE_BLOCK = 256

ATTN_WIDTH = N_HEADS * HEAD_DIM
KV_WIDTH = N_KV * HEAD_DIM
KV_OFF = ATTN_WIDTH
NSA_GATE_OFF = KV_OFF + 6 * KV_WIDTH
S5_OFF = NSA_GATE_OFF + 3 * N_HEADS
MERGE_OFF = S5_OFF + S5_WIDTH

LANES = 128
SUBLANES = 8
VMEM_LIMIT_BYTES = 56 * 1024 * 1024

ATTN_TQ = 128
SLC_TK = 512
SLC_NEAR_BACK = SLC_TK - ATTN_TQ
KV_PAD = WINDOW
SOFTMAX_SHIFT_MARGIN = 100.0
SOFTMAX_BOUND_SLACK = 1.001
GATE_PAD = LANES


def _gelu_tanh(x):
    c = math.sqrt(2.0 / math.pi)
    return x * (0.5 * (1.0 + jnp.tanh(c * (x + 0.044715 * (x * x * x)))))


def _dot(a, b):
    return jnp.dot(a, b, preferred_element_type=F32)


def _dot_nt(a, b):
    return lax.dot_general(a, b, (((1,), (1,)), ((), ())), preferred_element_type=F32)


def _compiler_params(semantics):
    return pltpu.CompilerParams(dimension_semantics=semantics, vmem_limit_bytes=VMEM_LIMIT_BYTES)


def _in_proj_layout(d_model):
    widths = (ATTN_WIDTH, 2 * KV_WIDTH, 2 * KV_WIDTH, 2 * KV_WIDTH, N_KV * GATE_PAD, S5_WIDTH, 2 * d_model)
    offs = [0]
    for w in widths:
        offs.append(offs[-1] + w)
    return widths, offs


def _pack_in_proj(w_in, b_in, d_model):
    def kv_cols(j):
        return KV_OFF + j * KV_WIDTH

    def pair(jk, jv):
        cols = []
        for g in range(N_KV):
            cols.append(jnp.arange(kv_cols(jk) + g * HEAD_DIM, kv_cols(jk) + (g + 1) * HEAD_DIM))
            cols.append(jnp.arange(kv_cols(jv) + g * HEAD_DIM, kv_cols(jv) + (g + 1) * HEAD_DIM))
        return jnp.concatenate(cols)

    idx = jnp.concatenate([
        jnp.arange(0, ATTN_WIDTH),
        pair(2, 3),
        pair(4, 5),
        jnp.arange(kv_cols(0), kv_cols(2)),
    ])
    idx2 = jnp.concatenate([jnp.arange(S5_OFF, S5_OFF + S5_WIDTH),
                            jnp.arange(MERGE_OFF, MERGE_OFF + 2 * d_model)])
    gpad = GATE_PAD - 3 * HPG
    w_parts, b_parts = [w_in[:, idx]], [b_in[idx]]
    for g in range(N_KV):
        cols = jnp.asarray([NSA_GATE_OFF + (g * HPG + h) * 3 + j for j in range(3) for h in range(HPG)])
        w_parts += [w_in[:, cols], jnp.zeros((d_model, gpad), F32)]
        b_parts += [b_in[cols], jnp.zeros((gpad,), F32)]
    w = jnp.concatenate(w_parts + [w_in[:, idx2]], axis=1)
    b = jnp.concatenate(b_parts + [b_in[idx2]])
    return w.astype(BF16), b.reshape(1, -1).astype(F32)


def _in_proj_kernel(offs, x_ref, w_ref, b_ref, q_ref, slc_ref, win_ref, cmp_ref, g_ref, u_ref, m_ref):
    xb = x_ref[...].astype(BF16)

    def proj(i):
        c0, c1 = offs[i], offs[i + 1]
        return _dot(xb, w_ref[:, c0:c1]) + b_ref[:, c0:c1]

    q_ref[...] = (proj(0) * (HEAD_DIM ** -0.5 * LOG2_E)).astype(BF16)
    slc_ref[...] = proj(1).astype(BF16)
    win_ref[...] = proj(2).astype(BF16)
    cmp_ref[...] = proj(3)
    g_ref[...] = jax.nn.sigmoid(proj(4))
    u_ref[...] = proj(5)
    m_ref[...] = jax.nn.sigmoid(proj(6))


def _in_proj(x2d, w_packed, b_packed, d_model, tm):
    n = x2d.shape[0]
    widths, offs = _in_proj_layout(d_model)
    ncols = offs[-1]
    dtypes = (BF16, BF16, BF16, F32, F32, F32, F32)
    return pl.pallas_call(
        functools.partial(_in_proj_kernel, tuple(offs)),
        grid=(n // tm,),
        in_specs=[
            pl.BlockSpec((tm, d_model), lambda i: (i, 0)),
            pl.BlockSpec((d_model, ncols), lambda i: (0, 0)),
            pl.BlockSpec((1, ncols), lambda i: (0, 0)),
        ],
        out_specs=[pl.BlockSpec((tm, w), lambda i: (i, 0)) for w in widths],
        out_shape=[jax.ShapeDtypeStruct((n, w), dt) for w, dt in zip(widths, dtypes)],
        compiler_params=_compiler_params(("parallel",)),
        name="in_proj",
    )(x2d, w_packed, b_packed)


def _compress_kernel(ck_ref, cv_ref, pos_ref, w1_ref, b1_ref, w2_ref, b2_ref, out_ref):
    n_c = ck_ref.shape[2]
    outs = []
    for i, c_ref in enumerate((ck_ref, cv_ref)):
        c = c_ref[0, 0]
        lo = (c + pos_ref[i, 0:1, :]).astype(BF16)
        hi = (c + pos_ref[i, 1:2, :]).astype(BF16)
        p_lo = _dot(lo, w1_ref[i, 0])
        p_hi = _dot(hi, w1_ref[i, 1])
        hid = p_lo + pltpu.roll(p_hi, n_c - 1, 0) + b1_ref[i]
        hid = _gelu_tanh(hid).astype(BF16)
        outs.append(_dot(hid, w2_ref[i]) + b2_ref[i])
    out_ref[0, 0] = jnp.concatenate(outs[::-1], axis=1).astype(BF16)


def _compress(cmp4, cmp_pos, cmp_w1, cmp_b1, cmp_w2, cmp_b2):
    b, _, n_c, cw = cmp4.shape
    half = CMP_STRIDE * HEAD_DIM
    pos = cmp_pos.reshape(2, 2, half).astype(F32)
    w1 = cmp_w1.reshape(2, 2, half, CMP_HIDDEN).astype(BF16)
    b1 = cmp_b1.reshape(2, 1, CMP_HIDDEN).astype(F32)
    w2 = cmp_w2.astype(BF16)
    b2 = cmp_b2.reshape(2, 1, HEAD_DIM).astype(F32)
    full = lambda shape: pl.BlockSpec(shape, lambda i, g: (0,) * len(shape))
    return pl.pallas_call(
        _compress_kernel,
        grid=(b, N_KV),
        in_specs=[
            pl.BlockSpec((1, 1, n_c, cw), lambda i, g: (i, g, 0, 0)),
            pl.BlockSpec((1, 1, n_c, cw), lambda i, g: (i, N_KV + g, 0, 0)),
            full((2, 2, half)),
            full((2, 2, half, CMP_HIDDEN)),
            full((2, 1, CMP_HIDDEN)),
            full((2, CMP_HIDDEN, HEAD_DIM)),
            full((2, 1, HEAD_DIM)),
        ],
        out_specs=pl.BlockSpec((1, 1, n_c, 2 * HEAD_DIM), lambda i, g: (i, g, 0, 0)),
        out_shape=jax.ShapeDtypeStruct((b, N_KV, n_c, 2 * HEAD_DIM), BF16),
        compiler_params=_compiler_params(("parallel", "parallel")),
        name="compress",
    )(cmp4, cmp4, pos, w1, b1, w2, b2)


def _t5_bucket(dist):
    n = jnp.maximum(dist, 0)
    max_exact = REL_BUCKETS // 2
    nf = jnp.maximum(n, 1).astype(F32)
    large = max_exact + (jnp.log(nf / max_exact) / math.log(REL_MAX_DIST / max_exact)
                         * (REL_BUCKETS - max_exact)).astype(jnp.int32)
    large = jnp.minimum(large, REL_BUCKETS - 1)
    return jnp.where(n < max_exact, n, large)


def _bucket_thresholds():
    buckets = _t5_bucket(jnp.arange(REL_MAX_DIST + 1))
    return jnp.sum(buckets[None, :] < jnp.arange(REL_BUCKETS)[:, None], axis=1).astype(jnp.int32)


def _bias_of_dist(dist, head, thr_ref, tbl_ref):
    bias = jnp.full(dist.shape, tbl_ref[head], F32)
    for k in range(1, REL_BUCKETS):
        bias = jnp.where(dist >= thr_ref[k], tbl_ref[k * N_HEADS + head], bias)
    return bias


BIAS_ROWS = 32


def _bias_c_kernel(thr_ref, tbl_ref, out_ref):
    _, tr, n_c = out_ref.shape
    r0 = pl.program_id(0) * tr

    def chunk(ci, carry):
        row0 = pl.multiple_of(ci * BIAS_ROWS, BIAS_ROWS)
        pos = r0 + row0 + lax.broadcasted_iota(jnp.int32, (BIAS_ROWS, n_c), 0)
        key_end = lax.broadcasted_iota(jnp.int32, (BIAS_ROWS, n_c), 1) * CMP_STRIDE + (CMP_BLOCK - 1)
        dist = pos - key_end
        for h in range(N_HEADS):
            bias = _bias_of_dist(dist, h, thr_ref, tbl_ref)
            out_ref[h, pl.ds(row0, BIAS_ROWS), :] = jnp.where(dist >= 0, bias, NEG_INF)
        return carry

    lax.fori_loop(0, tr // BIAS_ROWS, chunk, 0)


def _bias_near_kernel(thr_ref, tbl_ref, near_ref, win_ref):
    tq = ATTN_TQ
    h = pl.program_id(0)
    far_bias = tbl_ref[(REL_BUCKETS - 1) * N_HEADS + h]

    def table(out_ref, lo_keys, window, offset):
        width = out_ref.shape[2]

        def chunk(ci, carry):
            row0 = pl.multiple_of(ci * BIAS_ROWS, BIAS_ROWS)
            dist = (lo_keys + row0 + lax.broadcasted_iota(jnp.int32, (BIAS_ROWS, width), 0)
                    - lax.broadcasted_iota(jnp.int32, (BIAS_ROWS, width), 1))
            visible = (dist >= 0) & (dist < window)
            bias = jnp.full(dist.shape, tbl_ref[h], F32)
            for k in range(1, REL_BUCKETS):
                bias = jnp.where(dist >= thr_ref[k], tbl_ref[k * N_HEADS + h], bias)
            out_ref[0, pl.ds(row0, BIAS_ROWS), :] = jnp.where(visible, bias - offset, NEG_INF)
            return carry

        lax.fori_loop(0, tq // BIAS_ROWS, chunk, 0)

    table(near_ref, SLC_NEAR_BACK, 1 << 30, far_bias)
    table(win_ref, WINDOW, WINDOW, 0.0)


def _attention_bias_tables(rel_bias, seq):
    tbl = (rel_bias.astype(F32) * LOG2_E).reshape(REL_BUCKETS * N_HEADS)
    thr = _bucket_thresholds()
    tq = ATTN_TQ
    n_c = seq // CMP_STRIDE
    smem = pl.BlockSpec(memory_space=pltpu.SMEM)
    tr = min(512, seq)
    bias_c = pl.pallas_call(
        _bias_c_kernel,
        grid=(seq // tr,),
        in_specs=[smem, smem],
        out_specs=pl.BlockSpec((N_HEADS, tr, n_c), lambda i: (0, i, 0)),
        out_shape=jax.ShapeDtypeStruct((N_HEADS, seq, n_c), F32),
        compiler_params=_compiler_params(("parallel",)),
        name="bias_cmp",
    )(thr, tbl)
    head_block = lambda w: pl.BlockSpec((1, tq, w), lambda h: (h, 0, 0))
    widths = (SLC_NEAR_BACK + tq, WINDOW + tq)
    bias_near, bias_win = pl.pallas_call(
        _bias_near_kernel,
        grid=(N_HEADS,),
        in_specs=[smem, smem],
        out_specs=[head_block(w) for w in widths],
        out_shape=[jax.ShapeDtypeStruct((N_HEADS, tq, w), F32) for w in widths],
        compiler_params=_compiler_params(("parallel",)),
        name="bias_near",
    )(thr, tbl)
    return bias_c, bias_near, bias_win


def _nsa_constants(seq):
    n_c = seq // CMP_STRIDE
    n_blk = seq // SLC_BLOCK
    cmp_start = jnp.arange(n_c) * CMP_STRIDE
    blk_start = jnp.arange(n_blk) * SLC_BLOCK
    overlap_t = ((cmp_start[None, :] <= blk_start[:, None] + SLC_BLOCK - 1)
                 & (cmp_start[None, :] + CMP_BLOCK - 1 >= blk_start[:, None]))
    overlap_t = overlap_t & (cmp_start[None, :] + CMP_BLOCK <= seq)
    ones_rows = jnp.arange(SUBLANES)[:, None] == 0
    overlap_t = jnp.concatenate([overlap_t, jnp.broadcast_to(ones_rows, (SUBLANES, n_c))], axis=0)
    cmp_ones = jnp.broadcast_to(jnp.arange(LANES)[None, :] == 0, (n_c, LANES))
    return overlap_t.astype(BF16), cmp_ones.astype(BF16)


def _build_kv_scratch(seq, slc_ref, win_ref, g, ks_sc, vs_sc, kw_sc, vw_sc, knorm_sc):
    chunk = min(512, seq)
    lane_p = lax.broadcasted_iota(jnp.int32, (KV_PAD, LANES), 1)
    zeros = jnp.zeros((KV_PAD, LANES), BF16)
    ks_sc[0:KV_PAD] = jnp.where(lane_p >= HEAD_DIM, 1.0, 0.0).astype(BF16)
    kw_sc[0:KV_PAD] = jnp.where(lane_p == HEAD_DIM, NEG_INF, 0.0).astype(BF16)
    vs_sc[0:KV_PAD] = zeros
    vw_sc[0:KV_PAD] = zeros
    lane = lax.broadcasted_iota(jnp.int32, (chunk, LANES), 1)
    row = lax.broadcasted_iota(jnp.int32, (chunk, LANES), 0)
    lo_half = lane < HEAD_DIM
    ones_lane = jnp.where(lane == HEAD_DIM, 1.0, 0.0)
    k_sq_max = jnp.zeros((chunk, 1), F32)
    for c in range(seq // chunk):
        r0 = c * chunk
        dst = slice(KV_PAD + r0, KV_PAD + r0 + chunk)
        blk = jnp.right_shift(r0 + row, int(math.log2(SLC_BLOCK)))
        lanes_g = slice(g * LANES, (g + 1) * LANES)
        slab = slc_ref[0, r0:r0 + chunk, lanes_g].astype(F32)
        k_sq_max = jnp.maximum(k_sq_max, jnp.sum(jnp.where(lo_half, slab * slab, 0.0), axis=1, keepdims=True))
        ks_sc[dst] = jnp.where(lo_half, slab, jnp.where(lane - HEAD_DIM == blk, 1.0, 0.0)).astype(BF16)
        vs_sc[dst] = jnp.where(lo_half, pltpu.roll(slab, HEAD_DIM, 1), ones_lane).astype(BF16)
        slab = win_ref[0, r0:r0 + chunk, lanes_g].astype(F32)
        kw_sc[dst] = jnp.where(lo_half, slab, 0.0).astype(BF16)
        vw_sc[dst] = jnp.where(lo_half, pltpu.roll(slab, HEAD_DIM, 1), ones_lane).astype(BF16)
    knorm_sc[...] = jnp.broadcast_to(jnp.max(k_sq_max, axis=0, keepdims=True), knorm_sc.shape)


def _nsa_select(seq, g, q0, q_ref, vkc_ref, bias_c_ref, overlap_t_ref, cmp_ones_ref):
    tq = ATTN_TQ
    n_blk = seq // SLC_BLOCK
    n_sel = min(N_SEL, n_blk)
    rows = HPG * tq
    lane = lax.broadcasted_iota(jnp.int32, (tq, LANES), 1)
    lo_half = lane < HEAD_DIM
    lane_r = lax.broadcasted_iota(jnp.int32, (rows, LANES), 1)

    q_lo, q_hi = [], []
    for pair_idx in range(HPG // 2):
        lanes_p = slice((g * (HPG // 2) + pair_idx) * LANES, (g * (HPG // 2) + pair_idx + 1) * LANES)
        q2 = q_ref[0, :, lanes_p].astype(F32)
        q2r = pltpu.roll(q2, HEAD_DIM, 1)
        q_lo += [jnp.where(lo_half, q2, 0.0), jnp.where(lo_half, q2r, 0.0)]
        q_hi += [jnp.where(lo_half, 0.0, q2r), jnp.where(lo_half, 0.0, q2)]
    q_lo = jnp.concatenate(q_lo, axis=0)
    q_hi = jnp.concatenate(q_hi, axis=0)

    vkc = vkc_ref[0, g]
    s_c = _dot_nt(q_hi.astype(BF16), vkc) + bias_c_ref[g * HPG:(g + 1) * HPG].reshape(rows, -1)
    m_c = jnp.max(s_c, axis=1, keepdims=True)
    e_cb = jnp.exp2(s_c - m_c).astype(BF16)
    pv_c = _dot(e_cb, jnp.concatenate([vkc, cmp_ones_ref[...]], axis=1))
    row_pos = q0 + (lax.broadcasted_iota(jnp.int32, (rows, LANES), 0) & (tq - 1))
    has_key = row_pos >= CMP_BLOCK - 1
    o_cmp = jnp.where(has_key, pv_c[:, :LANES] / pv_c[:, LANES:LANES + 1], 0.0)

    imp_t4 = _dot_nt(overlap_t_ref[...], e_cb)
    imp_t = None
    for h in range(HPG):
        part = imp_t4[:, h * tq:(h + 1) * tq]
        part = part[:n_blk] / part[n_blk:n_blk + 1]
        imp_t = part if imp_t is None else imp_t + part

    blk = lax.broadcasted_iota(jnp.int32, (n_blk, tq), 0)
    pos = q0 + lax.broadcasted_iota(jnp.int32, (n_blk, tq), 1)
    cur = jnp.right_shift(pos, int(math.log2(SLC_BLOCK)))
    forced = (blk == 0) | (blk == cur) | (blk == cur - 1)
    valid = blk * SLC_BLOCK <= pos
    score = jnp.where(forced, BIG, jnp.where(valid, imp_t, -BIG))
    blk_f = blk.astype(F32)
    pen_t = jnp.full((n_blk, tq), NEG_INF, F32)
    for _ in range(n_sel):
        top = jnp.max(score, axis=0, keepdims=True)
        first = jnp.min(jnp.where(score == top, blk_f, float(n_blk)), axis=0, keepdims=True)
        hit = blk_f == first
        pen_t = jnp.where(hit, 0.0, pen_t)
        score = jnp.where(hit, -jnp.inf, score)
    pieces = [jnp.zeros((tq, HEAD_DIM), F32), jnp.transpose(pen_t)]
    if n_blk < LANES - HEAD_DIM:
        pieces.append(jnp.full((tq, LANES - HEAD_DIM - n_blk), NEG_INF, F32))
    pen_lanes = jnp.concatenate(pieces, axis=1)
    q_slc = jnp.where(lane_r < HEAD_DIM, q_lo, jnp.concatenate([pen_lanes] * HPG, axis=0)).astype(BF16)
    return o_cmp, q_lo, q_slc


def _nsa_kernel(seq, q_ref, vkc_ref, slc_ref, win_ref, gate_ref, bias_c_ref, bias_near_ref, bias_win_ref,
                overlap_t_ref, cmp_ones_ref, o_ref, *scratch):
    tq = ATTN_TQ
    rows = HPG * tq
    groups = range(N_KV)
    kv_sc = [scratch[4 * g:4 * g + 4] for g in groups]
    s_sc, mrun_sc, acc_sc, knorm_sc = (scratch[4 * N_KV + i * N_KV:4 * N_KV + (i + 1) * N_KV] for i in range(4))
    qt = pl.program_id(1)
    q0 = pl.multiple_of(qt * tq, tq)
    lane = lax.broadcasted_iota(jnp.int32, (tq, LANES), 1)
    lo_half = lane < HEAD_DIM
    lane_r = lax.broadcasted_iota(jnp.int32, (rows, LANES), 1)

    @pl.when(qt == 0)
    def _():
        for g in groups:
            _build_kv_scratch(seq, slc_ref, win_ref, g, *kv_sc[g], knorm_sc[g])

    sel = [_nsa_select(seq, g, q0, q_ref, vkc_ref, bias_c_ref, overlap_t_ref, cmp_ones_ref) for g in groups]
    o_cmp = [s[0] for s in sel]
    q_lo = [s[1] for s in sel]
    q_slc = [s[2] for s in sel]
    head_rows = lambda ref, g: ref[g * HPG:(g + 1) * HPG].reshape(rows, -1)

    o_win = []
    for g in groups:
        kw_sc, vw_sc = kv_sc[g][2], kv_sc[g][3]
        q_win = jnp.where(lane_r == HEAD_DIM, 1.0, q_lo[g]).astype(BF16)
        win_rows = pl.ds(q0, WINDOW + tq)
        s_w = _dot_nt(q_win, kw_sc[win_rows, :]) + head_rows(bias_win_ref, g)
        p_w = jnp.exp2(s_w - jnp.max(s_w, axis=1, keepdims=True)).astype(BF16)
        acc_w = _dot(p_w, vw_sc[win_rows, :])
        o_win.append(acc_w / acc_w[:, HEAD_DIM:HEAD_DIM + 1])

    tk = SLC_TK
    n_far = qt // (tk // tq)
    lane_tiles = lambda a: [a[:, j * LANES:(j + 1) * LANES] for j in range(a.shape[1] // LANES)]
    near0 = pl.multiple_of(q0 + (KV_PAD - SLC_NEAR_BACK), tq)

    def tile_rows(u):
        return pl.ds(pl.multiple_of(near0 - u * tk, tq), tk)

    def tile_cols(u):
        return pl.ds(pl.multiple_of(u * tk, tk), tk)

    def score_tile(u, g, bias):
        s = _dot_nt(q_slc[g], kv_sc[g][0][tile_rows(u), :])
        if bias is not None:
            s = s + bias
        s_sc[g][:, tile_cols(u)] = s
        return functools.reduce(jnp.maximum, lane_tiles(s))

    shift_cap = []
    slack = jnp.float32(-jnp.inf)
    for g in groups:
        lane_max = score_tile(0, g, head_rows(bias_near_ref, g))
        mrun_sc[g][...] = lane_max
        m_near = jnp.max(lane_max, axis=1, keepdims=True)
        q_norm = jnp.sqrt(jnp.sum(q_lo[g] * q_lo[g], axis=1, keepdims=True))
        far_bound = q_norm * jnp.sqrt(knorm_sc[g][0:1, 0:1]) * SOFTMAX_BOUND_SLACK
        shift_cap.append(jnp.maximum(m_near, far_bound - SOFTMAX_SHIFT_MARGIN))
        slack = jnp.maximum(slack, jnp.max(far_bound - m_near))
    single_pass = slack <= 2.0 * SOFTMAX_SHIFT_MARGIN

    def weigh(s, u, g):
        shift = mrun_sc[g][...]
        p = jnp.exp2(s - jnp.concatenate([shift] * (tk // LANES), axis=1))
        acc_sc[g][...] = acc_sc[g][...] + _dot(p.astype(BF16), kv_sc[g][1][tile_rows(u), :])

    for g in groups:
        acc_sc[g][...] = jnp.zeros((rows, LANES), F32)

    @pl.when(single_pass)
    def _():
        for g in groups:
            mrun_sc[g][...] = jnp.broadcast_to(shift_cap[g], (rows, LANES))
            weigh(s_sc[g][:, tile_cols(0)], 0, g)

        def fused(u, carry):
            for g in groups:
                weigh(_dot_nt(q_slc[g], kv_sc[g][0][tile_rows(u), :]), u, g)
            return carry

        lax.fori_loop(1, n_far + 1, fused, 0)

    @pl.when(jnp.logical_not(single_pass))
    def _():
        def pass1(u, carry):
            for g in groups:
                mrun_sc[g][...] = jnp.maximum(mrun_sc[g][...], score_tile(u, g, None))
            return carry

        lax.fori_loop(1, n_far + 1, pass1, 0)
        for g in groups:
            m_s = jnp.max(mrun_sc[g][...], axis=1, keepdims=True)
            mrun_sc[g][...] = jnp.broadcast_to(m_s, (rows, LANES))

        def pass2(u, carry):
            for g in groups:
                weigh(s_sc[g][:, tile_cols(u)], u, g)
            return carry

        lax.fori_loop(0, n_far + 1, pass2, 0)

    for g in groups:
        acc = acc_sc[g][...]
        o_slc = acc / acc[:, HEAD_DIM:HEAD_DIM + 1]

        gates = gate_ref[0, :, g * GATE_PAD:(g + 1) * GATE_PAD]
        outs = []
        for h in range(HPG):
            sl = slice(h * tq, (h + 1) * tq)
            g_c, g_s, g_w = (gates[:, br * HPG + h:br * HPG + h + 1] for br in range(3))
            outs.append(g_c * o_cmp[g][sl] + g_s * o_slc[sl] + g_w * o_win[g][sl])
        for pair_idx in range(HPG // 2):
            even, odd = outs[2 * pair_idx], outs[2 * pair_idx + 1]
            merged = jnp.where(lo_half, even, pltpu.roll(odd, HEAD_DIM, 1))
            lanes_p = slice((g * (HPG // 2) + pair_idx) * LANES, (g * (HPG // 2) + pair_idx + 1) * LANES)
            o_ref[0, :, lanes_p] = merged.astype(o_ref.dtype)


def _nsa(q, vkc, slc, win, gates, bias_c, bias_near, bias_win, overlap_t, cmp_ones):
    b, seq, _ = q.shape
    n_c = seq // CMP_STRIDE
    n_blk = seq // SLC_BLOCK
    assert n_blk <= LANES - HEAD_DIM and seq % SLC_TK == 0 and KV_PAD >= SLC_TK
    tq = ATTN_TQ
    rows = HPG * tq
    const = lambda a: pl.BlockSpec(a.shape, lambda i, t: (0,) * a.ndim)
    tile = lambda w: pl.BlockSpec((1, tq, w), lambda i, t: (i, t, 0))
    whole_seq = lambda w: pl.BlockSpec((1, seq, w), lambda i, t: (i, 0, 0))
    per_group = lambda shape: [pltpu.VMEM(shape, F32)] * N_KV
    return pl.pallas_call(
        functools.partial(_nsa_kernel, seq),
        grid=(b, seq // tq),
        in_specs=[
            tile(ATTN_WIDTH),
            pl.BlockSpec((1, N_KV, n_c, 2 * HEAD_DIM), lambda i, t: (i, 0, 0, 0)),
            whole_seq(N_KV * 2 * HEAD_DIM), whole_seq(N_KV * 2 * HEAD_DIM),
            tile(N_KV * GATE_PAD),
            pl.BlockSpec((N_HEADS, tq, n_c), lambda i, t: (0, t, 0)),
            const(bias_near), const(bias_win),
            const(overlap_t), const(cmp_ones),
        ],
        out_specs=tile(ATTN_WIDTH),
        out_shape=jax.ShapeDtypeStruct((b, seq, ATTN_WIDTH), BF16),
        scratch_shapes=(
            [pltpu.VMEM((KV_PAD + seq, LANES), BF16)] * (4 * N_KV)
            + per_group((rows, seq)) + per_group((rows, LANES)) + per_group((rows, LANES))
            + per_group((SUBLANES, LANES))
        ),
        compiler_params=_compiler_params(("parallel", "arbitrary")),
        name="nsa",
    )(q, vkc, slc, win, gates, bias_c, bias_near, bias_win, overlap_t, cmp_ones)


def _mixer_inputs(x, rel_bias, w_in, b_in, cmp_pos, cmp_w1, cmp_b1, cmp_w2, cmp_b2):
    b, seq, d = x.shape
    wp, bp = _pack_in_proj(w_in, b_in, d)
    q, slc, win, cmp, gates, u, merge = _in_proj(x.reshape(b * seq, d), wp, bp, d, 512)
    n_c = seq // CMP_STRIDE
    cmp4 = cmp.reshape(b, seq, 2 * N_KV, HEAD_DIM).transpose(0, 2, 1, 3).reshape(
        b, 2 * N_KV, n_c, CMP_STRIDE * HEAD_DIM)
    vkc = _compress(cmp4, cmp_pos, cmp_w1, cmp_b1, cmp_w2, cmp_b2)
    bias_c, bias_near, bias_win = _attention_bias_tables(rel_bias, seq)
    overlap_t, cmp_ones = _nsa_constants(seq)
    o = _nsa(q.reshape(b, seq, -1), vkc, slc.reshape(b, seq, -1), win.reshape(b, seq, -1),
             gates.reshape(b, seq, -1), bias_c, bias_near, bias_win, overlap_t, cmp_ones)
    return o, u, merge


S5_HALF_GROUPS = S5_GROUPS // 2
S5_HALF_IN = S5_HALF_GROUPS * S5_GROUP
S5_HALF_STATE = S5_HALF_GROUPS * S5_STATE
S5_SCAN_LANES = 512
S5_CHUNK = 64
S5_UNROLL = 8


def _s5_params(lam_re, lam_im, log_dt, b_re, b_im, c_re, c_im, nb):
    dt = jnp.exp(log_dt.astype(F32))[:, None]
    lr, li = lam_re.astype(F32), lam_im.astype(F32)
    mag = jnp.exp(lr * dt)
    ab_re, ab_im = mag * jnp.cos(li * dt), mag * jnp.sin(li * dt)
    nr, ni = ab_re - 1.0, ab_im
    den = lr * lr + li * li
    fr, fi = (nr * lr + ni * li) / den, (ni * lr - nr * li) / den
    br, bim = b_re.astype(F32), b_im.astype(F32)
    bb_re = fr[..., None] * br - fi[..., None] * bim
    bb_im = fr[..., None] * bim + fi[..., None] * br
    eye = jnp.eye(S5_HALF_GROUPS, dtype=F32)

    def in_mat(bb):
        t = bb.reshape(2, S5_HALF_GROUPS, S5_STATE, S5_GROUP)
        m = jnp.einsum('kgph,gj->kghjp', t, eye)
        return m.reshape(2, S5_HALF_IN, S5_HALF_STATE)

    def out_mat(c):
        t = c.astype(F32).reshape(2, S5_HALF_GROUPS, S5_GROUP, S5_STATE)
        m = jnp.einsum('kghp,gj->kgpjh', t, eye)
        return m.reshape(2, S5_HALF_STATE, S5_HALF_IN)

    bmat = jnp.concatenate([in_mat(bb_re), in_mat(bb_im)], axis=2).astype(BF16)
    cmat = jnp.concatenate([out_mat(c_re), -out_mat(c_im)], axis=1).astype(BF16)
    a = jnp.concatenate([ab_re.reshape(2, S5_HALF_STATE), ab_im.reshape(2, S5_HALF_STATE)], axis=1)
    a = jnp.broadcast_to(a.reshape(1, 4 * S5_HALF_STATE), (nb, 4 * S5_HALF_STATE))
    return bmat, cmat, a


def _s5_kernel(u_ref, bmat_ref, cmat_ref, a_ref, d_ref, y_ref, ut_sc, x_sc, st_sc):
    nb, t_len, _ = u_ref.shape
    half_w = 2 * S5_HALF_STATE

    @pl.when(pl.program_id(0) == 0)
    def _():
        st_sc[...] = jnp.zeros_like(st_sc)

    n_cb = ut_sc.shape[0]
    for b in range(nb):
        for cb in range(n_cb):
            ut_sc[cb, pl.ds(b, t_len, stride=nb), :] = u_ref[b, :, cb * LANES:(cb + 1) * LANES]
    ut = jnp.concatenate([ut_sc[cb] for cb in range(n_cb)], axis=1)
    ub = ut.astype(BF16)
    for k in range(2):
        x_sc[:, k * half_w:(k + 1) * half_w] = _dot(ub[:, k * S5_HALF_IN:(k + 1) * S5_HALF_IN], bmat_ref[k])

    for k in range(2):
        for j in range(S5_HALF_STATE // S5_SCAN_LANES):
            re0 = k * half_w + j * S5_SCAN_LANES
            im0 = re0 + S5_HALF_STATE
            re_sl, im_sl = pl.ds(re0, S5_SCAN_LANES), pl.ds(im0, S5_SCAN_LANES)
            ar, ai = a_ref[:, re_sl], a_ref[:, im_sl]

            def steps(c, carry):
                xr, xi = carry
                for s in range(S5_UNROLL):
                    rows = pl.ds(pl.multiple_of((c * S5_UNROLL + s) * nb, nb), nb)
                    nxr = ar * xr - ai * xi + x_sc[rows, re_sl]
                    nxi = ar * xi + ai * xr + x_sc[rows, im_sl]
                    x_sc[rows, re_sl] = nxr
                    x_sc[rows, im_sl] = nxi
                    xr, xi = nxr, nxi
                return xr, xi

            xr, xi = lax.fori_loop(0, t_len // S5_UNROLL, steps, (st_sc[:, re_sl], st_sc[:, im_sl]))
            st_sc[:, re_sl] = xr
            st_sc[:, im_sl] = xi

    xs = x_sc[...].astype(BF16)
    y = jnp.concatenate([_dot(xs[:, k * half_w:(k + 1) * half_w], cmat_ref[k]) for k in range(2)], axis=1)
    y = y + d_ref[...] * ut
    for cb in range(n_cb):
        ut_sc[cb] = y[:, cb * LANES:(cb + 1) * LANES]
    for b in range(nb):
        for cb in range(n_cb):
            y_ref[b, :, cb * LANES:(cb + 1) * LANES] = ut_sc[cb, pl.ds(b, t_len, stride=nb), :]


def _s5(u, bmat, cmat, a, d_skip):
    nb, seq, w = u.shape
    t_len = min(S5_CHUNK, seq)
    full = lambda shape: pl.BlockSpec(shape, lambda c: (0,) * len(shape))
    return pl.pallas_call(
        _s5_kernel,
        grid=(seq // t_len,),
        in_specs=[
            pl.BlockSpec((nb, t_len, w), lambda c: (0, c, 0)),
            full(bmat.shape), full(cmat.shape), full(a.shape), full((1, w)),
        ],
        out_specs=pl.BlockSpec((nb, t_len, w), lambda c: (0, c, 0)),
        out_shape=jax.ShapeDtypeStruct((nb, seq, w), F32),
        scratch_shapes=[
            pltpu.VMEM((w // LANES, t_len * nb, LANES), F32),
            pltpu.VMEM((t_len * nb, 4 * S5_HALF_STATE), F32),
            pltpu.VMEM((nb, 4 * S5_HALF_STATE), F32),
        ],
        compiler_params=_compiler_params(("arbitrary",)),
        name="s5",
    )(u, bmat, cmat, a, d_skip.reshape(1, w).astype(F32))


ROUTE_PAD = LANES
_R_E1, _R_E2, _R_W1, _R_W2, _R_RANK1, _R_RANK2 = range(6)


def _layer_norm(t, g, b):
    mu = jnp.mean(t, axis=1, keepdims=True)
    c = t - mu
    var = jnp.mean(c * c, axis=1, keepdims=True)
    return c * lax.rsqrt(var + LN_EPS) * g + b


def _post_kernel(x_ref, o_ref, y_ref, m_ref, wup_ref, wval_ref, wgate_ref, bgate_ref, wout_ref,
                 g1_ref, b1_ref, wr_ref, br_ref, h_ref, route_ref, cnt_ref, run_sc):
    tm, d = x_ref.shape

    @pl.when(pl.program_id(0) == 0)
    def _():
        run_sc[...] = jnp.zeros_like(run_sc)

    y_a = _dot(o_ref[...], wup_ref[...])
    z = _gelu_tanh(y_ref[...]).astype(BF16)
    y_b = _dot(z, wval_ref[...]) * jax.nn.sigmoid(_dot(z, wgate_ref[...]) + bgate_ref[...])
    mixed = m_ref[:, :d] * y_a + m_ref[:, d:] * y_b
    t = DN_ALPHA * x_ref[...] + _dot(mixed.astype(BF16), wout_ref[...])
    h = _layer_norm(t, g1_ref[...], b1_ref[...])
    h_ref[...] = h

    logits = _dot(h.astype(BF16), wr_ref[...]) + br_ref[...]
    lane = lax.broadcasted_iota(jnp.int32, (tm, ROUTE_PAD), 1)
    lane_f = lane.astype(F32)
    is_group = lane < N_EGROUPS

    def first_max(v):
        top = jnp.max(v, axis=1, keepdims=True)
        idx = jnp.min(jnp.where(v == top, lane_f, float(ROUTE_PAD)), axis=1, keepdims=True)
        return top, idx

    g_max, g_top = first_max(jnp.where(is_group, logits, -jnp.inf))
    p_group = 1.0 / jnp.sum(jnp.where(is_group, jnp.exp(logits - g_max), 0.0), axis=1, keepdims=True)
    grp_of_lane = jnp.right_shift(lane - N_EGROUPS, int(math.log2(EXPERTS_PER_GROUP))).astype(F32)
    in_group = (lane >= N_EGROUPS) & (lane < N_EGROUPS + N_EXPERTS) & (grp_of_lane == g_top)
    e_log = jnp.where(in_group, logits, -jnp.inf)
    v1, i1 = first_max(e_log)
    hit1 = lane_f == i1
    v2, i2 = first_max(jnp.where(hit1, -jnp.inf, e_log))
    hit2 = lane_f == i2
    e2 = jnp.exp(v2 - v1)
    w1 = p_group / (1.0 + e2)
    w2 = p_group * e2 / (1.0 + e2)

    hits = jnp.where(hit1 | hit2, 1.0, 0.0)
    row = lax.broadcasted_iota(jnp.int32, (tm, tm), 0)
    col = lax.broadcasted_iota(jnp.int32, (tm, tm), 1)
    earlier = jnp.where(col < row, 1.0, 0.0).astype(BF16)
    before = _dot(earlier, hits.astype(BF16)) + run_sc[...]
    rank1 = jnp.sum(jnp.where(hit1, before, 0.0), axis=1, keepdims=True)
    rank2 = jnp.sum(jnp.where(hit2, before, 0.0), axis=1, keepdims=True)
    run_sc[...] = run_sc[...] + jnp.sum(hits, axis=0, keepdims=True)
    cnt_ref[...] = run_sc[...]

    rec = jnp.zeros((tm, ROUTE_PAD), F32)
    for slot, val in ((_R_E1, i1 - N_EGROUPS), (_R_E2, i2 - N_EGROUPS), (_R_W1, w1), (_R_W2, w2),
                      (_R_RANK1, rank1), (_R_RANK2, rank2)):
        rec = jnp.where(lane == slot, val, rec)
    route_ref[...] = rec


def _post(x2d, o2d, y2d, merge, w_attn_up, s5_w_val, s5_w_gate, s5_b_gate, w_out, ln1_g, ln1_b,
          router_w_group, router_b_group, router_w_expert, router_b_expert, tm):
    n, d = x2d.shape
    rpad = ROUTE_PAD - N_EGROUPS - N_EXPERTS
    wr = jnp.concatenate([router_w_group, router_w_expert, jnp.zeros((d, rpad), F32)], axis=1).astype(BF16)
    br = jnp.concatenate([router_b_group, router_b_expert, jnp.zeros((rpad,), F32)]).reshape(1, -1).astype(F32)
    row = lambda w: pl.BlockSpec((tm, w), lambda i: (i, 0))
    full = lambda a: pl.BlockSpec(a.shape, lambda i: (0,) * a.ndim)
    weights = [w_attn_up.astype(BF16), s5_w_val.astype(BF16), s5_w_gate.astype(BF16),
               s5_b_gate.reshape(1, d).astype(F32), w_out.astype(BF16),
               ln1_g.reshape(1, d).astype(F32), ln1_b.reshape(1, d).astype(F32), wr, br]
    return pl.pallas_call(
        _post_kernel,
        grid=(n // tm,),
        in_specs=[row(d), row(ATTN_WIDTH), row(S5_WIDTH), row(2 * d)] + [full(w) for w in weights],
        out_specs=[row(d), row(ROUTE_PAD), pl.BlockSpec((1, ROUTE_PAD), lambda i: (0, 0))],
        out_shape=[jax.ShapeDtypeStruct((n, d), F32), jax.ShapeDtypeStruct((n, ROUTE_PAD), F32),
                   jax.ShapeDtypeStruct((1, ROUTE_PAD), F32)],
        scratch_shapes=[pltpu.VMEM((1, ROUTE_PAD), F32)],
        compiler_params=_compiler_params(("arbitrary",)),
        name="post_mixer",
    )(x2d, o2d, y2d, merge, *weights)


def _plan_kernel(route_ref, cnt_ref, dest_ref):
    tm = route_ref.shape[0]
    lane8 = lax.broadcasted_iota(jnp.int32, (SUBLANES, ROUTE_PAD), 1)
    counts = jnp.broadcast_to(cnt_ref[...], (SUBLANES, ROUTE_PAD)).astype(jnp.int32)
    shift = int(math.log2(MOE_BLOCK))
    padded = jnp.left_shift(jnp.right_shift(counts + (MOE_BLOCK - 1), shift), shift)
    incl = padded
    step = 1
    while step < ROUTE_PAD:
        incl = incl + jnp.where(lane8 >= step, pltpu.roll(incl, step, 1), 0)
        step *= 2
    pstart = (incl - padded)[0:1].astype(F32)
    route = route_ref[...]
    expert_of_lane = (lax.broadcasted_iota(jnp.int32, (tm, ROUTE_PAD), 1) - N_EGROUPS).astype(F32)
    lane = lax.broadcasted_iota(jnp.int32, (tm, ROUTE_PAD), 1)

    def dest(e_slot, rank_slot):
        hit = expert_of_lane == route[:, e_slot:e_slot + 1]
        return jnp.sum(jnp.where(hit, pstart, 0.0), axis=1, keepdims=True) + route[:, rank_slot:rank_slot + 1]

    d1 = dest(_R_E1, _R_RANK1)
    d2 = dest(_R_E2, _R_RANK2)
    dest_ref[...] = jnp.where(lane == 0, d1, jnp.where(lane == 1, d2, 0.0)).astype(jnp.int32)


def _plan(route, counts_row):
    n = route.shape[0]
    tm = min(1024, n)
    return pl.pallas_call(
        _plan_kernel,
        grid=(n // tm,),
        in_specs=[pl.BlockSpec((tm, ROUTE_PAD), lambda i: (i, 0)),
                  pl.BlockSpec((1, ROUTE_PAD), lambda i: (0, 0))],
        out_specs=pl.BlockSpec((tm, ROUTE_PAD), lambda i: (i, 0)),
        out_shape=jax.ShapeDtypeStruct((n, ROUTE_PAD), jnp.int32),
        compiler_params=_compiler_params(("parallel",)),
        name="moe_plan",
    )(route, counts_row)


SC_GATHER_ROWS = 32


def _sc_row_gather(table, idx):
    n_idx = idx.shape[0]
    d = table.shape[1]
    info = plsc.get_sparse_core_info()
    n_workers = info.num_cores * info.num_subcores
    per_worker = n_idx // n_workers
    assert n_idx % (n_workers * SC_GATHER_ROWS) == 0
    mesh = plsc.VectorSubcoreMesh(core_axis_name="c", subcore_axis_name="s")

    @functools.partial(
        pl.kernel, mesh=mesh,
        out_type=jax.ShapeDtypeStruct((n_idx, d), table.dtype),
        scratch_types=[
            pltpu.VMEM((SC_GATHER_ROWS,), jnp.int32),
            pltpu.VMEM((SC_GATHER_ROWS, d), table.dtype),
            pltpu.SemaphoreType.DMA,
        ],
    )
    def gather(table_hbm, idx_hbm, out_hbm, idx_v, rows_v, sem):
        worker = lax.axis_index("s") * info.num_cores + lax.axis_index("c")
        base = worker * per_worker

        @pl.loop(0, per_worker // SC_GATHER_ROWS)
        def _(j):
            off = base + j * SC_GATHER_ROWS
            pltpu.sync_copy(idx_hbm.at[pl.ds(off, SC_GATHER_ROWS)], idx_v)
            pltpu.async_copy(table_hbm.at[idx_v], rows_v, sem).wait()
            pltpu.sync_copy(rows_v, out_hbm.at[pl.ds(off, SC_GATHER_ROWS)])

    return gather(table, idx)


def _sc_row_scatter(rows, idx_a, idx_b, n_out):
    n, d = rows.shape
    info = plsc.get_sparse_core_info()
    n_workers = info.num_cores * info.num_subcores
    per_worker = n // n_workers
    assert n % (n_workers * SC_GATHER_ROWS) == 0
    mesh = plsc.VectorSubcoreMesh(core_axis_name="c", subcore_axis_name="s")

    @functools.partial(
        pl.kernel, mesh=mesh,
        out_type=jax.ShapeDtypeStruct((n_out, d), rows.dtype),
        scratch_types=[
            pltpu.VMEM((SC_GATHER_ROWS,), jnp.int32),
            pltpu.VMEM((SC_GATHER_ROWS,), jnp.int32),
            pltpu.VMEM((SC_GATHER_ROWS, d), rows.dtype),
        ],
    )
    def scatter(rows_hbm, idx_a_hbm, idx_b_hbm, out_hbm, idx_a_v, idx_b_v, rows_v):
        worker = lax.axis_index("s") * info.num_cores + lax.axis_index("c")
        base = worker * per_worker

        @pl.loop(0, per_worker // SC_GATHER_ROWS)
        def _(j):
            src = pl.ds(base + j * SC_GATHER_ROWS, SC_GATHER_ROWS)
            pltpu.sync_copy(rows_hbm.at[src], rows_v)
            pltpu.sync_copy(idx_a_hbm.at[src], idx_a_v)
            pltpu.sync_copy(idx_b_hbm.at[src], idx_b_v)
            pltpu.sync_copy(rows_v, out_hbm.at[idx_a_v])
            pltpu.sync_copy(rows_v, out_hbm.at[idx_b_v])

    return scatter(rows, idx_a, idx_b)


def _expert_kernel(blk_exp_ref, blk_valid_ref, x_ref, wg_ref, wu_ref, wd_ref, y_ref, wg_sc, wu_sc, wd_sc):
    i = pl.program_id(0)
    n_valid = blk_valid_ref[i]

    @pl.when((i == 0) | (blk_exp_ref[i] != blk_exp_ref[jnp.maximum(i - 1, 0)]))
    def _():
        wg_sc[...] = wg_ref[0].astype(BF16)
        wu_sc[...] = wu_ref[0].astype(BF16)
        wd_sc[...] = wd_ref[0].astype(BF16)

    @pl.when(n_valid > 0)
    def _():
        row = lax.broadcasted_iota(jnp.int32, x_ref.shape, 0)
        xb = jnp.where(row < n_valid, x_ref[...], 0.0).astype(BF16)
        h_gate = _dot(xb, wg_sc[...])
        h_up = _dot(xb, wu_sc[...])
        hb = (h_gate * jax.nn.sigmoid(h_gate) * h_up).astype(BF16)
        y_ref[...] = _dot(hb, wd_sc[...])

    @pl.when(n_valid == 0)
    def _():
        y_ref[...] = jnp.zeros_like(y_ref)


def _experts(blk_expert, blk_valid, xs, w_gate, w_up, w_down):
    n_blocks = blk_expert.shape[0]
    d = xs.shape[1]
    grid_spec = pltpu.PrefetchScalarGridSpec(
        num_scalar_prefetch=2,
        grid=(n_blocks,),
        in_specs=[
            pl.BlockSpec((MOE_BLOCK, d), lambda i, be, bv: (i, 0)),
            pl.BlockSpec((1, d, D_EXPERT), lambda i, be, bv: (be[i], 0, 0)),
            pl.BlockSpec((1, d, D_EXPERT), lambda i, be, bv: (be[i], 0, 0)),
            pl.BlockSpec((1, D_EXPERT, d), lambda i, be, bv: (be[i], 0, 0)),
        ],
        out_specs=pl.BlockSpec((MOE_BLOCK, d), lambda i, be, bv: (i, 0)),
        scratch_shapes=[
            pltpu.VMEM((d, D_EXPERT), BF16),
            pltpu.VMEM((d, D_EXPERT), BF16),
            pltpu.VMEM((D_EXPERT, d), BF16),
        ],
    )
    return pl.pallas_call(
        _expert_kernel,
        grid_spec=grid_spec,
        out_shape=jax.ShapeDtypeStruct((n_blocks * MOE_BLOCK, d), F32),
        compiler_params=_compiler_params(("arbitrary",)),
        name="experts",
    )(blk_expert, blk_valid, xs, w_gate, w_up, w_down)


COMBINE_TM = 512


def _combine_kernel(h_ref, y1_ref, y2_ref, route_ref, g2_ref, b2_ref, out_ref):
    route = route_ref[...]
    w1 = route[:, _R_W1:_R_W1 + 1]
    w2 = route[:, _R_W2:_R_W2 + 1]
    t = DN_ALPHA * h_ref[...] + (y1_ref[...] * w1 + y2_ref[...] * w2)
    out_ref[...] = _layer_norm(t, g2_ref[...], b2_ref[...])


def _combine(yg, h2d, route, ln2_g, ln2_b):
    n, d = h2d.shape
    tm = min(COMBINE_TM, n)
    n_tiles = n // tm
    row = lambda w: pl.BlockSpec((tm, w), lambda i: (i, 0))
    vec = pl.BlockSpec((1, d), lambda i: (0, 0))
    return pl.pallas_call(
        _combine_kernel,
        grid=(n_tiles,),
        in_specs=[row(d), row(d), pl.BlockSpec((tm, d), lambda i: (i + n_tiles, 0)), row(ROUTE_PAD), vec, vec],
        out_specs=row(d),
        out_shape=jax.ShapeDtypeStruct((n, d), F32),
        compiler_params=_compiler_params(("parallel",)),
        name="combine",
    )(h2d, yg, yg, route, ln2_g.reshape(1, d).astype(F32), ln2_b.reshape(1, d).astype(F32))


def _moe(h2d, route, counts_row, w_gate, w_up, w_down, ln2_g, ln2_b):
    n, d = h2d.shape
    dest = _plan(route, counts_row)
    dest1, dest2 = dest[:, 0], dest[:, 1]
    counts = counts_row[0, N_EGROUPS:N_EGROUPS + N_EXPERTS].astype(jnp.int32)
    padded = (counts + MOE_BLOCK - 1) // MOE_BLOCK * MOE_BLOCK
    pend = jnp.cumsum(padded)
    n_blocks = -(-(n * TOP_K_IN_GROUP) // MOE_BLOCK) + N_EXPERTS
    blk_row0 = jnp.arange(n_blocks, dtype=jnp.int32) * MOE_BLOCK
    blk_expert = jnp.minimum(jnp.sum(pend[None, :] <= blk_row0[:, None], axis=1), N_EXPERTS - 1).astype(jnp.int32)
    blk_valid = jnp.clip((pend - padded + counts)[blk_expert] - blk_row0, 0, MOE_BLOCK).astype(jnp.int32)
    xs = _sc_row_scatter(h2d, dest1, dest2, n_blocks * MOE_BLOCK)
    yb = _experts(blk_expert, blk_valid, xs, w_gate, w_up, w_down)
    yg = _sc_row_gather(yb, jnp.concatenate([dest1, dest2]))
    return _combine(yg, h2d, route, ln2_g, ln2_b)


def kernel(x, rel_bias, w_in, b_in, cmp_pos, cmp_w1, cmp_b1, cmp_w2, cmp_b2, w_attn_up, s5_lambda_re, s5_lambda_im, s5_log_dt, s5_b_re, s5_b_im, s5_c_re, s5_c_im, s5_d, s5_w_val, s5_w_gate, s5_b_gate, w_out, ln1_g, ln1_b, router_w_group, router_b_group, router_w_expert, router_b_expert, exp_w_gate, exp_w_up, exp_w_down, ln2_g, ln2_b):
    b, seq, d = x.shape
    n = b * seq
    assert w_in.shape[0] == DEPTH
    l = 0
    o, u, merge = _mixer_inputs(x, rel_bias, w_in[l], b_in[l], cmp_pos[l], cmp_w1[l], cmp_b1[l],
                                cmp_w2[l], cmp_b2[l])
    bmat, cmat, a = _s5_params(s5_lambda_re[l], s5_lambda_im[l], s5_log_dt[l], s5_b_re[l], s5_b_im[l],
                               s5_c_re[l], s5_c_im[l], b)
    y_s = _s5(u.reshape(b, seq, S5_WIDTH), bmat, cmat, a, s5_d[l])
    h2d, route, counts = _post(x.reshape(n, d), o.reshape(n, ATTN_WIDTH), y_s.reshape(n, S5_WIDTH), merge,
                               w_attn_up[l], s5_w_val[l], s5_w_gate[l], s5_b_gate[l], w_out[l], ln1_g[l],
                               ln1_b[l], router_w_group[l], router_b_group[l], router_w_expert[l],
                               router_b_expert[l], 512)
    out = _moe(h2d, route, counts, exp_w_gate[l], exp_w_up[l], exp_w_down[l], ln2_g[l], ln2_b[l])
    return out.reshape(b, seq, d)
```

```python
import functools
import math

import jax
import jax.numpy as jnp
from jax import lax
from jax.experimental import pallas as pl
from jax.experimental.pallas import tpu as pltpu
from jax.experimental.pallas import tpu_sc as plsc

F32 = jnp.float32
BF16 = jnp.bfloat16

N_HEADS = 8
HEAD_DIM = 64
N_KV = 2
HPG = N_HEADS // N_KV
CMP_STRIDE = 16
CMP_BLOCK = 2 * CMP_STRIDE
CMP_HIDDEN = 128
SLC_BLOCK = 64
N_SEL = 16
WINDOW = 512
REL_BUCKETS = 32
REL_MAX_DIST = 128
S5_WIDTH = 512
S5_GROUP = 16
S5_GROUPS = S5_WIDTH // S5_GROUP
S5_STATE = 64
N_EGROUPS = 8
EXPERTS_PER_GROUP = 8
N_EXPERTS = N_EGROUPS * EXPERTS_PER_GROUP
TOP_K_IN_GROUP = 2
D_EXPERT = 256
EXPERT_BLOCK = 128
DEPTH = 1
DN_ALPHA = (2.0 * DEPTH) ** 0.25
LN_EPS = 1e-5
NEG_INF = -1e30
BIG = 1e9
LOG2_E = math.log2(math.e)
MOE_BLOCK = 256

ATTN_WIDTH = N_HEADS * HEAD_DIM
KV_WIDTH = N_KV * HEAD_DIM
KV_OFF = ATTN_WIDTH
NSA_GATE_OFF = KV_OFF + 6 * KV_WIDTH
S5_OFF = NSA_GATE_OFF + 3 * N_HEADS
MERGE_OFF = S5_OFF + S5_WIDTH

LANES = 128
SUBLANES = 8
VMEM_LIMIT_BYTES = 56 * 1024 * 1024

ATTN_TQ = 128
SLC_TK = 512
SLC_NEAR_BACK = SLC_TK - ATTN_TQ
KV_PAD = WINDOW
SOFTMAX_SHIFT_MARGIN = 100.0
SOFTMAX_BOUND_SLACK = 1.001
GATE_PAD = LANES


def _gelu_tanh(x):
    c = math.sqrt(2.0 / math.pi)
    return x * (0.5 * (1.0 + jnp.tanh(c * (x + 0.044715 * (x * x * x)))))


def _dot(a, b):
    return jnp.dot(a, b, preferred_element_type=F32)


def _dot_nt(a, b):
    return lax.dot_general(a, b, (((1,), (1,)), ((), ())), preferred_element_type=F32)


def _pack_rows(x):
    half = x.shape[1] // 2
    xb = x.astype(BF16).astype(F32)
    hi = lax.bitcast_convert_type(xb[:, :half], jnp.uint32) & jnp.uint32(0xFFFF0000)
    lo = lax.shift_right_logical(lax.bitcast_convert_type(xb[:, half:], jnp.uint32), jnp.uint32(16))
    return hi | lo


def _unpack_rows(w):
    hi = lax.bitcast_convert_type(w & jnp.uint32(0xFFFF0000), F32)
    lo = lax.bitcast_convert_type(lax.shift_left(w, jnp.uint32(16)), F32)
    return jnp.concatenate([hi, lo], axis=1)


def _compiler_params(semantics):
    return pltpu.CompilerParams(dimension_semantics=semantics, vmem_limit_bytes=VMEM_LIMIT_BYTES)


def _in_proj_layout(d_model):
    widths = (ATTN_WIDTH, 2 * KV_WIDTH, 2 * KV_WIDTH, 2 * KV_WIDTH, N_KV * GATE_PAD, S5_WIDTH, 2 * d_model)
    offs = [0]
    for w in widths:
        offs.append(offs[-1] + w)
    return widths, offs


def _pack_in_proj(w_in, b_in, d_model):
    def kv_cols(j):
        return KV_OFF + j * KV_WIDTH

    def pair(jk, jv):
        cols = []
        for g in range(N_KV):
            cols.append(jnp.arange(kv_cols(jk) + g * HEAD_DIM, kv_cols(jk) + (g + 1) * HEAD_DIM))
            cols.append(jnp.arange(kv_cols(jv) + g * HEAD_DIM, kv_cols(jv) + (g + 1) * HEAD_DIM))
        return jnp.concatenate(cols)

    idx = jnp.concatenate([
        jnp.arange(0, ATTN_WIDTH),
        pair(2, 3),
        pair(4, 5),
        jnp.arange(kv_cols(0), kv_cols(2)),
    ])
    idx2 = jnp.concatenate([jnp.arange(S5_OFF, S5_OFF + S5_WIDTH),
                            jnp.arange(MERGE_OFF, MERGE_OFF + 2 * d_model)])
    gpad = GATE_PAD - 3 * HPG
    w_parts, b_parts = [w_in[:, idx]], [b_in[idx]]
    for g in range(N_KV):
        cols = jnp.asarray([NSA_GATE_OFF + (g * HPG + h) * 3 + j for j in range(3) for h in range(HPG)])
        w_parts += [w_in[:, cols], jnp.zeros((d_model, gpad), F32)]
        b_parts += [b_in[cols], jnp.zeros((gpad,), F32)]
    w = jnp.concatenate(w_parts + [w_in[:, idx2]], axis=1)
    b = jnp.concatenate(b_parts + [b_in[idx2]])
    return w.astype(BF16), b.reshape(1, -1).astype(F32)


def _in_proj_kernel(offs, x_ref, w_ref, b_ref, q_ref, slc_ref, win_ref, cmp_ref, g_ref, u_ref, m_ref):
    xb = x_ref[...].astype(BF16)

    def proj(i):
        c0, c1 = offs[i], offs[i + 1]
        return _dot(xb, w_ref[:, c0:c1]) + b_ref[:, c0:c1]

    q_ref[...] = (proj(0) * (HEAD_DIM ** -0.5 * LOG2_E)).astype(BF16)
    slc_ref[...] = proj(1).astype(BF16)
    win_ref[...] = proj(2).astype(BF16)
    cmp_ref[...] = proj(3)
    g_ref[...] = jax.nn.sigmoid(proj(4))
    u_ref[...] = proj(5)
    m_ref[...] = jax.nn.sigmoid(proj(6))


def _in_proj(x2d, w_packed, b_packed, d_model, tm):
    n = x2d.shape[0]
    widths, offs = _in_proj_layout(d_model)
    ncols = offs[-1]
    dtypes = (BF16, BF16, BF16, F32, F32, F32, F32)
    return pl.pallas_call(
        functools.partial(_in_proj_kernel, tuple(offs)),
        grid=(n // tm,),
        in_specs=[
            pl.BlockSpec((tm, d_model), lambda i: (i, 0)),
            pl.BlockSpec((d_model, ncols), lambda i: (0, 0)),
            pl.BlockSpec((1, ncols), lambda i: (0, 0)),
        ],
        out_specs=[pl.BlockSpec((tm, w), lambda i: (i, 0)) for w in widths],
        out_shape=[jax.ShapeDtypeStruct((n, w), dt) for w, dt in zip(widths, dtypes)],
        compiler_params=_compiler_params(("parallel",)),
        name="in_proj",
    )(x2d, w_packed, b_packed)


def _compress_kernel(ck_ref, cv_ref, pos_ref, w1_ref, b1_ref, w2_ref, b2_ref, out_ref):
    n_c = ck_ref.shape[2]
    outs = []
    for i, c_ref in enumerate((ck_ref, cv_ref)):
        c = c_ref[0, 0]
        lo = (c + pos_ref[i, 0:1, :]).astype(BF16)
        hi = (c + pos_ref[i, 1:2, :]).astype(BF16)
        p_lo = _dot(lo, w1_ref[i, 0])
        p_hi = _dot(hi, w1_ref[i, 1])
        hid = p_lo + pltpu.roll(p_hi, n_c - 1, 0) + b1_ref[i]
        hid = _gelu_tanh(hid).astype(BF16)
        outs.append(_dot(hid, w2_ref[i]) + b2_ref[i])
    out_ref[0, 0] = jnp.concatenate(outs[::-1], axis=1).astype(BF16)


def _compress(cmp4, cmp_pos, cmp_w1, cmp_b1, cmp_w2, cmp_b2):
    b, _, n_c, cw = cmp4.shape
    half = CMP_STRIDE * HEAD_DIM
    pos = cmp_pos.reshape(2, 2, half).astype(F32)
    w1 = cmp_w1.reshape(2, 2, half, CMP_HIDDEN).astype(BF16)
    b1 = cmp_b1.reshape(2, 1, CMP_HIDDEN).astype(F32)
    w2 = cmp_w2.astype(BF16)
    b2 = cmp_b2.reshape(2, 1, HEAD_DIM).astype(F32)
    full = lambda shape: pl.BlockSpec(shape, lambda i, g: (0,) * len(shape))
    return pl.pallas_call(
        _compress_kernel,
        grid=(b, N_KV),
        in_specs=[
            pl.BlockSpec((1, 1, n_c, cw), lambda i, g: (i, g, 0, 0)),
            pl.BlockSpec((1, 1, n_c, cw), lambda i, g: (i, N_KV + g, 0, 0)),
            full((2, 2, half)),
            full((2, 2, half, CMP_HIDDEN)),
            full((2, 1, CMP_HIDDEN)),
            full((2, CMP_HIDDEN, HEAD_DIM)),
            full((2, 1, HEAD_DIM)),
        ],
        out_specs=pl.BlockSpec((1, 1, n_c, 2 * HEAD_DIM), lambda i, g: (i, g, 0, 0)),
        out_shape=jax.ShapeDtypeStruct((b, N_KV, n_c, 2 * HEAD_DIM), BF16),
        compiler_params=_compiler_params(("parallel", "parallel")),
        name="compress",
    )(cmp4, cmp4, pos, w1, b1, w2, b2)


def _t5_bucket(dist):
    n = jnp.maximum(dist, 0)
    max_exact = REL_BUCKETS // 2
    nf = jnp.maximum(n, 1).astype(F32)
    large = max_exact + (jnp.log(nf / max_exact) / math.log(REL_MAX_DIST / max_exact)
                         * (REL_BUCKETS - max_exact)).astype(jnp.int32)
    large = jnp.minimum(large, REL_BUCKETS - 1)
    return jnp.where(n < max_exact, n, large)


def _bucket_thresholds():
    buckets = _t5_bucket(jnp.arange(REL_MAX_DIST + 1))
    return jnp.sum(buckets[None, :] < jnp.arange(REL_BUCKETS)[:, None], axis=1).astype(jnp.int32)


def _bias_of_dist(dist, head, thr_ref, tbl_ref):
    bias = jnp.full(dist.shape, tbl_ref[head], F32)
    for k in range(1, REL_BUCKETS):
        bias = jnp.where(dist >= thr_ref[k], tbl_ref[k * N_HEADS + head], bias)
    return bias


BIAS_ROWS = 32


def _bias_c_kernel(thr_ref, tbl_ref, out_ref):
    _, tr, n_c = out_ref.shape
    r0 = pl.program_id(0) * tr

    def chunk(ci, carry):
        row0 = pl.multiple_of(ci * BIAS_ROWS, BIAS_ROWS)
        pos = r0 + row0 + lax.broadcasted_iota(jnp.int32, (BIAS_ROWS, n_c), 0)
        key_end = lax.broadcasted_iota(jnp.int32, (BIAS_ROWS, n_c), 1) * CMP_STRIDE + (CMP_BLOCK - 1)
        dist = pos - key_end
        for h in range(N_HEADS):
            bias = _bias_of_dist(dist, h, thr_ref, tbl_ref)
            out_ref[h, pl.ds(row0, BIAS_ROWS), :] = jnp.where(dist >= 0, bias, NEG_INF)
        return carry

    lax.fori_loop(0, tr // BIAS_ROWS, chunk, 0)


def _bias_near_kernel(thr_ref, tbl_ref, near_ref, win_ref):
    tq = ATTN_TQ
    h = pl.program_id(0)
    far_bias = tbl_ref[(REL_BUCKETS - 1) * N_HEADS + h]

    def table(out_ref, lo_keys, window, offset):
        width = out_ref.shape[2]

        def chunk(ci, carry):
            row0 = pl.multiple_of(ci * BIAS_ROWS, BIAS_ROWS)
            dist = (lo_keys + row0 + lax.broadcasted_iota(jnp.int32, (BIAS_ROWS, width), 0)
                    - lax.broadcasted_iota(jnp.int32, (BIAS_ROWS, width), 1))
            visible = (dist >= 0) & (dist < window)
            bias = jnp.full(dist.shape, tbl_ref[h], F32)
            for k in range(1, REL_BUCKETS):
                bias = jnp.where(dist >= thr_ref[k], tbl_ref[k * N_HEADS + h], bias)
            out_ref[0, pl.ds(row0, BIAS_ROWS), :] = jnp.where(visible, bias - offset, NEG_INF)
            return carry

        lax.fori_loop(0, tq // BIAS_ROWS, chunk, 0)

    table(near_ref, SLC_NEAR_BACK, 1 << 30, far_bias)
    table(win_ref, WINDOW, WINDOW, 0.0)


def _attention_bias_tables(rel_bias, seq):
    tbl = (rel_bias.astype(F32) * LOG2_E).reshape(REL_BUCKETS * N_HEADS)
    thr = _bucket_thresholds()
    tq = ATTN_TQ
    n_c = seq // CMP_STRIDE
    smem = pl.BlockSpec(memory_space=pltpu.SMEM)
    tr = min(512, seq)
    bias_c = pl.pallas_call(
        _bias_c_kernel,
        grid=(seq // tr,),
        in_specs=[smem, smem],
        out_specs=pl.BlockSpec((N_HEADS, tr, n_c), lambda i: (0, i, 0)),
        out_shape=jax.ShapeDtypeStruct((N_HEADS, seq, n_c), F32),
        compiler_params=_compiler_params(("parallel",)),
        name="bias_cmp",
    )(thr, tbl)
    head_block = lambda w: pl.BlockSpec((1, tq, w), lambda h: (h, 0, 0))
    widths = (SLC_NEAR_BACK + tq, WINDOW + tq)
    bias_near, bias_win = pl.pallas_call(
        _bias_near_kernel,
        grid=(N_HEADS,),
        in_specs=[smem, smem],
        out_specs=[head_block(w) for w in widths],
        out_shape=[jax.ShapeDtypeStruct((N_HEADS, tq, w), F32) for w in widths],
        compiler_params=_compiler_params(("parallel",)),
        name="bias_near",
    )(thr, tbl)
    return bias_c, bias_near, bias_win


def _nsa_constants(seq):
    n_c = seq // CMP_STRIDE
    n_blk = seq // SLC_BLOCK
    cmp_start = jnp.arange(n_c) * CMP_STRIDE
    blk_start = jnp.arange(n_blk) * SLC_BLOCK
    overlap_t = ((cmp_start[None, :] <= blk_start[:, None] + SLC_BLOCK - 1)
                 & (cmp_start[None, :] + CMP_BLOCK - 1 >= blk_start[:, None]))
    overlap_t = overlap_t & (cmp_start[None, :] + CMP_BLOCK <= seq)
    ones_rows = jnp.arange(SUBLANES)[:, None] == 0
    overlap_t = jnp.concatenate([overlap_t, jnp.broadcast_to(ones_rows, (SUBLANES, n_c))], axis=0)
    cmp_ones = jnp.broadcast_to(jnp.arange(LANES)[None, :] == 0, (n_c, LANES))
    return overlap_t.astype(BF16), cmp_ones.astype(BF16)


def _build_kv_scratch(seq, slc_ref, win_ref, g, ks_sc, vs_sc, kw_sc, vw_sc, knorm_sc):
    chunk = min(512, seq)
    lane_p = lax.broadcasted_iota(jnp.int32, (KV_PAD, LANES), 1)
    zeros = jnp.zeros((KV_PAD, LANES), BF16)
    ks_sc[0:KV_PAD] = jnp.where(lane_p >= HEAD_DIM, 1.0, 0.0).astype(BF16)
    kw_sc[0:KV_PAD] = jnp.where(lane_p == HEAD_DIM, NEG_INF, 0.0).astype(BF16)
    vs_sc[0:KV_PAD] = zeros
    vw_sc[0:KV_PAD] = zeros
    lane = lax.broadcasted_iota(jnp.int32, (chunk, LANES), 1)
    row = lax.broadcasted_iota(jnp.int32, (chunk, LANES), 0)
    lo_half = lane < HEAD_DIM
    ones_lane = jnp.where(lane == HEAD_DIM, 1.0, 0.0)
    k_sq_max = jnp.zeros((chunk, 1), F32)
    for c in range(seq // chunk):
        r0 = c * chunk
        dst = slice(KV_PAD + r0, KV_PAD + r0 + chunk)
        blk = jnp.right_shift(r0 + row, int(math.log2(SLC_BLOCK)))
        lanes_g = slice(g * LANES, (g + 1) * LANES)
        slab = slc_ref[0, r0:r0 + chunk, lanes_g].astype(F32)
        k_sq_max = jnp.maximum(k_sq_max, jnp.sum(jnp.where(lo_half, slab * slab, 0.0), axis=1, keepdims=True))
        ks_sc[dst] = jnp.where(lo_half, slab, jnp.where(lane - HEAD_DIM == blk, 1.0, 0.0)).astype(BF16)
        vs_sc[dst] = jnp.where(lo_half, pltpu.roll(slab, HEAD_DIM, 1), ones_lane).astype(BF16)
        slab = win_ref[0, r0:r0 + chunk, lanes_g].astype(F32)
        kw_sc[dst] = jnp.where(lo_half, slab, 0.0).astype(BF16)
        vw_sc[dst] = jnp.where(lo_half, pltpu.roll(slab, HEAD_DIM, 1), ones_lane).astype(BF16)
    knorm_sc[...] = jnp.broadcast_to(jnp.max(k_sq_max, axis=0, keepdims=True), knorm_sc.shape)


def _nsa_select(seq, g, q0, q_ref, vkc_ref, bias_c_ref, overlap_t_ref, cmp_ones_ref):
    tq = ATTN_TQ
    n_blk = seq // SLC_BLOCK
    n_sel = min(N_SEL, n_blk)
    rows = HPG * tq
    lane = lax.broadcasted_iota(jnp.int32, (tq, LANES), 1)
    lo_half = lane < HEAD_DIM
    lane_r = lax.broadcasted_iota(jnp.int32, (rows, LANES), 1)

    q_lo, q_hi = [], []
    for pair_idx in range(HPG // 2):
        lanes_p = slice((g * (HPG // 2) + pair_idx) * LANES, (g * (HPG // 2) + pair_idx + 1) * LANES)
        q2 = q_ref[0, :, lanes_p].astype(F32)
        q2r = pltpu.roll(q2, HEAD_DIM, 1)
        q_lo += [jnp.where(lo_half, q2, 0.0), jnp.where(lo_half, q2r, 0.0)]
        q_hi += [jnp.where(lo_half, 0.0, q2r), jnp.where(lo_half, 0.0, q2)]
    q_lo = jnp.concatenate(q_lo, axis=0)
    q_hi = jnp.concatenate(q_hi, axis=0)

    vkc = vkc_ref[0, g]
    s_c = _dot_nt(q_hi.astype(BF16), vkc) + bias_c_ref[g * HPG:(g + 1) * HPG].reshape(rows, -1)
    m_c = jnp.max(s_c, axis=1, keepdims=True)
    e_cb = jnp.exp2(s_c - m_c).astype(BF16)
    pv_c = _dot(e_cb, jnp.concatenate([vkc, cmp_ones_ref[...]], axis=1))
    row_pos = q0 + (lax.broadcasted_iota(jnp.int32, (rows, LANES), 0) & (tq - 1))
    has_key = row_pos >= CMP_BLOCK - 1
    o_cmp = jnp.where(has_key, pv_c[:, :LANES] / pv_c[:, LANES:LANES + 1], 0.0)

    imp_t4 = _dot_nt(overlap_t_ref[...], e_cb)
    imp_t = None
    for h in range(HPG):
        part = imp_t4[:, h * tq:(h + 1) * tq]
        part = part[:n_blk] / part[n_blk:n_blk + 1]
        imp_t = part if imp_t is None else imp_t + part

    blk = lax.broadcasted_iota(jnp.int32, (n_blk, tq), 0)
    pos = q0 + lax.broadcasted_iota(jnp.int32, (n_blk, tq), 1)
    cur = jnp.right_shift(pos, int(math.log2(SLC_BLOCK)))
    forced = (blk == 0) | (blk == cur) | (blk == cur - 1)
    valid = blk * SLC_BLOCK <= pos
    score = jnp.where(forced, BIG, jnp.where(valid, imp_t, -BIG))
    blk_f = blk.astype(F32)
    pen_t = jnp.full((n_blk, tq), NEG_INF, F32)
    for _ in range(n_sel):
        top = jnp.max(score, axis=0, keepdims=True)
        first = jnp.min(jnp.where(score == top, blk_f, float(n_blk)), axis=0, keepdims=True)
        hit = blk_f == first
        pen_t = jnp.where(hit, 0.0, pen_t)
        score = jnp.where(hit, -jnp.inf, score)
    pieces = [jnp.zeros((tq, HEAD_DIM), F32), jnp.transpose(pen_t)]
    if n_blk < LANES - HEAD_DIM:
        pieces.append(jnp.full((tq, LANES - HEAD_DIM - n_blk), NEG_INF, F32))
    pen_lanes = jnp.concatenate(pieces, axis=1)
    q_slc = jnp.where(lane_r < HEAD_DIM, q_lo, jnp.concatenate([pen_lanes] * HPG, axis=0)).astype(BF16)
    return o_cmp, q_lo, q_slc


def _nsa_kernel(seq, q_ref, vkc_ref, slc_ref, win_ref, gate_ref, bias_c_ref, bias_near_ref, bias_win_ref,
                overlap_t_ref, cmp_ones_ref, o_ref, *scratch):
    tq = ATTN_TQ
    rows = HPG * tq
    groups = range(N_KV)
    kv_sc = [scratch[4 * g:4 * g + 4] for g in groups]
    s_sc, mrun_sc, acc_sc, knorm_sc = (scratch[4 * N_KV + i * N_KV:4 * N_KV + (i + 1) * N_KV] for i in range(4))
    qt = pl.program_id(1)
    q0 = pl.multiple_of(qt * tq, tq)
    lane = lax.broadcasted_iota(jnp.int32, (tq, LANES), 1)
    lo_half = lane < HEAD_DIM
    lane_r = lax.broadcasted_iota(jnp.int32, (rows, LANES), 1)

    @pl.when(qt == 0)
    def _():
        for g in groups:
            _build_kv_scratch(seq, slc_ref, win_ref, g, *kv_sc[g], knorm_sc[g])

    sel = [_nsa_select(seq, g, q0, q_ref, vkc_ref, bias_c_ref, overlap_t_ref, cmp_ones_ref) for g in groups]
    o_cmp = [s[0] for s in sel]
    q_lo = [s[1] for s in sel]
    q_slc = [s[2] for s in sel]
    head_rows = lambda ref, g: ref[g * HPG:(g + 1) * HPG].reshape(rows, -1)

    o_win = []
    for g in groups:
        kw_sc, vw_sc = kv_sc[g][2], kv_sc[g][3]
        q_win = jnp.where(lane_r == HEAD_DIM, 1.0, q_lo[g]).astype(BF16)
        win_rows = pl.ds(q0, WINDOW + tq)
        s_w = _dot_nt(q_win, kw_sc[win_rows, :]) + head_rows(bias_win_ref, g)
        p_w = jnp.exp2(s_w - jnp.max(s_w, axis=1, keepdims=True)).astype(BF16)
        acc_w = _dot(p_w, vw_sc[win_rows, :])
        o_win.append(acc_w / acc_w[:, HEAD_DIM:HEAD_DIM + 1])

    tk = SLC_TK
    n_far = qt // (tk // tq)
    lane_tiles = lambda a: [a[:, j * LANES:(j + 1) * LANES] for j in range(a.shape[1] // LANES)]
    near0 = pl.multiple_of(q0 + (KV_PAD - SLC_NEAR_BACK), tq)

    def tile_rows(u):
        return pl.ds(pl.multiple_of(near0 - u * tk, tq), tk)

    def tile_cols(u):
        return pl.ds(pl.multiple_of(u * tk, tk), tk)

    def score_tile(u, g, bias):
        s = _dot_nt(q_slc[g], kv_sc[g][0][tile_rows(u), :])
        if bias is not None:
            s = s + bias
        s_sc[g][:, tile_cols(u)] = s
        return functools.reduce(jnp.maximum, lane_tiles(s))

    shift_cap = []
    slack = jnp.float32(-jnp.inf)
    for g in groups:
        lane_max = score_tile(0, g, head_rows(bias_near_ref, g))
        mrun_sc[g][...] = lane_max
        m_near = jnp.max(lane_max, axis=1, keepdims=True)
        q_norm = jnp.sqrt(jnp.sum(q_lo[g] * q_lo[g], axis=1, keepdims=True))
        far_bound = q_norm * jnp.sqrt(knorm_sc[g][0:1, 0:1]) * SOFTMAX_BOUND_SLACK
        shift_cap.append(jnp.maximum(m_near, far_bound - SOFTMAX_SHIFT_MARGIN))
        slack = jnp.maximum(slack, jnp.max(far_bound - m_near))
    single_pass = slack <= 2.0 * SOFTMAX_SHIFT_MARGIN

    def weigh(s, u, g):
        shift = mrun_sc[g][...]
        p = jnp.exp2(s - jnp.concatenate([shift] * (tk // LANES), axis=1))
        acc_sc[g][...] = acc_sc[g][...] + _dot(p.astype(BF16), kv_sc[g][1][tile_rows(u), :])

    for g in groups:
        acc_sc[g][...] = jnp.zeros((rows, LANES), F32)

    @pl.when(single_pass)
    def _():
        for g in groups:
            mrun_sc[g][...] = jnp.broadcast_to(shift_cap[g], (rows, LANES))
            weigh(s_sc[g][:, tile_cols(0)], 0, g)

        def fused(u, carry):
            for g in groups:
                weigh(_dot_nt(q_slc[g], kv_sc[g][0][tile_rows(u), :]), u, g)
            return carry

        lax.fori_loop(1, n_far + 1, fused, 0)

    @pl.when(jnp.logical_not(single_pass))
    def _():
        def pass1(u, carry):
            for g in groups:
                mrun_sc[g][...] = jnp.maximum(mrun_sc[g][...], score_tile(u, g, None))
            return carry

        lax.fori_loop(1, n_far + 1, pass1, 0)
        for g in groups:
            m_s = jnp.max(mrun_sc[g][...], axis=1, keepdims=True)
            mrun_sc[g][...] = jnp.broadcast_to(m_s, (rows, LANES))

        def pass2(u, carry):
            for g in groups:
                weigh(s_sc[g][:, tile_cols(u)], u, g)
            return carry

        lax.fori_loop(0, n_far + 1, pass2, 0)

    for g in groups:
        acc = acc_sc[g][...]
        o_slc = acc / acc[:, HEAD_DIM:HEAD_DIM + 1]

        gates = gate_ref[0, :, g * GATE_PAD:(g + 1) * GATE_PAD]
        outs = []
        for h in range(HPG):
            sl = slice(h * tq, (h + 1) * tq)
            g_c, g_s, g_w = (gates[:, br * HPG + h:br * HPG + h + 1] for br in range(3))
            outs.append(g_c * o_cmp[g][sl] + g_s * o_slc[sl] + g_w * o_win[g][sl])
        for pair_idx in range(HPG // 2):
            even, odd = outs[2 * pair_idx], outs[2 * pair_idx + 1]
            merged = jnp.where(lo_half, even, pltpu.roll(odd, HEAD_DIM, 1))
            lanes_p = slice((g * (HPG // 2) + pair_idx) * LANES, (g * (HPG // 2) + pair_idx + 1) * LANES)
            o_ref[0, :, lanes_p] = merged.astype(o_ref.dtype)


def _nsa(q, vkc, slc, win, gates, bias_c, bias_near, bias_win, overlap_t, cmp_ones):
    b, seq, _ = q.shape
    n_c = seq // CMP_STRIDE
    n_blk = seq // SLC_BLOCK
    assert n_blk <= LANES - HEAD_DIM and seq % SLC_TK == 0 and KV_PAD >= SLC_TK
    tq = ATTN_TQ
    rows = HPG * tq
    const = lambda a: pl.BlockSpec(a.shape, lambda i, t: (0,) * a.ndim)
    tile = lambda w: pl.BlockSpec((1, tq, w), lambda i, t: (i, t, 0))
    whole_seq = lambda w: pl.BlockSpec((1, seq, w), lambda i, t: (i, 0, 0))
    per_group = lambda shape: [pltpu.VMEM(shape, F32)] * N_KV
    return pl.pallas_call(
        functools.partial(_nsa_kernel, seq),
        grid=(b, seq // tq),
        in_specs=[
            tile(ATTN_WIDTH),
            pl.BlockSpec((1, N_KV, n_c, 2 * HEAD_DIM), lambda i, t: (i, 0, 0, 0)),
            whole_seq(N_KV * 2 * HEAD_DIM), whole_seq(N_KV * 2 * HEAD_DIM),
            tile(N_KV * GATE_PAD),
            pl.BlockSpec((N_HEADS, tq, n_c), lambda i, t: (0, t, 0)),
            const(bias_near), const(bias_win),
            const(overlap_t), const(cmp_ones),
        ],
        out_specs=tile(ATTN_WIDTH),
        out_shape=jax.ShapeDtypeStruct((b, seq, ATTN_WIDTH), BF16),
        scratch_shapes=(
            [pltpu.VMEM((KV_PAD + seq, LANES), BF16)] * (4 * N_KV)
            + per_group((rows, seq)) + per_group((rows, LANES)) + per_group((rows, LANES))
            + per_group((SUBLANES, LANES))
        ),
        compiler_params=_compiler_params(("parallel", "arbitrary")),
        name="nsa",
    )(q, vkc, slc, win, gates, bias_c, bias_near, bias_win, overlap_t, cmp_ones)


def _mixer_inputs(x, rel_bias, w_in, b_in, cmp_pos, cmp_w1, cmp_b1, cmp_w2, cmp_b2):
    b, seq, d = x.shape
    wp, bp = _pack_in_proj(w_in, b_in, d)
    q, slc, win, cmp, gates, u, merge = _in_proj(x.reshape(b * seq, d), wp, bp, d, 512)
    n_c = seq // CMP_STRIDE
    cmp4 = cmp.reshape(b, seq, 2 * N_KV, HEAD_DIM).transpose(0, 2, 1, 3).reshape(
        b, 2 * N_KV, n_c, CMP_STRIDE * HEAD_DIM)
    vkc = _compress(cmp4, cmp_pos, cmp_w1, cmp_b1, cmp_w2, cmp_b2)
    bias_c, bias_near, bias_win = _attention_bias_tables(rel_bias, seq)
    overlap_t, cmp_ones = _nsa_constants(seq)
    o = _nsa(q.reshape(b, seq, -1), vkc, slc.reshape(b, seq, -1), win.reshape(b, seq, -1),
             gates.reshape(b, seq, -1), bias_c, bias_near, bias_win, overlap_t, cmp_ones)
    return o, u, merge


S5_HALF_GROUPS = S5_GROUPS // 2
S5_HALF_IN = S5_HALF_GROUPS * S5_GROUP
S5_HALF_STATE = S5_HALF_GROUPS * S5_STATE
S5_SCAN_LANES = 512
S5_CHUNK = 64
S5_UNROLL = 8


def _s5_params(lam_re, lam_im, log_dt, b_re, b_im, c_re, c_im, nb):
    dt = jnp.exp(log_dt.astype(F32))[:, None]
    lr, li = lam_re.astype(F32), lam_im.astype(F32)
    mag = jnp.exp(lr * dt)
    ab_re, ab_im = mag * jnp.cos(li * dt), mag * jnp.sin(li * dt)
    nr, ni = ab_re - 1.0, ab_im
    den = lr * lr + li * li
    fr, fi = (nr * lr + ni * li) / den, (ni * lr - nr * li) / den
    br, bim = b_re.astype(F32), b_im.astype(F32)
    bb_re = fr[..., None] * br - fi[..., None] * bim
    bb_im = fr[..., None] * bim + fi[..., None] * br
    eye = jnp.eye(S5_HALF_GROUPS, dtype=F32)

    def in_mat(bb):
        t = bb.reshape(2, S5_HALF_GROUPS, S5_STATE, S5_GROUP)
        m = jnp.einsum('kgph,gj->kghjp', t, eye)
        return m.reshape(2, S5_HALF_IN, S5_HALF_STATE)

    def out_mat(c):
        t = c.astype(F32).reshape(2, S5_HALF_GROUPS, S5_GROUP, S5_STATE)
        m = jnp.einsum('kghp,gj->kgpjh', t, eye)
        return m.reshape(2, S5_HALF_STATE, S5_HALF_IN)

    bmat = jnp.concatenate([in_mat(bb_re), in_mat(bb_im)], axis=2).astype(BF16)
    cmat = jnp.concatenate([out_mat(c_re), -out_mat(c_im)], axis=1).astype(BF16)
    a = jnp.concatenate([ab_re.reshape(2, S5_HALF_STATE), ab_im.reshape(2, S5_HALF_STATE)], axis=1)
    a = jnp.broadcast_to(a.reshape(1, 4 * S5_HALF_STATE), (nb, 4 * S5_HALF_STATE))
    return bmat, cmat, a


def _s5_kernel(u_ref, bmat_ref, cmat_ref, a_ref, d_ref, y_ref, ut_sc, x_sc, st_sc):
    nb, t_len, _ = u_ref.shape
    half_w = 2 * S5_HALF_STATE

    @pl.when(pl.program_id(0) == 0)
    def _():
        st_sc[...] = jnp.zeros_like(st_sc)

    n_cb = ut_sc.shape[0]
    for b in range(nb):
        for cb in range(n_cb):
            ut_sc[cb, pl.ds(b, t_len, stride=nb), :] = u_ref[b, :, cb * LANES:(cb + 1) * LANES]
    ut = jnp.concatenate([ut_sc[cb] for cb in range(n_cb)], axis=1)
    ub = ut.astype(BF16)
    for k in range(2):
        x_sc[:, k * half_w:(k + 1) * half_w] = _dot(ub[:, k * S5_HALF_IN:(k + 1) * S5_HALF_IN], bmat_ref[k])

    for k in range(2):
        for j in range(S5_HALF_STATE // S5_SCAN_LANES):
            re0 = k * half_w + j * S5_SCAN_LANES
            im0 = re0 + S5_HALF_STATE
            re_sl, im_sl = pl.ds(re0, S5_SCAN_LANES), pl.ds(im0, S5_SCAN_LANES)
            ar, ai = a_ref[:, re_sl], a_ref[:, im_sl]

            def steps(c, carry):
                xr, xi = carry
                for s in range(S5_UNROLL):
                    rows = pl.ds(pl.multiple_of((c * S5_UNROLL + s) * nb, nb), nb)
                    nxr = ar * xr - ai * xi + x_sc[rows, re_sl]
                    nxi = ar * xi + ai * xr + x_sc[rows, im_sl]
                    x_sc[rows, re_sl] = nxr
                    x_sc[rows, im_sl] = nxi
                    xr, xi = nxr, nxi
                return xr, xi

            xr, xi = lax.fori_loop(0, t_len // S5_UNROLL, steps, (st_sc[:, re_sl], st_sc[:, im_sl]))
            st_sc[:, re_sl] = xr
            st_sc[:, im_sl] = xi

    xs = x_sc[...].astype(BF16)
    y = jnp.concatenate([_dot(xs[:, k * half_w:(k + 1) * half_w], cmat_ref[k]) for k in range(2)], axis=1)
    y = y + d_ref[...] * ut
    for cb in range(n_cb):
        ut_sc[cb] = y[:, cb * LANES:(cb + 1) * LANES]
    for b in range(nb):
        for cb in range(n_cb):
            y_ref[b, :, cb * LANES:(cb + 1) * LANES] = ut_sc[cb, pl.ds(b, t_len, stride=nb), :]


def _s5(u, bmat, cmat, a, d_skip):
    nb, seq, w = u.shape
    t_len = min(S5_CHUNK, seq)
    full = lambda shape: pl.BlockSpec(shape, lambda c: (0,) * len(shape))
    return pl.pallas_call(
        _s5_kernel,
        grid=(seq // t_len,),
        in_specs=[
            pl.BlockSpec((nb, t_len, w), lambda c: (0, c, 0)),
            full(bmat.shape), full(cmat.shape), full(a.shape), full((1, w)),
        ],
        out_specs=pl.BlockSpec((nb, t_len, w), lambda c: (0, c, 0)),
        out_shape=jax.ShapeDtypeStruct((nb, seq, w), F32),
        scratch_shapes=[
            pltpu.VMEM((w // LANES, t_len * nb, LANES), F32),
            pltpu.VMEM((t_len * nb, 4 * S5_HALF_STATE), F32),
            pltpu.VMEM((nb, 4 * S5_HALF_STATE), F32),
        ],
        compiler_params=_compiler_params(("arbitrary",)),
        name="s5",
    )(u, bmat, cmat, a, d_skip.reshape(1, w).astype(F32))


ROUTE_PAD = LANES
_R_E1, _R_E2, _R_W1, _R_W2, _R_RANK1, _R_RANK2 = range(6)


def _layer_norm(t, g, b):
    mu = jnp.mean(t, axis=1, keepdims=True)
    c = t - mu
    var = jnp.mean(c * c, axis=1, keepdims=True)
    return c * lax.rsqrt(var + LN_EPS) * g + b


def _post_kernel(x_ref, o_ref, y_ref, m_ref, wup_ref, wval_ref, wgate_ref, bgate_ref, wout_ref,
                 g1_ref, b1_ref, wr_ref, br_ref, h_ref, hp_ref, route_ref, cnt_ref, run_sc):
    tm, d = x_ref.shape

    @pl.when(pl.program_id(0) == 0)
    def _():
        run_sc[...] = jnp.zeros_like(run_sc)

    y_a = _dot(o_ref[...], wup_ref[...])
    z = _gelu_tanh(y_ref[...]).astype(BF16)
    y_b = _dot(z, wval_ref[...]) * jax.nn.sigmoid(_dot(z, wgate_ref[...]) + bgate_ref[...])
    mixed = m_ref[:, :d] * y_a + m_ref[:, d:] * y_b
    t = DN_ALPHA * x_ref[...] + _dot(mixed.astype(BF16), wout_ref[...])
    h = _layer_norm(t, g1_ref[...], b1_ref[...])
    h_ref[...] = h
    hp_ref[...] = _pack_rows(h)

    logits = _dot(h.astype(BF16), wr_ref[...]) + br_ref[...]
    lane = lax.broadcasted_iota(jnp.int32, (tm, ROUTE_PAD), 1)
    lane_f = lane.astype(F32)
    is_group = lane < N_EGROUPS

    def first_max(v):
        top = jnp.max(v, axis=1, keepdims=True)
        idx = jnp.min(jnp.where(v == top, lane_f, float(ROUTE_PAD)), axis=1, keepdims=True)
        return top, idx

    g_max, g_top = first_max(jnp.where(is_group, logits, -jnp.inf))
    p_group = 1.0 / jnp.sum(jnp.where(is_group, jnp.exp(logits - g_max), 0.0), axis=1, keepdims=True)
    grp_of_lane = jnp.right_shift(lane - N_EGROUPS, int(math.log2(EXPERTS_PER_GROUP))).astype(F32)
    in_group = (lane >= N_EGROUPS) & (lane < N_EGROUPS + N_EXPERTS) & (grp_of_lane == g_top)
    e_log = jnp.where(in_group, logits, -jnp.inf)
    v1, i1 = first_max(e_log)
    hit1 = lane_f == i1
    v2, i2 = first_max(jnp.where(hit1, -jnp.inf, e_log))
    hit2 = lane_f == i2
    e2 = jnp.exp(v2 - v1)
    w1 = p_group / (1.0 + e2)
    w2 = p_group * e2 / (1.0 + e2)

    hits = jnp.where(hit1 | hit2, 1.0, 0.0)
    row = lax.broadcasted_iota(jnp.int32, (tm, tm), 0)
    col = lax.broadcasted_iota(jnp.int32, (tm, tm), 1)
    earlier = jnp.where(col < row, 1.0, 0.0).astype(BF16)
    before = _dot(earlier, hits.astype(BF16)) + run_sc[...]
    rank1 = jnp.sum(jnp.where(hit1, before, 0.0), axis=1, keepdims=True)
    rank2 = jnp.sum(jnp.where(hit2, before, 0.0), axis=1, keepdims=True)
    run_sc[...] = run_sc[...] + jnp.sum(hits, axis=0, keepdims=True)
    cnt_ref[...] = run_sc[...]

    rec = jnp.zeros((tm, ROUTE_PAD), F32)
    for slot, val in ((_R_E1, i1 - N_EGROUPS), (_R_E2, i2 - N_EGROUPS), (_R_W1, w1), (_R_W2, w2),
                      (_R_RANK1, rank1), (_R_RANK2, rank2)):
        rec = jnp.where(lane == slot, val, rec)
    route_ref[...] = rec


def _post(x2d, o2d, y2d, merge, w_attn_up, s5_w_val, s5_w_gate, s5_b_gate, w_out, ln1_g, ln1_b,
          router_w_group, router_b_group, router_w_expert, router_b_expert, tm):
    n, d = x2d.shape
    rpad = ROUTE_PAD - N_EGROUPS - N_EXPERTS
    wr = jnp.concatenate([router_w_group, router_w_expert, jnp.zeros((d, rpad), F32)], axis=1).astype(BF16)
    br = jnp.concatenate([router_b_group, router_b_expert, jnp.zeros((rpad,), F32)]).reshape(1, -1).astype(F32)
    row = lambda w: pl.BlockSpec((tm, w), lambda i: (i, 0))
    full = lambda a: pl.BlockSpec(a.shape, lambda i: (0,) * a.ndim)
    weights = [w_attn_up.astype(BF16), s5_w_val.astype(BF16), s5_w_gate.astype(BF16),
               s5_b_gate.reshape(1, d).astype(F32), w_out.astype(BF16),
               ln1_g.reshape(1, d).astype(F32), ln1_b.reshape(1, d).astype(F32), wr, br]
    return pl.pallas_call(
        _post_kernel,
        grid=(n // tm,),
        in_specs=[row(d), row(ATTN_WIDTH), row(S5_WIDTH), row(2 * d)] + [full(w) for w in weights],
        out_specs=[row(d), row(d // 2), row(ROUTE_PAD), pl.BlockSpec((1, ROUTE_PAD), lambda i: (0, 0))],
        out_shape=[jax.ShapeDtypeStruct((n, d), F32), jax.ShapeDtypeStruct((n, d // 2), jnp.uint32),
                   jax.ShapeDtypeStruct((n, ROUTE_PAD), F32), jax.ShapeDtypeStruct((1, ROUTE_PAD), F32)],
        scratch_shapes=[pltpu.VMEM((1, ROUTE_PAD), F32)],
        compiler_params=_compiler_params(("arbitrary",)),
        name="post_mixer",
    )(x2d, o2d, y2d, merge, *weights)


def _plan_kernel(route_ref, cnt_ref, dest_ref):
    tm = route_ref.shape[0]
    lane8 = lax.broadcasted_iota(jnp.int32, (SUBLANES, ROUTE_PAD), 1)
    counts = jnp.broadcast_to(cnt_ref[...], (SUBLANES, ROUTE_PAD)).astype(jnp.int32)
    shift = int(math.log2(MOE_BLOCK))
    padded = jnp.left_shift(jnp.right_shift(counts + (MOE_BLOCK - 1), shift), shift)
    incl = padded
    step = 1
    while step < ROUTE_PAD:
        incl = incl + jnp.where(lane8 >= step, pltpu.roll(incl, step, 1), 0)
        step *= 2
    pstart = (incl - padded)[0:1].astype(F32)
    route = route_ref[...]
    expert_of_lane = (lax.broadcasted_iota(jnp.int32, (tm, ROUTE_PAD), 1) - N_EGROUPS).astype(F32)
    lane = lax.broadcasted_iota(jnp.int32, (tm, ROUTE_PAD), 1)

    def dest(e_slot, rank_slot):
        hit = expert_of_lane == route[:, e_slot:e_slot + 1]
        return jnp.sum(jnp.where(hit, pstart, 0.0), axis=1, keepdims=True) + route[:, rank_slot:rank_slot + 1]

    d1 = dest(_R_E1, _R_RANK1)
    d2 = dest(_R_E2, _R_RANK2)
    dest_ref[...] = jnp.where(lane == 0, d1, jnp.where(lane == 1, d2, 0.0)).astype(jnp.int32)


def _plan(route, counts_row):
    n = route.shape[0]
    tm = min(1024, n)
    return pl.pallas_call(
        _plan_kernel,
        grid=(n // tm,),
        in_specs=[pl.BlockSpec((tm, ROUTE_PAD), lambda i: (i, 0)),
                  pl.BlockSpec((1, ROUTE_PAD), lambda i: (0, 0))],
        out_specs=pl.BlockSpec((tm, ROUTE_PAD), lambda i: (i, 0)),
        out_shape=jax.ShapeDtypeStruct((n, ROUTE_PAD), jnp.int32),
        compiler_params=_compiler_params(("parallel",)),
        name="moe_plan",
    )(route, counts_row)


SC_GATHER_ROWS = 32


def _sc_row_gather(table, idx):
    n_idx = idx.shape[0]
    d = table.shape[1]
    info = plsc.get_sparse_core_info()
    n_workers = info.num_cores * info.num_subcores
    per_worker = n_idx // n_workers
    assert n_idx % (n_workers * SC_GATHER_ROWS) == 0
    mesh = plsc.VectorSubcoreMesh(core_axis_name="c", subcore_axis_name="s")

    @functools.partial(
        pl.kernel, mesh=mesh,
        out_type=jax.ShapeDtypeStruct((n_idx, d), table.dtype),
        scratch_types=[
            pltpu.VMEM((SC_GATHER_ROWS,), jnp.int32),
            pltpu.VMEM((SC_GATHER_ROWS, d), table.dtype),
            pltpu.SemaphoreType.DMA,
        ],
    )
    def gather(table_hbm, idx_hbm, out_hbm, idx_v, rows_v, sem):
        worker = lax.axis_index("s") * info.num_cores + lax.axis_index("c")
        base = worker * per_worker

        @pl.loop(0, per_worker // SC_GATHER_ROWS)
        def _(j):
            off = base + j * SC_GATHER_ROWS
            pltpu.sync_copy(idx_hbm.at[pl.ds(off, SC_GATHER_ROWS)], idx_v)
            pltpu.async_copy(table_hbm.at[idx_v], rows_v, sem).wait()
            pltpu.sync_copy(rows_v, out_hbm.at[pl.ds(off, SC_GATHER_ROWS)])

    return gather(table, idx)


def _sc_row_scatter(rows, idx_a, idx_b, n_out):
    n, d = rows.shape
    info = plsc.get_sparse_core_info()
    n_workers = info.num_cores * info.num_subcores
    per_worker = n // n_workers
    assert n % (n_workers * SC_GATHER_ROWS) == 0
    mesh = plsc.VectorSubcoreMesh(core_axis_name="c", subcore_axis_name="s")

    @functools.partial(
        pl.kernel, mesh=mesh,
        out_type=jax.ShapeDtypeStruct((n_out, d), rows.dtype),
        scratch_types=[
            pltpu.VMEM((SC_GATHER_ROWS,), jnp.int32),
            pltpu.VMEM((SC_GATHER_ROWS,), jnp.int32),
            pltpu.VMEM((SC_GATHER_ROWS, d), rows.dtype),
        ],
    )
    def scatter(rows_hbm, idx_a_hbm, idx_b_hbm, out_hbm, idx_a_v, idx_b_v, rows_v):
        worker = lax.axis_index("s") * info.num_cores + lax.axis_index("c")
        base = worker * per_worker

        @pl.loop(0, per_worker // SC_GATHER_ROWS)
        def _(j):
            src = pl.ds(base + j * SC_GATHER_ROWS, SC_GATHER_ROWS)
            pltpu.sync_copy(rows_hbm.at[src], rows_v)
            pltpu.sync_copy(idx_a_hbm.at[src], idx_a_v)
            pltpu.sync_copy(idx_b_hbm.at[src], idx_b_v)
            pltpu.sync_copy(rows_v, out_hbm.at[idx_a_v])
            pltpu.sync_copy(rows_v, out_hbm.at[idx_b_v])

    return scatter(rows, idx_a, idx_b)


def _expert_kernel(blk_exp_ref, blk_valid_ref, x_ref, wg_ref, wu_ref, wd_ref, y_ref, wg_sc, wu_sc, wd_sc):
    i = pl.program_id(0)
    n_valid = blk_valid_ref[i]

    @pl.when((i == 0) | (blk_exp_ref[i] != blk_exp_ref[jnp.maximum(i - 1, 0)]))
    def _():
        wg_sc[...] = wg_ref[0].astype(BF16)
        wu_sc[...] = wu_ref[0].astype(BF16)
        wd_sc[...] = wd_ref[0].astype(BF16)

    @pl.when(n_valid > 0)
    def _():
        row = lax.broadcasted_iota(jnp.int32, x_ref.shape, 0)
        words = jnp.where(row < n_valid, x_ref[...], jnp.uint32(0))
        xb = _unpack_rows(words).astype(BF16)
        h_gate = _dot(xb, wg_sc[...])
        h_up = _dot(xb, wu_sc[...])
        hb = (h_gate * jax.nn.sigmoid(h_gate) * h_up).astype(BF16)
        y_ref[...] = _pack_rows(_dot(hb, wd_sc[...]))

    @pl.when(n_valid == 0)
    def _():
        y_ref[...] = jnp.zeros_like(y_ref)


def _experts(blk_expert, blk_valid, xs, w_gate, w_up, w_down):
    n_blocks = blk_expert.shape[0]
    d = 2 * xs.shape[1]
    grid_spec = pltpu.PrefetchScalarGridSpec(
        num_scalar_prefetch=2,
        grid=(n_blocks,),
        in_specs=[
            pl.BlockSpec((MOE_BLOCK, d // 2), lambda i, be, bv: (i, 0)),
            pl.BlockSpec((1, d, D_EXPERT), lambda i, be, bv: (be[i], 0, 0)),
            pl.BlockSpec((1, d, D_EXPERT), lambda i, be, bv: (be[i], 0, 0)),
            pl.BlockSpec((1, D_EXPERT, d), lambda i, be, bv: (be[i], 0, 0)),
        ],
        out_specs=pl.BlockSpec((MOE_BLOCK, d // 2), lambda i, be, bv: (i, 0)),
        scratch_shapes=[
            pltpu.VMEM((d, D_EXPERT), BF16),
            pltpu.VMEM((d, D_EXPERT), BF16),
            pltpu.VMEM((D_EXPERT, d), BF16),
        ],
    )
    return pl.pallas_call(
        _expert_kernel,
        grid_spec=grid_spec,
        out_shape=jax.ShapeDtypeStruct((n_blocks * MOE_BLOCK, d // 2), jnp.uint32),
        compiler_params=_compiler_params(("arbitrary",)),
        name="experts",
    )(blk_expert, blk_valid, xs, w_gate, w_up, w_down)


COMBINE_TM = 512


def _combine_kernel(h_ref, y1_ref, y2_ref, route_ref, g2_ref, b2_ref, out_ref):
    route = route_ref[...]
    w1 = route[:, _R_W1:_R_W1 + 1]
    w2 = route[:, _R_W2:_R_W2 + 1]
    t = DN_ALPHA * h_ref[...] + (_unpack_rows(y1_ref[...]) * w1 + _unpack_rows(y2_ref[...]) * w2)
    out_ref[...] = _layer_norm(t, g2_ref[...], b2_ref[...])


def _combine(yg, h2d, route, ln2_g, ln2_b):
    n, d = h2d.shape
    tm = min(COMBINE_TM, n)
    n_tiles = n // tm
    row = lambda w: pl.BlockSpec((tm, w), lambda i: (i, 0))
    vec = pl.BlockSpec((1, d), lambda i: (0, 0))
    return pl.pallas_call(
        _combine_kernel,
        grid=(n_tiles,),
        in_specs=[row(d), row(d // 2), pl.BlockSpec((tm, d // 2), lambda i: (i + n_tiles, 0)), row(ROUTE_PAD),
                  vec, vec],
        out_specs=row(d),
        out_shape=jax.ShapeDtypeStruct((n, d), F32),
        compiler_params=_compiler_params(("parallel",)),
        name="combine",
    )(h2d, yg, yg, route, ln2_g.reshape(1, d).astype(F32), ln2_b.reshape(1, d).astype(F32))


def _moe(h2d, h_packed, route, counts_row, w_gate, w_up, w_down, ln2_g, ln2_b):
    n, d = h2d.shape
    dest = _plan(route, counts_row)
    dest1, dest2 = dest[:, 0], dest[:, 1]
    counts = counts_row[0, N_EGROUPS:N_EGROUPS + N_EXPERTS].astype(jnp.int32)
    padded = (counts + MOE_BLOCK - 1) // MOE_BLOCK * MOE_BLOCK
    pend = jnp.cumsum(padded)
    n_blocks = -(-(n * TOP_K_IN_GROUP) // MOE_BLOCK) + N_EXPERTS
    blk_row0 = jnp.arange(n_blocks, dtype=jnp.int32) * MOE_BLOCK
    blk_expert = jnp.minimum(jnp.sum(pend[None, :] <= blk_row0[:, None], axis=1), N_EXPERTS - 1).astype(jnp.int32)
    blk_valid = jnp.clip((pend - padded + counts)[blk_expert] - blk_row0, 0, MOE_BLOCK).astype(jnp.int32)
    xs = _sc_row_scatter(h_packed, dest1, dest2, n_blocks * MOE_BLOCK)
    yb = _experts(blk_expert, blk_valid, xs, w_gate, w_up, w_down)
    yg = _sc_row_gather(yb, jnp.concatenate([dest1, dest2]))
    return _combine(yg, h2d, route, ln2_g, ln2_b)


def kernel(x, rel_bias, w_in, b_in, cmp_pos, cmp_w1, cmp_b1, cmp_w2, cmp_b2, w_attn_up, s5_lambda_re, s5_lambda_im, s5_log_dt, s5_b_re, s5_b_im, s5_c_re, s5_c_im, s5_d, s5_w_val, s5_w_gate, s5_b_gate, w_out, ln1_g, ln1_b, router_w_group, router_b_group, router_w_expert, router_b_expert, exp_w_gate, exp_w_up, exp_w_down, ln2_g, ln2_b):
    b, seq, d = x.shape
    n = b * seq
    assert w_in.shape[0] == DEPTH
    l = 0
    o, u, merge = _mixer_inputs(x, rel_bias, w_in[l], b_in[l], cmp_pos[l], cmp_w1[l], cmp_b1[l],
                                cmp_w2[l], cmp_b2[l])
    bmat, cmat, a = _s5_params(s5_lambda_re[l], s5_lambda_im[l], s5_log_dt[l], s5_b_re[l], s5_b_im[l],
                               s5_c_re[l], s5_c_im[l], b)
    y_s = _s5(u.reshape(b, seq, S5_WIDTH), bmat, cmat, a, s5_d[l])
    h2d, h_packed, route, counts = _post(
        x.reshape(n, d), o.reshape(n, ATTN_WIDTH), y_s.reshape(n, S5_WIDTH), merge, w_attn_up[l], s5_w_val[l],
        s5_w_gate[l], s5_b_gate[l], w_out[l], ln1_g[l], ln1_b[l], router_w_group[l], router_b_group[l],
        router_w_expert[l], router_b_expert[l], 512)
    out = _moe(h2d, h_packed, route, counts, exp_w_gate[l], exp_w_up[l], exp_w_down[l], ln2_g[l], ln2_b[l])
    return out.reshape(b, seq, d)
```

```python
import functools
import math

import jax
import jax.numpy as jnp
from jax import lax
from jax.experimental import pallas as pl
from jax.experimental.pallas import tpu as pltpu
from jax.experimental.pallas import tpu_sc as plsc

F32 = jnp.float32
BF16 = jnp.bfloat16

N_HEADS = 8
HEAD_DIM = 64
N_KV = 2
HPG = N_HEADS // N_KV
CMP_STRIDE = 16
CMP_BLOCK = 2 * CMP_STRIDE
CMP_HIDDEN = 128
SLC_BLOCK = 64
N_SEL = 16
WINDOW = 512
REL_BUCKETS = 32
REL_MAX_DIST = 128
S5_WIDTH = 512
S5_GROUP = 16
S5_GROUPS = S5_WIDTH // S5_GROUP
S5_STATE = 64
N_EGROUPS = 8
EXPERTS_PER_GROUP = 8
N_EXPERTS = N_EGROUPS * EXPERTS_PER_GROUP
TOP_K_IN_GROUP = 2
D_EXPERT = 256
EXPERT_BLOCK = 128
DEPTH = 1
DN_ALPHA = (2.0 * DEPTH) ** 0.25
LN_EPS = 1e-5
NEG_INF = -1e30
BIG = 1e9
LOG2_E = math.log2(math.e)
MOE_BLOCK = 256

ATTN_WIDTH = N_HEADS * HEAD_DIM
KV_WIDTH = N_KV * HEAD_DIM
KV_OFF = ATTN_WIDTH
NSA_GATE_OFF = KV_OFF + 6 * KV_WIDTH
S5_OFF = NSA_GATE_OFF + 3 * N_HEADS
MERGE_OFF = S5_OFF + S5_WIDTH

LANES = 128
SUBLANES = 8
VMEM_LIMIT_BYTES = 56 * 1024 * 1024

ATTN_TQ = 128
SLC_TK = 512
SLC_NEAR_BACK = SLC_TK - ATTN_TQ
KV_PAD = WINDOW
SOFTMAX_SHIFT_MARGIN = 100.0
SOFTMAX_BOUND_SLACK = 1.001
GATE_PAD = LANES


def _gelu_tanh(x):
    c = math.sqrt(2.0 / math.pi)
    return x * (0.5 * (1.0 + jnp.tanh(c * (x + 0.044715 * (x * x * x)))))


def _dot(a, b):
    return jnp.dot(a, b, preferred_element_type=F32)


def _dot_nt(a, b):
    return lax.dot_general(a, b, (((1,), (1,)), ((), ())), preferred_element_type=F32)


def _pack_rows(x):
    half = x.shape[1] // 2
    xb = x.astype(BF16).astype(F32)
    hi = lax.bitcast_convert_type(xb[:, :half], jnp.uint32) & jnp.uint32(0xFFFF0000)
    lo = lax.shift_right_logical(lax.bitcast_convert_type(xb[:, half:], jnp.uint32), jnp.uint32(16))
    return hi | lo


def _unpack_rows(w):
    hi = lax.bitcast_convert_type(w & jnp.uint32(0xFFFF0000), F32)
    lo = lax.bitcast_convert_type(lax.shift_left(w, jnp.uint32(16)), F32)
    return jnp.concatenate([hi, lo], axis=1)


def _compiler_params(semantics):
    return pltpu.CompilerParams(dimension_semantics=semantics, vmem_limit_bytes=VMEM_LIMIT_BYTES)


def _in_proj_layout(d_model):
    widths = (ATTN_WIDTH, 2 * KV_WIDTH, 2 * KV_WIDTH, 2 * KV_WIDTH, N_KV * GATE_PAD, S5_WIDTH, 2 * d_model)
    offs = [0]
    for w in widths:
        offs.append(offs[-1] + w)
    return widths, offs


def _pack_in_proj(w_in, b_in, d_model):
    def kv_cols(j):
        return KV_OFF + j * KV_WIDTH

    def pair(jk, jv):
        cols = []
        for g in range(N_KV):
            cols.append(jnp.arange(kv_cols(jk) + g * HEAD_DIM, kv_cols(jk) + (g + 1) * HEAD_DIM))
            cols.append(jnp.arange(kv_cols(jv) + g * HEAD_DIM, kv_cols(jv) + (g + 1) * HEAD_DIM))
        return jnp.concatenate(cols)

    idx = jnp.concatenate([
        jnp.arange(0, ATTN_WIDTH),
        pair(2, 3),
        pair(4, 5),
        jnp.arange(kv_cols(0), kv_cols(2)),
    ])
    idx2 = jnp.concatenate([jnp.arange(S5_OFF, S5_OFF + S5_WIDTH),
                            jnp.arange(MERGE_OFF, MERGE_OFF + 2 * d_model)])
    gpad = GATE_PAD - 3 * HPG
    w_parts, b_parts = [w_in[:, idx]], [b_in[idx]]
    for g in range(N_KV):
        cols = jnp.asarray([NSA_GATE_OFF + (g * HPG + h) * 3 + j for j in range(3) for h in range(HPG)])
        w_parts += [w_in[:, cols], jnp.zeros((d_model, gpad), F32)]
        b_parts += [b_in[cols], jnp.zeros((gpad,), F32)]
    w = jnp.concatenate(w_parts + [w_in[:, idx2]], axis=1)
    b = jnp.concatenate(b_parts + [b_in[idx2]])
    return w.astype(BF16), b.reshape(1, -1).astype(F32)


def _in_proj_kernel(offs, x_ref, w_ref, b_ref, q_ref, slc_ref, win_ref, cmp_ref, g_ref, u_ref, m_ref):
    xb = x_ref[...].astype(BF16)

    def proj(i):
        c0, c1 = offs[i], offs[i + 1]
        return _dot(xb, w_ref[:, c0:c1]) + b_ref[:, c0:c1]

    q_ref[...] = (proj(0) * (HEAD_DIM ** -0.5 * LOG2_E)).astype(BF16)
    slc_ref[...] = proj(1).astype(BF16)
    win_ref[...] = proj(2).astype(BF16)
    cmp_ref[...] = proj(3)
    g_ref[...] = jax.nn.sigmoid(proj(4))
    u_ref[...] = proj(5)
    m_ref[...] = jax.nn.sigmoid(proj(6)).astype(BF16)


def _in_proj(x2d, w_packed, b_packed, d_model, tm):
    n = x2d.shape[0]
    widths, offs = _in_proj_layout(d_model)
    ncols = offs[-1]
    dtypes = (BF16, BF16, BF16, F32, F32, F32, BF16)
    return pl.pallas_call(
        functools.partial(_in_proj_kernel, tuple(offs)),
        grid=(n // tm,),
        in_specs=[
            pl.BlockSpec((tm, d_model), lambda i: (i, 0)),
            pl.BlockSpec((d_model, ncols), lambda i: (0, 0)),
            pl.BlockSpec((1, ncols), lambda i: (0, 0)),
        ],
        out_specs=[pl.BlockSpec((tm, w), lambda i: (i, 0)) for w in widths],
        out_shape=[jax.ShapeDtypeStruct((n, w), dt) for w, dt in zip(widths, dtypes)],
        compiler_params=_compiler_params(("parallel",)),
        name="in_proj",
    )(x2d, w_packed, b_packed)


def _compress_kernel(ck_ref, cv_ref, pos_ref, w1_ref, b1_ref, w2_ref, b2_ref, out_ref):
    n_c = ck_ref.shape[2]
    outs = []
    for i, c_ref in enumerate((ck_ref, cv_ref)):
        c = c_ref[0, 0]
        lo = (c + pos_ref[i, 0:1, :]).astype(BF16)
        hi = (c + pos_ref[i, 1:2, :]).astype(BF16)
        p_lo = _dot(lo, w1_ref[i, 0])
        p_hi = _dot(hi, w1_ref[i, 1])
        hid = p_lo + pltpu.roll(p_hi, n_c - 1, 0) + b1_ref[i]
        hid = _gelu_tanh(hid).astype(BF16)
        outs.append(_dot(hid, w2_ref[i]) + b2_ref[i])
    out_ref[0, 0] = jnp.concatenate(outs[::-1], axis=1).astype(BF16)


def _compress(cmp4, cmp_pos, cmp_w1, cmp_b1, cmp_w2, cmp_b2):
    b, _, n_c, cw = cmp4.shape
    half = CMP_STRIDE * HEAD_DIM
    pos = cmp_pos.reshape(2, 2, half).astype(F32)
    w1 = cmp_w1.reshape(2, 2, half, CMP_HIDDEN).astype(BF16)
    b1 = cmp_b1.reshape(2, 1, CMP_HIDDEN).astype(F32)
    w2 = cmp_w2.astype(BF16)
    b2 = cmp_b2.reshape(2, 1, HEAD_DIM).astype(F32)
    full = lambda shape: pl.BlockSpec(shape, lambda i, g: (0,) * len(shape))
    return pl.pallas_call(
        _compress_kernel,
        grid=(b, N_KV),
        in_specs=[
            pl.BlockSpec((1, 1, n_c, cw), lambda i, g: (i, g, 0, 0)),
            pl.BlockSpec((1, 1, n_c, cw), lambda i, g: (i, N_KV + g, 0, 0)),
            full((2, 2, half)),
            full((2, 2, half, CMP_HIDDEN)),
            full((2, 1, CMP_HIDDEN)),
            full((2, CMP_HIDDEN, HEAD_DIM)),
            full((2, 1, HEAD_DIM)),
        ],
        out_specs=pl.BlockSpec((1, 1, n_c, 2 * HEAD_DIM), lambda i, g: (i, g, 0, 0)),
        out_shape=jax.ShapeDtypeStruct((b, N_KV, n_c, 2 * HEAD_DIM), BF16),
        compiler_params=_compiler_params(("parallel", "parallel")),
        name="compress",
    )(cmp4, cmp4, pos, w1, b1, w2, b2)


def _t5_bucket(dist):
    n = jnp.maximum(dist, 0)
    max_exact = REL_BUCKETS // 2
    nf = jnp.maximum(n, 1).astype(F32)
    large = max_exact + (jnp.log(nf / max_exact) / math.log(REL_MAX_DIST / max_exact)
                         * (REL_BUCKETS - max_exact)).astype(jnp.int32)
    large = jnp.minimum(large, REL_BUCKETS - 1)
    return jnp.where(n < max_exact, n, large)


def _bucket_thresholds():
    buckets = _t5_bucket(jnp.arange(REL_MAX_DIST + 1))
    return jnp.sum(buckets[None, :] < jnp.arange(REL_BUCKETS)[:, None], axis=1).astype(jnp.int32)


def _bias_of_dist(dist, head, thr_ref, tbl_ref):
    bias = jnp.full(dist.shape, tbl_ref[head], F32)
    for k in range(1, REL_BUCKETS):
        bias = jnp.where(dist >= thr_ref[k], tbl_ref[k * N_HEADS + head], bias)
    return bias


BIAS_ROWS = 32


def _bias_c_kernel(thr_ref, tbl_ref, out_ref):
    _, tr, n_c = out_ref.shape
    r0 = pl.program_id(0) * tr

    def chunk(ci, carry):
        row0 = pl.multiple_of(ci * BIAS_ROWS, BIAS_ROWS)
        rows = pl.ds(row0, BIAS_ROWS)
        for c0 in range(0, n_c, LANES):
            width = min(LANES, n_c - c0)
            cols = slice(c0, c0 + width)
            pos = r0 + row0 + lax.broadcasted_iota(jnp.int32, (BIAS_ROWS, width), 0)
            key_end = ((c0 + lax.broadcasted_iota(jnp.int32, (BIAS_ROWS, width), 1)) * CMP_STRIDE
                       + (CMP_BLOCK - 1))
            dist = pos - key_end
            d_min = r0 + row0 - ((c0 + width - 1) * CMP_STRIDE + CMP_BLOCK - 1)
            d_max = r0 + row0 + (BIAS_ROWS - 1) - (c0 * CMP_STRIDE + CMP_BLOCK - 1)

            @pl.when(d_max < 0)
            def _():
                for h in range(N_HEADS):
                    out_ref[h, rows, cols] = jnp.full((BIAS_ROWS, width), NEG_INF, F32)

            @pl.when(d_min >= REL_MAX_DIST)
            def _():
                for h in range(N_HEADS):
                    out_ref[h, rows, cols] = jnp.full((BIAS_ROWS, width),
                                                      tbl_ref[(REL_BUCKETS - 1) * N_HEADS + h], F32)

            @pl.when((d_max >= 0) & (d_min < REL_MAX_DIST))
            def _():
                for h in range(N_HEADS):
                    bias = _bias_of_dist(dist, h, thr_ref, tbl_ref)
                    out_ref[h, rows, cols] = jnp.where(dist >= 0, bias, NEG_INF)
        return carry

    lax.fori_loop(0, tr // BIAS_ROWS, chunk, 0)


def _bias_near_kernel(thr_ref, tbl_ref, near_ref, win_ref):
    tq = ATTN_TQ
    h = pl.program_id(0)
    far_bias = tbl_ref[(REL_BUCKETS - 1) * N_HEADS + h]

    def table(out_ref, lo_keys, window, offset):
        width = out_ref.shape[2]

        def chunk(ci, carry):
            row0 = pl.multiple_of(ci * BIAS_ROWS, BIAS_ROWS)
            dist = (lo_keys + row0 + lax.broadcasted_iota(jnp.int32, (BIAS_ROWS, width), 0)
                    - lax.broadcasted_iota(jnp.int32, (BIAS_ROWS, width), 1))
            visible = (dist >= 0) & (dist < window)
            bias = jnp.full(dist.shape, tbl_ref[h], F32)
            for k in range(1, REL_BUCKETS):
                bias = jnp.where(dist >= thr_ref[k], tbl_ref[k * N_HEADS + h], bias)
            out_ref[0, pl.ds(row0, BIAS_ROWS), :] = jnp.where(visible, bias - offset, NEG_INF)
            return carry

        lax.fori_loop(0, tq // BIAS_ROWS, chunk, 0)

    table(near_ref, SLC_NEAR_BACK, 1 << 30, far_bias)
    table(win_ref, WINDOW, WINDOW, 0.0)


def _attention_bias_tables(rel_bias, seq):
    tbl = (rel_bias.astype(F32) * LOG2_E).reshape(REL_BUCKETS * N_HEADS)
    thr = _bucket_thresholds()
    tq = ATTN_TQ
    n_c = seq // CMP_STRIDE
    smem = pl.BlockSpec(memory_space=pltpu.SMEM)
    tr = min(512, seq)
    bias_c = pl.pallas_call(
        _bias_c_kernel,
        grid=(seq // tr,),
        in_specs=[smem, smem],
        out_specs=pl.BlockSpec((N_HEADS, tr, n_c), lambda i: (0, i, 0)),
        out_shape=jax.ShapeDtypeStruct((N_HEADS, seq, n_c), F32),
        compiler_params=_compiler_params(("parallel",)),
        name="bias_cmp",
    )(thr, tbl)
    head_block = lambda w: pl.BlockSpec((1, tq, w), lambda h: (h, 0, 0))
    widths = (SLC_NEAR_BACK + tq, WINDOW + tq)
    bias_near, bias_win = pl.pallas_call(
        _bias_near_kernel,
        grid=(N_HEADS,),
        in_specs=[smem, smem],
        out_specs=[head_block(w) for w in widths],
        out_shape=[jax.ShapeDtypeStruct((N_HEADS, tq, w), F32) for w in widths],
        compiler_params=_compiler_params(("parallel",)),
        name="bias_near",
    )(thr, tbl)
    return bias_c, bias_near, bias_win


def _nsa_constants(seq):
    n_c = seq // CMP_STRIDE
    n_blk = seq // SLC_BLOCK
    cmp_start = jnp.arange(n_c) * CMP_STRIDE
    blk_start = jnp.arange(n_blk) * SLC_BLOCK
    overlap_t = ((cmp_start[None, :] <= blk_start[:, None] + SLC_BLOCK - 1)
                 & (cmp_start[None, :] + CMP_BLOCK - 1 >= blk_start[:, None]))
    overlap_t = overlap_t & (cmp_start[None, :] + CMP_BLOCK <= seq)
    ones_rows = jnp.arange(SUBLANES)[:, None] == 0
    overlap_t = jnp.concatenate([overlap_t, jnp.broadcast_to(ones_rows, (SUBLANES, n_c))], axis=0)
    cmp_ones = jnp.broadcast_to(jnp.arange(LANES)[None, :] == 0, (n_c, LANES))
    return overlap_t.astype(BF16), cmp_ones.astype(BF16)


def _build_kv_scratch(seq, slc_ref, win_ref, g, ks_sc, vs_sc, kw_sc, vw_sc, knorm_sc):
    chunk = min(512, seq)
    lane_p = lax.broadcasted_iota(jnp.int32, (KV_PAD, LANES), 1)
    zeros = jnp.zeros((KV_PAD, LANES), BF16)
    ks_sc[0:KV_PAD] = jnp.where(lane_p >= HEAD_DIM, 1.0, 0.0).astype(BF16)
    kw_sc[0:KV_PAD] = jnp.where(lane_p == HEAD_DIM, NEG_INF, 0.0).astype(BF16)
    vs_sc[0:KV_PAD] = zeros
    vw_sc[0:KV_PAD] = zeros
    lane = lax.broadcasted_iota(jnp.int32, (chunk, LANES), 1)
    row = lax.broadcasted_iota(jnp.int32, (chunk, LANES), 0)
    lo_half = lane < HEAD_DIM
    ones_lane = jnp.where(lane == HEAD_DIM, 1.0, 0.0)
    k_sq_max = jnp.zeros((chunk, 1), F32)
    for c in range(seq // chunk):
        r0 = c * chunk
        dst = slice(KV_PAD + r0, KV_PAD + r0 + chunk)
        blk = jnp.right_shift(r0 + row, int(math.log2(SLC_BLOCK)))
        lanes_g = slice(g * LANES, (g + 1) * LANES)
        slab = slc_ref[0, r0:r0 + chunk, lanes_g].astype(F32)
        k_sq_max = jnp.maximum(k_sq_max, jnp.sum(jnp.where(lo_half, slab * slab, 0.0), axis=1, keepdims=True))
        ks_sc[dst] = jnp.where(lo_half, slab, jnp.where(lane - HEAD_DIM == blk, 1.0, 0.0)).astype(BF16)
        vs_sc[dst] = jnp.where(lo_half, pltpu.roll(slab, HEAD_DIM, 1), ones_lane).astype(BF16)
        slab = win_ref[0, r0:r0 + chunk, lanes_g].astype(F32)
        kw_sc[dst] = jnp.where(lo_half, slab, 0.0).astype(BF16)
        vw_sc[dst] = jnp.where(lo_half, pltpu.roll(slab, HEAD_DIM, 1), ones_lane).astype(BF16)
    knorm_sc[...] = jnp.broadcast_to(jnp.max(k_sq_max, axis=0, keepdims=True), knorm_sc.shape)


def _nsa_select(seq, g, q0, q_ref, vkc_ref, bias_c_ref, overlap_t_ref, cmp_ones_ref):
    tq = ATTN_TQ
    n_blk = seq // SLC_BLOCK
    n_sel = min(N_SEL, n_blk)
    rows = HPG * tq
    lane = lax.broadcasted_iota(jnp.int32, (tq, LANES), 1)
    lo_half = lane < HEAD_DIM
    lane_r = lax.broadcasted_iota(jnp.int32, (rows, LANES), 1)

    q_lo, q_hi = [], []
    for pair_idx in range(HPG // 2):
        lanes_p = slice((g * (HPG // 2) + pair_idx) * LANES, (g * (HPG // 2) + pair_idx + 1) * LANES)
        q2 = q_ref[0, :, lanes_p].astype(F32)
        q2r = pltpu.roll(q2, HEAD_DIM, 1)
        q_lo += [jnp.where(lo_half, q2, 0.0), jnp.where(lo_half, q2r, 0.0)]
        q_hi += [jnp.where(lo_half, 0.0, q2r), jnp.where(lo_half, 0.0, q2)]
    q_lo = jnp.concatenate(q_lo, axis=0)
    q_hi = jnp.concatenate(q_hi, axis=0)

    vkc = vkc_ref[0, g]
    s_c = _dot_nt(q_hi.astype(BF16), vkc) + bias_c_ref[g * HPG:(g + 1) * HPG].reshape(rows, -1)
    m_c = jnp.max(s_c, axis=1, keepdims=True)
    e_cb = jnp.exp2(s_c - m_c).astype(BF16)
    pv_c = _dot(e_cb, jnp.concatenate([vkc, cmp_ones_ref[...]], axis=1))
    row_pos = q0 + (lax.broadcasted_iota(jnp.int32, (rows, LANES), 0) & (tq - 1))
    has_key = row_pos >= CMP_BLOCK - 1
    o_cmp = jnp.where(has_key, pv_c[:, :LANES] / pv_c[:, LANES:LANES + 1], 0.0)

    imp_t4 = _dot_nt(overlap_t_ref[...], e_cb)
    imp_t = None
    for h in range(HPG):
        part = imp_t4[:, h * tq:(h + 1) * tq]
        part = part[:n_blk] / part[n_blk:n_blk + 1]
        imp_t = part if imp_t is None else imp_t + part

    blk = lax.broadcasted_iota(jnp.int32, (n_blk, tq), 0)
    pos = q0 + lax.broadcasted_iota(jnp.int32, (n_blk, tq), 1)
    cur = jnp.right_shift(pos, int(math.log2(SLC_BLOCK)))
    forced = (blk == 0) | (blk == cur) | (blk == cur - 1)
    valid = blk * SLC_BLOCK <= pos
    score = jnp.where(forced, BIG, jnp.where(valid, imp_t, -BIG))
    blk_f = blk.astype(F32)
    pen_t = jnp.full((n_blk, tq), NEG_INF, F32)
    for _ in range(n_sel):
        top = jnp.max(score, axis=0, keepdims=True)
        first = jnp.min(jnp.where(score == top, blk_f, float(n_blk)), axis=0, keepdims=True)
        hit = blk_f == first
        pen_t = jnp.where(hit, 0.0, pen_t)
        score = jnp.where(hit, -jnp.inf, score)
    pieces = [jnp.zeros((tq, HEAD_DIM), F32), jnp.transpose(pen_t)]
    if n_blk < LANES - HEAD_DIM:
        pieces.append(jnp.full((tq, LANES - HEAD_DIM - n_blk), NEG_INF, F32))
    pen_lanes = jnp.concatenate(pieces, axis=1)
    q_slc = jnp.where(lane_r < HEAD_DIM, q_lo, jnp.concatenate([pen_lanes] * HPG, axis=0)).astype(BF16)
    return o_cmp, q_lo, q_slc


def _nsa_kernel(seq, q_ref, vkc_ref, slc_ref, win_ref, gate_ref, bias_c_ref, bias_near_ref, bias_win_ref,
                overlap_t_ref, cmp_ones_ref, o_ref, *scratch):
    tq = ATTN_TQ
    rows = HPG * tq
    groups = range(N_KV)
    kv_sc = [scratch[4 * g:4 * g + 4] for g in groups]
    s_sc, mrun_sc, acc_sc, knorm_sc = (scratch[4 * N_KV + i * N_KV:4 * N_KV + (i + 1) * N_KV] for i in range(4))
    qt = pl.program_id(1)
    q0 = pl.multiple_of(qt * tq, tq)
    lane = lax.broadcasted_iota(jnp.int32, (tq, LANES), 1)
    lo_half = lane < HEAD_DIM
    lane_r = lax.broadcasted_iota(jnp.int32, (rows, LANES), 1)

    @pl.when(qt == 0)
    def _():
        for g in groups:
            _build_kv_scratch(seq, slc_ref, win_ref, g, *kv_sc[g], knorm_sc[g])

    sel = [_nsa_select(seq, g, q0, q_ref, vkc_ref, bias_c_ref, overlap_t_ref, cmp_ones_ref) for g in groups]
    o_cmp = [s[0] for s in sel]
    q_lo = [s[1] for s in sel]
    q_slc = [s[2] for s in sel]
    head_rows = lambda ref, g: ref[g * HPG:(g + 1) * HPG].reshape(rows, -1)

    o_win = []
    for g in groups:
        kw_sc, vw_sc = kv_sc[g][2], kv_sc[g][3]
        q_win = jnp.where(lane_r == HEAD_DIM, 1.0, q_lo[g]).astype(BF16)
        win_rows = pl.ds(q0, WINDOW + tq)
        s_w = _dot_nt(q_win, kw_sc[win_rows, :]) + head_rows(bias_win_ref, g)
        p_w = jnp.exp2(s_w - jnp.max(s_w, axis=1, keepdims=True)).astype(BF16)
        acc_w = _dot(p_w, vw_sc[win_rows, :])
        o_win.append(acc_w / acc_w[:, HEAD_DIM:HEAD_DIM + 1])

    tk = SLC_TK
    n_far = qt // (tk // tq)
    lane_tiles = lambda a: [a[:, j * LANES:(j + 1) * LANES] for j in range(a.shape[1] // LANES)]
    near0 = pl.multiple_of(q0 + (KV_PAD - SLC_NEAR_BACK), tq)

    def tile_rows(u):
        return pl.ds(pl.multiple_of(near0 - u * tk, tq), tk)

    def tile_cols(u):
        return pl.ds(pl.multiple_of(u * tk, tk), tk)

    def score_tile(u, g, bias):
        s = _dot_nt(q_slc[g], kv_sc[g][0][tile_rows(u), :])
        if bias is not None:
            s = s + bias
        s_sc[g][:, tile_cols(u)] = s
        return functools.reduce(jnp.maximum, lane_tiles(s))

    shift_cap = []
    slack = jnp.float32(-jnp.inf)
    for g in groups:
        lane_max = score_tile(0, g, head_rows(bias_near_ref, g))
        mrun_sc[g][...] = lane_max
        m_near = jnp.max(lane_max, axis=1, keepdims=True)
        q_norm = jnp.sqrt(jnp.sum(q_lo[g] * q_lo[g], axis=1, keepdims=True))
        far_bound = q_norm * jnp.sqrt(knorm_sc[g][0:1, 0:1]) * SOFTMAX_BOUND_SLACK
        shift_cap.append(jnp.maximum(m_near, far_bound - SOFTMAX_SHIFT_MARGIN))
        slack = jnp.maximum(slack, jnp.max(far_bound - m_near))
    single_pass = slack <= 2.0 * SOFTMAX_SHIFT_MARGIN

    def weigh(s, u, g):
        shift = mrun_sc[g][...]
        p = jnp.exp2(s - jnp.concatenate([shift] * (tk // LANES), axis=1))
        acc_sc[g][...] = acc_sc[g][...] + _dot(p.astype(BF16), kv_sc[g][1][tile_rows(u), :])

    for g in groups:
        acc_sc[g][...] = jnp.zeros((rows, LANES), F32)

    @pl.when(single_pass)
    def _():
        for g in groups:
            mrun_sc[g][...] = jnp.broadcast_to(shift_cap[g], (rows, LANES))
            weigh(s_sc[g][:, tile_cols(0)], 0, g)

        def fused(u, carry):
            for g in groups:
                weigh(_dot_nt(q_slc[g], kv_sc[g][0][tile_rows(u), :]), u, g)
            return carry

        lax.fori_loop(1, n_far + 1, fused, 0)

    @pl.when(jnp.logical_not(single_pass))
    def _():
        def pass1(u, carry):
            for g in groups:
                mrun_sc[g][...] = jnp.maximum(mrun_sc[g][...], score_tile(u, g, None))
            return carry

        lax.fori_loop(1, n_far + 1, pass1, 0)
        for g in groups:
            m_s = jnp.max(mrun_sc[g][...], axis=1, keepdims=True)
            mrun_sc[g][...] = jnp.broadcast_to(m_s, (rows, LANES))

        def pass2(u, carry):
            for g in groups:
                weigh(s_sc[g][:, tile_cols(u)], u, g)
            return carry

        lax.fori_loop(0, n_far + 1, pass2, 0)

    for g in groups:
        acc = acc_sc[g][...]
        o_slc = acc / acc[:, HEAD_DIM:HEAD_DIM + 1]

        gates = gate_ref[0, :, g * GATE_PAD:(g + 1) * GATE_PAD]
        outs = []
        for h in range(HPG):
            sl = slice(h * tq, (h + 1) * tq)
            g_c, g_s, g_w = (gates[:, br * HPG + h:br * HPG + h + 1] for br in range(3))
            outs.append(g_c * o_cmp[g][sl] + g_s * o_slc[sl] + g_w * o_win[g][sl])
        for pair_idx in range(HPG // 2):
            even, odd = outs[2 * pair_idx], outs[2 * pair_idx + 1]
            merged = jnp.where(lo_half, even, pltpu.roll(odd, HEAD_DIM, 1))
            lanes_p = slice((g * (HPG // 2) + pair_idx) * LANES, (g * (HPG // 2) + pair_idx + 1) * LANES)
            o_ref[0, :, lanes_p] = merged.astype(o_ref.dtype)


def _nsa(q, vkc, slc, win, gates, bias_c, bias_near, bias_win, overlap_t, cmp_ones):
    b, seq, _ = q.shape
    n_c = seq // CMP_STRIDE
    n_blk = seq // SLC_BLOCK
    assert n_blk <= LANES - HEAD_DIM and seq % SLC_TK == 0 and KV_PAD >= SLC_TK
    tq = ATTN_TQ
    rows = HPG * tq
    const = lambda a: pl.BlockSpec(a.shape, lambda i, t: (0,) * a.ndim)
    tile = lambda w: pl.BlockSpec((1, tq, w), lambda i, t: (i, t, 0))
    whole_seq = lambda w: pl.BlockSpec((1, seq, w), lambda i, t: (i, 0, 0))
    per_group = lambda shape: [pltpu.VMEM(shape, F32)] * N_KV
    return pl.pallas_call(
        functools.partial(_nsa_kernel, seq),
        grid=(b, seq // tq),
        in_specs=[
            tile(ATTN_WIDTH),
            pl.BlockSpec((1, N_KV, n_c, 2 * HEAD_DIM), lambda i, t: (i, 0, 0, 0)),
            whole_seq(N_KV * 2 * HEAD_DIM), whole_seq(N_KV * 2 * HEAD_DIM),
            tile(N_KV * GATE_PAD),
            pl.BlockSpec((N_HEADS, tq, n_c), lambda i, t: (0, t, 0)),
            const(bias_near), const(bias_win),
            const(overlap_t), const(cmp_ones),
        ],
        out_specs=tile(ATTN_WIDTH),
        out_shape=jax.ShapeDtypeStruct((b, seq, ATTN_WIDTH), BF16),
        scratch_shapes=(
            [pltpu.VMEM((KV_PAD + seq, LANES), BF16)] * (4 * N_KV)
            + per_group((rows, seq)) + per_group((rows, LANES)) + per_group((rows, LANES))
            + per_group((SUBLANES, LANES))
        ),
        compiler_params=_compiler_params(("parallel", "arbitrary")),
        name="nsa",
    )(q, vkc, slc, win, gates, bias_c, bias_near, bias_win, overlap_t, cmp_ones)


def _mixer_inputs(x, rel_bias, w_in, b_in, cmp_pos, cmp_w1, cmp_b1, cmp_w2, cmp_b2):
    b, seq, d = x.shape
    wp, bp = _pack_in_proj(w_in, b_in, d)
    q, slc, win, cmp, gates, u, merge = _in_proj(x.reshape(b * seq, d), wp, bp, d, 512)
    n_c = seq // CMP_STRIDE
    cmp4 = cmp.reshape(b, seq, 2 * N_KV, HEAD_DIM).transpose(0, 2, 1, 3).reshape(
        b, 2 * N_KV, n_c, CMP_STRIDE * HEAD_DIM)
    vkc = _compress(cmp4, cmp_pos, cmp_w1, cmp_b1, cmp_w2, cmp_b2)
    bias_c, bias_near, bias_win = _attention_bias_tables(rel_bias, seq)
    overlap_t, cmp_ones = _nsa_constants(seq)
    o = _nsa(q.reshape(b, seq, -1), vkc, slc.reshape(b, seq, -1), win.reshape(b, seq, -1),
             gates.reshape(b, seq, -1), bias_c, bias_near, bias_win, overlap_t, cmp_ones)
    return o, u, merge


S5_HALF_GROUPS = S5_GROUPS // 2
S5_HALF_IN = S5_HALF_GROUPS * S5_GROUP
S5_HALF_STATE = S5_HALF_GROUPS * S5_STATE
S5_SCAN_LANES = 512
S5_CHUNK = 64
S5_UNROLL = 8


def _s5_params(lam_re, lam_im, log_dt, b_re, b_im, c_re, c_im, nb):
    dt = jnp.exp(log_dt.astype(F32))[:, None]
    lr, li = lam_re.astype(F32), lam_im.astype(F32)
    mag = jnp.exp(lr * dt)
    ab_re, ab_im = mag * jnp.cos(li * dt), mag * jnp.sin(li * dt)
    nr, ni = ab_re - 1.0, ab_im
    den = lr * lr + li * li
    fr, fi = (nr * lr + ni * li) / den, (ni * lr - nr * li) / den
    br, bim = b_re.astype(F32), b_im.astype(F32)
    bb_re = fr[..., None] * br - fi[..., None] * bim
    bb_im = fr[..., None] * bim + fi[..., None] * br
    eye = jnp.eye(S5_HALF_GROUPS, dtype=F32)

    def in_mat(bb):
        t = bb.reshape(2, S5_HALF_GROUPS, S5_STATE, S5_GROUP)
        m = jnp.einsum('kgph,gj->kghjp', t, eye)
        return m.reshape(2, S5_HALF_IN, S5_HALF_STATE)

    def out_mat(c):
        t = c.astype(F32).reshape(2, S5_HALF_GROUPS, S5_GROUP, S5_STATE)
        m = jnp.einsum('kghp,gj->kgpjh', t, eye)
        return m.reshape(2, S5_HALF_STATE, S5_HALF_IN)

    bmat = jnp.concatenate([in_mat(bb_re), in_mat(bb_im)], axis=2).astype(BF16)
    cmat = jnp.concatenate([out_mat(c_re), -out_mat(c_im)], axis=1).astype(BF16)
    a = jnp.concatenate([ab_re.reshape(2, S5_HALF_STATE), ab_im.reshape(2, S5_HALF_STATE)], axis=1)
    a = jnp.broadcast_to(a.reshape(1, 4 * S5_HALF_STATE), (nb, 4 * S5_HALF_STATE))
    return bmat, cmat, a


def _s5_kernel(u_ref, bmat_ref, cmat_ref, a_ref, d_ref, y_ref, ut_sc, x_sc, st_sc):
    nb, t_len, _ = u_ref.shape
    half_w = 2 * S5_HALF_STATE

    @pl.when(pl.program_id(0) == 0)
    def _():
        st_sc[...] = jnp.zeros_like(st_sc)

    n_cb = ut_sc.shape[0]
    for b in range(nb):
        for cb in range(n_cb):
            ut_sc[cb, pl.ds(b, t_len, stride=nb), :] = u_ref[b, :, cb * LANES:(cb + 1) * LANES]
    ut = jnp.concatenate([ut_sc[cb] for cb in range(n_cb)], axis=1)
    ub = ut.astype(BF16)
    for k in range(2):
        x_sc[:, k * half_w:(k + 1) * half_w] = _dot(ub[:, k * S5_HALF_IN:(k + 1) * S5_HALF_IN], bmat_ref[k])

    for k in range(2):
        for j in range(S5_HALF_STATE // S5_SCAN_LANES):
            re0 = k * half_w + j * S5_SCAN_LANES
            im0 = re0 + S5_HALF_STATE
            re_sl, im_sl = pl.ds(re0, S5_SCAN_LANES), pl.ds(im0, S5_SCAN_LANES)
            ar, ai = a_ref[:, re_sl], a_ref[:, im_sl]

            def steps(c, carry):
                xr, xi = carry
                for s in range(S5_UNROLL):
                    rows = pl.ds(pl.multiple_of((c * S5_UNROLL + s) * nb, nb), nb)
                    nxr = ar * xr - ai * xi + x_sc[rows, re_sl]
                    nxi = ar * xi + ai * xr + x_sc[rows, im_sl]
                    x_sc[rows, re_sl] = nxr
                    x_sc[rows, im_sl] = nxi
                    xr, xi = nxr, nxi
                return xr, xi

            xr, xi = lax.fori_loop(0, t_len // S5_UNROLL, steps, (st_sc[:, re_sl], st_sc[:, im_sl]))
            st_sc[:, re_sl] = xr
            st_sc[:, im_sl] = xi

    xs = x_sc[...].astype(BF16)
    y = jnp.concatenate([_dot(xs[:, k * half_w:(k + 1) * half_w], cmat_ref[k]) for k in range(2)], axis=1)
    y = _gelu_tanh(y + d_ref[...] * ut)
    for cb in range(n_cb):
        ut_sc[cb] = y[:, cb * LANES:(cb + 1) * LANES]
    for b in range(nb):
        for cb in range(n_cb):
            y_ref[b, :, cb * LANES:(cb + 1) * LANES] = ut_sc[cb, pl.ds(b, t_len, stride=nb), :].astype(BF16)


def _s5(u, bmat, cmat, a, d_skip):
    nb, seq, w = u.shape
    t_len = min(S5_CHUNK, seq)
    full = lambda shape: pl.BlockSpec(shape, lambda c: (0,) * len(shape))
    return pl.pallas_call(
        _s5_kernel,
        grid=(seq // t_len,),
        in_specs=[
            pl.BlockSpec((nb, t_len, w), lambda c: (0, c, 0)),
            full(bmat.shape), full(cmat.shape), full(a.shape), full((1, w)),
        ],
        out_specs=pl.BlockSpec((nb, t_len, w), lambda c: (0, c, 0)),
        out_shape=jax.ShapeDtypeStruct((nb, seq, w), BF16),
        scratch_shapes=[
            pltpu.VMEM((w // LANES, t_len * nb, LANES), F32),
            pltpu.VMEM((t_len * nb, 4 * S5_HALF_STATE), F32),
            pltpu.VMEM((nb, 4 * S5_HALF_STATE), F32),
        ],
        compiler_params=_compiler_params(("arbitrary",)),
        name="s5",
    )(u, bmat, cmat, a, d_skip.reshape(1, w).astype(F32))


ROUTE_PAD = LANES
_R_E1, _R_E2, _R_W1, _R_W2, _R_RANK1, _R_RANK2 = range(6)


def _layer_norm(t, g, b):
    mu = jnp.mean(t, axis=1, keepdims=True)
    c = t - mu
    var = jnp.mean(c * c, axis=1, keepdims=True)
    return c * lax.rsqrt(var + LN_EPS) * g + b


def _post_kernel(x_ref, o_ref, y_ref, m_ref, wup_ref, wval_ref, wgate_ref, bgate_ref, wout_ref,
                 g1_ref, b1_ref, wr_ref, br_ref, h_ref, hp_ref, route_ref, cnt_ref, run_sc):
    tm, d = x_ref.shape

    @pl.when(pl.program_id(0) == 0)
    def _():
        run_sc[...] = jnp.zeros_like(run_sc)

    y_a = _dot(o_ref[...], wup_ref[...])
    z = y_ref[...]
    y_b = _dot(z, wval_ref[...]) * jax.nn.sigmoid(_dot(z, wgate_ref[...]) + bgate_ref[...])
    mixed = m_ref[:, :d].astype(F32) * y_a + m_ref[:, d:].astype(F32) * y_b
    t = DN_ALPHA * x_ref[...] + _dot(mixed.astype(BF16), wout_ref[...])
    h = _layer_norm(t, g1_ref[...], b1_ref[...])
    h_ref[...] = h
    hp_ref[...] = _pack_rows(h)

    logits = _dot(h.astype(BF16), wr_ref[...]) + br_ref[...]
    lane = lax.broadcasted_iota(jnp.int32, (tm, ROUTE_PAD), 1)
    lane_f = lane.astype(F32)
    is_group = lane < N_EGROUPS

    def first_max(v):
        top = jnp.max(v, axis=1, keepdims=True)
        idx = jnp.min(jnp.where(v == top, lane_f, float(ROUTE_PAD)), axis=1, keepdims=True)
        return top, idx

    g_max, g_top = first_max(jnp.where(is_group, logits, -jnp.inf))
    p_group = 1.0 / jnp.sum(jnp.where(is_group, jnp.exp(logits - g_max), 0.0), axis=1, keepdims=True)
    grp_of_lane = jnp.right_shift(lane - N_EGROUPS, int(math.log2(EXPERTS_PER_GROUP))).astype(F32)
    in_group = (lane >= N_EGROUPS) & (lane < N_EGROUPS + N_EXPERTS) & (grp_of_lane == g_top)
    e_log = jnp.where(in_group, logits, -jnp.inf)
    v1, i1 = first_max(e_log)
    hit1 = lane_f == i1
    v2, i2 = first_max(jnp.where(hit1, -jnp.inf, e_log))
    hit2 = lane_f == i2
    e2 = jnp.exp(v2 - v1)
    w1 = p_group / (1.0 + e2)
    w2 = p_group * e2 / (1.0 + e2)

    hits = jnp.where(hit1 | hit2, 1.0, 0.0)
    row = lax.broadcasted_iota(jnp.int32, (tm, tm), 0)
    col = lax.broadcasted_iota(jnp.int32, (tm, tm), 1)
    earlier = jnp.where(col < row, 1.0, 0.0).astype(BF16)
    before = _dot(earlier, hits.astype(BF16)) + run_sc[...]
    rank1 = jnp.sum(jnp.where(hit1, before, 0.0), axis=1, keepdims=True)
    rank2 = jnp.sum(jnp.where(hit2, before, 0.0), axis=1, keepdims=True)
    run_sc[...] = run_sc[...] + jnp.sum(hits, axis=0, keepdims=True)
    cnt_ref[...] = run_sc[...]

    rec = jnp.zeros((tm, ROUTE_PAD), F32)
    for slot, val in ((_R_E1, i1 - N_EGROUPS), (_R_E2, i2 - N_EGROUPS), (_R_W1, w1), (_R_W2, w2),
                      (_R_RANK1, rank1), (_R_RANK2, rank2)):
        rec = jnp.where(lane == slot, val, rec)
    route_ref[...] = rec


def _post(x2d, o2d, y2d, merge, w_attn_up, s5_w_val, s5_w_gate, s5_b_gate, w_out, ln1_g, ln1_b,
          router_w_group, router_b_group, router_w_expert, router_b_expert, tm):
    n, d = x2d.shape
    rpad = ROUTE_PAD - N_EGROUPS - N_EXPERTS
    wr = jnp.concatenate([router_w_group, router_w_expert, jnp.zeros((d, rpad), F32)], axis=1).astype(BF16)
    br = jnp.concatenate([router_b_group, router_b_expert, jnp.zeros((rpad,), F32)]).reshape(1, -1).astype(F32)
    row = lambda w: pl.BlockSpec((tm, w), lambda i: (i, 0))
    full = lambda a: pl.BlockSpec(a.shape, lambda i: (0,) * a.ndim)
    weights = [w_attn_up.astype(BF16), s5_w_val.astype(BF16), s5_w_gate.astype(BF16),
               s5_b_gate.reshape(1, d).astype(F32), w_out.astype(BF16),
               ln1_g.reshape(1, d).astype(F32), ln1_b.reshape(1, d).astype(F32), wr, br]
    return pl.pallas_call(
        _post_kernel,
        grid=(n // tm,),
        in_specs=[row(d), row(ATTN_WIDTH), row(S5_WIDTH), row(2 * d)] + [full(w) for w in weights],
        out_specs=[row(d), row(d // 2), row(ROUTE_PAD), pl.BlockSpec((1, ROUTE_PAD), lambda i: (0, 0))],
        out_shape=[jax.ShapeDtypeStruct((n, d), F32), jax.ShapeDtypeStruct((n, d // 2), jnp.uint32),
                   jax.ShapeDtypeStruct((n, ROUTE_PAD), F32), jax.ShapeDtypeStruct((1, ROUTE_PAD), F32)],
        scratch_shapes=[pltpu.VMEM((1, ROUTE_PAD), F32)],
        compiler_params=_compiler_params(("arbitrary",)),
        name="post_mixer",
    )(x2d, o2d, y2d, merge, *weights)


def _plan_kernel(route_ref, cnt_ref, dest_ref):
    tm = route_ref.shape[0]
    lane8 = lax.broadcasted_iota(jnp.int32, (SUBLANES, ROUTE_PAD), 1)
    counts = jnp.broadcast_to(cnt_ref[...], (SUBLANES, ROUTE_PAD)).astype(jnp.int32)
    shift = int(math.log2(MOE_BLOCK))
    padded = jnp.left_shift(jnp.right_shift(counts + (MOE_BLOCK - 1), shift), shift)
    incl = padded
    step = 1
    while step < ROUTE_PAD:
        incl = incl + jnp.where(lane8 >= step, pltpu.roll(incl, step, 1), 0)
        step *= 2
    pstart = (incl - padded)[0:1].astype(F32)
    route = route_ref[...]
    expert_of_lane = (lax.broadcasted_iota(jnp.int32, (tm, ROUTE_PAD), 1) - N_EGROUPS).astype(F32)
    lane = lax.broadcasted_iota(jnp.int32, (tm, ROUTE_PAD), 1)

    def dest(e_slot, rank_slot):
        hit = expert_of_lane == route[:, e_slot:e_slot + 1]
        return jnp.sum(jnp.where(hit, pstart, 0.0), axis=1, keepdims=True) + route[:, rank_slot:rank_slot + 1]

    d1 = dest(_R_E1, _R_RANK1)
    d2 = dest(_R_E2, _R_RANK2)
    dest_ref[...] = jnp.where(lane == 0, d1, jnp.where(lane == 1, d2, 0.0)).astype(jnp.int32)


def _plan(route, counts_row):
    n = route.shape[0]
    tm = min(1024, n)
    return pl.pallas_call(
        _plan_kernel,
        grid=(n // tm,),
        in_specs=[pl.BlockSpec((tm, ROUTE_PAD), lambda i: (i, 0)),
                  pl.BlockSpec((1, ROUTE_PAD), lambda i: (0, 0))],
        out_specs=pl.BlockSpec((tm, ROUTE_PAD), lambda i: (i, 0)),
        out_shape=jax.ShapeDtypeStruct((n, ROUTE_PAD), jnp.int32),
        compiler_params=_compiler_params(("parallel",)),
        name="moe_plan",
    )(route, counts_row)


SC_GATHER_ROWS = 32


def _sc_row_gather(table, idx):
    n_idx = idx.shape[0]
    d = table.shape[1]
    info = plsc.get_sparse_core_info()
    n_workers = info.num_cores * info.num_subcores
    per_worker = n_idx // n_workers
    assert n_idx % (n_workers * SC_GATHER_ROWS) == 0
    mesh = plsc.VectorSubcoreMesh(core_axis_name="c", subcore_axis_name="s")

    @functools.partial(
        pl.kernel, mesh=mesh,
        out_type=jax.ShapeDtypeStruct((n_idx, d), table.dtype),
        scratch_types=[
            pltpu.VMEM((SC_GATHER_ROWS,), jnp.int32),
            pltpu.VMEM((SC_GATHER_ROWS, d), table.dtype),
            pltpu.SemaphoreType.DMA,
        ],
    )
    def gather(table_hbm, idx_hbm, out_hbm, idx_v, rows_v, sem):
        worker = lax.axis_index("s") * info.num_cores + lax.axis_index("c")
        base = worker * per_worker

        @pl.loop(0, per_worker // SC_GATHER_ROWS)
        def _(j):
            off = base + j * SC_GATHER_ROWS
            pltpu.sync_copy(idx_hbm.at[pl.ds(off, SC_GATHER_ROWS)], idx_v)
            pltpu.async_copy(table_hbm.at[idx_v], rows_v, sem).wait()
            pltpu.sync_copy(rows_v, out_hbm.at[pl.ds(off, SC_GATHER_ROWS)])

    return gather(table, idx)


def _sc_row_scatter(rows, idx_a, idx_b, n_out):
    n, d = rows.shape
    info = plsc.get_sparse_core_info()
    n_workers = info.num_cores * info.num_subcores
    per_worker = n // n_workers
    assert n % (n_workers * SC_GATHER_ROWS) == 0
    mesh = plsc.VectorSubcoreMesh(core_axis_name="c", subcore_axis_name="s")

    @functools.partial(
        pl.kernel, mesh=mesh,
        out_type=jax.ShapeDtypeStruct((n_out, d), rows.dtype),
        scratch_types=[
            pltpu.VMEM((SC_GATHER_ROWS,), jnp.int32),
            pltpu.VMEM((SC_GATHER_ROWS,), jnp.int32),
            pltpu.VMEM((SC_GATHER_ROWS, d), rows.dtype),
        ],
    )
    def scatter(rows_hbm, idx_a_hbm, idx_b_hbm, out_hbm, idx_a_v, idx_b_v, rows_v):
        worker = lax.axis_index("s") * info.num_cores + lax.axis_index("c")
        base = worker * per_worker

        @pl.loop(0, per_worker // SC_GATHER_ROWS)
        def _(j):
            src = pl.ds(base + j * SC_GATHER_ROWS, SC_GATHER_ROWS)
            pltpu.sync_copy(rows_hbm.at[src], rows_v)
            pltpu.sync_copy(idx_a_hbm.at[src], idx_a_v)
            pltpu.sync_copy(idx_b_hbm.at[src], idx_b_v)
            pltpu.sync_copy(rows_v, out_hbm.at[idx_a_v])
            pltpu.sync_copy(rows_v, out_hbm.at[idx_b_v])

    return scatter(rows, idx_a, idx_b)


def _expert_kernel(blk_exp_ref, blk_valid_ref, x_ref, wg_ref, wu_ref, wd_ref, y_ref, wg_sc, wu_sc, wd_sc):
    i = pl.program_id(0)
    n_valid = blk_valid_ref[i]

    @pl.when((i == 0) | (blk_exp_ref[i] != blk_exp_ref[jnp.maximum(i - 1, 0)]))
    def _():
        wg_sc[...] = wg_ref[0].astype(BF16)
        wu_sc[...] = wu_ref[0].astype(BF16)
        wd_sc[...] = wd_ref[0].astype(BF16)

    @pl.when(n_valid > 0)
    def _():
        row = lax.broadcasted_iota(jnp.int32, x_ref.shape, 0)
        words = jnp.where(row < n_valid, x_ref[...], jnp.uint32(0))
        xb = _unpack_rows(words).astype(BF16)
        h_gate = _dot(xb, wg_sc[...])
        h_up = _dot(xb, wu_sc[...])
        hb = (h_gate * jax.nn.sigmoid(h_gate) * h_up).astype(BF16)
        y_ref[...] = _pack_rows(_dot(hb, wd_sc[...]))

    @pl.when(n_valid == 0)
    def _():
        y_ref[...] = jnp.zeros_like(y_ref)


def _experts(blk_expert, blk_valid, xs, w_gate, w_up, w_down):
    n_blocks = blk_expert.shape[0]
    d = 2 * xs.shape[1]
    grid_spec = pltpu.PrefetchScalarGridSpec(
        num_scalar_prefetch=2,
        grid=(n_blocks,),
        in_specs=[
            pl.BlockSpec((MOE_BLOCK, d // 2), lambda i, be, bv: (i, 0)),
            pl.BlockSpec((1, d, D_EXPERT), lambda i, be, bv: (be[i], 0, 0)),
            pl.BlockSpec((1, d, D_EXPERT), lambda i, be, bv: (be[i], 0, 0)),
            pl.BlockSpec((1, D_EXPERT, d), lambda i, be, bv: (be[i], 0, 0)),
        ],
        out_specs=pl.BlockSpec((MOE_BLOCK, d // 2), lambda i, be, bv: (i, 0)),
        scratch_shapes=[
            pltpu.VMEM((d, D_EXPERT), BF16),
            pltpu.VMEM((d, D_EXPERT), BF16),
            pltpu.VMEM((D_EXPERT, d), BF16),
        ],
    )
    return pl.pallas_call(
        _expert_kernel,
        grid_spec=grid_spec,
        out_shape=jax.ShapeDtypeStruct((n_blocks * MOE_BLOCK, d // 2), jnp.uint32),
        compiler_params=_compiler_params(("arbitrary",)),
        name="experts",
    )(blk_expert, blk_valid, xs, w_gate, w_up, w_down)


COMBINE_TM = 512


def _combine_kernel(h_ref, y1_ref, y2_ref, route_ref, g2_ref, b2_ref, out_ref):
    route = route_ref[...]
    w1 = route[:, _R_W1:_R_W1 + 1]
    w2 = route[:, _R_W2:_R_W2 + 1]
    t = DN_ALPHA * h_ref[...] + (_unpack_rows(y1_ref[...]) * w1 + _unpack_rows(y2_ref[...]) * w2)
    out_ref[...] = _layer_norm(t, g2_ref[...], b2_ref[...])


def _combine(yg, h2d, route, ln2_g, ln2_b):
    n, d = h2d.shape
    tm = min(COMBINE_TM, n)
    n_tiles = n // tm
    row = lambda w: pl.BlockSpec((tm, w), lambda i: (i, 0))
    vec = pl.BlockSpec((1, d), lambda i: (0, 0))
    return pl.pallas_call(
        _combine_kernel,
        grid=(n_tiles,),
        in_specs=[row(d), row(d // 2), pl.BlockSpec((tm, d // 2), lambda i: (i + n_tiles, 0)), row(ROUTE_PAD),
                  vec, vec],
        out_specs=row(d),
        out_shape=jax.ShapeDtypeStruct((n, d), F32),
        compiler_params=_compiler_params(("parallel",)),
        name="combine",
    )(h2d, yg, yg, route, ln2_g.reshape(1, d).astype(F32), ln2_b.reshape(1, d).astype(F32))


def _moe(h2d, h_packed, route, counts_row, w_gate, w_up, w_down, ln2_g, ln2_b):
    n, d = h2d.shape
    dest = _plan(route, counts_row)
    dest1, dest2 = dest[:, 0], dest[:, 1]
    counts = counts_row[0, N_EGROUPS:N_EGROUPS + N_EXPERTS].astype(jnp.int32)
    padded = (counts + MOE_BLOCK - 1) // MOE_BLOCK * MOE_BLOCK
    pend = jnp.cumsum(padded)
    n_blocks = -(-(n * TOP_K_IN_GROUP) // MOE_BLOCK) + N_EXPERTS
    blk_row0 = (jnp.arange(n_blocks, dtype=jnp.int32) * MOE_BLOCK)[:, None]
    owns = (pend - padded <= blk_row0) & (blk_row0 < pend)
    blk_expert = jnp.sum(jnp.where(owns, jnp.arange(N_EXPERTS, dtype=jnp.int32), 0), axis=1)
    blk_valid = jnp.sum(jnp.where(owns, jnp.clip(pend - padded + counts - blk_row0, 0, MOE_BLOCK), 0), axis=1)
    xs = _sc_row_scatter(h_packed, dest1, dest2, n_blocks * MOE_BLOCK)
    yb = _experts(blk_expert, blk_valid, xs, w_gate, w_up, w_down)
    yg = _sc_row_gather(yb, jnp.concatenate([dest1, dest2]))
    return _combine(yg, h2d, route, ln2_g, ln2_b)


def kernel(x, rel_bias, w_in, b_in, cmp_pos, cmp_w1, cmp_b1, cmp_w2, cmp_b2, w_attn_up, s5_lambda_re, s5_lambda_im, s5_log_dt, s5_b_re, s5_b_im, s5_c_re, s5_c_im, s5_d, s5_w_val, s5_w_gate, s5_b_gate, w_out, ln1_g, ln1_b, router_w_group, router_b_group, router_w_expert, router_b_expert, exp_w_gate, exp_w_up, exp_w_down, ln2_g, ln2_b):
    b, seq, d = x.shape
    n = b * seq
    assert w_in.shape[0] == DEPTH
    l = 0
    o, u, merge = _mixer_inputs(x, rel_bias, w_in[l], b_in[l], cmp_pos[l], cmp_w1[l], cmp_b1[l],
                                cmp_w2[l], cmp_b2[l])
    bmat, cmat, a = _s5_params(s5_lambda_re[l], s5_lambda_im[l], s5_log_dt[l], s5_b_re[l], s5_b_im[l],
                               s5_c_re[l], s5_c_im[l], b)
    y_s = _s5(u.reshape(b, seq, S5_WIDTH), bmat, cmat, a, s5_d[l])
    h2d, h_packed, route, counts = _post(
        x.reshape(n, d), o.reshape(n, ATTN_WIDTH), y_s.reshape(n, S5_WIDTH), merge, w_attn_up[l], s5_w_val[l],
        s5_w_gate[l], s5_b_gate[l], w_out[l], ln1_g[l], ln1_b[l], router_w_group[l], router_b_group[l],
        router_w_expert[l], router_b_expert[l], 512)
    out = _moe(h2d, h_packed, route, counts, exp_w_gate[l], exp_w_up[l], exp_w_down[l], ln2_g[l], ln2_b[l])
    return out.reshape(b, seq, d)
```

```python
import functools
import math

import jax
import jax.numpy as jnp
from jax import lax
from jax.experimental import pallas as pl
from jax.experimental.pallas import tpu as pltpu
from jax.experimental.pallas import tpu_sc as plsc

F32 = jnp.float32
BF16 = jnp.bfloat16

N_HEADS = 8
HEAD_DIM = 64
N_KV = 2
HPG = N_HEADS // N_KV
CMP_STRIDE = 16
CMP_BLOCK = 2 * CMP_STRIDE
CMP_HIDDEN = 128
SLC_BLOCK = 64
N_SEL = 16
WINDOW = 512
REL_BUCKETS = 32
REL_MAX_DIST = 128
S5_WIDTH = 512
S5_GROUP = 16
S5_GROUPS = S5_WIDTH // S5_GROUP
S5_STATE = 64
N_EGROUPS = 8
EXPERTS_PER_GROUP = 8
N_EXPERTS = N_EGROUPS * EXPERTS_PER_GROUP
TOP_K_IN_GROUP = 2
D_EXPERT = 256
EXPERT_BLOCK = 128
DEPTH = 1
DN_ALPHA = (2.0 * DEPTH) ** 0.25
LN_EPS = 1e-5
NEG_INF = -1e30
BIG = 1e9
LOG2_E = math.log2(math.e)
MOE_BLOCK = 256

ATTN_WIDTH = N_HEADS * HEAD_DIM
KV_WIDTH = N_KV * HEAD_DIM
KV_OFF = ATTN_WIDTH
NSA_GATE_OFF = KV_OFF + 6 * KV_WIDTH
S5_OFF = NSA_GATE_OFF + 3 * N_HEADS
MERGE_OFF = S5_OFF + S5_WIDTH

LANES = 128
SUBLANES = 8
VMEM_LIMIT_BYTES = 56 * 1024 * 1024

ATTN_TQ = 128
SLC_TK = 512
SLC_NEAR_BACK = SLC_TK - ATTN_TQ
KV_PAD = WINDOW
SOFTMAX_SHIFT_MARGIN = 100.0
SOFTMAX_BOUND_SLACK = 1.001
GATE_PAD = LANES


def _gelu_tanh(x):
    c = math.sqrt(2.0 / math.pi)
    return x * (0.5 * (1.0 + jnp.tanh(c * (x + 0.044715 * (x * x * x)))))


def _dot(a, b):
    return jnp.dot(a, b, preferred_element_type=F32)


def _dot_nt(a, b):
    return lax.dot_general(a, b, (((1,), (1,)), ((), ())), preferred_element_type=F32)


def _pack_rows(x):
    half = x.shape[1] // 2
    xb = x.astype(BF16).astype(F32)
    hi = lax.bitcast_convert_type(xb[:, :half], jnp.uint32) & jnp.uint32(0xFFFF0000)
    lo = lax.shift_right_logical(lax.bitcast_convert_type(xb[:, half:], jnp.uint32), jnp.uint32(16))
    return hi | lo


def _unpack_rows(w):
    hi = lax.bitcast_convert_type(w & jnp.uint32(0xFFFF0000), F32)
    lo = lax.bitcast_convert_type(lax.shift_left(w, jnp.uint32(16)), F32)
    return jnp.concatenate([hi, lo], axis=1)


def _compiler_params(semantics):
    return pltpu.CompilerParams(dimension_semantics=semantics, vmem_limit_bytes=VMEM_LIMIT_BYTES)


def _in_proj_layout(d_model):
    widths = (ATTN_WIDTH, 2 * KV_WIDTH, 2 * KV_WIDTH, 2 * KV_WIDTH, N_KV * GATE_PAD, S5_WIDTH, 2 * d_model)
    offs = [0]
    for w in widths:
        offs.append(offs[-1] + w)
    return widths, offs


def _pack_in_proj(w_in, b_in, d_model):
    def kv_cols(j):
        return KV_OFF + j * KV_WIDTH

    def pair(jk, jv):
        cols = []
        for g in range(N_KV):
            cols.append(jnp.arange(kv_cols(jk) + g * HEAD_DIM, kv_cols(jk) + (g + 1) * HEAD_DIM))
            cols.append(jnp.arange(kv_cols(jv) + g * HEAD_DIM, kv_cols(jv) + (g + 1) * HEAD_DIM))
        return jnp.concatenate(cols)

    idx = jnp.concatenate([
        jnp.arange(0, ATTN_WIDTH),
        pair(2, 3),
        pair(4, 5),
        jnp.arange(kv_cols(0), kv_cols(2)),
    ])
    idx2 = jnp.concatenate([jnp.arange(S5_OFF, S5_OFF + S5_WIDTH),
                            jnp.arange(MERGE_OFF, MERGE_OFF + 2 * d_model)])
    gpad = GATE_PAD - 3 * HPG
    w_parts, b_parts = [w_in[:, idx]], [b_in[idx]]
    for g in range(N_KV):
        cols = jnp.asarray([NSA_GATE_OFF + (g * HPG + h) * 3 + j for j in range(3) for h in range(HPG)])
        w_parts += [w_in[:, cols], jnp.zeros((d_model, gpad), F32)]
        b_parts += [b_in[cols], jnp.zeros((gpad,), F32)]
    w = jnp.concatenate(w_parts + [w_in[:, idx2]], axis=1)
    b = jnp.concatenate(b_parts + [b_in[idx2]])
    return w.astype(BF16), b.reshape(1, -1).astype(F32)


def _in_proj_kernel(offs, x_ref, w_ref, b_ref, q_ref, slc_ref, win_ref, cmp_ref, g_ref, u_ref, m_ref):
    xb = x_ref[...].astype(BF16)

    def proj(i):
        c0, c1 = offs[i], offs[i + 1]
        return _dot(xb, w_ref[:, c0:c1]) + b_ref[:, c0:c1]

    q_ref[...] = (proj(0) * (HEAD_DIM ** -0.5 * LOG2_E)).astype(BF16)
    slc_ref[...] = proj(1).astype(BF16)
    win_ref[...] = proj(2).astype(BF16)
    cmp_ref[...] = proj(3)
    g_ref[...] = jax.nn.sigmoid(proj(4))
    u_ref[...] = proj(5)
    m_ref[...] = jax.nn.sigmoid(proj(6)).astype(BF16)


def _in_proj(x2d, w_packed, b_packed, d_model, tm):
    n = x2d.shape[0]
    widths, offs = _in_proj_layout(d_model)
    ncols = offs[-1]
    dtypes = (BF16, BF16, BF16, F32, F32, F32, BF16)
    return pl.pallas_call(
        functools.partial(_in_proj_kernel, tuple(offs)),
        grid=(n // tm,),
        in_specs=[
            pl.BlockSpec((tm, d_model), lambda i: (i, 0)),
            pl.BlockSpec((d_model, ncols), lambda i: (0, 0), pipeline_mode=pl.Buffered(1)),
            pl.BlockSpec((1, ncols), lambda i: (0, 0)),
        ],
        out_specs=[pl.BlockSpec((tm, w), lambda i: (i, 0)) for w in widths],
        out_shape=[jax.ShapeDtypeStruct((n, w), dt) for w, dt in zip(widths, dtypes)],
        compiler_params=_compiler_params(("parallel",)),
        name="in_proj",
    )(x2d, w_packed, b_packed)


def _compress_kernel(ck_ref, cv_ref, pos_ref, w1_ref, b1_ref, w2_ref, b2_ref, out_ref):
    n_c = ck_ref.shape[2]
    outs = []
    for i, c_ref in enumerate((ck_ref, cv_ref)):
        c = c_ref[0, 0]
        lo = (c + pos_ref[i, 0:1, :]).astype(BF16)
        hi = (c + pos_ref[i, 1:2, :]).astype(BF16)
        p_lo = _dot(lo, w1_ref[i, 0])
        p_hi = _dot(hi, w1_ref[i, 1])
        hid = p_lo + pltpu.roll(p_hi, n_c - 1, 0) + b1_ref[i]
        hid = _gelu_tanh(hid).astype(BF16)
        outs.append(_dot(hid, w2_ref[i]) + b2_ref[i])
    out_ref[0, 0] = jnp.concatenate(outs[::-1], axis=1).astype(BF16)


def _compress(cmp4, cmp_pos, cmp_w1, cmp_b1, cmp_w2, cmp_b2):
    b, _, n_c, cw = cmp4.shape
    half = CMP_STRIDE * HEAD_DIM
    pos = cmp_pos.reshape(2, 2, half).astype(F32)
    w1 = cmp_w1.reshape(2, 2, half, CMP_HIDDEN).astype(BF16)
    b1 = cmp_b1.reshape(2, 1, CMP_HIDDEN).astype(F32)
    w2 = cmp_w2.astype(BF16)
    b2 = cmp_b2.reshape(2, 1, HEAD_DIM).astype(F32)
    full = lambda shape: pl.BlockSpec(shape, lambda i, g: (0,) * len(shape))
    return pl.pallas_call(
        _compress_kernel,
        grid=(b, N_KV),
        in_specs=[
            pl.BlockSpec((1, 1, n_c, cw), lambda i, g: (i, g, 0, 0)),
            pl.BlockSpec((1, 1, n_c, cw), lambda i, g: (i, N_KV + g, 0, 0)),
            full((2, 2, half)),
            full((2, 2, half, CMP_HIDDEN)),
            full((2, 1, CMP_HIDDEN)),
            full((2, CMP_HIDDEN, HEAD_DIM)),
            full((2, 1, HEAD_DIM)),
        ],
        out_specs=pl.BlockSpec((1, 1, n_c, 2 * HEAD_DIM), lambda i, g: (i, g, 0, 0)),
        out_shape=jax.ShapeDtypeStruct((b, N_KV, n_c, 2 * HEAD_DIM), BF16),
        compiler_params=_compiler_params(("parallel", "parallel")),
        name="compress",
    )(cmp4, cmp4, pos, w1, b1, w2, b2)


def _t5_bucket(dist):
    n = jnp.maximum(dist, 0)
    max_exact = REL_BUCKETS // 2
    nf = jnp.maximum(n, 1).astype(F32)
    large = max_exact + (jnp.log(nf / max_exact) / math.log(REL_MAX_DIST / max_exact)
                         * (REL_BUCKETS - max_exact)).astype(jnp.int32)
    large = jnp.minimum(large, REL_BUCKETS - 1)
    return jnp.where(n < max_exact, n, large)


def _bucket_thresholds():
    buckets = _t5_bucket(jnp.arange(REL_MAX_DIST + 1))
    return jnp.sum(buckets[None, :] < jnp.arange(REL_BUCKETS)[:, None], axis=1).astype(jnp.int32)


def _bias_of_dist(dist, head, thr_ref, tbl_ref):
    bias = jnp.full(dist.shape, tbl_ref[head], F32)
    for k in range(1, REL_BUCKETS):
        bias = jnp.where(dist >= thr_ref[k], tbl_ref[k * N_HEADS + head], bias)
    return bias


BIAS_ROWS = 32


def _bias_c_kernel(thr_ref, tbl_ref, out_ref):
    _, tr, n_c = out_ref.shape
    r0 = pl.program_id(0) * tr

    def chunk(ci, carry):
        row0 = pl.multiple_of(ci * BIAS_ROWS, BIAS_ROWS)
        rows = pl.ds(row0, BIAS_ROWS)
        for c0 in range(0, n_c, LANES):
            width = min(LANES, n_c - c0)
            cols = slice(c0, c0 + width)
            pos = r0 + row0 + lax.broadcasted_iota(jnp.int32, (BIAS_ROWS, width), 0)
            key_end = ((c0 + lax.broadcasted_iota(jnp.int32, (BIAS_ROWS, width), 1)) * CMP_STRIDE
                       + (CMP_BLOCK - 1))
            dist = pos - key_end
            d_min = r0 + row0 - ((c0 + width - 1) * CMP_STRIDE + CMP_BLOCK - 1)
            d_max = r0 + row0 + (BIAS_ROWS - 1) - (c0 * CMP_STRIDE + CMP_BLOCK - 1)

            @pl.when(d_max < 0)
            def _():
                for h in range(N_HEADS):
                    out_ref[h, rows, cols] = jnp.full((BIAS_ROWS, width), NEG_INF, F32)

            @pl.when(d_min >= REL_MAX_DIST)
            def _():
                for h in range(N_HEADS):
                    out_ref[h, rows, cols] = jnp.full((BIAS_ROWS, width),
                                                      tbl_ref[(REL_BUCKETS - 1) * N_HEADS + h], F32)

            @pl.when((d_max >= 0) & (d_min < REL_MAX_DIST))
            def _():
                for h in range(N_HEADS):
                    bias = _bias_of_dist(dist, h, thr_ref, tbl_ref)
                    out_ref[h, rows, cols] = jnp.where(dist >= 0, bias, NEG_INF)
        return carry

    lax.fori_loop(0, tr // BIAS_ROWS, chunk, 0)


def _bias_near_kernel(thr_ref, tbl_ref, near_ref, win_ref):
    tq = ATTN_TQ
    h = pl.program_id(0)
    far_bias = tbl_ref[(REL_BUCKETS - 1) * N_HEADS + h]

    def table(out_ref, lo_keys, window, offset):
        width = out_ref.shape[2]

        def chunk(ci, carry):
            row0 = pl.multiple_of(ci * BIAS_ROWS, BIAS_ROWS)
            dist = (lo_keys + row0 + lax.broadcasted_iota(jnp.int32, (BIAS_ROWS, width), 0)
                    - lax.broadcasted_iota(jnp.int32, (BIAS_ROWS, width), 1))
            visible = (dist >= 0) & (dist < window)
            bias = jnp.full(dist.shape, tbl_ref[h], F32)
            for k in range(1, REL_BUCKETS):
                bias = jnp.where(dist >= thr_ref[k], tbl_ref[k * N_HEADS + h], bias)
            out_ref[0, pl.ds(row0, BIAS_ROWS), :] = jnp.where(visible, bias - offset, NEG_INF)
            return carry

        lax.fori_loop(0, tq // BIAS_ROWS, chunk, 0)

    table(near_ref, SLC_NEAR_BACK, 1 << 30, far_bias)
    table(win_ref, WINDOW, WINDOW, 0.0)


def _attention_bias_tables(rel_bias, seq):
    tbl = (rel_bias.astype(F32) * LOG2_E).reshape(REL_BUCKETS * N_HEADS)
    thr = _bucket_thresholds()
    tq = ATTN_TQ
    n_c = seq // CMP_STRIDE
    smem = pl.BlockSpec(memory_space=pltpu.SMEM)
    tr = min(512, seq)
    bias_c = pl.pallas_call(
        _bias_c_kernel,
        grid=(seq // tr,),
        in_specs=[smem, smem],
        out_specs=pl.BlockSpec((N_HEADS, tr, n_c), lambda i: (0, i, 0)),
        out_shape=jax.ShapeDtypeStruct((N_HEADS, seq, n_c), F32),
        compiler_params=_compiler_params(("parallel",)),
        name="bias_cmp",
    )(thr, tbl)
    head_block = lambda w: pl.BlockSpec((1, tq, w), lambda h: (h, 0, 0))
    widths = (SLC_NEAR_BACK + tq, WINDOW + tq)
    bias_near, bias_win = pl.pallas_call(
        _bias_near_kernel,
        grid=(N_HEADS,),
        in_specs=[smem, smem],
        out_specs=[head_block(w) for w in widths],
        out_shape=[jax.ShapeDtypeStruct((N_HEADS, tq, w), F32) for w in widths],
        compiler_params=_compiler_params(("parallel",)),
        name="bias_near",
    )(thr, tbl)
    return bias_c, bias_near, bias_win


def _nsa_constants(seq):
    n_c = seq // CMP_STRIDE
    n_blk = seq // SLC_BLOCK
    cmp_start = jnp.arange(n_c) * CMP_STRIDE
    blk_start = jnp.arange(n_blk) * SLC_BLOCK
    overlap_t = ((cmp_start[None, :] <= blk_start[:, None] + SLC_BLOCK - 1)
                 & (cmp_start[None, :] + CMP_BLOCK - 1 >= blk_start[:, None]))
    overlap_t = overlap_t & (cmp_start[None, :] + CMP_BLOCK <= seq)
    ones_rows = jnp.arange(SUBLANES)[:, None] == 0
    overlap_t = jnp.concatenate([overlap_t, jnp.broadcast_to(ones_rows, (SUBLANES, n_c))], axis=0)
    cmp_ones = jnp.broadcast_to(jnp.arange(LANES)[None, :] == 0, (n_c, LANES))
    return overlap_t.astype(BF16), cmp_ones.astype(BF16)


def _build_kv_scratch(seq, slc_ref, win_ref, g, ks_sc, vs_sc, kw_sc, vw_sc, knorm_sc):
    chunk = min(512, seq)
    lane_p = lax.broadcasted_iota(jnp.int32, (KV_PAD, LANES), 1)
    zeros = jnp.zeros((KV_PAD, LANES), BF16)
    ks_sc[0:KV_PAD] = jnp.where(lane_p >= HEAD_DIM, 1.0, 0.0).astype(BF16)
    kw_sc[0:KV_PAD] = jnp.where(lane_p == HEAD_DIM, NEG_INF, 0.0).astype(BF16)
    vs_sc[0:KV_PAD] = zeros
    vw_sc[0:KV_PAD] = zeros
    lane = lax.broadcasted_iota(jnp.int32, (chunk, LANES), 1)
    row = lax.broadcasted_iota(jnp.int32, (chunk, LANES), 0)
    lo_half = lane < HEAD_DIM
    ones_lane = jnp.where(lane == HEAD_DIM, 1.0, 0.0)
    k_sq_max = jnp.zeros((chunk, 1), F32)
    for c in range(seq // chunk):
        r0 = c * chunk
        dst = slice(KV_PAD + r0, KV_PAD + r0 + chunk)
        blk = jnp.right_shift(r0 + row, int(math.log2(SLC_BLOCK)))
        lanes_g = slice(g * LANES, (g + 1) * LANES)
        slab = slc_ref[0, r0:r0 + chunk, lanes_g].astype(F32)
        k_sq_max = jnp.maximum(k_sq_max, jnp.sum(jnp.where(lo_half, slab * slab, 0.0), axis=1, keepdims=True))
        ks_sc[dst] = jnp.where(lo_half, slab, jnp.where(lane - HEAD_DIM == blk, 1.0, 0.0)).astype(BF16)
        vs_sc[dst] = jnp.where(lo_half, pltpu.roll(slab, HEAD_DIM, 1), ones_lane).astype(BF16)
        slab = win_ref[0, r0:r0 + chunk, lanes_g].astype(F32)
        kw_sc[dst] = jnp.where(lo_half, slab, 0.0).astype(BF16)
        vw_sc[dst] = jnp.where(lo_half, pltpu.roll(slab, HEAD_DIM, 1), ones_lane).astype(BF16)
    knorm_sc[...] = jnp.broadcast_to(jnp.max(k_sq_max, axis=0, keepdims=True), knorm_sc.shape)


def _nsa_select(seq, g, q0, q_ref, vkc_ref, bias_c_ref, overlap_t_ref, cmp_ones_ref):
    tq = ATTN_TQ
    n_blk = seq // SLC_BLOCK
    n_sel = min(N_SEL, n_blk)
    rows = HPG * tq
    lane = lax.broadcasted_iota(jnp.int32, (tq, LANES), 1)
    lo_half = lane < HEAD_DIM
    lane_r = lax.broadcasted_iota(jnp.int32, (rows, LANES), 1)

    q_lo, q_hi = [], []
    for pair_idx in range(HPG // 2):
        lanes_p = slice((g * (HPG // 2) + pair_idx) * LANES, (g * (HPG // 2) + pair_idx + 1) * LANES)
        q2 = q_ref[0, :, lanes_p].astype(F32)
        q2r = pltpu.roll(q2, HEAD_DIM, 1)
        q_lo += [jnp.where(lo_half, q2, 0.0), jnp.where(lo_half, q2r, 0.0)]
        q_hi += [jnp.where(lo_half, 0.0, q2r), jnp.where(lo_half, 0.0, q2)]
    q_lo = jnp.concatenate(q_lo, axis=0)
    q_hi = jnp.concatenate(q_hi, axis=0)

    vkc = vkc_ref[0, g]
    s_c = _dot_nt(q_hi.astype(BF16), vkc) + bias_c_ref[g * HPG:(g + 1) * HPG].reshape(rows, -1)
    m_c = jnp.max(s_c, axis=1, keepdims=True)
    e_cb = jnp.exp2(s_c - m_c).astype(BF16)
    pv_c = _dot(e_cb, jnp.concatenate([vkc, cmp_ones_ref[...]], axis=1))
    row_pos = q0 + (lax.broadcasted_iota(jnp.int32, (rows, LANES), 0) & (tq - 1))
    has_key = row_pos >= CMP_BLOCK - 1
    o_cmp = jnp.where(has_key, pv_c[:, :LANES] / pv_c[:, LANES:LANES + 1], 0.0)

    imp_t4 = _dot_nt(overlap_t_ref[...], e_cb)
    imp_t = None
    for h in range(HPG):
        part = imp_t4[:, h * tq:(h + 1) * tq]
        part = part[:n_blk] / part[n_blk:n_blk + 1]
        imp_t = part if imp_t is None else imp_t + part

    blk = lax.broadcasted_iota(jnp.int32, (n_blk, tq), 0)
    pos = q0 + lax.broadcasted_iota(jnp.int32, (n_blk, tq), 1)
    cur = jnp.right_shift(pos, int(math.log2(SLC_BLOCK)))
    forced = (blk == 0) | (blk == cur) | (blk == cur - 1)
    valid = blk * SLC_BLOCK <= pos
    score = jnp.where(forced, BIG, jnp.where(valid, imp_t, -BIG))
    blk_f = blk.astype(F32)
    pen_t = jnp.full((n_blk, tq), NEG_INF, F32)
    for _ in range(n_sel):
        top = jnp.max(score, axis=0, keepdims=True)
        first = jnp.min(jnp.where(score == top, blk_f, float(n_blk)), axis=0, keepdims=True)
        hit = blk_f == first
        pen_t = jnp.where(hit, 0.0, pen_t)
        score = jnp.where(hit, -jnp.inf, score)
    pieces = [jnp.zeros((tq, HEAD_DIM), F32), jnp.transpose(pen_t)]
    if n_blk < LANES - HEAD_DIM:
        pieces.append(jnp.full((tq, LANES - HEAD_DIM - n_blk), NEG_INF, F32))
    pen_lanes = jnp.concatenate(pieces, axis=1)
    q_slc = jnp.where(lane_r < HEAD_DIM, q_lo, jnp.concatenate([pen_lanes] * HPG, axis=0)).astype(BF16)
    return o_cmp, q_lo, q_slc


def _nsa_kernel(seq, q_ref, vkc_ref, slc_ref, win_ref, gate_ref, bias_c_ref, bias_near_ref, bias_win_ref,
                overlap_t_ref, cmp_ones_ref, o_ref, *scratch):
    tq = ATTN_TQ
    rows = HPG * tq
    groups = range(N_KV)
    kv_sc = [scratch[4 * g:4 * g + 4] for g in groups]
    s_sc, mrun_sc, acc_sc, knorm_sc = (scratch[4 * N_KV + i * N_KV:4 * N_KV + (i + 1) * N_KV] for i in range(4))
    qt = pl.program_id(1)
    q0 = pl.multiple_of(qt * tq, tq)
    lane = lax.broadcasted_iota(jnp.int32, (tq, LANES), 1)
    lo_half = lane < HEAD_DIM
    lane_r = lax.broadcasted_iota(jnp.int32, (rows, LANES), 1)

    @pl.when(qt == 0)
    def _():
        for g in groups:
            _build_kv_scratch(seq, slc_ref, win_ref, g, *kv_sc[g], knorm_sc[g])

    sel = [_nsa_select(seq, g, q0, q_ref, vkc_ref, bias_c_ref, overlap_t_ref, cmp_ones_ref) for g in groups]
    o_cmp = [s[0] for s in sel]
    q_lo = [s[1] for s in sel]
    q_slc = [s[2] for s in sel]
    head_rows = lambda ref, g: ref[g * HPG:(g + 1) * HPG].reshape(rows, -1)

    o_win = []
    for g in groups:
        kw_sc, vw_sc = kv_sc[g][2], kv_sc[g][3]
        q_win = jnp.where(lane_r == HEAD_DIM, 1.0, q_lo[g]).astype(BF16)
        win_rows = pl.ds(q0, WINDOW + tq)
        s_w = _dot_nt(q_win, kw_sc[win_rows, :]) + head_rows(bias_win_ref, g)
        p_w = jnp.exp2(s_w - jnp.max(s_w, axis=1, keepdims=True)).astype(BF16)
        acc_w = _dot(p_w, vw_sc[win_rows, :])
        o_win.append(acc_w / acc_w[:, HEAD_DIM:HEAD_DIM + 1])

    tk = SLC_TK
    n_far = qt // (tk // tq)
    lane_tiles = lambda a: [a[:, j * LANES:(j + 1) * LANES] for j in range(a.shape[1] // LANES)]
    near0 = pl.multiple_of(q0 + (KV_PAD - SLC_NEAR_BACK), tq)

    def tile_rows(u):
        return pl.ds(pl.multiple_of(near0 - u * tk, tq), tk)

    def tile_cols(u):
        return pl.ds(pl.multiple_of(u * tk, tk), tk)

    def score_tile(u, g, bias):
        s = _dot_nt(q_slc[g], kv_sc[g][0][tile_rows(u), :])
        if bias is not None:
            s = s + bias
        s_sc[g][:, tile_cols(u)] = s
        return functools.reduce(jnp.maximum, lane_tiles(s))

    shift_cap = []
    slack = jnp.float32(-jnp.inf)
    for g in groups:
        lane_max = score_tile(0, g, head_rows(bias_near_ref, g))
        mrun_sc[g][...] = lane_max
        m_near = jnp.max(lane_max, axis=1, keepdims=True)
        q_norm = jnp.sqrt(jnp.sum(q_lo[g] * q_lo[g], axis=1, keepdims=True))
        far_bound = q_norm * jnp.sqrt(knorm_sc[g][0:1, 0:1]) * SOFTMAX_BOUND_SLACK
        shift_cap.append(jnp.maximum(m_near, far_bound - SOFTMAX_SHIFT_MARGIN))
        slack = jnp.maximum(slack, jnp.max(far_bound - m_near))
    single_pass = slack <= 2.0 * SOFTMAX_SHIFT_MARGIN

    def weigh(s, u, g):
        shift = mrun_sc[g][...]
        p = jnp.exp2(s - jnp.concatenate([shift] * (tk // LANES), axis=1))
        acc_sc[g][...] = acc_sc[g][...] + _dot(p.astype(BF16), kv_sc[g][1][tile_rows(u), :])

    for g in groups:
        acc_sc[g][...] = jnp.zeros((rows, LANES), F32)

    @pl.when(single_pass)
    def _():
        for g in groups:
            mrun_sc[g][...] = jnp.broadcast_to(shift_cap[g], (rows, LANES))
            weigh(s_sc[g][:, tile_cols(0)], 0, g)

        def fused(u, carry):
            for g in groups:
                weigh(_dot_nt(q_slc[g], kv_sc[g][0][tile_rows(u), :]), u, g)
            return carry

        lax.fori_loop(1, n_far + 1, fused, 0)

    @pl.when(jnp.logical_not(single_pass))
    def _():
        def pass1(u, carry):
            for g in groups:
                mrun_sc[g][...] = jnp.maximum(mrun_sc[g][...], score_tile(u, g, None))
            return carry

        lax.fori_loop(1, n_far + 1, pass1, 0)
        for g in groups:
            m_s = jnp.max(mrun_sc[g][...], axis=1, keepdims=True)
            mrun_sc[g][...] = jnp.broadcast_to(m_s, (rows, LANES))

        def pass2(u, carry):
            for g in groups:
                weigh(s_sc[g][:, tile_cols(u)], u, g)
            return carry

        lax.fori_loop(0, n_far + 1, pass2, 0)

    for g in groups:
        acc = acc_sc[g][...]
        o_slc = acc / acc[:, HEAD_DIM:HEAD_DIM + 1]

        gates = gate_ref[0, :, g * GATE_PAD:(g + 1) * GATE_PAD]
        outs = []
        for h in range(HPG):
            sl = slice(h * tq, (h + 1) * tq)
            g_c, g_s, g_w = (gates[:, br * HPG + h:br * HPG + h + 1] for br in range(3))
            outs.append(g_c * o_cmp[g][sl] + g_s * o_slc[sl] + g_w * o_win[g][sl])
        for pair_idx in range(HPG // 2):
            even, odd = outs[2 * pair_idx], outs[2 * pair_idx + 1]
            merged = jnp.where(lo_half, even, pltpu.roll(odd, HEAD_DIM, 1))
            lanes_p = slice((g * (HPG // 2) + pair_idx) * LANES, (g * (HPG // 2) + pair_idx + 1) * LANES)
            o_ref[0, :, lanes_p] = merged.astype(o_ref.dtype)


def _nsa(q, vkc, slc, win, gates, bias_c, bias_near, bias_win, overlap_t, cmp_ones):
    b, seq, _ = q.shape
    n_c = seq // CMP_STRIDE
    n_blk = seq // SLC_BLOCK
    assert n_blk <= LANES - HEAD_DIM and seq % SLC_TK == 0 and KV_PAD >= SLC_TK
    tq = ATTN_TQ
    rows = HPG * tq
    const = lambda a: pl.BlockSpec(a.shape, lambda i, t: (0,) * a.ndim)
    tile = lambda w: pl.BlockSpec((1, tq, w), lambda i, t: (i, t, 0))
    whole_seq = lambda w: pl.BlockSpec((1, seq, w), lambda i, t: (i, 0, 0))
    per_group = lambda shape: [pltpu.VMEM(shape, F32)] * N_KV
    return pl.pallas_call(
        functools.partial(_nsa_kernel, seq),
        grid=(b, seq // tq),
        in_specs=[
            tile(ATTN_WIDTH),
            pl.BlockSpec((1, N_KV, n_c, 2 * HEAD_DIM), lambda i, t: (i, 0, 0, 0)),
            whole_seq(N_KV * 2 * HEAD_DIM), whole_seq(N_KV * 2 * HEAD_DIM),
            tile(N_KV * GATE_PAD),
            pl.BlockSpec((N_HEADS, tq, n_c), lambda i, t: (0, t, 0)),
            const(bias_near), const(bias_win),
            const(overlap_t), const(cmp_ones),
        ],
        out_specs=tile(ATTN_WIDTH),
        out_shape=jax.ShapeDtypeStruct((b, seq, ATTN_WIDTH), BF16),
        scratch_shapes=(
            [pltpu.VMEM((KV_PAD + seq, LANES), BF16)] * (4 * N_KV)
            + per_group((rows, seq)) + per_group((rows, LANES)) + per_group((rows, LANES))
            + per_group((SUBLANES, LANES))
        ),
        compiler_params=_compiler_params(("parallel", "arbitrary")),
        name="nsa",
    )(q, vkc, slc, win, gates, bias_c, bias_near, bias_win, overlap_t, cmp_ones)


def _mixer_inputs(x, rel_bias, w_in, b_in, cmp_pos, cmp_w1, cmp_b1, cmp_w2, cmp_b2):
    b, seq, d = x.shape
    wp, bp = _pack_in_proj(w_in, b_in, d)
    q, slc, win, cmp, gates, u, merge = _in_proj(x.reshape(b * seq, d), wp, bp, d, 1024)
    n_c = seq // CMP_STRIDE
    cmp4 = cmp.reshape(b, seq, 2 * N_KV, HEAD_DIM).transpose(0, 2, 1, 3).reshape(
        b, 2 * N_KV, n_c, CMP_STRIDE * HEAD_DIM)
    vkc = _compress(cmp4, cmp_pos, cmp_w1, cmp_b1, cmp_w2, cmp_b2)
    bias_c, bias_near, bias_win = _attention_bias_tables(rel_bias, seq)
    overlap_t, cmp_ones = _nsa_constants(seq)
    o = _nsa(q.reshape(b, seq, -1), vkc, slc.reshape(b, seq, -1), win.reshape(b, seq, -1),
             gates.reshape(b, seq, -1), bias_c, bias_near, bias_win, overlap_t, cmp_ones)
    return o, u, merge


S5_HALF_GROUPS = S5_GROUPS // 2
S5_HALF_IN = S5_HALF_GROUPS * S5_GROUP
S5_HALF_STATE = S5_HALF_GROUPS * S5_STATE
S5_SCAN_LANES = 512
S5_CHUNK = 128
S5_UNROLL = 8


def _s5_params(lam_re, lam_im, log_dt, b_re, b_im, c_re, c_im, nb):
    dt = jnp.exp(log_dt.astype(F32))[:, None]
    lr, li = lam_re.astype(F32), lam_im.astype(F32)
    mag = jnp.exp(lr * dt)
    ab_re, ab_im = mag * jnp.cos(li * dt), mag * jnp.sin(li * dt)
    nr, ni = ab_re - 1.0, ab_im
    den = lr * lr + li * li
    fr, fi = (nr * lr + ni * li) / den, (ni * lr - nr * li) / den
    br, bim = b_re.astype(F32), b_im.astype(F32)
    bb_re = fr[..., None] * br - fi[..., None] * bim
    bb_im = fr[..., None] * bim + fi[..., None] * br
    eye = jnp.eye(S5_HALF_GROUPS, dtype=F32)

    def in_mat(bb):
        t = bb.reshape(2, S5_HALF_GROUPS, S5_STATE, S5_GROUP)
        m = jnp.einsum('kgph,gj->kghjp', t, eye)
        return m.reshape(2, S5_HALF_IN, S5_HALF_STATE)

    def out_mat(c):
        t = c.astype(F32).reshape(2, S5_HALF_GROUPS, S5_GROUP, S5_STATE)
        m = jnp.einsum('kghp,gj->kgpjh', t, eye)
        return m.reshape(2, S5_HALF_STATE, S5_HALF_IN)

    bmat = jnp.concatenate([in_mat(bb_re), in_mat(bb_im)], axis=2).astype(BF16)
    cmat = jnp.concatenate([out_mat(c_re), -out_mat(c_im)], axis=1).astype(BF16)
    a = jnp.concatenate([ab_re.reshape(2, S5_HALF_STATE), ab_im.reshape(2, S5_HALF_STATE)], axis=1)
    a = jnp.broadcast_to(a.reshape(1, 4 * S5_HALF_STATE), (nb, 4 * S5_HALF_STATE))
    return bmat, cmat, a


def _s5_kernel(u_ref, bmat_ref, cmat_ref, a_ref, d_ref, y_ref, ut_sc, x_sc, st_sc):
    nb, t_len, _ = u_ref.shape
    half_w = 2 * S5_HALF_STATE

    @pl.when(pl.program_id(0) == 0)
    def _():
        st_sc[...] = jnp.zeros_like(st_sc)

    n_cb = ut_sc.shape[0]
    for b in range(nb):
        for cb in range(n_cb):
            ut_sc[cb, pl.ds(b, t_len, stride=nb), :] = u_ref[b, :, cb * LANES:(cb + 1) * LANES]
    ut = jnp.concatenate([ut_sc[cb] for cb in range(n_cb)], axis=1)
    ub = ut.astype(BF16)
    for k in range(2):
        x_sc[:, k * half_w:(k + 1) * half_w] = _dot(ub[:, k * S5_HALF_IN:(k + 1) * S5_HALF_IN], bmat_ref[k])

    for k in range(2):
        for j in range(S5_HALF_STATE // S5_SCAN_LANES):
            re0 = k * half_w + j * S5_SCAN_LANES
            im0 = re0 + S5_HALF_STATE
            re_sl, im_sl = pl.ds(re0, S5_SCAN_LANES), pl.ds(im0, S5_SCAN_LANES)
            ar, ai = a_ref[:, re_sl], a_ref[:, im_sl]

            def steps(c, carry):
                xr, xi = carry
                for s in range(S5_UNROLL):
                    rows = pl.ds(pl.multiple_of((c * S5_UNROLL + s) * nb, nb), nb)
                    nxr = ar * xr - ai * xi + x_sc[rows, re_sl]
                    nxi = ar * xi + ai * xr + x_sc[rows, im_sl]
                    x_sc[rows, re_sl] = nxr
                    x_sc[rows, im_sl] = nxi
                    xr, xi = nxr, nxi
                return xr, xi

            xr, xi = lax.fori_loop(0, t_len // S5_UNROLL, steps, (st_sc[:, re_sl], st_sc[:, im_sl]))
            st_sc[:, re_sl] = xr
            st_sc[:, im_sl] = xi

    xs = x_sc[...].astype(BF16)
    y = jnp.concatenate([_dot(xs[:, k * half_w:(k + 1) * half_w], cmat_ref[k]) for k in range(2)], axis=1)
    y = _gelu_tanh(y + d_ref[...] * ut)
    for cb in range(n_cb):
        ut_sc[cb] = y[:, cb * LANES:(cb + 1) * LANES]
    for b in range(nb):
        for cb in range(n_cb):
            y_ref[b, :, cb * LANES:(cb + 1) * LANES] = ut_sc[cb, pl.ds(b, t_len, stride=nb), :].astype(BF16)


def _s5(u, bmat, cmat, a, d_skip):
    nb, seq, w = u.shape
    t_len = min(S5_CHUNK, seq)
    full = lambda shape: pl.BlockSpec(shape, lambda c: (0,) * len(shape))
    return pl.pallas_call(
        _s5_kernel,
        grid=(seq // t_len,),
        in_specs=[
            pl.BlockSpec((nb, t_len, w), lambda c: (0, c, 0)),
            full(bmat.shape), full(cmat.shape), full(a.shape), full((1, w)),
        ],
        out_specs=pl.BlockSpec((nb, t_len, w), lambda c: (0, c, 0)),
        out_shape=jax.ShapeDtypeStruct((nb, seq, w), BF16),
        scratch_shapes=[
            pltpu.VMEM((w // LANES, t_len * nb, LANES), F32),
            pltpu.VMEM((t_len * nb, 4 * S5_HALF_STATE), F32),
            pltpu.VMEM((nb, 4 * S5_HALF_STATE), F32),
        ],
        compiler_params=_compiler_params(("arbitrary",)),
        name="s5",
    )(u, bmat, cmat, a, d_skip.reshape(1, w).astype(F32))


ROUTE_PAD = LANES
_R_E1, _R_E2, _R_W1, _R_W2, _R_RANK1, _R_RANK2 = range(6)


def _layer_norm(t, g, b):
    mu = jnp.mean(t, axis=1, keepdims=True)
    c = t - mu
    var = jnp.mean(c * c, axis=1, keepdims=True)
    return c * lax.rsqrt(var + LN_EPS) * g + b


def _post_kernel(x_ref, o_ref, y_ref, m_ref, wup_ref, wval_ref, wgate_ref, bgate_ref, wout_ref,
                 g1_ref, b1_ref, wr_ref, br_ref, h_ref, hp_ref, route_ref, cnt_ref, run_sc):
    tm, d = x_ref.shape

    @pl.when(pl.program_id(0) == 0)
    def _():
        run_sc[...] = jnp.zeros_like(run_sc)

    y_a = _dot(o_ref[...], wup_ref[...])
    z = y_ref[...]
    y_b = _dot(z, wval_ref[...]) * jax.nn.sigmoid(_dot(z, wgate_ref[...]) + bgate_ref[...])
    mixed = m_ref[:, :d].astype(F32) * y_a + m_ref[:, d:].astype(F32) * y_b
    t = DN_ALPHA * x_ref[...] + _dot(mixed.astype(BF16), wout_ref[...])
    h = _layer_norm(t, g1_ref[...], b1_ref[...])
    h_ref[...] = h
    hp_ref[...] = _pack_rows(h)

    logits = _dot(h.astype(BF16), wr_ref[...]) + br_ref[...]
    lane = lax.broadcasted_iota(jnp.int32, (tm, ROUTE_PAD), 1)
    lane_f = lane.astype(F32)
    is_group = lane < N_EGROUPS

    def first_max(v):
        top = jnp.max(v, axis=1, keepdims=True)
        idx = jnp.min(jnp.where(v == top, lane_f, float(ROUTE_PAD)), axis=1, keepdims=True)
        return top, idx

    g_max, g_top = first_max(jnp.where(is_group, logits, -jnp.inf))
    p_group = 1.0 / jnp.sum(jnp.where(is_group, jnp.exp(logits - g_max), 0.0), axis=1, keepdims=True)
    grp_of_lane = jnp.right_shift(lane - N_EGROUPS, int(math.log2(EXPERTS_PER_GROUP))).astype(F32)
    in_group = (lane >= N_EGROUPS) & (lane < N_EGROUPS + N_EXPERTS) & (grp_of_lane == g_top)
    e_log = jnp.where(in_group, logits, -jnp.inf)
    v1, i1 = first_max(e_log)
    hit1 = lane_f == i1
    v2, i2 = first_max(jnp.where(hit1, -jnp.inf, e_log))
    hit2 = lane_f == i2
    e2 = jnp.exp(v2 - v1)
    w1 = p_group / (1.0 + e2)
    w2 = p_group * e2 / (1.0 + e2)

    hits = jnp.where(hit1 | hit2, 1.0, 0.0)
    row = lax.broadcasted_iota(jnp.int32, (tm, tm), 0)
    col = lax.broadcasted_iota(jnp.int32, (tm, tm), 1)
    earlier = jnp.where(col < row, 1.0, 0.0).astype(BF16)
    before = _dot(earlier, hits.astype(BF16)) + run_sc[...]
    rank1 = jnp.sum(jnp.where(hit1, before, 0.0), axis=1, keepdims=True)
    rank2 = jnp.sum(jnp.where(hit2, before, 0.0), axis=1, keepdims=True)
    run_sc[...] = run_sc[...] + jnp.sum(hits, axis=0, keepdims=True)
    cnt_ref[...] = run_sc[...]

    rec = jnp.zeros((tm, ROUTE_PAD), F32)
    for slot, val in ((_R_E1, i1 - N_EGROUPS), (_R_E2, i2 - N_EGROUPS), (_R_W1, w1), (_R_W2, w2),
                      (_R_RANK1, rank1), (_R_RANK2, rank2)):
        rec = jnp.where(lane == slot, val, rec)
    route_ref[...] = rec


def _post(x2d, o2d, y2d, merge, w_attn_up, s5_w_val, s5_w_gate, s5_b_gate, w_out, ln1_g, ln1_b,
          router_w_group, router_b_group, router_w_expert, router_b_expert, tm):
    n, d = x2d.shape
    rpad = ROUTE_PAD - N_EGROUPS - N_EXPERTS
    wr = jnp.concatenate([router_w_group, router_w_expert, jnp.zeros((d, rpad), F32)], axis=1).astype(BF16)
    br = jnp.concatenate([router_b_group, router_b_expert, jnp.zeros((rpad,), F32)]).reshape(1, -1).astype(F32)
    row = lambda w: pl.BlockSpec((tm, w), lambda i: (i, 0))
    full = lambda a: pl.BlockSpec(a.shape, lambda i: (0,) * a.ndim)
    weights = [w_attn_up.astype(BF16), s5_w_val.astype(BF16), s5_w_gate.astype(BF16),
               s5_b_gate.reshape(1, d).astype(F32), w_out.astype(BF16),
               ln1_g.reshape(1, d).astype(F32), ln1_b.reshape(1, d).astype(F32), wr, br]
    return pl.pallas_call(
        _post_kernel,
        grid=(n // tm,),
        in_specs=[row(d), row(ATTN_WIDTH), row(S5_WIDTH), row(2 * d)] + [full(w) for w in weights],
        out_specs=[row(d), row(d // 2), row(ROUTE_PAD), pl.BlockSpec((1, ROUTE_PAD), lambda i: (0, 0))],
        out_shape=[jax.ShapeDtypeStruct((n, d), F32), jax.ShapeDtypeStruct((n, d // 2), jnp.uint32),
                   jax.ShapeDtypeStruct((n, ROUTE_PAD), F32), jax.ShapeDtypeStruct((1, ROUTE_PAD), F32)],
        scratch_shapes=[pltpu.VMEM((1, ROUTE_PAD), F32)],
        compiler_params=_compiler_params(("arbitrary",)),
        name="post_mixer",
    )(x2d, o2d, y2d, merge, *weights)


def _plan_kernel(route_ref, cnt_ref, dest_ref):
    tm = route_ref.shape[0]
    lane8 = lax.broadcasted_iota(jnp.int32, (SUBLANES, ROUTE_PAD), 1)
    counts = jnp.broadcast_to(cnt_ref[...], (SUBLANES, ROUTE_PAD)).astype(jnp.int32)
    shift = int(math.log2(MOE_BLOCK))
    padded = jnp.left_shift(jnp.right_shift(counts + (MOE_BLOCK - 1), shift), shift)
    incl = padded
    step = 1
    while step < ROUTE_PAD:
        incl = incl + jnp.where(lane8 >= step, pltpu.roll(incl, step, 1), 0)
        step *= 2
    pstart = (incl - padded)[0:1].astype(F32)
    route = route_ref[...]
    expert_of_lane = (lax.broadcasted_iota(jnp.int32, (tm, ROUTE_PAD), 1) - N_EGROUPS).astype(F32)
    lane = lax.broadcasted_iota(jnp.int32, (tm, ROUTE_PAD), 1)

    def dest(e_slot, rank_slot):
        hit = expert_of_lane == route[:, e_slot:e_slot + 1]
        return jnp.sum(jnp.where(hit, pstart, 0.0), axis=1, keepdims=True) + route[:, rank_slot:rank_slot + 1]

    d1 = dest(_R_E1, _R_RANK1)
    d2 = dest(_R_E2, _R_RANK2)
    dest_ref[...] = jnp.where(lane == 0, d1, jnp.where(lane == 1, d2, 0.0)).astype(jnp.int32)


def _plan(route, counts_row):
    n = route.shape[0]
    tm = min(1024, n)
    return pl.pallas_call(
        _plan_kernel,
        grid=(n // tm,),
        in_specs=[pl.BlockSpec((tm, ROUTE_PAD), lambda i: (i, 0)),
                  pl.BlockSpec((1, ROUTE_PAD), lambda i: (0, 0))],
        out_specs=pl.BlockSpec((tm, ROUTE_PAD), lambda i: (i, 0)),
        out_shape=jax.ShapeDtypeStruct((n, ROUTE_PAD), jnp.int32),
        compiler_params=_compiler_params(("parallel",)),
        name="moe_plan",
    )(route, counts_row)


SC_GATHER_ROWS = 128


def _sc_row_gather(table, idx):
    n_idx = idx.shape[0]
    d = table.shape[1]
    info = plsc.get_sparse_core_info()
    n_workers = info.num_cores * info.num_subcores
    per_worker = n_idx // n_workers
    assert n_idx % (n_workers * SC_GATHER_ROWS) == 0
    mesh = plsc.VectorSubcoreMesh(core_axis_name="c", subcore_axis_name="s")

    @functools.partial(
        pl.kernel, mesh=mesh,
        out_type=jax.ShapeDtypeStruct((n_idx, d), table.dtype),
        scratch_types=[
            pltpu.VMEM((SC_GATHER_ROWS,), jnp.int32),
            pltpu.VMEM((SC_GATHER_ROWS, d), table.dtype),
            pltpu.SemaphoreType.DMA,
        ],
    )
    def gather(table_hbm, idx_hbm, out_hbm, idx_v, rows_v, sem):
        worker = lax.axis_index("s") * info.num_cores + lax.axis_index("c")
        base = worker * per_worker

        @pl.loop(0, per_worker // SC_GATHER_ROWS)
        def _(j):
            off = base + j * SC_GATHER_ROWS
            pltpu.sync_copy(idx_hbm.at[pl.ds(off, SC_GATHER_ROWS)], idx_v)
            pltpu.async_copy(table_hbm.at[idx_v], rows_v, sem).wait()
            pltpu.sync_copy(rows_v, out_hbm.at[pl.ds(off, SC_GATHER_ROWS)])

    return gather(table, idx)


def _sc_row_scatter(rows, idx_a, idx_b, n_out):
    n, d = rows.shape
    info = plsc.get_sparse_core_info()
    n_workers = info.num_cores * info.num_subcores
    per_worker = n // n_workers
    assert n % (n_workers * SC_GATHER_ROWS) == 0
    mesh = plsc.VectorSubcoreMesh(core_axis_name="c", subcore_axis_name="s")

    @functools.partial(
        pl.kernel, mesh=mesh,
        out_type=jax.ShapeDtypeStruct((n_out, d), rows.dtype),
        scratch_types=[
            pltpu.VMEM((SC_GATHER_ROWS,), jnp.int32),
            pltpu.VMEM((SC_GATHER_ROWS,), jnp.int32),
            pltpu.VMEM((SC_GATHER_ROWS, d), rows.dtype),
        ],
    )
    def scatter(rows_hbm, idx_a_hbm, idx_b_hbm, out_hbm, idx_a_v, idx_b_v, rows_v):
        worker = lax.axis_index("s") * info.num_cores + lax.axis_index("c")
        base = worker * per_worker

        @pl.loop(0, per_worker // SC_GATHER_ROWS)
        def _(j):
            src = pl.ds(base + j * SC_GATHER_ROWS, SC_GATHER_ROWS)
            pltpu.sync_copy(rows_hbm.at[src], rows_v)
            pltpu.sync_copy(idx_a_hbm.at[src], idx_a_v)
            pltpu.sync_copy(idx_b_hbm.at[src], idx_b_v)
            pltpu.sync_copy(rows_v, out_hbm.at[idx_a_v])
            pltpu.sync_copy(rows_v, out_hbm.at[idx_b_v])

    return scatter(rows, idx_a, idx_b)


def _expert_kernel(blk_exp_ref, blk_valid_ref, x_ref, wg_ref, wu_ref, wd_ref, y_ref, wg_sc, wu_sc, wd_sc):
    i = pl.program_id(0)
    n_valid = blk_valid_ref[i]

    @pl.when((i == 0) | (blk_exp_ref[i] != blk_exp_ref[jnp.maximum(i - 1, 0)]))
    def _():
        wg_sc[...] = wg_ref[0].astype(BF16)
        wu_sc[...] = wu_ref[0].astype(BF16)
        wd_sc[...] = wd_ref[0].astype(BF16)

    @pl.when(n_valid > 0)
    def _():
        row = lax.broadcasted_iota(jnp.int32, x_ref.shape, 0)
        words = jnp.where(row < n_valid, x_ref[...], jnp.uint32(0))
        xb = _unpack_rows(words).astype(BF16)
        h_gate = _dot(xb, wg_sc[...])
        h_up = _dot(xb, wu_sc[...])
        hb = (h_gate * jax.nn.sigmoid(h_gate) * h_up).astype(BF16)
        y_ref[...] = _pack_rows(_dot(hb, wd_sc[...]))

    @pl.when(n_valid == 0)
    def _():
        y_ref[...] = jnp.zeros_like(y_ref)


def _experts(blk_expert, blk_valid, xs, w_gate, w_up, w_down):
    n_blocks = blk_expert.shape[0]
    d = 2 * xs.shape[1]
    grid_spec = pltpu.PrefetchScalarGridSpec(
        num_scalar_prefetch=2,
        grid=(n_blocks,),
        in_specs=[
            pl.BlockSpec((MOE_BLOCK, d // 2), lambda i, be, bv: (i, 0)),
            pl.BlockSpec((1, d, D_EXPERT), lambda i, be, bv: (be[i], 0, 0)),
            pl.BlockSpec((1, d, D_EXPERT), lambda i, be, bv: (be[i], 0, 0)),
            pl.BlockSpec((1, D_EXPERT, d), lambda i, be, bv: (be[i], 0, 0)),
        ],
        out_specs=pl.BlockSpec((MOE_BLOCK, d // 2), lambda i, be, bv: (i, 0)),
        scratch_shapes=[
            pltpu.VMEM((d, D_EXPERT), BF16),
            pltpu.VMEM((d, D_EXPERT), BF16),
            pltpu.VMEM((D_EXPERT, d), BF16),
        ],
    )
    return pl.pallas_call(
        _expert_kernel,
        grid_spec=grid_spec,
        out_shape=jax.ShapeDtypeStruct((n_blocks * MOE_BLOCK, d // 2), jnp.uint32),
        compiler_params=_compiler_params(("arbitrary",)),
        name="experts",
    )(blk_expert, blk_valid, xs, w_gate, w_up, w_down)


COMBINE_TM = 512


def _combine_kernel(h_ref, y1_ref, y2_ref, route_ref, g2_ref, b2_ref, out_ref):
    route = route_ref[...]
    w1 = route[:, _R_W1:_R_W1 + 1]
    w2 = route[:, _R_W2:_R_W2 + 1]
    t = DN_ALPHA * h_ref[...] + (_unpack_rows(y1_ref[...]) * w1 + _unpack_rows(y2_ref[...]) * w2)
    out_ref[...] = _layer_norm(t, g2_ref[...], b2_ref[...])


def _combine(yg, h2d, route, ln2_g, ln2_b):
    n, d = h2d.shape
    tm = min(COMBINE_TM, n)
    n_tiles = n // tm
    row = lambda w: pl.BlockSpec((tm, w), lambda i: (i, 0))
    vec = pl.BlockSpec((1, d), lambda i: (0, 0))
    return pl.pallas_call(
        _combine_kernel,
        grid=(n_tiles,),
        in_specs=[row(d), row(d // 2), pl.BlockSpec((tm, d // 2), lambda i: (i + n_tiles, 0)), row(ROUTE_PAD),
                  vec, vec],
        out_specs=row(d),
        out_shape=jax.ShapeDtypeStruct((n, d), F32),
        compiler_params=_compiler_params(("parallel",)),
        name="combine",
    )(h2d, yg, yg, route, ln2_g.reshape(1, d).astype(F32), ln2_b.reshape(1, d).astype(F32))


def _moe(h2d, h_packed, route, counts_row, w_gate, w_up, w_down, ln2_g, ln2_b):
    n, d = h2d.shape
    dest = _plan(route, counts_row)
    dest1, dest2 = dest[:, 0], dest[:, 1]
    counts = counts_row[0, N_EGROUPS:N_EGROUPS + N_EXPERTS].astype(jnp.int32)
    padded = (counts + MOE_BLOCK - 1) // MOE_BLOCK * MOE_BLOCK
    pend = jnp.cumsum(padded)
    n_blocks = -(-(n * TOP_K_IN_GROUP) // MOE_BLOCK) + N_EXPERTS
    blk_row0 = (jnp.arange(n_blocks, dtype=jnp.int32) * MOE_BLOCK)[:, None]
    owns = (pend - padded <= blk_row0) & (blk_row0 < pend)
    blk_expert = jnp.sum(jnp.where(owns, jnp.arange(N_EXPERTS, dtype=jnp.int32), 0), axis=1)
    blk_valid = jnp.sum(jnp.where(owns, jnp.clip(pend - padded + counts - blk_row0, 0, MOE_BLOCK), 0), axis=1)
    xs = _sc_row_scatter(h_packed, dest1, dest2, n_blocks * MOE_BLOCK)
    yb = _experts(blk_expert, blk_valid, xs, w_gate, w_up, w_down)
    yg = _sc_row_gather(yb, jnp.concatenate([dest1, dest2]))
    return _combine(yg, h2d, route, ln2_g, ln2_b)


def kernel(x, rel_bias, w_in, b_in, cmp_pos, cmp_w1, cmp_b1, cmp_w2, cmp_b2, w_attn_up, s5_lambda_re, s5_lambda_im, s5_log_dt, s5_b_re, s5_b_im, s5_c_re, s5_c_im, s5_d, s5_w_val, s5_w_gate, s5_b_gate, w_out, ln1_g, ln1_b, router_w_group, router_b_group, router_w_expert, router_b_expert, exp_w_gate, exp_w_up, exp_w_down, ln2_g, ln2_b):
    b, seq, d = x.shape
    n = b * seq
    assert w_in.shape[0] == DEPTH
    l = 0
    o, u, merge = _mixer_inputs(x, rel_bias, w_in[l], b_in[l], cmp_pos[l], cmp_w1[l], cmp_b1[l],
                                cmp_w2[l], cmp_b2[l])
    bmat, cmat, a = _s5_params(s5_lambda_re[l], s5_lambda_im[l], s5_log_dt[l], s5_b_re[l], s5_b_im[l],
                               s5_c_re[l], s5_c_im[l], b)
    y_s = _s5(u.reshape(b, seq, S5_WIDTH), bmat, cmat, a, s5_d[l])
    h2d, h_packed, route, counts = _post(
        x.reshape(n, d), o.reshape(n, ATTN_WIDTH), y_s.reshape(n, S5_WIDTH), merge, w_attn_up[l], s5_w_val[l],
        s5_w_gate[l], s5_b_gate[l], w_out[l], ln1_g[l], ln1_b[l], router_w_group[l], router_b_group[l],
        router_w_expert[l], router_b_expert[l], 512)
    out = _moe(h2d, h_packed, route, counts, exp_w_gate[l], exp_w_up[l], exp_w_down[l], ln2_g[l], ln2_b[l])
    return out.reshape(b, seq, d)
```

```python
import functools
import math

import jax
import jax.numpy as jnp
from jax import lax
from jax.experimental import pallas as pl
from jax.experimental.pallas import tpu as pltpu
from jax.experimental.pallas import tpu_sc as plsc

F32 = jnp.float32
BF16 = jnp.bfloat16

N_HEADS = 8
HEAD_DIM = 64
N_KV = 2
HPG = N_HEADS // N_KV
CMP_STRIDE = 16
CMP_BLOCK = 2 * CMP_STRIDE
CMP_HIDDEN = 128
SLC_BLOCK = 64
N_SEL = 16
WINDOW = 512
REL_BUCKETS = 32
REL_MAX_DIST = 128
S5_WIDTH = 512
S5_GROUP = 16
S5_GROUPS = S5_WIDTH // S5_GROUP
S5_STATE = 64
N_EGROUPS = 8
EXPERTS_PER_GROUP = 8
N_EXPERTS = N_EGROUPS * EXPERTS_PER_GROUP
TOP_K_IN_GROUP = 2
D_EXPERT = 256
EXPERT_BLOCK = 128
DEPTH = 1
DN_ALPHA = (2.0 * DEPTH) ** 0.25
LN_EPS = 1e-5
NEG_INF = -1e30
BIG = 1e9
LOG2_E = math.log2(math.e)
MOE_BLOCK = 512

ATTN_WIDTH = N_HEADS * HEAD_DIM
KV_WIDTH = N_KV * HEAD_DIM
KV_OFF = ATTN_WIDTH
NSA_GATE_OFF = KV_OFF + 6 * KV_WIDTH
S5_OFF = NSA_GATE_OFF + 3 * N_HEADS
MERGE_OFF = S5_OFF + S5_WIDTH

LANES = 128
SUBLANES = 8
VMEM_LIMIT_BYTES = 56 * 1024 * 1024

ATTN_TQ = 128
SLC_TK = 512
SLC_NEAR_BACK = SLC_TK - ATTN_TQ
KV_PAD = WINDOW
SOFTMAX_SHIFT_MARGIN = 100.0
SOFTMAX_BOUND_SLACK = 1.001
GATE_PAD = LANES


def _gelu_tanh(x):
    c = math.sqrt(2.0 / math.pi)
    return x * (0.5 * (1.0 + jnp.tanh(c * (x + 0.044715 * (x * x * x)))))


def _dot(a, b):
    return jnp.dot(a, b, preferred_element_type=F32)


def _dot_nt(a, b):
    return lax.dot_general(a, b, (((1,), (1,)), ((), ())), preferred_element_type=F32)


def _pack_rows(x):
    half = x.shape[1] // 2
    xb = x.astype(BF16).astype(F32)
    hi = lax.bitcast_convert_type(xb[:, :half], jnp.uint32) & jnp.uint32(0xFFFF0000)
    lo = lax.shift_right_logical(lax.bitcast_convert_type(xb[:, half:], jnp.uint32), jnp.uint32(16))
    return hi | lo


def _unpack_rows(w):
    hi = lax.bitcast_convert_type(w & jnp.uint32(0xFFFF0000), F32)
    lo = lax.bitcast_convert_type(lax.shift_left(w, jnp.uint32(16)), F32)
    return jnp.concatenate([hi, lo], axis=1)


def _compiler_params(semantics):
    return pltpu.CompilerParams(dimension_semantics=semantics, vmem_limit_bytes=VMEM_LIMIT_BYTES)


def _in_proj_layout(d_model):
    widths = (ATTN_WIDTH, 2 * KV_WIDTH, 2 * KV_WIDTH, 2 * KV_WIDTH, N_KV * GATE_PAD, S5_WIDTH, 2 * d_model)
    offs = [0]
    for w in widths:
        offs.append(offs[-1] + w)
    return widths, offs


def _pack_in_proj(w_in, b_in, d_model):
    def kv_cols(j):
        return KV_OFF + j * KV_WIDTH

    def pair(jk, jv):
        cols = []
        for g in range(N_KV):
            cols.append(jnp.arange(kv_cols(jk) + g * HEAD_DIM, kv_cols(jk) + (g + 1) * HEAD_DIM))
            cols.append(jnp.arange(kv_cols(jv) + g * HEAD_DIM, kv_cols(jv) + (g + 1) * HEAD_DIM))
        return jnp.concatenate(cols)

    idx = jnp.concatenate([
        jnp.arange(0, ATTN_WIDTH),
        pair(2, 3),
        pair(4, 5),
        jnp.arange(kv_cols(0), kv_cols(2)),
    ])
    idx2 = jnp.concatenate([jnp.arange(S5_OFF, S5_OFF + S5_WIDTH),
                            jnp.arange(MERGE_OFF, MERGE_OFF + 2 * d_model)])
    gpad = GATE_PAD - 3 * HPG
    w_parts, b_parts = [w_in[:, idx]], [b_in[idx]]
    for g in range(N_KV):
        cols = jnp.asarray([NSA_GATE_OFF + (g * HPG + h) * 3 + j for j in range(3) for h in range(HPG)])
        w_parts += [w_in[:, cols], jnp.zeros((d_model, gpad), F32)]
        b_parts += [b_in[cols], jnp.zeros((gpad,), F32)]
    w = jnp.concatenate(w_parts + [w_in[:, idx2]], axis=1)
    b = jnp.concatenate(b_parts + [b_in[idx2]])
    return w.astype(BF16), b.reshape(1, -1).astype(F32)


def _in_proj_kernel(offs, x_ref, w_ref, b_ref, q_ref, slc_ref, win_ref, cmp_ref, g_ref, u_ref, m_ref):
    xb = x_ref[...].astype(BF16)

    def proj(i):
        c0, c1 = offs[i], offs[i + 1]
        return _dot(xb, w_ref[:, c0:c1]) + b_ref[:, c0:c1]

    q_ref[...] = (proj(0) * (HEAD_DIM ** -0.5 * LOG2_E)).astype(BF16)
    slc_ref[...] = proj(1).astype(BF16)
    win_ref[...] = proj(2).astype(BF16)
    cmp_ref[...] = proj(3)
    g_ref[...] = jax.nn.sigmoid(proj(4))
    u_ref[...] = proj(5)
    m_ref[...] = jax.nn.sigmoid(proj(6)).astype(BF16)


def _in_proj(x2d, w_packed, b_packed, d_model, tm):
    n = x2d.shape[0]
    widths, offs = _in_proj_layout(d_model)
    ncols = offs[-1]
    dtypes = (BF16, BF16, BF16, F32, F32, F32, BF16)
    return pl.pallas_call(
        functools.partial(_in_proj_kernel, tuple(offs)),
        grid=(n // tm,),
        in_specs=[
            pl.BlockSpec((tm, d_model), lambda i: (i, 0)),
            pl.BlockSpec((d_model, ncols), lambda i: (0, 0), pipeline_mode=pl.Buffered(1)),
            pl.BlockSpec((1, ncols), lambda i: (0, 0)),
        ],
        out_specs=[pl.BlockSpec((tm, w), lambda i: (i, 0)) for w in widths],
        out_shape=[jax.ShapeDtypeStruct((n, w), dt) for w, dt in zip(widths, dtypes)],
        compiler_params=_compiler_params(("parallel",)),
        name="in_proj",
    )(x2d, w_packed, b_packed)


def _compress_kernel(ck_ref, cv_ref, pos_ref, w1_ref, b1_ref, w2_ref, b2_ref, out_ref):
    n_c = ck_ref.shape[2]
    outs = []
    for i, c_ref in enumerate((ck_ref, cv_ref)):
        c = c_ref[0, 0]
        lo = (c + pos_ref[i, 0:1, :]).astype(BF16)
        hi = (c + pos_ref[i, 1:2, :]).astype(BF16)
        p_lo = _dot(lo, w1_ref[i, 0])
        p_hi = _dot(hi, w1_ref[i, 1])
        hid = p_lo + pltpu.roll(p_hi, n_c - 1, 0) + b1_ref[i]
        hid = _gelu_tanh(hid).astype(BF16)
        outs.append(_dot(hid, w2_ref[i]) + b2_ref[i])
    out_ref[0, 0] = jnp.concatenate(outs[::-1], axis=1).astype(BF16)


def _compress(cmp4, cmp_pos, cmp_w1, cmp_b1, cmp_w2, cmp_b2):
    b, _, n_c, cw = cmp4.shape
    half = CMP_STRIDE * HEAD_DIM
    pos = cmp_pos.reshape(2, 2, half).astype(F32)
    w1 = cmp_w1.reshape(2, 2, half, CMP_HIDDEN).astype(BF16)
    b1 = cmp_b1.reshape(2, 1, CMP_HIDDEN).astype(F32)
    w2 = cmp_w2.astype(BF16)
    b2 = cmp_b2.reshape(2, 1, HEAD_DIM).astype(F32)
    full = lambda shape: pl.BlockSpec(shape, lambda i, g: (0,) * len(shape))
    return pl.pallas_call(
        _compress_kernel,
        grid=(b, N_KV),
        in_specs=[
            pl.BlockSpec((1, 1, n_c, cw), lambda i, g: (i, g, 0, 0)),
            pl.BlockSpec((1, 1, n_c, cw), lambda i, g: (i, N_KV + g, 0, 0)),
            full((2, 2, half)),
            full((2, 2, half, CMP_HIDDEN)),
            full((2, 1, CMP_HIDDEN)),
            full((2, CMP_HIDDEN, HEAD_DIM)),
            full((2, 1, HEAD_DIM)),
        ],
        out_specs=pl.BlockSpec((1, 1, n_c, 2 * HEAD_DIM), lambda i, g: (i, g, 0, 0)),
        out_shape=jax.ShapeDtypeStruct((b, N_KV, n_c, 2 * HEAD_DIM), BF16),
        compiler_params=_compiler_params(("parallel", "parallel")),
        name="compress",
    )(cmp4, cmp4, pos, w1, b1, w2, b2)


def _t5_bucket(dist):
    n = jnp.maximum(dist, 0)
    max_exact = REL_BUCKETS // 2
    nf = jnp.maximum(n, 1).astype(F32)
    large = max_exact + (jnp.log(nf / max_exact) / math.log(REL_MAX_DIST / max_exact)
                         * (REL_BUCKETS - max_exact)).astype(jnp.int32)
    large = jnp.minimum(large, REL_BUCKETS - 1)
    return jnp.where(n < max_exact, n, large)


def _bucket_thresholds():
    buckets = _t5_bucket(jnp.arange(REL_MAX_DIST + 1))
    return jnp.sum(buckets[None, :] < jnp.arange(REL_BUCKETS)[:, None], axis=1).astype(jnp.int32)


def _bias_of_dist(dist, head, thr_ref, tbl_ref):
    bias = jnp.full(dist.shape, tbl_ref[head], F32)
    for k in range(1, REL_BUCKETS):
        bias = jnp.where(dist >= thr_ref[k], tbl_ref[k * N_HEADS + head], bias)
    return bias


BIAS_ROWS = 32


def _bias_c_kernel(thr_ref, tbl_ref, out_ref):
    _, tr, n_c = out_ref.shape
    r0 = pl.program_id(0) * tr

    def chunk(ci, carry):
        row0 = pl.multiple_of(ci * BIAS_ROWS, BIAS_ROWS)
        rows = pl.ds(row0, BIAS_ROWS)
        for c0 in range(0, n_c, LANES):
            width = min(LANES, n_c - c0)
            cols = slice(c0, c0 + width)
            pos = r0 + row0 + lax.broadcasted_iota(jnp.int32, (BIAS_ROWS, width), 0)
            key_end = ((c0 + lax.broadcasted_iota(jnp.int32, (BIAS_ROWS, width), 1)) * CMP_STRIDE
                       + (CMP_BLOCK - 1))
            dist = pos - key_end
            d_min = r0 + row0 - ((c0 + width - 1) * CMP_STRIDE + CMP_BLOCK - 1)
            d_max = r0 + row0 + (BIAS_ROWS - 1) - (c0 * CMP_STRIDE + CMP_BLOCK - 1)

            @pl.when(d_max < 0)
            def _():
                for h in range(N_HEADS):
                    out_ref[h, rows, cols] = jnp.full((BIAS_ROWS, width), NEG_INF, F32)

            @pl.when(d_min >= REL_MAX_DIST)
            def _():
                for h in range(N_HEADS):
                    out_ref[h, rows, cols] = jnp.full((BIAS_ROWS, width),
                                                      tbl_ref[(REL_BUCKETS - 1) * N_HEADS + h], F32)

            @pl.when((d_max >= 0) & (d_min < REL_MAX_DIST))
            def _():
                for h in range(N_HEADS):
                    bias = _bias_of_dist(dist, h, thr_ref, tbl_ref)
                    out_ref[h, rows, cols] = jnp.where(dist >= 0, bias, NEG_INF)
        return carry

    lax.fori_loop(0, tr // BIAS_ROWS, chunk, 0)


def _bias_near_kernel(thr_ref, tbl_ref, near_ref, win_ref):
    tq = ATTN_TQ
    h = pl.program_id(0)
    far_bias = tbl_ref[(REL_BUCKETS - 1) * N_HEADS + h]

    def table(out_ref, lo_keys, window, offset):
        width = out_ref.shape[2]

        def chunk(ci, carry):
            row0 = pl.multiple_of(ci * BIAS_ROWS, BIAS_ROWS)
            dist = (lo_keys + row0 + lax.broadcasted_iota(jnp.int32, (BIAS_ROWS, width), 0)
                    - lax.broadcasted_iota(jnp.int32, (BIAS_ROWS, width), 1))
            visible = (dist >= 0) & (dist < window)
            bias = jnp.full(dist.shape, tbl_ref[h], F32)
            for k in range(1, REL_BUCKETS):
                bias = jnp.where(dist >= thr_ref[k], tbl_ref[k * N_HEADS + h], bias)
            out_ref[0, pl.ds(row0, BIAS_ROWS), :] = jnp.where(visible, bias - offset, NEG_INF)
            return carry

        lax.fori_loop(0, tq // BIAS_ROWS, chunk, 0)

    table(near_ref, SLC_NEAR_BACK, 1 << 30, far_bias)
    table(win_ref, WINDOW, WINDOW, 0.0)


def _attention_bias_tables(rel_bias, seq):
    tbl = (rel_bias.astype(F32) * LOG2_E).reshape(REL_BUCKETS * N_HEADS)
    thr = _bucket_thresholds()
    tq = ATTN_TQ
    n_c = seq // CMP_STRIDE
    smem = pl.BlockSpec(memory_space=pltpu.SMEM)
    tr = min(512, seq)
    bias_c = pl.pallas_call(
        _bias_c_kernel,
        grid=(seq // tr,),
        in_specs=[smem, smem],
        out_specs=pl.BlockSpec((N_HEADS, tr, n_c), lambda i: (0, i, 0)),
        out_shape=jax.ShapeDtypeStruct((N_HEADS, seq, n_c), F32),
        compiler_params=_compiler_params(("parallel",)),
        name="bias_cmp",
    )(thr, tbl)
    head_block = lambda w: pl.BlockSpec((1, tq, w), lambda h: (h, 0, 0))
    widths = (SLC_NEAR_BACK + tq, WINDOW + tq)
    bias_near, bias_win = pl.pallas_call(
        _bias_near_kernel,
        grid=(N_HEADS,),
        in_specs=[smem, smem],
        out_specs=[head_block(w) for w in widths],
        out_shape=[jax.ShapeDtypeStruct((N_HEADS, tq, w), F32) for w in widths],
        compiler_params=_compiler_params(("parallel",)),
        name="bias_near",
    )(thr, tbl)
    return bias_c, bias_near, bias_win


def _nsa_constants(seq):
    n_c = seq // CMP_STRIDE
    n_blk = seq // SLC_BLOCK
    cmp_start = jnp.arange(n_c) * CMP_STRIDE
    blk_start = jnp.arange(n_blk) * SLC_BLOCK
    overlap_t = ((cmp_start[None, :] <= blk_start[:, None] + SLC_BLOCK - 1)
                 & (cmp_start[None, :] + CMP_BLOCK - 1 >= blk_start[:, None]))
    overlap_t = overlap_t & (cmp_start[None, :] + CMP_BLOCK <= seq)
    ones_rows = jnp.arange(SUBLANES)[:, None] == 0
    overlap_t = jnp.concatenate([overlap_t, jnp.broadcast_to(ones_rows, (SUBLANES, n_c))], axis=0)
    cmp_ones = jnp.broadcast_to(jnp.arange(LANES)[None, :] == 0, (n_c, LANES))
    return overlap_t.astype(BF16), cmp_ones.astype(BF16)


def _build_kv_scratch(seq, slc_ref, win_ref, g, ks_sc, vs_sc, kw_sc, vw_sc, knorm_sc):
    chunk = min(512, seq)
    lane_p = lax.broadcasted_iota(jnp.int32, (KV_PAD, LANES), 1)
    zeros = jnp.zeros((KV_PAD, LANES), BF16)
    ks_sc[0:KV_PAD] = jnp.where(lane_p >= HEAD_DIM, 1.0, 0.0).astype(BF16)
    kw_sc[0:KV_PAD] = jnp.where(lane_p == HEAD_DIM, NEG_INF, 0.0).astype(BF16)
    vs_sc[0:KV_PAD] = zeros
    vw_sc[0:KV_PAD] = zeros
    lane = lax.broadcasted_iota(jnp.int32, (chunk, LANES), 1)
    row = lax.broadcasted_iota(jnp.int32, (chunk, LANES), 0)
    lo_half = lane < HEAD_DIM
    ones_lane = jnp.where(lane == HEAD_DIM, 1.0, 0.0)
    k_sq_max = jnp.zeros((chunk, 1), F32)
    for c in range(seq // chunk):
        r0 = c * chunk
        dst = slice(KV_PAD + r0, KV_PAD + r0 + chunk)
        blk = jnp.right_shift(r0 + row, int(math.log2(SLC_BLOCK)))
        lanes_g = slice(g * LANES, (g + 1) * LANES)
        slab = slc_ref[0, r0:r0 + chunk, lanes_g].astype(F32)
        k_sq_max = jnp.maximum(k_sq_max, jnp.sum(jnp.where(lo_half, slab * slab, 0.0), axis=1, keepdims=True))
        ks_sc[dst] = jnp.where(lo_half, slab, jnp.where(lane - HEAD_DIM == blk, 1.0, 0.0)).astype(BF16)
        vs_sc[dst] = jnp.where(lo_half, pltpu.roll(slab, HEAD_DIM, 1), ones_lane).astype(BF16)
        slab = win_ref[0, r0:r0 + chunk, lanes_g].astype(F32)
        kw_sc[dst] = jnp.where(lo_half, slab, 0.0).astype(BF16)
        vw_sc[dst] = jnp.where(lo_half, pltpu.roll(slab, HEAD_DIM, 1), ones_lane).astype(BF16)
    knorm_sc[...] = jnp.broadcast_to(jnp.max(k_sq_max, axis=0, keepdims=True), knorm_sc.shape)


def _nsa_select(seq, g, q0, q_ref, vkc_ref, bias_c_ref, overlap_t_ref, cmp_ones_ref):
    tq = ATTN_TQ
    n_blk = seq // SLC_BLOCK
    n_sel = min(N_SEL, n_blk)
    rows = HPG * tq
    lane = lax.broadcasted_iota(jnp.int32, (tq, LANES), 1)
    lo_half = lane < HEAD_DIM
    lane_r = lax.broadcasted_iota(jnp.int32, (rows, LANES), 1)

    q_lo, q_hi = [], []
    for pair_idx in range(HPG // 2):
        lanes_p = slice((g * (HPG // 2) + pair_idx) * LANES, (g * (HPG // 2) + pair_idx + 1) * LANES)
        q2 = q_ref[0, :, lanes_p].astype(F32)
        q2r = pltpu.roll(q2, HEAD_DIM, 1)
        q_lo += [jnp.where(lo_half, q2, 0.0), jnp.where(lo_half, q2r, 0.0)]
        q_hi += [jnp.where(lo_half, 0.0, q2r), jnp.where(lo_half, 0.0, q2)]
    q_lo = jnp.concatenate(q_lo, axis=0)
    q_hi = jnp.concatenate(q_hi, axis=0)

    vkc = vkc_ref[0, g]
    s_c = _dot_nt(q_hi.astype(BF16), vkc) + bias_c_ref[g * HPG:(g + 1) * HPG].reshape(rows, -1)
    m_c = jnp.max(s_c, axis=1, keepdims=True)
    e_cb = jnp.exp2(s_c - m_c).astype(BF16)
    pv_c = _dot(e_cb, jnp.concatenate([vkc, cmp_ones_ref[...]], axis=1))
    row_pos = q0 + (lax.broadcasted_iota(jnp.int32, (rows, LANES), 0) & (tq - 1))
    has_key = row_pos >= CMP_BLOCK - 1
    o_cmp = jnp.where(has_key, pv_c[:, :LANES] / pv_c[:, LANES:LANES + 1], 0.0)

    imp_t4 = _dot_nt(overlap_t_ref[...], e_cb)
    imp_t = None
    for h in range(HPG):
        part = imp_t4[:, h * tq:(h + 1) * tq]
        part = part[:n_blk] / part[n_blk:n_blk + 1]
        imp_t = part if imp_t is None else imp_t + part

    blk = lax.broadcasted_iota(jnp.int32, (n_blk, tq), 0)
    pos = q0 + lax.broadcasted_iota(jnp.int32, (n_blk, tq), 1)
    cur = jnp.right_shift(pos, int(math.log2(SLC_BLOCK)))
    forced = (blk == 0) | (blk == cur) | (blk == cur - 1)
    valid = blk * SLC_BLOCK <= pos
    score = jnp.where(forced, BIG, jnp.where(valid, imp_t, -BIG))
    blk_f = blk.astype(F32)
    pen_t = jnp.full((n_blk, tq), NEG_INF, F32)
    for _ in range(n_sel):
        top = jnp.max(score, axis=0, keepdims=True)
        first = jnp.min(jnp.where(score == top, blk_f, float(n_blk)), axis=0, keepdims=True)
        hit = blk_f == first
        pen_t = jnp.where(hit, 0.0, pen_t)
        score = jnp.where(hit, -jnp.inf, score)
    pieces = [jnp.zeros((tq, HEAD_DIM), F32), jnp.transpose(pen_t)]
    if n_blk < LANES - HEAD_DIM:
        pieces.append(jnp.full((tq, LANES - HEAD_DIM - n_blk), NEG_INF, F32))
    pen_lanes = jnp.concatenate(pieces, axis=1)
    q_slc = jnp.where(lane_r < HEAD_DIM, q_lo, jnp.concatenate([pen_lanes] * HPG, axis=0)).astype(BF16)
    return o_cmp, q_lo, q_slc


def _nsa_kernel(seq, q_ref, vkc_ref, slc_ref, win_ref, gate_ref, bias_c_ref, bias_near_ref, bias_win_ref,
                overlap_t_ref, cmp_ones_ref, o_ref, *scratch):
    tq = ATTN_TQ
    rows = HPG * tq
    groups = range(N_KV)
    kv_sc = [scratch[4 * g:4 * g + 4] for g in groups]
    s_sc, mrun_sc, acc_sc, knorm_sc = (scratch[4 * N_KV + i * N_KV:4 * N_KV + (i + 1) * N_KV] for i in range(4))
    qt = pl.program_id(1)
    q0 = pl.multiple_of(qt * tq, tq)
    lane = lax.broadcasted_iota(jnp.int32, (tq, LANES), 1)
    lo_half = lane < HEAD_DIM
    lane_r = lax.broadcasted_iota(jnp.int32, (rows, LANES), 1)

    @pl.when(qt == 0)
    def _():
        for g in groups:
            _build_kv_scratch(seq, slc_ref, win_ref, g, *kv_sc[g], knorm_sc[g])

    sel = [_nsa_select(seq, g, q0, q_ref, vkc_ref, bias_c_ref, overlap_t_ref, cmp_ones_ref) for g in groups]
    o_cmp = [s[0] for s in sel]
    q_lo = [s[1] for s in sel]
    q_slc = [s[2] for s in sel]
    head_rows = lambda ref, g: ref[g * HPG:(g + 1) * HPG].reshape(rows, -1)

    o_win = []
    for g in groups:
        kw_sc, vw_sc = kv_sc[g][2], kv_sc[g][3]
        q_win = jnp.where(lane_r == HEAD_DIM, 1.0, q_lo[g]).astype(BF16)
        win_rows = pl.ds(q0, WINDOW + tq)
        s_w = _dot_nt(q_win, kw_sc[win_rows, :]) + head_rows(bias_win_ref, g)
        p_w = jnp.exp2(s_w - jnp.max(s_w, axis=1, keepdims=True)).astype(BF16)
        acc_w = _dot(p_w, vw_sc[win_rows, :])
        o_win.append(acc_w / acc_w[:, HEAD_DIM:HEAD_DIM + 1])

    tk = SLC_TK
    n_far = qt // (tk // tq)
    lane_tiles = lambda a: [a[:, j * LANES:(j + 1) * LANES] for j in range(a.shape[1] // LANES)]
    near0 = pl.multiple_of(q0 + (KV_PAD - SLC_NEAR_BACK), tq)

    def tile_rows(u):
        return pl.ds(pl.multiple_of(near0 - u * tk, tq), tk)

    def tile_cols(u):
        return pl.ds(pl.multiple_of(u * tk, tk), tk)

    def score_tile(u, g, bias):
        s = _dot_nt(q_slc[g], kv_sc[g][0][tile_rows(u), :])
        if bias is not None:
            s = s + bias
        s_sc[g][:, tile_cols(u)] = s
        return functools.reduce(jnp.maximum, lane_tiles(s))

    shift_cap = []
    slack = jnp.float32(-jnp.inf)
    for g in groups:
        lane_max = score_tile(0, g, head_rows(bias_near_ref, g))
        mrun_sc[g][...] = lane_max
        m_near = jnp.max(lane_max, axis=1, keepdims=True)
        q_norm = jnp.sqrt(jnp.sum(q_lo[g] * q_lo[g], axis=1, keepdims=True))
        far_bound = q_norm * jnp.sqrt(knorm_sc[g][0:1, 0:1]) * SOFTMAX_BOUND_SLACK
        shift_cap.append(jnp.maximum(m_near, far_bound - SOFTMAX_SHIFT_MARGIN))
        slack = jnp.maximum(slack, jnp.max(far_bound - m_near))
    single_pass = slack <= 2.0 * SOFTMAX_SHIFT_MARGIN

    def weigh(s, u, g):
        shift = mrun_sc[g][...]
        p = jnp.exp2(s - jnp.concatenate([shift] * (tk // LANES), axis=1))
        acc_sc[g][...] = acc_sc[g][...] + _dot(p.astype(BF16), kv_sc[g][1][tile_rows(u), :])

    for g in groups:
        acc_sc[g][...] = jnp.zeros((rows, LANES), F32)

    @pl.when(single_pass)
    def _():
        for g in groups:
            mrun_sc[g][...] = jnp.broadcast_to(shift_cap[g], (rows, LANES))
            weigh(s_sc[g][:, tile_cols(0)], 0, g)

        def fused(u, carry):
            for g in groups:
                weigh(_dot_nt(q_slc[g], kv_sc[g][0][tile_rows(u), :]), u, g)
            return carry

        lax.fori_loop(1, n_far + 1, fused, 0)

    @pl.when(jnp.logical_not(single_pass))
    def _():
        def pass1(u, carry):
            for g in groups:
                mrun_sc[g][...] = jnp.maximum(mrun_sc[g][...], score_tile(u, g, None))
            return carry

        lax.fori_loop(1, n_far + 1, pass1, 0)
        for g in groups:
            m_s = jnp.max(mrun_sc[g][...], axis=1, keepdims=True)
            mrun_sc[g][...] = jnp.broadcast_to(m_s, (rows, LANES))

        def pass2(u, carry):
            for g in groups:
                weigh(s_sc[g][:, tile_cols(u)], u, g)
            return carry

        lax.fori_loop(0, n_far + 1, pass2, 0)

    for g in groups:
        acc = acc_sc[g][...]
        o_slc = acc / acc[:, HEAD_DIM:HEAD_DIM + 1]

        gates = gate_ref[0, :, g * GATE_PAD:(g + 1) * GATE_PAD]
        outs = []
        for h in range(HPG):
            sl = slice(h * tq, (h + 1) * tq)
            g_c, g_s, g_w = (gates[:, br * HPG + h:br * HPG + h + 1] for br in range(3))
            outs.append(g_c * o_cmp[g][sl] + g_s * o_slc[sl] + g_w * o_win[g][sl])
        for pair_idx in range(HPG // 2):
            even, odd = outs[2 * pair_idx], outs[2 * pair_idx + 1]
            merged = jnp.where(lo_half, even, pltpu.roll(odd, HEAD_DIM, 1))
            lanes_p = slice((g * (HPG // 2) + pair_idx) * LANES, (g * (HPG // 2) + pair_idx + 1) * LANES)
            o_ref[0, :, lanes_p] = merged.astype(o_ref.dtype)


def _nsa(q, vkc, slc, win, gates, bias_c, bias_near, bias_win, overlap_t, cmp_ones):
    b, seq, _ = q.shape
    n_c = seq // CMP_STRIDE
    n_blk = seq // SLC_BLOCK
    assert n_blk <= LANES - HEAD_DIM and seq % SLC_TK == 0 and KV_PAD >= SLC_TK
    tq = ATTN_TQ
    rows = HPG * tq
    const = lambda a: pl.BlockSpec(a.shape, lambda i, t: (0,) * a.ndim)
    tile = lambda w: pl.BlockSpec((1, tq, w), lambda i, t: (i, t, 0))
    whole_seq = lambda w: pl.BlockSpec((1, seq, w), lambda i, t: (i, 0, 0))
    per_group = lambda shape: [pltpu.VMEM(shape, F32)] * N_KV
    return pl.pallas_call(
        functools.partial(_nsa_kernel, seq),
        grid=(b, seq // tq),
        in_specs=[
            tile(ATTN_WIDTH),
            pl.BlockSpec((1, N_KV, n_c, 2 * HEAD_DIM), lambda i, t: (i, 0, 0, 0)),
            whole_seq(N_KV * 2 * HEAD_DIM), whole_seq(N_KV * 2 * HEAD_DIM),
            tile(N_KV * GATE_PAD),
            pl.BlockSpec((N_HEADS, tq, n_c), lambda i, t: (0, t, 0)),
            const(bias_near), const(bias_win),
            const(overlap_t), const(cmp_ones),
        ],
        out_specs=tile(ATTN_WIDTH),
        out_shape=jax.ShapeDtypeStruct((b, seq, ATTN_WIDTH), BF16),
        scratch_shapes=(
            [pltpu.VMEM((KV_PAD + seq, LANES), BF16)] * (4 * N_KV)
            + per_group((rows, seq)) + per_group((rows, LANES)) + per_group((rows, LANES))
            + per_group((SUBLANES, LANES))
        ),
        compiler_params=_compiler_params(("parallel", "arbitrary")),
        name="nsa",
    )(q, vkc, slc, win, gates, bias_c, bias_near, bias_win, overlap_t, cmp_ones)


def _mixer_inputs(x, rel_bias, w_in, b_in, cmp_pos, cmp_w1, cmp_b1, cmp_w2, cmp_b2):
    b, seq, d = x.shape
    wp, bp = _pack_in_proj(w_in, b_in, d)
    q, slc, win, cmp, gates, u, merge = _in_proj(x.reshape(b * seq, d), wp, bp, d, 1024)
    n_c = seq // CMP_STRIDE
    cmp4 = cmp.reshape(b, seq, 2 * N_KV, HEAD_DIM).transpose(0, 2, 1, 3).reshape(
        b, 2 * N_KV, n_c, CMP_STRIDE * HEAD_DIM)
    vkc = _compress(cmp4, cmp_pos, cmp_w1, cmp_b1, cmp_w2, cmp_b2)
    bias_c, bias_near, bias_win = _attention_bias_tables(rel_bias, seq)
    overlap_t, cmp_ones = _nsa_constants(seq)
    o = _nsa(q.reshape(b, seq, -1), vkc, slc.reshape(b, seq, -1), win.reshape(b, seq, -1),
             gates.reshape(b, seq, -1), bias_c, bias_near, bias_win, overlap_t, cmp_ones)
    return o, u, merge


S5_HALF_GROUPS = S5_GROUPS // 2
S5_HALF_IN = S5_HALF_GROUPS * S5_GROUP
S5_HALF_STATE = S5_HALF_GROUPS * S5_STATE
S5_SCAN_LANES = 512
S5_CHUNK = 128
S5_UNROLL = 8


def _s5_params(lam_re, lam_im, log_dt, b_re, b_im, c_re, c_im, nb):
    dt = jnp.exp(log_dt.astype(F32))[:, None]
    lr, li = lam_re.astype(F32), lam_im.astype(F32)
    mag = jnp.exp(lr * dt)
    ab_re, ab_im = mag * jnp.cos(li * dt), mag * jnp.sin(li * dt)
    nr, ni = ab_re - 1.0, ab_im
    den = lr * lr + li * li
    fr, fi = (nr * lr + ni * li) / den, (ni * lr - nr * li) / den
    br, bim = b_re.astype(F32), b_im.astype(F32)
    bb_re = fr[..., None] * br - fi[..., None] * bim
    bb_im = fr[..., None] * bim + fi[..., None] * br
    eye = jnp.eye(S5_HALF_GROUPS, dtype=F32)

    def in_mat(bb):
        t = bb.reshape(2, S5_HALF_GROUPS, S5_STATE, S5_GROUP)
        m = jnp.einsum('kgph,gj->kghjp', t, eye)
        return m.reshape(2, S5_HALF_IN, S5_HALF_STATE)

    def out_mat(c):
        t = c.astype(F32).reshape(2, S5_HALF_GROUPS, S5_GROUP, S5_STATE)
        m = jnp.einsum('kghp,gj->kgpjh', t, eye)
        return m.reshape(2, S5_HALF_STATE, S5_HALF_IN)

    bmat = jnp.concatenate([in_mat(bb_re), in_mat(bb_im)], axis=2).astype(BF16)
    cmat = jnp.concatenate([out_mat(c_re), -out_mat(c_im)], axis=1).astype(BF16)
    a = jnp.concatenate([ab_re.reshape(2, S5_HALF_STATE), ab_im.reshape(2, S5_HALF_STATE)], axis=1)
    a = jnp.broadcast_to(a.reshape(1, 4 * S5_HALF_STATE), (nb, 4 * S5_HALF_STATE))
    return bmat, cmat, a


def _s5_kernel(u_ref, bmat_ref, cmat_ref, a_ref, d_ref, y_ref, ut_sc, x_sc, st_sc):
    nb, t_len, _ = u_ref.shape
    half_w = 2 * S5_HALF_STATE

    @pl.when(pl.program_id(0) == 0)
    def _():
        st_sc[...] = jnp.zeros_like(st_sc)

    n_cb = ut_sc.shape[0]
    for b in range(nb):
        for cb in range(n_cb):
            ut_sc[cb, pl.ds(b, t_len, stride=nb), :] = u_ref[b, :, cb * LANES:(cb + 1) * LANES]
    ut = jnp.concatenate([ut_sc[cb] for cb in range(n_cb)], axis=1)
    ub = ut.astype(BF16)
    for k in range(2):
        x_sc[:, k * half_w:(k + 1) * half_w] = _dot(ub[:, k * S5_HALF_IN:(k + 1) * S5_HALF_IN], bmat_ref[k])

    for k in range(2):
        for j in range(S5_HALF_STATE // S5_SCAN_LANES):
            re0 = k * half_w + j * S5_SCAN_LANES
            im0 = re0 + S5_HALF_STATE
            re_sl, im_sl = pl.ds(re0, S5_SCAN_LANES), pl.ds(im0, S5_SCAN_LANES)
            ar, ai = a_ref[:, re_sl], a_ref[:, im_sl]

            def steps(c, carry):
                xr, xi = carry
                for s in range(S5_UNROLL):
                    rows = pl.ds(pl.multiple_of((c * S5_UNROLL + s) * nb, nb), nb)
                    nxr = ar * xr - ai * xi + x_sc[rows, re_sl]
                    nxi = ar * xi + ai * xr + x_sc[rows, im_sl]
                    x_sc[rows, re_sl] = nxr
                    x_sc[rows, im_sl] = nxi
                    xr, xi = nxr, nxi
                return xr, xi

            xr, xi = lax.fori_loop(0, t_len // S5_UNROLL, steps, (st_sc[:, re_sl], st_sc[:, im_sl]))
            st_sc[:, re_sl] = xr
            st_sc[:, im_sl] = xi

    xs = x_sc[...].astype(BF16)
    y = jnp.concatenate([_dot(xs[:, k * half_w:(k + 1) * half_w], cmat_ref[k]) for k in range(2)], axis=1)
    y = _gelu_tanh(y + d_ref[...] * ut)
    for cb in range(n_cb):
        ut_sc[cb] = y[:, cb * LANES:(cb + 1) * LANES]
    for b in range(nb):
        for cb in range(n_cb):
            y_ref[b, :, cb * LANES:(cb + 1) * LANES] = ut_sc[cb, pl.ds(b, t_len, stride=nb), :].astype(BF16)


def _s5(u, bmat, cmat, a, d_skip):
    nb, seq, w = u.shape
    t_len = min(S5_CHUNK, seq)
    full = lambda shape: pl.BlockSpec(shape, lambda c: (0,) * len(shape))
    return pl.pallas_call(
        _s5_kernel,
        grid=(seq // t_len,),
        in_specs=[
            pl.BlockSpec((nb, t_len, w), lambda c: (0, c, 0)),
            full(bmat.shape), full(cmat.shape), full(a.shape), full((1, w)),
        ],
        out_specs=pl.BlockSpec((nb, t_len, w), lambda c: (0, c, 0)),
        out_shape=jax.ShapeDtypeStruct((nb, seq, w), BF16),
        scratch_shapes=[
            pltpu.VMEM((w // LANES, t_len * nb, LANES), F32),
            pltpu.VMEM((t_len * nb, 4 * S5_HALF_STATE), F32),
            pltpu.VMEM((nb, 4 * S5_HALF_STATE), F32),
        ],
        compiler_params=_compiler_params(("arbitrary",)),
        name="s5",
    )(u, bmat, cmat, a, d_skip.reshape(1, w).astype(F32))


ROUTE_PAD = LANES
_R_E1, _R_E2, _R_W1, _R_W2, _R_RANK1, _R_RANK2 = range(6)


def _layer_norm(t, g, b):
    mu = jnp.mean(t, axis=1, keepdims=True)
    c = t - mu
    var = jnp.mean(c * c, axis=1, keepdims=True)
    return c * lax.rsqrt(var + LN_EPS) * g + b


def _post_kernel(x_ref, o_ref, y_ref, m_ref, wup_ref, wval_ref, wgate_ref, bgate_ref, wout_ref,
                 g1_ref, b1_ref, wr_ref, br_ref, h_ref, hp_ref, route_ref, cnt_ref, run_sc):
    tm, d = x_ref.shape

    @pl.when(pl.program_id(0) == 0)
    def _():
        run_sc[...] = jnp.zeros_like(run_sc)

    y_a = _dot(o_ref[...], wup_ref[...])
    z = y_ref[...]
    y_b = _dot(z, wval_ref[...]) * jax.nn.sigmoid(_dot(z, wgate_ref[...]) + bgate_ref[...])
    mixed = m_ref[:, :d].astype(F32) * y_a + m_ref[:, d:].astype(F32) * y_b
    t = DN_ALPHA * x_ref[...] + _dot(mixed.astype(BF16), wout_ref[...])
    h = _layer_norm(t, g1_ref[...], b1_ref[...])
    h_ref[...] = h
    hp_ref[...] = _pack_rows(h)

    logits = _dot(h.astype(BF16), wr_ref[...]) + br_ref[...]
    lane = lax.broadcasted_iota(jnp.int32, (tm, ROUTE_PAD), 1)
    lane_f = lane.astype(F32)
    is_group = lane < N_EGROUPS

    def first_max(v):
        top = jnp.max(v, axis=1, keepdims=True)
        idx = jnp.min(jnp.where(v == top, lane_f, float(ROUTE_PAD)), axis=1, keepdims=True)
        return top, idx

    g_max, g_top = first_max(jnp.where(is_group, logits, -jnp.inf))
    p_group = 1.0 / jnp.sum(jnp.where(is_group, jnp.exp(logits - g_max), 0.0), axis=1, keepdims=True)
    grp_of_lane = jnp.right_shift(lane - N_EGROUPS, int(math.log2(EXPERTS_PER_GROUP))).astype(F32)
    in_group = (lane >= N_EGROUPS) & (lane < N_EGROUPS + N_EXPERTS) & (grp_of_lane == g_top)
    e_log = jnp.where(in_group, logits, -jnp.inf)
    v1, i1 = first_max(e_log)
    hit1 = lane_f == i1
    v2, i2 = first_max(jnp.where(hit1, -jnp.inf, e_log))
    hit2 = lane_f == i2
    e2 = jnp.exp(v2 - v1)
    w1 = p_group / (1.0 + e2)
    w2 = p_group * e2 / (1.0 + e2)

    hits = jnp.where(hit1 | hit2, 1.0, 0.0)
    row = lax.broadcasted_iota(jnp.int32, (tm, tm), 0)
    col = lax.broadcasted_iota(jnp.int32, (tm, tm), 1)
    earlier = jnp.where(col < row, 1.0, 0.0).astype(BF16)
    before = _dot(earlier, hits.astype(BF16)) + run_sc[...]
    rank1 = jnp.sum(jnp.where(hit1, before, 0.0), axis=1, keepdims=True)
    rank2 = jnp.sum(jnp.where(hit2, before, 0.0), axis=1, keepdims=True)
    run_sc[...] = run_sc[...] + jnp.sum(hits, axis=0, keepdims=True)
    cnt_ref[...] = run_sc[...]

    rec = jnp.zeros((tm, ROUTE_PAD), F32)
    for slot, val in ((_R_E1, i1 - N_EGROUPS), (_R_E2, i2 - N_EGROUPS), (_R_W1, w1), (_R_W2, w2),
                      (_R_RANK1, rank1), (_R_RANK2, rank2)):
        rec = jnp.where(lane == slot, val, rec)
    route_ref[...] = rec


def _post(x2d, o2d, y2d, merge, w_attn_up, s5_w_val, s5_w_gate, s5_b_gate, w_out, ln1_g, ln1_b,
          router_w_group, router_b_group, router_w_expert, router_b_expert, tm):
    n, d = x2d.shape
    rpad = ROUTE_PAD - N_EGROUPS - N_EXPERTS
    wr = jnp.concatenate([router_w_group, router_w_expert, jnp.zeros((d, rpad), F32)], axis=1).astype(BF16)
    br = jnp.concatenate([router_b_group, router_b_expert, jnp.zeros((rpad,), F32)]).reshape(1, -1).astype(F32)
    row = lambda w: pl.BlockSpec((tm, w), lambda i: (i, 0))
    full = lambda a: pl.BlockSpec(a.shape, lambda i: (0,) * a.ndim, pipeline_mode=pl.Buffered(1))
    weights = [w_attn_up.astype(BF16), s5_w_val.astype(BF16), s5_w_gate.astype(BF16),
               s5_b_gate.reshape(1, d).astype(F32), w_out.astype(BF16),
               ln1_g.reshape(1, d).astype(F32), ln1_b.reshape(1, d).astype(F32), wr, br]
    return pl.pallas_call(
        _post_kernel,
        grid=(n // tm,),
        in_specs=[row(d), row(ATTN_WIDTH), row(S5_WIDTH), row(2 * d)] + [full(w) for w in weights],
        out_specs=[row(d), row(d // 2), row(ROUTE_PAD), pl.BlockSpec((1, ROUTE_PAD), lambda i: (0, 0))],
        out_shape=[jax.ShapeDtypeStruct((n, d), F32), jax.ShapeDtypeStruct((n, d // 2), jnp.uint32),
                   jax.ShapeDtypeStruct((n, ROUTE_PAD), F32), jax.ShapeDtypeStruct((1, ROUTE_PAD), F32)],
        scratch_shapes=[pltpu.VMEM((1, ROUTE_PAD), F32)],
        compiler_params=_compiler_params(("arbitrary",)),
        name="post_mixer",
    )(x2d, o2d, y2d, merge, *weights)


def _plan_kernel(route_ref, cnt_ref, dest_ref):
    tm = route_ref.shape[0]
    lane8 = lax.broadcasted_iota(jnp.int32, (SUBLANES, ROUTE_PAD), 1)
    counts = jnp.broadcast_to(cnt_ref[...], (SUBLANES, ROUTE_PAD)).astype(jnp.int32)
    shift = int(math.log2(MOE_BLOCK))
    padded = jnp.left_shift(jnp.right_shift(counts + (MOE_BLOCK - 1), shift), shift)
    incl = padded
    step = 1
    while step < ROUTE_PAD:
        incl = incl + jnp.where(lane8 >= step, pltpu.roll(incl, step, 1), 0)
        step *= 2
    pstart = (incl - padded)[0:1].astype(F32)
    route = route_ref[...]
    expert_of_lane = (lax.broadcasted_iota(jnp.int32, (tm, ROUTE_PAD), 1) - N_EGROUPS).astype(F32)
    lane = lax.broadcasted_iota(jnp.int32, (tm, ROUTE_PAD), 1)

    def dest(e_slot, rank_slot):
        hit = expert_of_lane == route[:, e_slot:e_slot + 1]
        return jnp.sum(jnp.where(hit, pstart, 0.0), axis=1, keepdims=True) + route[:, rank_slot:rank_slot + 1]

    d1 = dest(_R_E1, _R_RANK1)
    d2 = dest(_R_E2, _R_RANK2)
    dest_ref[...] = jnp.where(lane == 0, d1, jnp.where(lane == 1, d2, 0.0)).astype(jnp.int32)


def _plan(route, counts_row):
    n = route.shape[0]
    tm = min(4096, n)
    return pl.pallas_call(
        _plan_kernel,
        grid=(n // tm,),
        in_specs=[pl.BlockSpec((tm, ROUTE_PAD), lambda i: (i, 0)),
                  pl.BlockSpec((1, ROUTE_PAD), lambda i: (0, 0))],
        out_specs=pl.BlockSpec((tm, ROUTE_PAD), lambda i: (i, 0)),
        out_shape=jax.ShapeDtypeStruct((n, ROUTE_PAD), jnp.int32),
        compiler_params=_compiler_params(("parallel",)),
        name="moe_plan",
    )(route, counts_row)


SC_GATHER_ROWS = 128


def _sc_row_gather(table, idx):
    n_idx = idx.shape[0]
    d = table.shape[1]
    info = plsc.get_sparse_core_info()
    n_workers = info.num_cores * info.num_subcores
    per_worker = n_idx // n_workers
    assert n_idx % (n_workers * SC_GATHER_ROWS) == 0
    mesh = plsc.VectorSubcoreMesh(core_axis_name="c", subcore_axis_name="s")

    @functools.partial(
        pl.kernel, mesh=mesh,
        out_type=jax.ShapeDtypeStruct((n_idx, d), table.dtype),
        scratch_types=[
            pltpu.VMEM((SC_GATHER_ROWS,), jnp.int32),
            pltpu.VMEM((SC_GATHER_ROWS, d), table.dtype),
            pltpu.SemaphoreType.DMA,
        ],
    )
    def gather(table_hbm, idx_hbm, out_hbm, idx_v, rows_v, sem):
        worker = lax.axis_index("s") * info.num_cores + lax.axis_index("c")
        base = worker * per_worker

        @pl.loop(0, per_worker // SC_GATHER_ROWS)
        def _(j):
            off = base + j * SC_GATHER_ROWS
            pltpu.sync_copy(idx_hbm.at[pl.ds(off, SC_GATHER_ROWS)], idx_v)
            pltpu.async_copy(table_hbm.at[idx_v], rows_v, sem).wait()
            pltpu.sync_copy(rows_v, out_hbm.at[pl.ds(off, SC_GATHER_ROWS)])

    return gather(table, idx)


def _sc_row_scatter(rows, idx_a, idx_b, n_out):
    n, d = rows.shape
    info = plsc.get_sparse_core_info()
    n_workers = info.num_cores * info.num_subcores
    per_worker = n // n_workers
    assert n % (n_workers * SC_GATHER_ROWS) == 0
    mesh = plsc.VectorSubcoreMesh(core_axis_name="c", subcore_axis_name="s")

    @functools.partial(
        pl.kernel, mesh=mesh,
        out_type=jax.ShapeDtypeStruct((n_out, d), rows.dtype),
        scratch_types=[
            pltpu.VMEM((SC_GATHER_ROWS,), jnp.int32),
            pltpu.VMEM((SC_GATHER_ROWS,), jnp.int32),
            pltpu.VMEM((SC_GATHER_ROWS, d), rows.dtype),
        ],
    )
    def scatter(rows_hbm, idx_a_hbm, idx_b_hbm, out_hbm, idx_a_v, idx_b_v, rows_v):
        worker = lax.axis_index("s") * info.num_cores + lax.axis_index("c")
        base = worker * per_worker

        @pl.loop(0, per_worker // SC_GATHER_ROWS)
        def _(j):
            src = pl.ds(base + j * SC_GATHER_ROWS, SC_GATHER_ROWS)
            pltpu.sync_copy(rows_hbm.at[src], rows_v)
            pltpu.sync_copy(idx_a_hbm.at[src], idx_a_v)
            pltpu.sync_copy(idx_b_hbm.at[src], idx_b_v)
            pltpu.sync_copy(rows_v, out_hbm.at[idx_a_v])
            pltpu.sync_copy(rows_v, out_hbm.at[idx_b_v])

    return scatter(rows, idx_a, idx_b)


def _expert_kernel(blk_exp_ref, blk_valid_ref, x_ref, wg_ref, wu_ref, wd_ref, y_ref, wg_sc, wu_sc, wd_sc):
    i = pl.program_id(0)
    n_valid = blk_valid_ref[i]

    @pl.when((i == 0) | (blk_exp_ref[i] != blk_exp_ref[jnp.maximum(i - 1, 0)]))
    def _():
        wg_sc[...] = wg_ref[0].astype(BF16)
        wu_sc[...] = wu_ref[0].astype(BF16)
        wd_sc[...] = wd_ref[0].astype(BF16)

    @pl.when(n_valid > 0)
    def _():
        row = lax.broadcasted_iota(jnp.int32, x_ref.shape, 0)
        words = jnp.where(row < n_valid, x_ref[...], jnp.uint32(0))
        xb = _unpack_rows(words).astype(BF16)
        h_gate = _dot(xb, wg_sc[...])
        h_up = _dot(xb, wu_sc[...])
        hb = (h_gate * jax.nn.sigmoid(h_gate) * h_up).astype(BF16)
        y_ref[...] = _pack_rows(_dot(hb, wd_sc[...]))

    @pl.when(n_valid == 0)
    def _():
        y_ref[...] = jnp.zeros_like(y_ref)


def _experts(blk_expert, blk_valid, xs, w_gate, w_up, w_down):
    n_blocks = blk_expert.shape[0]
    d = 2 * xs.shape[1]
    grid_spec = pltpu.PrefetchScalarGridSpec(
        num_scalar_prefetch=2,
        grid=(n_blocks,),
        in_specs=[
            pl.BlockSpec((MOE_BLOCK, d // 2), lambda i, be, bv: (i, 0)),
            pl.BlockSpec((1, d, D_EXPERT), lambda i, be, bv: (be[i], 0, 0)),
            pl.BlockSpec((1, d, D_EXPERT), lambda i, be, bv: (be[i], 0, 0)),
            pl.BlockSpec((1, D_EXPERT, d), lambda i, be, bv: (be[i], 0, 0)),
        ],
        out_specs=pl.BlockSpec((MOE_BLOCK, d // 2), lambda i, be, bv: (i, 0)),
        scratch_shapes=[
            pltpu.VMEM((d, D_EXPERT), BF16),
            pltpu.VMEM((d, D_EXPERT), BF16),
            pltpu.VMEM((D_EXPERT, d), BF16),
        ],
    )
    return pl.pallas_call(
        _expert_kernel,
        grid_spec=grid_spec,
        out_shape=jax.ShapeDtypeStruct((n_blocks * MOE_BLOCK, d // 2), jnp.uint32),
        compiler_params=_compiler_params(("arbitrary",)),
        name="experts",
    )(blk_expert, blk_valid, xs, w_gate, w_up, w_down)


COMBINE_TM = 512


def _combine_kernel(h_ref, y1_ref, y2_ref, route_ref, g2_ref, b2_ref, out_ref):
    route = route_ref[...]
    w1 = route[:, _R_W1:_R_W1 + 1]
    w2 = route[:, _R_W2:_R_W2 + 1]
    t = DN_ALPHA * h_ref[...] + (_unpack_rows(y1_ref[...]) * w1 + _unpack_rows(y2_ref[...]) * w2)
    out_ref[...] = _layer_norm(t, g2_ref[...], b2_ref[...])


def _combine(yg, h2d, route, ln2_g, ln2_b):
    n, d = h2d.shape
    tm = min(COMBINE_TM, n)
    n_tiles = n // tm
    row = lambda w: pl.BlockSpec((tm, w), lambda i: (i, 0))
    vec = pl.BlockSpec((1, d), lambda i: (0, 0))
    return pl.pallas_call(
        _combine_kernel,
        grid=(n_tiles,),
        in_specs=[row(d), row(d // 2), pl.BlockSpec((tm, d // 2), lambda i: (i + n_tiles, 0)), row(ROUTE_PAD),
                  vec, vec],
        out_specs=row(d),
        out_shape=jax.ShapeDtypeStruct((n, d), F32),
        compiler_params=_compiler_params(("parallel",)),
        name="combine",
    )(h2d, yg, yg, route, ln2_g.reshape(1, d).astype(F32), ln2_b.reshape(1, d).astype(F32))


def _moe(h2d, h_packed, route, counts_row, w_gate, w_up, w_down, ln2_g, ln2_b):
    n, d = h2d.shape
    dest = _plan(route, counts_row)
    dest1, dest2 = dest[:, 0], dest[:, 1]
    counts = counts_row[0, N_EGROUPS:N_EGROUPS + N_EXPERTS].astype(jnp.int32)
    padded = (counts + MOE_BLOCK - 1) // MOE_BLOCK * MOE_BLOCK
    pend = jnp.cumsum(padded)
    n_blocks = -(-(n * TOP_K_IN_GROUP) // MOE_BLOCK) + N_EXPERTS
    blk_row0 = (jnp.arange(n_blocks, dtype=jnp.int32) * MOE_BLOCK)[:, None]
    owns = (pend - padded <= blk_row0) & (blk_row0 < pend)
    blk_expert = jnp.sum(jnp.where(owns, jnp.arange(N_EXPERTS, dtype=jnp.int32), 0), axis=1)
    blk_valid = jnp.sum(jnp.where(owns, jnp.clip(pend - padded + counts - blk_row0, 0, MOE_BLOCK), 0), axis=1)
    xs = _sc_row_scatter(h_packed, dest1, dest2, n_blocks * MOE_BLOCK)
    yb = _experts(blk_expert, blk_valid, xs, w_gate, w_up, w_down)
    yg = _sc_row_gather(yb, jnp.concatenate([dest1, dest2]))
    return _combine(yg, h2d, route, ln2_g, ln2_b)


def kernel(x, rel_bias, w_in, b_in, cmp_pos, cmp_w1, cmp_b1, cmp_w2, cmp_b2, w_attn_up, s5_lambda_re, s5_lambda_im, s5_log_dt, s5_b_re, s5_b_im, s5_c_re, s5_c_im, s5_d, s5_w_val, s5_w_gate, s5_b_gate, w_out, ln1_g, ln1_b, router_w_group, router_b_group, router_w_expert, router_b_expert, exp_w_gate, exp_w_up, exp_w_down, ln2_g, ln2_b):
    b, seq, d = x.shape
    n = b * seq
    assert w_in.shape[0] == DEPTH
    l = 0
    o, u, merge = _mixer_inputs(x, rel_bias, w_in[l], b_in[l], cmp_pos[l], cmp_w1[l], cmp_b1[l],
                                cmp_w2[l], cmp_b2[l])
    bmat, cmat, a = _s5_params(s5_lambda_re[l], s5_lambda_im[l], s5_log_dt[l], s5_b_re[l], s5_b_im[l],
                               s5_c_re[l], s5_c_im[l], b)
    y_s = _s5(u.reshape(b, seq, S5_WIDTH), bmat, cmat, a, s5_d[l])
    h2d, h_packed, route, counts = _post(
        x.reshape(n, d), o.reshape(n, ATTN_WIDTH), y_s.reshape(n, S5_WIDTH), merge, w_attn_up[l], s5_w_val[l],
        s5_w_gate[l], s5_b_gate[l], w_out[l], ln1_g[l], ln1_b[l], router_w_group[l], router_b_group[l],
        router_w_expert[l], router_b_expert[l], 1024)
    out = _moe(h2d, h_packed, route, counts, exp_w_gate[l], exp_w_up[l], exp_w_down[l], ln2_g[l], ln2_b[l])
    return out.reshape(b, seq, d)
```

```python
import functools
import math

import jax
import jax.numpy as jnp
from jax import lax
from jax.experimental import pallas as pl
from jax.experimental.pallas import tpu as pltpu
from jax.experimental.pallas import tpu_sc as plsc

F32 = jnp.float32
BF16 = jnp.bfloat16

N_HEADS = 8
HEAD_DIM = 64
N_KV = 2
HPG = N_HEADS // N_KV
CMP_STRIDE = 16
CMP_BLOCK = 2 * CMP_STRIDE
CMP_HIDDEN = 128
SLC_BLOCK = 64
N_SEL = 16
WINDOW = 512
REL_BUCKETS = 32
REL_MAX_DIST = 128
S5_WIDTH = 512
S5_GROUP = 16
S5_GROUPS = S5_WIDTH // S5_GROUP
S5_STATE = 64
N_EGROUPS = 8
EXPERTS_PER_GROUP = 8
N_EXPERTS = N_EGROUPS * EXPERTS_PER_GROUP
TOP_K_IN_GROUP = 2
D_EXPERT = 256
EXPERT_BLOCK = 128
DEPTH = 1
DN_ALPHA = (2.0 * DEPTH) ** 0.25
LN_EPS = 1e-5
NEG_INF = -1e30
BIG = 1e9
LOG2_E = math.log2(math.e)
MOE_BLOCK = 512

ATTN_WIDTH = N_HEADS * HEAD_DIM
KV_WIDTH = N_KV * HEAD_DIM
KV_OFF = ATTN_WIDTH
NSA_GATE_OFF = KV_OFF + 6 * KV_WIDTH
S5_OFF = NSA_GATE_OFF + 3 * N_HEADS
MERGE_OFF = S5_OFF + S5_WIDTH

LANES = 128
SUBLANES = 8
V7X_VMEM_BYTES = 64 * 1024 * 1024
VMEM_LIMIT_BYTES = V7X_VMEM_BYTES - 8 * 1024 * 1024

ATTN_TQ = 128
SLC_TK = 512
SLC_NEAR_BACK = SLC_TK - ATTN_TQ
KV_PAD = WINDOW
SOFTMAX_SHIFT_MARGIN = 100.0
SOFTMAX_BOUND_SLACK = 1.001
GATE_PAD = LANES


def _gelu_tanh(x):
    c = math.sqrt(2.0 / math.pi)
    return x * (0.5 * (1.0 + jnp.tanh(c * (x + 0.044715 * (x * x * x)))))


def _dot(a, b):
    return jnp.dot(a, b, preferred_element_type=F32)


def _dot_nt(a, b):
    return lax.dot_general(a, b, (((1,), (1,)), ((), ())), preferred_element_type=F32)


def _pack_rows(x):
    half = x.shape[1] // 2
    xb = x.astype(BF16).astype(F32)
    hi = lax.bitcast_convert_type(xb[:, :half], jnp.uint32) & jnp.uint32(0xFFFF0000)
    lo = lax.shift_right_logical(lax.bitcast_convert_type(xb[:, half:], jnp.uint32), jnp.uint32(16))
    return hi | lo


def _unpack_rows(w):
    hi = lax.bitcast_convert_type(w & jnp.uint32(0xFFFF0000), F32)
    lo = lax.bitcast_convert_type(lax.shift_left(w, jnp.uint32(16)), F32)
    return jnp.concatenate([hi, lo], axis=1)


def _compiler_params(semantics):
    return pltpu.CompilerParams(dimension_semantics=semantics, vmem_limit_bytes=VMEM_LIMIT_BYTES)


def _in_proj_layout(d_model):
    widths = (ATTN_WIDTH, 2 * KV_WIDTH, 2 * KV_WIDTH, 2 * KV_WIDTH, N_KV * GATE_PAD, S5_WIDTH, 2 * d_model)
    offs = [0]
    for w in widths:
        offs.append(offs[-1] + w)
    return widths, offs


def _pack_in_proj(w_in, b_in, d_model):
    def kv_cols(j):
        return KV_OFF + j * KV_WIDTH

    def pair(jk, jv):
        cols = []
        for g in range(N_KV):
            cols.append(jnp.arange(kv_cols(jk) + g * HEAD_DIM, kv_cols(jk) + (g + 1) * HEAD_DIM))
            cols.append(jnp.arange(kv_cols(jv) + g * HEAD_DIM, kv_cols(jv) + (g + 1) * HEAD_DIM))
        return jnp.concatenate(cols)

    idx = jnp.concatenate([
        jnp.arange(0, ATTN_WIDTH),
        pair(2, 3),
        pair(4, 5),
        jnp.arange(kv_cols(0), kv_cols(2)),
    ])
    idx2 = jnp.concatenate([jnp.arange(S5_OFF, S5_OFF + S5_WIDTH),
                            jnp.arange(MERGE_OFF, MERGE_OFF + 2 * d_model)])
    gpad = GATE_PAD - 3 * HPG
    w_parts, b_parts = [w_in[:, idx]], [b_in[idx]]
    for g in range(N_KV):
        cols = jnp.asarray([NSA_GATE_OFF + (g * HPG + h) * 3 + j for j in range(3) for h in range(HPG)])
        w_parts += [w_in[:, cols], jnp.zeros((d_model, gpad), F32)]
        b_parts += [b_in[cols], jnp.zeros((gpad,), F32)]
    w = jnp.concatenate(w_parts + [w_in[:, idx2]], axis=1)
    b = jnp.concatenate(b_parts + [b_in[idx2]])
    return w.astype(BF16), b.reshape(1, -1).astype(F32)


def _in_proj_kernel(offs, x_ref, w_ref, b_ref, q_ref, slc_ref, win_ref, cmp_ref, g_ref, u_ref, m_ref, cmp_sc):
    xb = x_ref[...].astype(BF16)

    def proj(i):
        c0, c1 = offs[i], offs[i + 1]
        return _dot(xb, w_ref[:, c0:c1]) + b_ref[:, c0:c1]

    q_ref[...] = (proj(0) * (HEAD_DIM ** -0.5 * LOG2_E)).astype(BF16)
    slc_ref[...] = proj(1).astype(BF16)
    win_ref[...] = proj(2).astype(BF16)
    g_ref[...] = jax.nn.sigmoid(proj(4))
    u_ref[...] = proj(5)
    m_ref[...] = jax.nn.sigmoid(proj(6)).astype(BF16)

    c = proj(3)
    n_chunks = c.shape[0] // CMP_STRIDE
    lo_half = lax.broadcasted_iota(jnp.int32, (n_chunks, LANES), 1) < HEAD_DIM
    for p in range(c.shape[1] // LANES):
        cmp_sc[p] = c[:, p * LANES:(p + 1) * LANES]
        for t in range(CMP_STRIDE // 2):
            tok_a = cmp_sc[p, pl.ds(2 * t, n_chunks, stride=CMP_STRIDE), :]
            tok_b = cmp_sc[p, pl.ds(2 * t + 1, n_chunks, stride=CMP_STRIDE), :]
            lanes_t = slice(t * LANES, (t + 1) * LANES)
            cmp_ref[0, 2 * p, :, lanes_t] = jnp.where(lo_half, tok_a, pltpu.roll(tok_b, HEAD_DIM, 1))
            cmp_ref[0, 2 * p + 1, :, lanes_t] = jnp.where(lo_half, pltpu.roll(tok_a, HEAD_DIM, 1), tok_b)


def _in_proj(x2d, w_packed, b_packed, d_model, tm, seq):
    n = x2d.shape[0]
    widths, offs = _in_proj_layout(d_model)
    ncols = offs[-1]
    dtypes = (BF16, BF16, BF16, F32, F32, F32, BF16)
    tiles_per_seq = seq // tm
    assert seq % tm == 0 and tm % (CMP_STRIDE * SUBLANES) == 0
    planes = widths[3] // HEAD_DIM
    cmp_spec = pl.BlockSpec((1, planes, tm // CMP_STRIDE, CMP_STRIDE * HEAD_DIM),
                            lambda i: (i // tiles_per_seq, 0, i % tiles_per_seq, 0))
    cmp_shape = jax.ShapeDtypeStruct((n // seq, planes, seq // CMP_STRIDE, CMP_STRIDE * HEAD_DIM), F32)
    row_spec = lambda w: pl.BlockSpec((tm, w), lambda i: (i, 0))
    return pl.pallas_call(
        functools.partial(_in_proj_kernel, tuple(offs)),
        grid=(n // tm,),
        in_specs=[
            pl.BlockSpec((tm, d_model), lambda i: (i, 0)),
            pl.BlockSpec((d_model, ncols), lambda i: (0, 0), pipeline_mode=pl.Buffered(1)),
            pl.BlockSpec((1, ncols), lambda i: (0, 0)),
        ],
        out_specs=[cmp_spec if j == 3 else row_spec(w) for j, w in enumerate(widths)],
        out_shape=[cmp_shape if j == 3 else jax.ShapeDtypeStruct((n, w), dt)
                   for j, (w, dt) in enumerate(zip(widths, dtypes))],
        scratch_shapes=[pltpu.VMEM((widths[3] // LANES, tm, LANES), F32)],
        compiler_params=_compiler_params(("parallel",)),
        name="in_proj",
    )(x2d, w_packed, b_packed)


def _compress_kernel(ck_ref, cv_ref, pos_ref, w1_ref, b1_ref, w2_ref, b2_ref, out_ref):
    n_c = ck_ref.shape[2]
    outs = []
    for i, c_ref in enumerate((ck_ref, cv_ref)):
        c = c_ref[0, 0]
        lo = (c + pos_ref[i, 0:1, :]).astype(BF16)
        hi = (c + pos_ref[i, 1:2, :]).astype(BF16)
        p_lo = _dot(lo, w1_ref[i, 0])
        p_hi = _dot(hi, w1_ref[i, 1])
        hid = p_lo + pltpu.roll(p_hi, n_c - 1, 0) + b1_ref[i]
        hid = _gelu_tanh(hid).astype(BF16)
        outs.append(_dot(hid, w2_ref[i]) + b2_ref[i])
    out_ref[0, 0] = jnp.concatenate(outs[::-1], axis=1).astype(BF16)


def _compress(cmp4, cmp_pos, cmp_w1, cmp_b1, cmp_w2, cmp_b2):
    b, _, n_c, cw = cmp4.shape
    half = CMP_STRIDE * HEAD_DIM
    pos = cmp_pos.reshape(2, 2, half).astype(F32)
    w1 = cmp_w1.reshape(2, 2, half, CMP_HIDDEN).astype(BF16)
    b1 = cmp_b1.reshape(2, 1, CMP_HIDDEN).astype(F32)
    w2 = cmp_w2.astype(BF16)
    b2 = cmp_b2.reshape(2, 1, HEAD_DIM).astype(F32)
    full = lambda shape: pl.BlockSpec(shape, lambda i, g: (0,) * len(shape))
    return pl.pallas_call(
        _compress_kernel,
        grid=(b, N_KV),
        in_specs=[
            pl.BlockSpec((1, 1, n_c, cw), lambda i, g: (i, g, 0, 0)),
            pl.BlockSpec((1, 1, n_c, cw), lambda i, g: (i, N_KV + g, 0, 0)),
            full((2, 2, half)),
            full((2, 2, half, CMP_HIDDEN)),
            full((2, 1, CMP_HIDDEN)),
            full((2, CMP_HIDDEN, HEAD_DIM)),
            full((2, 1, HEAD_DIM)),
        ],
        out_specs=pl.BlockSpec((1, 1, n_c, 2 * HEAD_DIM), lambda i, g: (i, g, 0, 0)),
        out_shape=jax.ShapeDtypeStruct((b, N_KV, n_c, 2 * HEAD_DIM), BF16),
        compiler_params=_compiler_params(("parallel", "parallel")),
        name="compress",
    )(cmp4, cmp4, pos, w1, b1, w2, b2)


def _t5_bucket(dist):
    n = jnp.maximum(dist, 0)
    max_exact = REL_BUCKETS // 2
    nf = jnp.maximum(n, 1).astype(F32)
    large = max_exact + (jnp.log(nf / max_exact) / math.log(REL_MAX_DIST / max_exact)
                         * (REL_BUCKETS - max_exact)).astype(jnp.int32)
    large = jnp.minimum(large, REL_BUCKETS - 1)
    return jnp.where(n < max_exact, n, large)


def _bucket_thresholds():
    buckets = _t5_bucket(jnp.arange(REL_MAX_DIST + 1))
    return jnp.sum(buckets[None, :] < jnp.arange(REL_BUCKETS)[:, None], axis=1).astype(jnp.int32)


def _bias_of_dist(dist, head, thr_ref, tbl_ref):
    bias = jnp.full(dist.shape, tbl_ref[head], F32)
    for k in range(1, REL_BUCKETS):
        bias = jnp.where(dist >= thr_ref[k], tbl_ref[k * N_HEADS + head], bias)
    return bias


BIAS_ROWS = 32


def _bias_c_kernel(thr_ref, tbl_ref, out_ref):
    _, tr, n_c = out_ref.shape
    r0 = pl.program_id(0) * tr

    def chunk(ci, carry):
        row0 = pl.multiple_of(ci * BIAS_ROWS, BIAS_ROWS)
        rows = pl.ds(row0, BIAS_ROWS)
        for c0 in range(0, n_c, LANES):
            width = min(LANES, n_c - c0)
            cols = slice(c0, c0 + width)
            pos = r0 + row0 + lax.broadcasted_iota(jnp.int32, (BIAS_ROWS, width), 0)
            key_end = ((c0 + lax.broadcasted_iota(jnp.int32, (BIAS_ROWS, width), 1)) * CMP_STRIDE
                       + (CMP_BLOCK - 1))
            dist = pos - key_end
            d_min = r0 + row0 - ((c0 + width - 1) * CMP_STRIDE + CMP_BLOCK - 1)
            d_max = r0 + row0 + (BIAS_ROWS - 1) - (c0 * CMP_STRIDE + CMP_BLOCK - 1)

            @pl.when(d_max < 0)
            def _():
                for h in range(N_HEADS):
                    out_ref[h, rows, cols] = jnp.full((BIAS_ROWS, width), NEG_INF, F32)

            @pl.when(d_min >= REL_MAX_DIST)
            def _():
                for h in range(N_HEADS):
                    out_ref[h, rows, cols] = jnp.full((BIAS_ROWS, width),
                                                      tbl_ref[(REL_BUCKETS - 1) * N_HEADS + h], F32)

            @pl.when((d_max >= 0) & (d_min < REL_MAX_DIST))
            def _():
                for h in range(N_HEADS):
                    bias = _bias_of_dist(dist, h, thr_ref, tbl_ref)
                    out_ref[h, rows, cols] = jnp.where(dist >= 0, bias, NEG_INF)
        return carry

    lax.fori_loop(0, tr // BIAS_ROWS, chunk, 0)


def _bias_near_kernel(thr_ref, tbl_ref, near_ref, win_ref):
    tq = ATTN_TQ
    h = pl.program_id(0)
    far_bias = tbl_ref[(REL_BUCKETS - 1) * N_HEADS + h]

    def table(out_ref, lo_keys, window, offset):
        width = out_ref.shape[2]

        def chunk(ci, carry):
            row0 = pl.multiple_of(ci * BIAS_ROWS, BIAS_ROWS)
            dist = (lo_keys + row0 + lax.broadcasted_iota(jnp.int32, (BIAS_ROWS, width), 0)
                    - lax.broadcasted_iota(jnp.int32, (BIAS_ROWS, width), 1))
            visible = (dist >= 0) & (dist < window)
            bias = jnp.full(dist.shape, tbl_ref[h], F32)
            for k in range(1, REL_BUCKETS):
                bias = jnp.where(dist >= thr_ref[k], tbl_ref[k * N_HEADS + h], bias)
            out_ref[0, pl.ds(row0, BIAS_ROWS), :] = jnp.where(visible, bias - offset, NEG_INF)
            return carry

        lax.fori_loop(0, tq // BIAS_ROWS, chunk, 0)

    table(near_ref, SLC_NEAR_BACK, 1 << 30, far_bias)
    table(win_ref, WINDOW, WINDOW, 0.0)


def _attention_bias_tables(rel_bias, seq):
    tbl = (rel_bias.astype(F32) * LOG2_E).reshape(REL_BUCKETS * N_HEADS)
    thr = _bucket_thresholds()
    tq = ATTN_TQ
    n_c = seq // CMP_STRIDE
    smem = pl.BlockSpec(memory_space=pltpu.SMEM)
    tr = min(512, seq)
    bias_c = pl.pallas_call(
        _bias_c_kernel,
        grid=(seq // tr,),
        in_specs=[smem, smem],
        out_specs=pl.BlockSpec((N_HEADS, tr, n_c), lambda i: (0, i, 0)),
        out_shape=jax.ShapeDtypeStruct((N_HEADS, seq, n_c), F32),
        compiler_params=_compiler_params(("parallel",)),
        name="bias_cmp",
    )(thr, tbl)
    head_block = lambda w: pl.BlockSpec((1, tq, w), lambda h: (h, 0, 0))
    widths = (SLC_NEAR_BACK + tq, WINDOW + tq)
    bias_near, bias_win = pl.pallas_call(
        _bias_near_kernel,
        grid=(N_HEADS,),
        in_specs=[smem, smem],
        out_specs=[head_block(w) for w in widths],
        out_shape=[jax.ShapeDtypeStruct((N_HEADS, tq, w), F32) for w in widths],
        compiler_params=_compiler_params(("parallel",)),
        name="bias_near",
    )(thr, tbl)
    return bias_c, bias_near, bias_win


def _nsa_constants(seq):
    n_c = seq // CMP_STRIDE
    n_blk = seq // SLC_BLOCK
    cmp_start = jnp.arange(n_c) * CMP_STRIDE
    blk_start = jnp.arange(n_blk) * SLC_BLOCK
    overlap_t = ((cmp_start[None, :] <= blk_start[:, None] + SLC_BLOCK - 1)
                 & (cmp_start[None, :] + CMP_BLOCK - 1 >= blk_start[:, None]))
    overlap_t = overlap_t & (cmp_start[None, :] + CMP_BLOCK <= seq)
    ones_rows = jnp.arange(SUBLANES)[:, None] == 0
    overlap_t = jnp.concatenate([overlap_t, jnp.broadcast_to(ones_rows, (SUBLANES, n_c))], axis=0)
    cmp_ones = jnp.broadcast_to(jnp.arange(LANES)[None, :] == 0, (n_c, LANES))
    return overlap_t.astype(BF16), cmp_ones.astype(BF16)


def _build_kv_scratch(seq, slc_ref, win_ref, g, ks_sc, vs_sc, kw_sc, vw_sc, knorm_sc):
    chunk = min(512, seq)
    lane_p = lax.broadcasted_iota(jnp.int32, (KV_PAD, LANES), 1)
    zeros = jnp.zeros((KV_PAD, LANES), BF16)
    ks_sc[0:KV_PAD] = jnp.where(lane_p >= HEAD_DIM, 1.0, 0.0).astype(BF16)
    kw_sc[0:KV_PAD] = jnp.where(lane_p == HEAD_DIM, NEG_INF, 0.0).astype(BF16)
    vs_sc[0:KV_PAD] = zeros
    vw_sc[0:KV_PAD] = zeros
    lane = lax.broadcasted_iota(jnp.int32, (chunk, LANES), 1)
    row = lax.broadcasted_iota(jnp.int32, (chunk, LANES), 0)
    lo_half = lane < HEAD_DIM
    ones_lane = jnp.where(lane == HEAD_DIM, 1.0, 0.0)
    k_sq_max = jnp.zeros((chunk, 1), F32)
    for c in range(seq // chunk):
        r0 = c * chunk
        dst = slice(KV_PAD + r0, KV_PAD + r0 + chunk)
        blk = jnp.right_shift(r0 + row, int(math.log2(SLC_BLOCK)))
        lanes_g = slice(g * LANES, (g + 1) * LANES)
        slab = slc_ref[0, r0:r0 + chunk, lanes_g].astype(F32)
        k_sq_max = jnp.maximum(k_sq_max, jnp.sum(jnp.where(lo_half, slab * slab, 0.0), axis=1, keepdims=True))
        ks_sc[dst] = jnp.where(lo_half, slab, jnp.where(lane - HEAD_DIM == blk, 1.0, 0.0)).astype(BF16)
        vs_sc[dst] = jnp.where(lo_half, pltpu.roll(slab, HEAD_DIM, 1), ones_lane).astype(BF16)
        slab = win_ref[0, r0:r0 + chunk, lanes_g].astype(F32)
        kw_sc[dst] = jnp.where(lo_half, slab, 0.0).astype(BF16)
        vw_sc[dst] = jnp.where(lo_half, pltpu.roll(slab, HEAD_DIM, 1), ones_lane).astype(BF16)
    knorm_sc[...] = jnp.broadcast_to(jnp.max(k_sq_max, axis=0, keepdims=True), knorm_sc.shape)


def _nsa_select(seq, g, q0, q_ref, vkc_ref, bias_c_ref, overlap_t_ref, cmp_ones_ref):
    tq = ATTN_TQ
    n_blk = seq // SLC_BLOCK
    n_sel = min(N_SEL, n_blk)
    rows = HPG * tq
    lane = lax.broadcasted_iota(jnp.int32, (tq, LANES), 1)
    lo_half = lane < HEAD_DIM
    lane_r = lax.broadcasted_iota(jnp.int32, (rows, LANES), 1)

    q_lo, q_hi = [], []
    for pair_idx in range(HPG // 2):
        lanes_p = slice((g * (HPG // 2) + pair_idx) * LANES, (g * (HPG // 2) + pair_idx + 1) * LANES)
        q2 = q_ref[0, :, lanes_p].astype(F32)
        q2r = pltpu.roll(q2, HEAD_DIM, 1)
        q_lo += [jnp.where(lo_half, q2, 0.0), jnp.where(lo_half, q2r, 0.0)]
        q_hi += [jnp.where(lo_half, 0.0, q2r), jnp.where(lo_half, 0.0, q2)]
    q_lo = jnp.concatenate(q_lo, axis=0)
    q_hi = jnp.concatenate(q_hi, axis=0)

    vkc = vkc_ref[0, g]
    s_c = _dot_nt(q_hi.astype(BF16), vkc) + bias_c_ref[g * HPG:(g + 1) * HPG].reshape(rows, -1)
    m_c = jnp.max(s_c, axis=1, keepdims=True)
    e_cb = jnp.exp2(s_c - m_c).astype(BF16)
    pv_c = _dot(e_cb, jnp.concatenate([vkc, cmp_ones_ref[...]], axis=1))
    row_pos = q0 + (lax.broadcasted_iota(jnp.int32, (rows, LANES), 0) & (tq - 1))
    has_key = row_pos >= CMP_BLOCK - 1
    o_cmp = jnp.where(has_key, pv_c[:, :LANES] / pv_c[:, LANES:LANES + 1], 0.0)

    imp_t4 = _dot_nt(overlap_t_ref[...], e_cb)
    imp_t = None
    for h in range(HPG):
        part = imp_t4[:, h * tq:(h + 1) * tq]
        part = part[:n_blk] / part[n_blk:n_blk + 1]
        imp_t = part if imp_t is None else imp_t + part

    blk = lax.broadcasted_iota(jnp.int32, (n_blk, tq), 0)
    pos = q0 + lax.broadcasted_iota(jnp.int32, (n_blk, tq), 1)
    cur = jnp.right_shift(pos, int(math.log2(SLC_BLOCK)))
    forced = (blk == 0) | (blk == cur) | (blk == cur - 1)
    valid = blk * SLC_BLOCK <= pos
    score = jnp.where(forced, BIG, jnp.where(valid, imp_t, -BIG))
    blk_f = blk.astype(F32)
    pen_t = jnp.full((n_blk, tq), NEG_INF, F32)
    for _ in range(n_sel):
        top = jnp.max(score, axis=0, keepdims=True)
        first = jnp.min(jnp.where(score == top, blk_f, float(n_blk)), axis=0, keepdims=True)
        hit = blk_f == first
        pen_t = jnp.where(hit, 0.0, pen_t)
        score = jnp.where(hit, -jnp.inf, score)
    pieces = [jnp.zeros((tq, HEAD_DIM), F32), jnp.transpose(pen_t)]
    if n_blk < LANES - HEAD_DIM:
        pieces.append(jnp.full((tq, LANES - HEAD_DIM - n_blk), NEG_INF, F32))
    pen_lanes = jnp.concatenate(pieces, axis=1)
    q_slc = jnp.where(lane_r < HEAD_DIM, q_lo, jnp.concatenate([pen_lanes] * HPG, axis=0)).astype(BF16)
    return o_cmp, q_lo, q_slc


def _nsa_kernel(seq, q_ref, vkc_ref, slc_ref, win_ref, gate_ref, bias_c_ref, bias_near_ref, bias_win_ref,
                overlap_t_ref, cmp_ones_ref, o_ref, *scratch):
    tq = ATTN_TQ
    rows = HPG * tq
    groups = range(N_KV)
    kv_sc = [scratch[4 * g:4 * g + 4] for g in groups]
    s_sc, mrun_sc, acc_sc, knorm_sc = (scratch[4 * N_KV + i * N_KV:4 * N_KV + (i + 1) * N_KV] for i in range(4))
    qt = pl.program_id(1)
    q0 = pl.multiple_of(qt * tq, tq)
    lane = lax.broadcasted_iota(jnp.int32, (tq, LANES), 1)
    lo_half = lane < HEAD_DIM
    lane_r = lax.broadcasted_iota(jnp.int32, (rows, LANES), 1)

    @pl.when(qt == 0)
    def _():
        for g in groups:
            _build_kv_scratch(seq, slc_ref, win_ref, g, *kv_sc[g], knorm_sc[g])

    sel = [_nsa_select(seq, g, q0, q_ref, vkc_ref, bias_c_ref, overlap_t_ref, cmp_ones_ref) for g in groups]
    o_cmp = [s[0] for s in sel]
    q_lo = [s[1] for s in sel]
    q_slc = [s[2] for s in sel]
    head_rows = lambda ref, g: ref[g * HPG:(g + 1) * HPG].reshape(rows, -1)

    o_win = []
    for g in groups:
        kw_sc, vw_sc = kv_sc[g][2], kv_sc[g][3]
        q_win = jnp.where(lane_r == HEAD_DIM, 1.0, q_lo[g]).astype(BF16)
        win_rows = pl.ds(q0, WINDOW + tq)
        s_w = _dot_nt(q_win, kw_sc[win_rows, :]) + head_rows(bias_win_ref, g)
        p_w = jnp.exp2(s_w - jnp.max(s_w, axis=1, keepdims=True)).astype(BF16)
        acc_w = _dot(p_w, vw_sc[win_rows, :])
        o_win.append(acc_w / acc_w[:, HEAD_DIM:HEAD_DIM + 1])

    tk = SLC_TK
    n_far = qt // (tk // tq)
    lane_tiles = lambda a: [a[:, j * LANES:(j + 1) * LANES] for j in range(a.shape[1] // LANES)]
    near0 = pl.multiple_of(q0 + (KV_PAD - SLC_NEAR_BACK), tq)

    def tile_rows(u):
        return pl.ds(pl.multiple_of(near0 - u * tk, tq), tk)

    def tile_cols(u):
        return pl.ds(pl.multiple_of(u * tk, tk), tk)

    def score_tile(u, g, bias):
        s = _dot_nt(q_slc[g], kv_sc[g][0][tile_rows(u), :])
        if bias is not None:
            s = s + bias
        s_sc[g][:, tile_cols(u)] = s
        return functools.reduce(jnp.maximum, lane_tiles(s))

    shift_cap = []
    slack = jnp.float32(-jnp.inf)
    for g in groups:
        lane_max = score_tile(0, g, head_rows(bias_near_ref, g))
        mrun_sc[g][...] = lane_max
        m_near = jnp.max(lane_max, axis=1, keepdims=True)
        q_norm = jnp.sqrt(jnp.sum(q_lo[g] * q_lo[g], axis=1, keepdims=True))
        far_bound = q_norm * jnp.sqrt(knorm_sc[g][0:1, 0:1]) * SOFTMAX_BOUND_SLACK
        shift_cap.append(jnp.maximum(m_near, far_bound - SOFTMAX_SHIFT_MARGIN))
        slack = jnp.maximum(slack, jnp.max(far_bound - m_near))
    single_pass = slack <= 2.0 * SOFTMAX_SHIFT_MARGIN

    def weigh(s, u, g):
        shift = mrun_sc[g][...]
        p = jnp.exp2(s - jnp.concatenate([shift] * (tk // LANES), axis=1))
        acc_sc[g][...] = acc_sc[g][...] + _dot(p.astype(BF16), kv_sc[g][1][tile_rows(u), :])

    for g in groups:
        acc_sc[g][...] = jnp.zeros((rows, LANES), F32)

    @pl.when(single_pass)
    def _():
        for g in groups:
            mrun_sc[g][...] = jnp.broadcast_to(shift_cap[g], (rows, LANES))
            weigh(s_sc[g][:, tile_cols(0)], 0, g)

        def fused(u, carry):
            for g in groups:
                weigh(_dot_nt(q_slc[g], kv_sc[g][0][tile_rows(u), :]), u, g)
            return carry

        lax.fori_loop(1, n_far + 1, fused, 0)

    @pl.when(jnp.logical_not(single_pass))
    def _():
        def pass1(u, carry):
            for g in groups:
                mrun_sc[g][...] = jnp.maximum(mrun_sc[g][...], score_tile(u, g, None))
            return carry

        lax.fori_loop(1, n_far + 1, pass1, 0)
        for g in groups:
            m_s = jnp.max(mrun_sc[g][...], axis=1, keepdims=True)
            mrun_sc[g][...] = jnp.broadcast_to(m_s, (rows, LANES))

        def pass2(u, carry):
            for g in groups:
                weigh(s_sc[g][:, tile_cols(u)], u, g)
            return carry

        lax.fori_loop(0, n_far + 1, pass2, 0)

    for g in groups:
        acc = acc_sc[g][...]
        o_slc = acc / acc[:, HEAD_DIM:HEAD_DIM + 1]

        gates = gate_ref[0, :, g * GATE_PAD:(g + 1) * GATE_PAD]
        outs = []
        for h in range(HPG):
            sl = slice(h * tq, (h + 1) * tq)
            g_c, g_s, g_w = (gates[:, br * HPG + h:br * HPG + h + 1] for br in range(3))
            outs.append(g_c * o_cmp[g][sl] + g_s * o_slc[sl] + g_w * o_win[g][sl])
        for pair_idx in range(HPG // 2):
            even, odd = outs[2 * pair_idx], outs[2 * pair_idx + 1]
            merged = jnp.where(lo_half, even, pltpu.roll(odd, HEAD_DIM, 1))
            lanes_p = slice((g * (HPG // 2) + pair_idx) * LANES, (g * (HPG // 2) + pair_idx + 1) * LANES)
            o_ref[0, :, lanes_p] = merged.astype(o_ref.dtype)


def _nsa(q, vkc, slc, win, gates, bias_c, bias_near, bias_win, overlap_t, cmp_ones):
    b, seq, _ = q.shape
    n_c = seq // CMP_STRIDE
    n_blk = seq // SLC_BLOCK
    assert n_blk <= LANES - HEAD_DIM and seq % SLC_TK == 0 and KV_PAD >= SLC_TK
    tq = ATTN_TQ
    rows = HPG * tq
    const = lambda a: pl.BlockSpec(a.shape, lambda i, t: (0,) * a.ndim)
    tile = lambda w: pl.BlockSpec((1, tq, w), lambda i, t: (i, t, 0))
    whole_seq = lambda w: pl.BlockSpec((1, seq, w), lambda i, t: (i, 0, 0))
    per_group = lambda shape: [pltpu.VMEM(shape, F32)] * N_KV
    return pl.pallas_call(
        functools.partial(_nsa_kernel, seq),
        grid=(b, seq // tq),
        in_specs=[
            tile(ATTN_WIDTH),
            pl.BlockSpec((1, N_KV, n_c, 2 * HEAD_DIM), lambda i, t: (i, 0, 0, 0)),
            whole_seq(N_KV * 2 * HEAD_DIM), whole_seq(N_KV * 2 * HEAD_DIM),
            tile(N_KV * GATE_PAD),
            pl.BlockSpec((N_HEADS, tq, n_c), lambda i, t: (0, t, 0)),
            const(bias_near), const(bias_win),
            const(overlap_t), const(cmp_ones),
        ],
        out_specs=tile(ATTN_WIDTH),
        out_shape=jax.ShapeDtypeStruct((b, seq, ATTN_WIDTH), BF16),
        scratch_shapes=(
            [pltpu.VMEM((KV_PAD + seq, LANES), BF16)] * (4 * N_KV)
            + per_group((rows, seq)) + per_group((rows, LANES)) + per_group((rows, LANES))
            + per_group((SUBLANES, LANES))
        ),
        compiler_params=_compiler_params(("parallel", "arbitrary")),
        name="nsa",
    )(q, vkc, slc, win, gates, bias_c, bias_near, bias_win, overlap_t, cmp_ones)


def _mixer_inputs(x, rel_bias, w_in, b_in, cmp_pos, cmp_w1, cmp_b1, cmp_w2, cmp_b2):
    b, seq, d = x.shape
    wp, bp = _pack_in_proj(w_in, b_in, d)
    q, slc, win, cmp4, gates, u, merge = _in_proj(x.reshape(b * seq, d), wp, bp, d, min(1024, seq), seq)
    vkc = _compress(cmp4, cmp_pos, cmp_w1, cmp_b1, cmp_w2, cmp_b2)
    bias_c, bias_near, bias_win = _attention_bias_tables(rel_bias, seq)
    overlap_t, cmp_ones = _nsa_constants(seq)
    o = _nsa(q.reshape(b, seq, -1), vkc, slc.reshape(b, seq, -1), win.reshape(b, seq, -1),
             gates.reshape(b, seq, -1), bias_c, bias_near, bias_win, overlap_t, cmp_ones)
    return o, u, merge


S5_HALF_GROUPS = S5_GROUPS // 2
S5_HALF_IN = S5_HALF_GROUPS * S5_GROUP
S5_HALF_STATE = S5_HALF_GROUPS * S5_STATE
S5_SCAN_LANES = 512
S5_CHUNK = 128
S5_UNROLL = 8


def _s5_params(lam_re, lam_im, log_dt, b_re, b_im, c_re, c_im, nb):
    dt = jnp.exp(log_dt.astype(F32))[:, None]
    lr, li = lam_re.astype(F32), lam_im.astype(F32)
    mag = jnp.exp(lr * dt)
    ab_re, ab_im = mag * jnp.cos(li * dt), mag * jnp.sin(li * dt)
    nr, ni = ab_re - 1.0, ab_im
    den = lr * lr + li * li
    fr, fi = (nr * lr + ni * li) / den, (ni * lr - nr * li) / den
    br, bim = b_re.astype(F32), b_im.astype(F32)
    bb_re = fr[..., None] * br - fi[..., None] * bim
    bb_im = fr[..., None] * bim + fi[..., None] * br
    eye = jnp.eye(S5_HALF_GROUPS, dtype=F32)

    def in_mat(bb):
        t = bb.reshape(2, S5_HALF_GROUPS, S5_STATE, S5_GROUP)
        m = jnp.einsum('kgph,gj->kghjp', t, eye)
        return m.reshape(2, S5_HALF_IN, S5_HALF_STATE)

    def out_mat(c):
        t = c.astype(F32).reshape(2, S5_HALF_GROUPS, S5_GROUP, S5_STATE)
        m = jnp.einsum('kghp,gj->kgpjh', t, eye)
        return m.reshape(2, S5_HALF_STATE, S5_HALF_IN)

    bmat = jnp.concatenate([in_mat(bb_re), in_mat(bb_im)], axis=2).astype(BF16)
    cmat = jnp.concatenate([out_mat(c_re), -out_mat(c_im)], axis=1).astype(BF16)
    a = jnp.concatenate([ab_re.reshape(2, S5_HALF_STATE), ab_im.reshape(2, S5_HALF_STATE)], axis=1)
    a = jnp.broadcast_to(a.reshape(1, 4 * S5_HALF_STATE), (nb, 4 * S5_HALF_STATE))
    return bmat, cmat, a


def _s5_kernel(u_ref, bmat_ref, cmat_ref, a_ref, d_ref, y_ref, ut_sc, x_sc, st_sc):
    nb, t_len, _ = u_ref.shape
    half_w = 2 * S5_HALF_STATE

    @pl.when(pl.program_id(0) == 0)
    def _():
        st_sc[...] = jnp.zeros_like(st_sc)

    n_cb = ut_sc.shape[0]
    for b in range(nb):
        for cb in range(n_cb):
            ut_sc[cb, pl.ds(b, t_len, stride=nb), :] = u_ref[b, :, cb * LANES:(cb + 1) * LANES]
    ut = jnp.concatenate([ut_sc[cb] for cb in range(n_cb)], axis=1)
    ub = ut.astype(BF16)
    for k in range(2):
        x_sc[:, k * half_w:(k + 1) * half_w] = _dot(ub[:, k * S5_HALF_IN:(k + 1) * S5_HALF_IN], bmat_ref[k])

    for k in range(2):
        for j in range(S5_HALF_STATE // S5_SCAN_LANES):
            re0 = k * half_w + j * S5_SCAN_LANES
            im0 = re0 + S5_HALF_STATE
            re_sl, im_sl = pl.ds(re0, S5_SCAN_LANES), pl.ds(im0, S5_SCAN_LANES)
            ar, ai = a_ref[:, re_sl], a_ref[:, im_sl]

            def steps(c, carry):
                xr, xi = carry
                for s in range(S5_UNROLL):
                    rows = pl.ds(pl.multiple_of((c * S5_UNROLL + s) * nb, nb), nb)
                    nxr = ar * xr - ai * xi + x_sc[rows, re_sl]
                    nxi = ar * xi + ai * xr + x_sc[rows, im_sl]
                    x_sc[rows, re_sl] = nxr
                    x_sc[rows, im_sl] = nxi
                    xr, xi = nxr, nxi
                return xr, xi

            xr, xi = lax.fori_loop(0, t_len // S5_UNROLL, steps, (st_sc[:, re_sl], st_sc[:, im_sl]))
            st_sc[:, re_sl] = xr
            st_sc[:, im_sl] = xi

    xs = x_sc[...].astype(BF16)
    y = jnp.concatenate([_dot(xs[:, k * half_w:(k + 1) * half_w], cmat_ref[k]) for k in range(2)], axis=1)
    y = _gelu_tanh(y + d_ref[...] * ut)
    for cb in range(n_cb):
        ut_sc[cb] = y[:, cb * LANES:(cb + 1) * LANES]
    for b in range(nb):
        for cb in range(n_cb):
            y_ref[b, :, cb * LANES:(cb + 1) * LANES] = ut_sc[cb, pl.ds(b, t_len, stride=nb), :].astype(BF16)


def _s5(u, bmat, cmat, a, d_skip):
    nb, seq, w = u.shape
    t_len = min(S5_CHUNK, seq)
    full = lambda shape: pl.BlockSpec(shape, lambda c: (0,) * len(shape))
    return pl.pallas_call(
        _s5_kernel,
        grid=(seq // t_len,),
        in_specs=[
            pl.BlockSpec((nb, t_len, w), lambda c: (0, c, 0)),
            full(bmat.shape), full(cmat.shape), full(a.shape), full((1, w)),
        ],
        out_specs=pl.BlockSpec((nb, t_len, w), lambda c: (0, c, 0)),
        out_shape=jax.ShapeDtypeStruct((nb, seq, w), BF16),
        scratch_shapes=[
            pltpu.VMEM((w // LANES, t_len * nb, LANES), F32),
            pltpu.VMEM((t_len * nb, 4 * S5_HALF_STATE), F32),
            pltpu.VMEM((nb, 4 * S5_HALF_STATE), F32),
        ],
        compiler_params=_compiler_params(("arbitrary",)),
        name="s5",
    )(u, bmat, cmat, a, d_skip.reshape(1, w).astype(F32))


ROUTE_PAD = LANES
_R_E1, _R_E2, _R_W1, _R_W2, _R_RANK1, _R_RANK2 = range(6)


def _layer_norm(t, g, b):
    mu = jnp.mean(t, axis=1, keepdims=True)
    c = t - mu
    var = jnp.mean(c * c, axis=1, keepdims=True)
    return c * lax.rsqrt(var + LN_EPS) * g + b


def _post_kernel(x_ref, o_ref, y_ref, m_ref, wup_ref, wval_ref, wgate_ref, bgate_ref, wout_ref,
                 g1_ref, b1_ref, wr_ref, br_ref, h_ref, hp_ref, route_ref, cnt_ref, run_sc):
    tm, d = x_ref.shape

    @pl.when(pl.program_id(0) == 0)
    def _():
        run_sc[...] = jnp.zeros_like(run_sc)

    y_a = _dot(o_ref[...], wup_ref[...])
    z = y_ref[...]
    y_b = _dot(z, wval_ref[...]) * jax.nn.sigmoid(_dot(z, wgate_ref[...]) + bgate_ref[...])
    mixed = m_ref[:, :d].astype(F32) * y_a + m_ref[:, d:].astype(F32) * y_b
    t = DN_ALPHA * x_ref[...] + _dot(mixed.astype(BF16), wout_ref[...])
    h = _layer_norm(t, g1_ref[...], b1_ref[...])
    h_ref[...] = h
    hp_ref[...] = _pack_rows(h)

    logits = _dot(h.astype(BF16), wr_ref[...]) + br_ref[...]
    lane = lax.broadcasted_iota(jnp.int32, (tm, ROUTE_PAD), 1)
    lane_f = lane.astype(F32)
    is_group = lane < N_EGROUPS

    def first_max(v):
        top = jnp.max(v, axis=1, keepdims=True)
        idx = jnp.min(jnp.where(v == top, lane_f, float(ROUTE_PAD)), axis=1, keepdims=True)
        return top, idx

    g_max, g_top = first_max(jnp.where(is_group, logits, -jnp.inf))
    p_group = 1.0 / jnp.sum(jnp.where(is_group, jnp.exp(logits - g_max), 0.0), axis=1, keepdims=True)
    grp_of_lane = jnp.right_shift(lane - N_EGROUPS, int(math.log2(EXPERTS_PER_GROUP))).astype(F32)
    in_group = (lane >= N_EGROUPS) & (lane < N_EGROUPS + N_EXPERTS) & (grp_of_lane == g_top)
    e_log = jnp.where(in_group, logits, -jnp.inf)
    v1, i1 = first_max(e_log)
    hit1 = lane_f == i1
    v2, i2 = first_max(jnp.where(hit1, -jnp.inf, e_log))
    hit2 = lane_f == i2
    e2 = jnp.exp(v2 - v1)
    w1 = p_group / (1.0 + e2)
    w2 = p_group * e2 / (1.0 + e2)

    hits = jnp.where(hit1 | hit2, 1.0, 0.0)
    row = lax.broadcasted_iota(jnp.int32, (tm, tm), 0)
    col = lax.broadcasted_iota(jnp.int32, (tm, tm), 1)
    earlier = jnp.where(col < row, 1.0, 0.0).astype(BF16)
    before = _dot(earlier, hits.astype(BF16)) + run_sc[...]
    rank1 = jnp.sum(jnp.where(hit1, before, 0.0), axis=1, keepdims=True)
    rank2 = jnp.sum(jnp.where(hit2, before, 0.0), axis=1, keepdims=True)
    run_sc[...] = run_sc[...] + jnp.sum(hits, axis=0, keepdims=True)
    cnt_ref[...] = run_sc[...]

    rec = jnp.zeros((tm, ROUTE_PAD), F32)
    for slot, val in ((_R_E1, i1 - N_EGROUPS), (_R_E2, i2 - N_EGROUPS), (_R_W1, w1), (_R_W2, w2),
                      (_R_RANK1, rank1), (_R_RANK2, rank2)):
        rec = jnp.where(lane == slot, val, rec)
    route_ref[...] = rec


def _post(x2d, o2d, y2d, merge, w_attn_up, s5_w_val, s5_w_gate, s5_b_gate, w_out, ln1_g, ln1_b,
          router_w_group, router_b_group, router_w_expert, router_b_expert, tm):
    n, d = x2d.shape
    rpad = ROUTE_PAD - N_EGROUPS - N_EXPERTS
    wr = jnp.concatenate([router_w_group, router_w_expert, jnp.zeros((d, rpad), F32)], axis=1).astype(BF16)
    br = jnp.concatenate([router_b_group, router_b_expert, jnp.zeros((rpad,), F32)]).reshape(1, -1).astype(F32)
    row = lambda w: pl.BlockSpec((tm, w), lambda i: (i, 0))
    full = lambda a: pl.BlockSpec(a.shape, lambda i: (0,) * a.ndim, pipeline_mode=pl.Buffered(1))
    weights = [w_attn_up.astype(BF16), s5_w_val.astype(BF16), s5_w_gate.astype(BF16),
               s5_b_gate.reshape(1, d).astype(F32), w_out.astype(BF16),
               ln1_g.reshape(1, d).astype(F32), ln1_b.reshape(1, d).astype(F32), wr, br]
    return pl.pallas_call(
        _post_kernel,
        grid=(n // tm,),
        in_specs=[row(d), row(ATTN_WIDTH), row(S5_WIDTH), row(2 * d)] + [full(w) for w in weights],
        out_specs=[row(d), row(d // 2), row(ROUTE_PAD), pl.BlockSpec((1, ROUTE_PAD), lambda i: (0, 0))],
        out_shape=[jax.ShapeDtypeStruct((n, d), F32), jax.ShapeDtypeStruct((n, d // 2), jnp.uint32),
                   jax.ShapeDtypeStruct((n, ROUTE_PAD), F32), jax.ShapeDtypeStruct((1, ROUTE_PAD), F32)],
        scratch_shapes=[pltpu.VMEM((1, ROUTE_PAD), F32)],
        compiler_params=_compiler_params(("arbitrary",)),
        name="post_mixer",
    )(x2d, o2d, y2d, merge, *weights)


def _plan_kernel(route_ref, cnt_ref, dest_ref):
    tm = route_ref.shape[0]
    lane8 = lax.broadcasted_iota(jnp.int32, (SUBLANES, ROUTE_PAD), 1)
    counts = jnp.broadcast_to(cnt_ref[...], (SUBLANES, ROUTE_PAD)).astype(jnp.int32)
    shift = int(math.log2(MOE_BLOCK))
    padded = jnp.left_shift(jnp.right_shift(counts + (MOE_BLOCK - 1), shift), shift)
    incl = padded
    step = 1
    while step < ROUTE_PAD:
        incl = incl + jnp.where(lane8 >= step, pltpu.roll(incl, step, 1), 0)
        step *= 2
    pstart = (incl - padded)[0:1].astype(F32)
    route = route_ref[...]
    expert_of_lane = (lax.broadcasted_iota(jnp.int32, (tm, ROUTE_PAD), 1) - N_EGROUPS).astype(F32)
    lane = lax.broadcasted_iota(jnp.int32, (tm, ROUTE_PAD), 1)

    def dest(e_slot, rank_slot):
        hit = expert_of_lane == route[:, e_slot:e_slot + 1]
        return jnp.sum(jnp.where(hit, pstart, 0.0), axis=1, keepdims=True) + route[:, rank_slot:rank_slot + 1]

    d1 = dest(_R_E1, _R_RANK1)
    d2 = dest(_R_E2, _R_RANK2)
    dest_ref[...] = jnp.where(lane == 0, d1, jnp.where(lane == 1, d2, 0.0)).astype(jnp.int32)


def _plan(route, counts_row):
    n = route.shape[0]
    tm = min(4096, n)
    return pl.pallas_call(
        _plan_kernel,
        grid=(n // tm,),
        in_specs=[pl.BlockSpec((tm, ROUTE_PAD), lambda i: (i, 0)),
                  pl.BlockSpec((1, ROUTE_PAD), lambda i: (0, 0))],
        out_specs=pl.BlockSpec((tm, ROUTE_PAD), lambda i: (i, 0)),
        out_shape=jax.ShapeDtypeStruct((n, ROUTE_PAD), jnp.int32),
        compiler_params=_compiler_params(("parallel",)),
        name="moe_plan",
    )(route, counts_row)


SC_GATHER_ROWS = 128


def _sc_row_gather(table, idx):
    n_idx = idx.shape[0]
    d = table.shape[1]
    info = plsc.get_sparse_core_info()
    n_workers = info.num_cores * info.num_subcores
    per_worker = n_idx // n_workers
    assert n_idx % (n_workers * SC_GATHER_ROWS) == 0
    mesh = plsc.VectorSubcoreMesh(core_axis_name="c", subcore_axis_name="s")

    @functools.partial(
        pl.kernel, mesh=mesh,
        out_type=jax.ShapeDtypeStruct((n_idx, d), table.dtype),
        scratch_types=[
            pltpu.VMEM((SC_GATHER_ROWS,), jnp.int32),
            pltpu.VMEM((SC_GATHER_ROWS, d), table.dtype),
            pltpu.SemaphoreType.DMA,
        ],
    )
    def gather(table_hbm, idx_hbm, out_hbm, idx_v, rows_v, sem):
        worker = lax.axis_index("s") * info.num_cores + lax.axis_index("c")
        base = worker * per_worker

        @pl.loop(0, per_worker // SC_GATHER_ROWS)
        def _(j):
            off = base + j * SC_GATHER_ROWS
            pltpu.sync_copy(idx_hbm.at[pl.ds(off, SC_GATHER_ROWS)], idx_v)
            pltpu.async_copy(table_hbm.at[idx_v], rows_v, sem).wait()
            pltpu.sync_copy(rows_v, out_hbm.at[pl.ds(off, SC_GATHER_ROWS)])

    return gather(table, idx)


def _sc_row_scatter(rows, idx_a, idx_b, n_out):
    n, d = rows.shape
    info = plsc.get_sparse_core_info()
    n_workers = info.num_cores * info.num_subcores
    per_worker = n // n_workers
    assert n % (n_workers * SC_GATHER_ROWS) == 0
    mesh = plsc.VectorSubcoreMesh(core_axis_name="c", subcore_axis_name="s")

    @functools.partial(
        pl.kernel, mesh=mesh,
        out_type=jax.ShapeDtypeStruct((n_out, d), rows.dtype),
        scratch_types=[
            pltpu.VMEM((SC_GATHER_ROWS,), jnp.int32),
            pltpu.VMEM((SC_GATHER_ROWS,), jnp.int32),
            pltpu.VMEM((SC_GATHER_ROWS, d), rows.dtype),
        ],
    )
    def scatter(rows_hbm, idx_a_hbm, idx_b_hbm, out_hbm, idx_a_v, idx_b_v, rows_v):
        worker = lax.axis_index("s") * info.num_cores + lax.axis_index("c")
        base = worker * per_worker

        @pl.loop(0, per_worker // SC_GATHER_ROWS)
        def _(j):
            src = pl.ds(base + j * SC_GATHER_ROWS, SC_GATHER_ROWS)
            pltpu.sync_copy(rows_hbm.at[src], rows_v)
            pltpu.sync_copy(idx_a_hbm.at[src], idx_a_v)
            pltpu.sync_copy(idx_b_hbm.at[src], idx_b_v)
            pltpu.sync_copy(rows_v, out_hbm.at[idx_a_v])
            pltpu.sync_copy(rows_v, out_hbm.at[idx_b_v])

    return scatter(rows, idx_a, idx_b)


def _expert_kernel(blk_exp_ref, blk_valid_ref, x_ref, wg_ref, wu_ref, wd_ref, y_ref, wg_sc, wu_sc, wd_sc):
    i = pl.program_id(0)
    n_valid = blk_valid_ref[i]

    @pl.when((i == 0) | (blk_exp_ref[i] != blk_exp_ref[jnp.maximum(i - 1, 0)]))
    def _():
        wg_sc[...] = wg_ref[0].astype(BF16)
        wu_sc[...] = wu_ref[0].astype(BF16)
        wd_sc[...] = wd_ref[0].astype(BF16)

    @pl.when(n_valid > 0)
    def _():
        row = lax.broadcasted_iota(jnp.int32, x_ref.shape, 0)
        words = jnp.where(row < n_valid, x_ref[...], jnp.uint32(0))
        xb = _unpack_rows(words).astype(BF16)
        h_gate = _dot(xb, wg_sc[...])
        h_up = _dot(xb, wu_sc[...])
        hb = (h_gate * jax.nn.sigmoid(h_gate) * h_up).astype(BF16)
        y_ref[...] = _pack_rows(_dot(hb, wd_sc[...]))

    @pl.when(n_valid == 0)
    def _():
        y_ref[...] = jnp.zeros_like(y_ref)


def _experts(blk_expert, blk_valid, xs, w_gate, w_up, w_down):
    n_blocks = blk_expert.shape[0]
    d = 2 * xs.shape[1]
    grid_spec = pltpu.PrefetchScalarGridSpec(
        num_scalar_prefetch=2,
        grid=(n_blocks,),
        in_specs=[
            pl.BlockSpec((MOE_BLOCK, d // 2), lambda i, be, bv: (i, 0)),
            pl.BlockSpec((1, d, D_EXPERT), lambda i, be, bv: (be[i], 0, 0)),
            pl.BlockSpec((1, d, D_EXPERT), lambda i, be, bv: (be[i], 0, 0)),
            pl.BlockSpec((1, D_EXPERT, d), lambda i, be, bv: (be[i], 0, 0)),
        ],
        out_specs=pl.BlockSpec((MOE_BLOCK, d // 2), lambda i, be, bv: (i, 0)),
        scratch_shapes=[
            pltpu.VMEM((d, D_EXPERT), BF16),
            pltpu.VMEM((d, D_EXPERT), BF16),
            pltpu.VMEM((D_EXPERT, d), BF16),
        ],
    )
    return pl.pallas_call(
        _expert_kernel,
        grid_spec=grid_spec,
        out_shape=jax.ShapeDtypeStruct((n_blocks * MOE_BLOCK, d // 2), jnp.uint32),
        compiler_params=_compiler_params(("arbitrary",)),
        name="experts",
    )(blk_expert, blk_valid, xs, w_gate, w_up, w_down)


COMBINE_TM = 512


def _combine_kernel(h_ref, y1_ref, y2_ref, route_ref, g2_ref, b2_ref, out_ref):
    route = route_ref[...]
    w1 = route[:, _R_W1:_R_W1 + 1]
    w2 = route[:, _R_W2:_R_W2 + 1]
    t = DN_ALPHA * h_ref[...] + (_unpack_rows(y1_ref[...]) * w1 + _unpack_rows(y2_ref[...]) * w2)
    out_ref[...] = _layer_norm(t, g2_ref[...], b2_ref[...])


def _combine(yg, h2d, route, ln2_g, ln2_b):
    n, d = h2d.shape
    tm = min(COMBINE_TM, n)
    n_tiles = n // tm
    row = lambda w: pl.BlockSpec((tm, w), lambda i: (i, 0))
    vec = pl.BlockSpec((1, d), lambda i: (0, 0))
    return pl.pallas_call(
        _combine_kernel,
        grid=(n_tiles,),
        in_specs=[row(d), row(d // 2), pl.BlockSpec((tm, d // 2), lambda i: (i + n_tiles, 0)), row(ROUTE_PAD),
                  vec, vec],
        out_specs=row(d),
        out_shape=jax.ShapeDtypeStruct((n, d), F32),
        compiler_params=_compiler_params(("parallel",)),
        name="combine",
    )(h2d, yg, yg, route, ln2_g.reshape(1, d).astype(F32), ln2_b.reshape(1, d).astype(F32))


def _moe(h2d, h_packed, route, counts_row, w_gate, w_up, w_down, ln2_g, ln2_b):
    n, d = h2d.shape
    dest = _plan(route, counts_row)
    dest1, dest2 = dest[:, 0], dest[:, 1]
    counts = counts_row[0, N_EGROUPS:N_EGROUPS + N_EXPERTS].astype(jnp.int32)
    padded = (counts + MOE_BLOCK - 1) // MOE_BLOCK * MOE_BLOCK
    pend = jnp.cumsum(padded)
    n_blocks = -(-(n * TOP_K_IN_GROUP) // MOE_BLOCK) + N_EXPERTS
    blk_row0 = (jnp.arange(n_blocks, dtype=jnp.int32) * MOE_BLOCK)[:, None]
    owns = (pend - padded <= blk_row0) & (blk_row0 < pend)
    blk_expert = jnp.sum(jnp.where(owns, jnp.arange(N_EXPERTS, dtype=jnp.int32), 0), axis=1)
    blk_valid = jnp.sum(jnp.where(owns, jnp.clip(pend - padded + counts - blk_row0, 0, MOE_BLOCK), 0), axis=1)
    xs = _sc_row_scatter(h_packed, dest1, dest2, n_blocks * MOE_BLOCK)
    yb = _experts(blk_expert, blk_valid, xs, w_gate, w_up, w_down)
    yg = _sc_row_gather(yb, jnp.concatenate([dest1, dest2]))
    return _combine(yg, h2d, route, ln2_g, ln2_b)


def kernel(x, rel_bias, w_in, b_in, cmp_pos, cmp_w1, cmp_b1, cmp_w2, cmp_b2, w_attn_up, s5_lambda_re, s5_lambda_im, s5_log_dt, s5_b_re, s5_b_im, s5_c_re, s5_c_im, s5_d, s5_w_val, s5_w_gate, s5_b_gate, w_out, ln1_g, ln1_b, router_w_group, router_b_group, router_w_expert, router_b_expert, exp_w_gate, exp_w_up, exp_w_down, ln2_g, ln2_b):
    b, seq, d = x.shape
    n = b * seq
    assert w_in.shape[0] == DEPTH
    l = 0
    o, u, merge = _mixer_inputs(x, rel_bias, w_in[l], b_in[l], cmp_pos[l], cmp_w1[l], cmp_b1[l],
                                cmp_w2[l], cmp_b2[l])
    bmat, cmat, a = _s5_params(s5_lambda_re[l], s5_lambda_im[l], s5_log_dt[l], s5_b_re[l], s5_b_im[l],
                               s5_c_re[l], s5_c_im[l], b)
    y_s = _s5(u.reshape(b, seq, S5_WIDTH), bmat, cmat, a, s5_d[l])
    h2d, h_packed, route, counts = _post(
        x.reshape(n, d), o.reshape(n, ATTN_WIDTH), y_s.reshape(n, S5_WIDTH), merge, w_attn_up[l], s5_w_val[l],
        s5_w_gate[l], s5_b_gate[l], w_out[l], ln1_g[l], ln1_b[l], router_w_group[l], router_b_group[l],
        router_w_expert[l], router_b_expert[l], 1024)
    out = _moe(h2d, h_packed, route, counts, exp_w_gate[l], exp_w_up[l], exp_w_down[l], ln2_g[l], ln2_b[l])
    return out.reshape(b, seq, d)
```

```python
import functools
import math

import jax
import jax.numpy as jnp
from jax import lax
from jax.experimental import pallas as pl
from jax.experimental.pallas import tpu as pltpu
from jax.experimental.pallas import tpu_sc as plsc

F32 = jnp.float32
BF16 = jnp.bfloat16

N_HEADS = 8
HEAD_DIM = 64
N_KV = 2
HPG = N_HEADS // N_KV
CMP_STRIDE = 16
CMP_BLOCK = 2 * CMP_STRIDE
CMP_HIDDEN = 128
SLC_BLOCK = 64
N_SEL = 16
WINDOW = 512
REL_BUCKETS = 32
REL_MAX_DIST = 128
S5_WIDTH = 512
S5_GROUP = 16
S5_GROUPS = S5_WIDTH // S5_GROUP
S5_STATE = 64
N_EGROUPS = 8
EXPERTS_PER_GROUP = 8
N_EXPERTS = N_EGROUPS * EXPERTS_PER_GROUP
TOP_K_IN_GROUP = 2
D_EXPERT = 256
EXPERT_BLOCK = 128
DEPTH = 1
DN_ALPHA = (2.0 * DEPTH) ** 0.25
LN_EPS = 1e-5
NEG_INF = -1e30
BIG = 1e9
LOG2_E = math.log2(math.e)
MOE_BLOCK = 512

ATTN_WIDTH = N_HEADS * HEAD_DIM
KV_WIDTH = N_KV * HEAD_DIM
KV_OFF = ATTN_WIDTH
NSA_GATE_OFF = KV_OFF + 6 * KV_WIDTH
S5_OFF = NSA_GATE_OFF + 3 * N_HEADS
MERGE_OFF = S5_OFF + S5_WIDTH

LANES = 128
SUBLANES = 8
V7X_VMEM_BYTES = 64 * 1024 * 1024
VMEM_LIMIT_BYTES = V7X_VMEM_BYTES - 8 * 1024 * 1024

ATTN_TQ = 128
SLC_TK = 512
SLC_NEAR_BACK = SLC_TK - ATTN_TQ
KV_PAD = WINDOW
SOFTMAX_SHIFT_MARGIN = 100.0
SOFTMAX_BOUND_SLACK = 1.001
GATE_PAD = LANES


def _gelu_tanh(x):
    c = math.sqrt(2.0 / math.pi)
    return x * (0.5 * (1.0 + jnp.tanh(c * (x + 0.044715 * (x * x * x)))))


def _dot(a, b):
    return jnp.dot(a, b, preferred_element_type=F32)


def _dot_nt(a, b):
    return lax.dot_general(a, b, (((1,), (1,)), ((), ())), preferred_element_type=F32)


def _pack_rows(x):
    half = x.shape[1] // 2
    xb = x.astype(BF16).astype(F32)
    hi = lax.bitcast_convert_type(xb[:, :half], jnp.uint32) & jnp.uint32(0xFFFF0000)
    lo = lax.shift_right_logical(lax.bitcast_convert_type(xb[:, half:], jnp.uint32), jnp.uint32(16))
    return hi | lo


def _unpack_rows(w):
    hi = lax.bitcast_convert_type(w & jnp.uint32(0xFFFF0000), F32)
    lo = lax.bitcast_convert_type(lax.shift_left(w, jnp.uint32(16)), F32)
    return jnp.concatenate([hi, lo], axis=1)


def _compiler_params(semantics):
    return pltpu.CompilerParams(dimension_semantics=semantics, vmem_limit_bytes=VMEM_LIMIT_BYTES)


def _in_proj_layout(d_model):
    widths = (ATTN_WIDTH, 2 * KV_WIDTH, 2 * KV_WIDTH, 2 * KV_WIDTH, N_KV * GATE_PAD, S5_WIDTH, 2 * d_model)
    offs = [0]
    for w in widths:
        offs.append(offs[-1] + w)
    return widths, offs


def _pack_in_proj(w_in, b_in, d_model):
    def kv_cols(j):
        return KV_OFF + j * KV_WIDTH

    def pair(jk, jv):
        cols = []
        for g in range(N_KV):
            cols.append(jnp.arange(kv_cols(jk) + g * HEAD_DIM, kv_cols(jk) + (g + 1) * HEAD_DIM))
            cols.append(jnp.arange(kv_cols(jv) + g * HEAD_DIM, kv_cols(jv) + (g + 1) * HEAD_DIM))
        return jnp.concatenate(cols)

    idx = jnp.concatenate([
        jnp.arange(0, ATTN_WIDTH),
        pair(2, 3),
        pair(4, 5),
        jnp.arange(kv_cols(0), kv_cols(2)),
    ])
    idx2 = jnp.concatenate([jnp.arange(S5_OFF, S5_OFF + S5_WIDTH),
                            jnp.arange(MERGE_OFF, MERGE_OFF + 2 * d_model)])
    gpad = GATE_PAD - 3 * HPG
    w_parts, b_parts = [w_in[:, idx]], [b_in[idx]]
    for g in range(N_KV):
        cols = jnp.asarray([NSA_GATE_OFF + (g * HPG + h) * 3 + j for j in range(3) for h in range(HPG)])
        w_parts += [w_in[:, cols], jnp.zeros((d_model, gpad), F32)]
        b_parts += [b_in[cols], jnp.zeros((gpad,), F32)]
    w = jnp.concatenate(w_parts + [w_in[:, idx2]], axis=1)
    b = jnp.concatenate(b_parts + [b_in[idx2]])
    return w.astype(BF16), b.reshape(1, -1).astype(F32)


def _in_proj_kernel(offs, x_ref, w_ref, b_ref, q_ref, slc_ref, win_ref, cmp_ref, g_ref, u_ref, m_ref, cmp_sc):
    xb = x_ref[...].astype(BF16)

    def proj(i):
        c0, c1 = offs[i], offs[i + 1]
        return _dot(xb, w_ref[:, c0:c1]) + b_ref[:, c0:c1]

    q_ref[...] = (proj(0) * (HEAD_DIM ** -0.5 * LOG2_E)).astype(BF16)
    slc_ref[...] = proj(1).astype(BF16)
    win_ref[...] = proj(2).astype(BF16)
    g_ref[...] = jax.nn.sigmoid(proj(4))
    u_ref[...] = proj(5)
    m_ref[...] = jax.nn.sigmoid(proj(6)).astype(BF16)

    c = proj(3)
    n_chunks = c.shape[0] // CMP_STRIDE
    lo_half = lax.broadcasted_iota(jnp.int32, (n_chunks, LANES), 1) < HEAD_DIM
    for p in range(c.shape[1] // LANES):
        cmp_sc[p] = c[:, p * LANES:(p + 1) * LANES]
        for t in range(CMP_STRIDE // 2):
            tok_a = cmp_sc[p, pl.ds(2 * t, n_chunks, stride=CMP_STRIDE), :]
            tok_b = cmp_sc[p, pl.ds(2 * t + 1, n_chunks, stride=CMP_STRIDE), :]
            lanes_t = slice(t * LANES, (t + 1) * LANES)
            cmp_ref[0, 2 * p, :, lanes_t] = jnp.where(lo_half, tok_a, pltpu.roll(tok_b, HEAD_DIM, 1))
            cmp_ref[0, 2 * p + 1, :, lanes_t] = jnp.where(lo_half, pltpu.roll(tok_a, HEAD_DIM, 1), tok_b)


def _in_proj(x2d, w_packed, b_packed, d_model, tm, seq):
    n = x2d.shape[0]
    widths, offs = _in_proj_layout(d_model)
    ncols = offs[-1]
    dtypes = (BF16, BF16, BF16, F32, F32, F32, BF16)
    tiles_per_seq = seq // tm
    assert seq % tm == 0 and tm % (CMP_STRIDE * SUBLANES) == 0
    planes = widths[3] // HEAD_DIM
    cmp_spec = pl.BlockSpec((1, planes, tm // CMP_STRIDE, CMP_STRIDE * HEAD_DIM),
                            lambda i: (i // tiles_per_seq, 0, i % tiles_per_seq, 0))
    cmp_shape = jax.ShapeDtypeStruct((n // seq, planes, seq // CMP_STRIDE, CMP_STRIDE * HEAD_DIM), F32)
    row_spec = lambda w: pl.BlockSpec((tm, w), lambda i: (i, 0))
    return pl.pallas_call(
        functools.partial(_in_proj_kernel, tuple(offs)),
        grid=(n // tm,),
        in_specs=[
            pl.BlockSpec((tm, d_model), lambda i: (i, 0)),
            pl.BlockSpec((d_model, ncols), lambda i: (0, 0), pipeline_mode=pl.Buffered(1)),
            pl.BlockSpec((1, ncols), lambda i: (0, 0)),
        ],
        out_specs=[cmp_spec if j == 3 else row_spec(w) for j, w in enumerate(widths)],
        out_shape=[cmp_shape if j == 3 else jax.ShapeDtypeStruct((n, w), dt)
                   for j, (w, dt) in enumerate(zip(widths, dtypes))],
        scratch_shapes=[pltpu.VMEM((widths[3] // LANES, tm, LANES), F32)],
        compiler_params=_compiler_params(("parallel",)),
        name="in_proj",
    )(x2d, w_packed, b_packed)


def _compress_kernel(ck_ref, cv_ref, pos_ref, w1_ref, b1_ref, w2_ref, b2_ref, out_ref):
    n_c = ck_ref.shape[2]
    outs = []
    for i, c_ref in enumerate((ck_ref, cv_ref)):
        c = c_ref[0, 0]
        lo = (c + pos_ref[i, 0:1, :]).astype(BF16)
        hi = (c + pos_ref[i, 1:2, :]).astype(BF16)
        p_lo = _dot(lo, w1_ref[i, 0])
        p_hi = _dot(hi, w1_ref[i, 1])
        hid = p_lo + pltpu.roll(p_hi, n_c - 1, 0) + b1_ref[i]
        hid = _gelu_tanh(hid).astype(BF16)
        outs.append(_dot(hid, w2_ref[i]) + b2_ref[i])
    out_ref[0, 0] = jnp.concatenate(outs[::-1], axis=1).astype(BF16)


def _compress(cmp4, cmp_pos, cmp_w1, cmp_b1, cmp_w2, cmp_b2):
    b, _, n_c, cw = cmp4.shape
    half = CMP_STRIDE * HEAD_DIM
    pos = cmp_pos.reshape(2, 2, half).astype(F32)
    w1 = cmp_w1.reshape(2, 2, half, CMP_HIDDEN).astype(BF16)
    b1 = cmp_b1.reshape(2, 1, CMP_HIDDEN).astype(F32)
    w2 = cmp_w2.astype(BF16)
    b2 = cmp_b2.reshape(2, 1, HEAD_DIM).astype(F32)
    full = lambda shape: pl.BlockSpec(shape, lambda i, g: (0,) * len(shape))
    return pl.pallas_call(
        _compress_kernel,
        grid=(b, N_KV),
        in_specs=[
            pl.BlockSpec((1, 1, n_c, cw), lambda i, g: (i, g, 0, 0)),
            pl.BlockSpec((1, 1, n_c, cw), lambda i, g: (i, N_KV + g, 0, 0)),
            full((2, 2, half)),
            full((2, 2, half, CMP_HIDDEN)),
            full((2, 1, CMP_HIDDEN)),
            full((2, CMP_HIDDEN, HEAD_DIM)),
            full((2, 1, HEAD_DIM)),
        ],
        out_specs=pl.BlockSpec((1, 1, n_c, 2 * HEAD_DIM), lambda i, g: (i, g, 0, 0)),
        out_shape=jax.ShapeDtypeStruct((b, N_KV, n_c, 2 * HEAD_DIM), BF16),
        compiler_params=_compiler_params(("parallel", "parallel")),
        name="compress",
    )(cmp4, cmp4, pos, w1, b1, w2, b2)


def _t5_bucket(dist):
    n = jnp.maximum(dist, 0)
    max_exact = REL_BUCKETS // 2
    nf = jnp.maximum(n, 1).astype(F32)
    large = max_exact + (jnp.log(nf / max_exact) / math.log(REL_MAX_DIST / max_exact)
                         * (REL_BUCKETS - max_exact)).astype(jnp.int32)
    large = jnp.minimum(large, REL_BUCKETS - 1)
    return jnp.where(n < max_exact, n, large)


def _bucket_thresholds():
    buckets = _t5_bucket(jnp.arange(REL_MAX_DIST + 1))
    return jnp.sum(buckets[None, :] < jnp.arange(REL_BUCKETS)[:, None], axis=1).astype(jnp.int32)


def _bias_of_dist(dist, head, thr_ref, tbl_ref):
    bias = jnp.full(dist.shape, tbl_ref[head], F32)
    for k in range(1, REL_BUCKETS):
        bias = jnp.where(dist >= thr_ref[k], tbl_ref[k * N_HEADS + head], bias)
    return bias


BIAS_ROWS = 32


def _bias_c_kernel(thr_ref, tbl_ref, out_ref):
    _, tr, n_c = out_ref.shape
    r0 = pl.program_id(0) * tr

    def chunk(ci, carry):
        row0 = pl.multiple_of(ci * BIAS_ROWS, BIAS_ROWS)
        rows = pl.ds(row0, BIAS_ROWS)
        for c0 in range(0, n_c, LANES):
            width = min(LANES, n_c - c0)
            cols = slice(c0, c0 + width)
            pos = r0 + row0 + lax.broadcasted_iota(jnp.int32, (BIAS_ROWS, width), 0)
            key_end = ((c0 + lax.broadcasted_iota(jnp.int32, (BIAS_ROWS, width), 1)) * CMP_STRIDE
                       + (CMP_BLOCK - 1))
            dist = pos - key_end
            d_min = r0 + row0 - ((c0 + width - 1) * CMP_STRIDE + CMP_BLOCK - 1)
            d_max = r0 + row0 + (BIAS_ROWS - 1) - (c0 * CMP_STRIDE + CMP_BLOCK - 1)

            @pl.when(d_max < 0)
            def _():
                for h in range(N_HEADS):
                    out_ref[h, rows, cols] = jnp.full((BIAS_ROWS, width), NEG_INF, F32)

            @pl.when(d_min >= REL_MAX_DIST)
            def _():
                for h in range(N_HEADS):
                    out_ref[h, rows, cols] = jnp.full((BIAS_ROWS, width),
                                                      tbl_ref[(REL_BUCKETS - 1) * N_HEADS + h], F32)

            @pl.when((d_max >= 0) & (d_min < REL_MAX_DIST))
            def _():
                for h in range(N_HEADS):
                    bias = _bias_of_dist(dist, h, thr_ref, tbl_ref)
                    out_ref[h, rows, cols] = jnp.where(dist >= 0, bias, NEG_INF)
        return carry

    lax.fori_loop(0, tr // BIAS_ROWS, chunk, 0)


def _bias_near_kernel(thr_ref, tbl_ref, near_ref, win_ref):
    tq = ATTN_TQ
    h = pl.program_id(0)
    far_bias = tbl_ref[(REL_BUCKETS - 1) * N_HEADS + h]

    def table(out_ref, lo_keys, window, offset):
        width = out_ref.shape[2]

        def chunk(ci, carry):
            row0 = pl.multiple_of(ci * BIAS_ROWS, BIAS_ROWS)
            dist = (lo_keys + row0 + lax.broadcasted_iota(jnp.int32, (BIAS_ROWS, width), 0)
                    - lax.broadcasted_iota(jnp.int32, (BIAS_ROWS, width), 1))
            visible = (dist >= 0) & (dist < window)
            bias = jnp.full(dist.shape, tbl_ref[h], F32)
            for k in range(1, REL_BUCKETS):
                bias = jnp.where(dist >= thr_ref[k], tbl_ref[k * N_HEADS + h], bias)
            out_ref[0, pl.ds(row0, BIAS_ROWS), :] = jnp.where(visible, bias - offset, NEG_INF)
            return carry

        lax.fori_loop(0, tq // BIAS_ROWS, chunk, 0)

    table(near_ref, SLC_NEAR_BACK, 1 << 30, far_bias)
    table(win_ref, WINDOW, WINDOW, 0.0)


def _attention_bias_tables(rel_bias, seq):
    tbl = (rel_bias.astype(F32) * LOG2_E).reshape(REL_BUCKETS * N_HEADS)
    thr = _bucket_thresholds()
    tq = ATTN_TQ
    n_c = seq // CMP_STRIDE
    smem = pl.BlockSpec(memory_space=pltpu.SMEM)
    tr = min(512, seq)
    bias_c = pl.pallas_call(
        _bias_c_kernel,
        grid=(seq // tr,),
        in_specs=[smem, smem],
        out_specs=pl.BlockSpec((N_HEADS, tr, n_c), lambda i: (0, i, 0)),
        out_shape=jax.ShapeDtypeStruct((N_HEADS, seq, n_c), F32),
        compiler_params=_compiler_params(("parallel",)),
        name="bias_cmp",
    )(thr, tbl)
    head_block = lambda w: pl.BlockSpec((1, tq, w), lambda h: (h, 0, 0))
    widths = (SLC_NEAR_BACK + tq, WINDOW + tq)
    bias_near, bias_win = pl.pallas_call(
        _bias_near_kernel,
        grid=(N_HEADS,),
        in_specs=[smem, smem],
        out_specs=[head_block(w) for w in widths],
        out_shape=[jax.ShapeDtypeStruct((N_HEADS, tq, w), F32) for w in widths],
        compiler_params=_compiler_params(("parallel",)),
        name="bias_near",
    )(thr, tbl)
    return bias_c, bias_near, bias_win


def _nsa_constants(seq):
    n_c = seq // CMP_STRIDE
    n_blk = seq // SLC_BLOCK
    cmp_start = jnp.arange(n_c) * CMP_STRIDE
    blk_start = jnp.arange(n_blk) * SLC_BLOCK
    overlap_t = ((cmp_start[None, :] <= blk_start[:, None] + SLC_BLOCK - 1)
                 & (cmp_start[None, :] + CMP_BLOCK - 1 >= blk_start[:, None]))
    overlap_t = overlap_t & (cmp_start[None, :] + CMP_BLOCK <= seq)
    ones_rows = jnp.arange(SUBLANES)[:, None] == 0
    overlap_t = jnp.concatenate([overlap_t, jnp.broadcast_to(ones_rows, (SUBLANES, n_c))], axis=0)
    cmp_ones = jnp.broadcast_to(jnp.arange(LANES)[None, :] == 0, (n_c, LANES))
    return overlap_t.astype(BF16), cmp_ones.astype(BF16)


def _build_kv_scratch(seq, slc_ref, win_ref, g, ks_sc, vs_sc, kw_sc, vw_sc, knorm_sc):
    chunk = min(512, seq)
    lane_p = lax.broadcasted_iota(jnp.int32, (KV_PAD, LANES), 1)
    zeros = jnp.zeros((KV_PAD, LANES), BF16)
    ks_sc[0:KV_PAD] = jnp.where(lane_p >= HEAD_DIM, 1.0, 0.0).astype(BF16)
    kw_sc[0:KV_PAD] = jnp.where(lane_p == HEAD_DIM, NEG_INF, 0.0).astype(BF16)
    vs_sc[0:KV_PAD] = zeros
    vw_sc[0:KV_PAD] = zeros
    lane = lax.broadcasted_iota(jnp.int32, (chunk, LANES), 1)
    row = lax.broadcasted_iota(jnp.int32, (chunk, LANES), 0)
    lo_half = lane < HEAD_DIM
    ones_lane = jnp.where(lane == HEAD_DIM, 1.0, 0.0)
    k_sq_max = jnp.zeros((chunk, 1), F32)
    for c in range(seq // chunk):
        r0 = c * chunk
        dst = slice(KV_PAD + r0, KV_PAD + r0 + chunk)
        blk = jnp.right_shift(r0 + row, int(math.log2(SLC_BLOCK)))
        lanes_g = slice(g * LANES, (g + 1) * LANES)
        slab = slc_ref[0, r0:r0 + chunk, lanes_g].astype(F32)
        k_sq_max = jnp.maximum(k_sq_max, jnp.sum(jnp.where(lo_half, slab * slab, 0.0), axis=1, keepdims=True))
        ks_sc[dst] = jnp.where(lo_half, slab, jnp.where(lane - HEAD_DIM == blk, 1.0, 0.0)).astype(BF16)
        vs_sc[dst] = jnp.where(lo_half, pltpu.roll(slab, HEAD_DIM, 1), ones_lane).astype(BF16)
        slab = win_ref[0, r0:r0 + chunk, lanes_g].astype(F32)
        kw_sc[dst] = jnp.where(lo_half, slab, 0.0).astype(BF16)
        vw_sc[dst] = jnp.where(lo_half, pltpu.roll(slab, HEAD_DIM, 1), ones_lane).astype(BF16)
    knorm_sc[...] = jnp.broadcast_to(jnp.max(k_sq_max, axis=0, keepdims=True), knorm_sc.shape)


def _nsa_select(seq, g, q0, q_ref, vkc_ref, bias_c_ref, overlap_t_ref, cmp_ones_ref):
    tq = ATTN_TQ
    n_blk = seq // SLC_BLOCK
    n_sel = min(N_SEL, n_blk)
    rows = HPG * tq
    lane = lax.broadcasted_iota(jnp.int32, (tq, LANES), 1)
    lo_half = lane < HEAD_DIM
    lane_r = lax.broadcasted_iota(jnp.int32, (rows, LANES), 1)

    q_lo, q_hi = [], []
    for pair_idx in range(HPG // 2):
        lanes_p = slice((g * (HPG // 2) + pair_idx) * LANES, (g * (HPG // 2) + pair_idx + 1) * LANES)
        q2 = q_ref[0, :, lanes_p].astype(F32)
        q2r = pltpu.roll(q2, HEAD_DIM, 1)
        q_lo += [jnp.where(lo_half, q2, 0.0), jnp.where(lo_half, q2r, 0.0)]
        q_hi += [jnp.where(lo_half, 0.0, q2r), jnp.where(lo_half, 0.0, q2)]
    q_lo = jnp.concatenate(q_lo, axis=0)
    q_hi = jnp.concatenate(q_hi, axis=0)

    vkc = vkc_ref[0, g]
    s_c = _dot_nt(q_hi.astype(BF16), vkc) + bias_c_ref[g * HPG:(g + 1) * HPG].reshape(rows, -1)
    m_c = jnp.max(s_c, axis=1, keepdims=True)
    e_cb = jnp.exp2(s_c - m_c).astype(BF16)
    pv_c = _dot(e_cb, jnp.concatenate([vkc, cmp_ones_ref[...]], axis=1))
    row_pos = q0 + (lax.broadcasted_iota(jnp.int32, (rows, LANES), 0) & (tq - 1))
    has_key = row_pos >= CMP_BLOCK - 1
    o_cmp = jnp.where(has_key, pv_c[:, :LANES] / pv_c[:, LANES:LANES + 1], 0.0)

    imp_t4 = _dot_nt(overlap_t_ref[...], e_cb)
    imp_t = None
    for h in range(HPG):
        part = imp_t4[:, h * tq:(h + 1) * tq]
        part = part[:n_blk] / part[n_blk:n_blk + 1]
        imp_t = part if imp_t is None else imp_t + part

    blk = lax.broadcasted_iota(jnp.int32, (n_blk, tq), 0)
    pos = q0 + lax.broadcasted_iota(jnp.int32, (n_blk, tq), 1)
    cur = jnp.right_shift(pos, int(math.log2(SLC_BLOCK)))
    forced = (blk == 0) | (blk == cur) | (blk == cur - 1)
    valid = blk * SLC_BLOCK <= pos
    score = jnp.where(forced, BIG, jnp.where(valid, imp_t, -BIG))
    blk_f = blk.astype(F32)
    pen_t = jnp.full((n_blk, tq), NEG_INF, F32)
    for _ in range(n_sel):
        top = jnp.max(score, axis=0, keepdims=True)
        first = jnp.min(jnp.where(score == top, blk_f, float(n_blk)), axis=0, keepdims=True)
        hit = blk_f == first
        pen_t = jnp.where(hit, 0.0, pen_t)
        score = jnp.where(hit, -jnp.inf, score)
    pieces = [jnp.zeros((tq, HEAD_DIM), F32), jnp.transpose(pen_t)]
    if n_blk < LANES - HEAD_DIM:
        pieces.append(jnp.full((tq, LANES - HEAD_DIM - n_blk), NEG_INF, F32))
    pen_lanes = jnp.concatenate(pieces, axis=1)
    q_slc = jnp.where(lane_r < HEAD_DIM, q_lo, jnp.concatenate([pen_lanes] * HPG, axis=0)).astype(BF16)
    return o_cmp, q_lo, q_slc


def _nsa_kernel(seq, q_ref, vkc_ref, slc_ref, win_ref, gate_ref, bias_c_ref, bias_near_ref, bias_win_ref,
                overlap_t_ref, cmp_ones_ref, o_ref, *scratch):
    tq = ATTN_TQ
    rows = HPG * tq
    groups = range(N_KV)
    kv_sc = [scratch[4 * g:4 * g + 4] for g in groups]
    s_sc, mrun_sc, acc_sc, knorm_sc = (scratch[4 * N_KV + i * N_KV:4 * N_KV + (i + 1) * N_KV] for i in range(4))
    qt = pl.program_id(1)
    q0 = pl.multiple_of(qt * tq, tq)
    lane = lax.broadcasted_iota(jnp.int32, (tq, LANES), 1)
    lo_half = lane < HEAD_DIM
    lane_r = lax.broadcasted_iota(jnp.int32, (rows, LANES), 1)

    @pl.when(qt == 0)
    def _():
        for g in groups:
            _build_kv_scratch(seq, slc_ref, win_ref, g, *kv_sc[g], knorm_sc[g])

    sel = [_nsa_select(seq, g, q0, q_ref, vkc_ref, bias_c_ref, overlap_t_ref, cmp_ones_ref) for g in groups]
    o_cmp = [s[0] for s in sel]
    q_lo = [s[1] for s in sel]
    q_slc = [s[2] for s in sel]
    head_rows = lambda ref, g: ref[g * HPG:(g + 1) * HPG].reshape(rows, -1)

    o_win = []
    for g in groups:
        kw_sc, vw_sc = kv_sc[g][2], kv_sc[g][3]
        q_win = jnp.where(lane_r == HEAD_DIM, 1.0, q_lo[g]).astype(BF16)
        win_rows = pl.ds(q0, WINDOW + tq)
        s_w = _dot_nt(q_win, kw_sc[win_rows, :]) + head_rows(bias_win_ref, g)
        p_w = jnp.exp2(s_w - jnp.max(s_w, axis=1, keepdims=True)).astype(BF16)
        acc_w = _dot(p_w, vw_sc[win_rows, :])
        o_win.append(acc_w / acc_w[:, HEAD_DIM:HEAD_DIM + 1])

    tk = SLC_TK
    n_far = qt // (tk // tq)
    lane_tiles = lambda a: [a[:, j * LANES:(j + 1) * LANES] for j in range(a.shape[1] // LANES)]
    near0 = pl.multiple_of(q0 + (KV_PAD - SLC_NEAR_BACK), tq)

    def tile_rows(u):
        return pl.ds(pl.multiple_of(near0 - u * tk, tq), tk)

    def tile_cols(u):
        return pl.ds(pl.multiple_of(u * tk, tk), tk)

    def score_tile(u, g, bias):
        s = _dot_nt(q_slc[g], kv_sc[g][0][tile_rows(u), :])
        if bias is not None:
            s = s + bias
        s_sc[g][:, tile_cols(u)] = s
        return functools.reduce(jnp.maximum, lane_tiles(s))

    shift_cap = []
    slack = jnp.float32(-jnp.inf)
    for g in groups:
        lane_max = score_tile(0, g, head_rows(bias_near_ref, g))
        mrun_sc[g][...] = lane_max
        m_near = jnp.max(lane_max, axis=1, keepdims=True)
        q_norm = jnp.sqrt(jnp.sum(q_lo[g] * q_lo[g], axis=1, keepdims=True))
        far_bound = q_norm * jnp.sqrt(knorm_sc[g][0:1, 0:1]) * SOFTMAX_BOUND_SLACK
        shift_cap.append(jnp.maximum(m_near, far_bound - SOFTMAX_SHIFT_MARGIN))
        slack = jnp.maximum(slack, jnp.max(far_bound - m_near))
    single_pass = slack <= 2.0 * SOFTMAX_SHIFT_MARGIN

    def weigh(s, u, g):
        shift = mrun_sc[g][...]
        p = jnp.exp2(s - jnp.concatenate([shift] * (tk // LANES), axis=1))
        acc_sc[g][...] = acc_sc[g][...] + _dot(p.astype(BF16), kv_sc[g][1][tile_rows(u), :])

    for g in groups:
        acc_sc[g][...] = jnp.zeros((rows, LANES), F32)

    @pl.when(single_pass)
    def _():
        for g in groups:
            mrun_sc[g][...] = jnp.broadcast_to(shift_cap[g], (rows, LANES))
            weigh(s_sc[g][:, tile_cols(0)], 0, g)

        def fused(u, carry):
            for g in groups:
                weigh(_dot_nt(q_slc[g], kv_sc[g][0][tile_rows(u), :]), u, g)
            return carry

        lax.fori_loop(1, n_far + 1, fused, 0)

    @pl.when(jnp.logical_not(single_pass))
    def _():
        def pass1(u, carry):
            for g in groups:
                mrun_sc[g][...] = jnp.maximum(mrun_sc[g][...], score_tile(u, g, None))
            return carry

        lax.fori_loop(1, n_far + 1, pass1, 0)
        for g in groups:
            m_s = jnp.max(mrun_sc[g][...], axis=1, keepdims=True)
            mrun_sc[g][...] = jnp.broadcast_to(m_s, (rows, LANES))

        def pass2(u, carry):
            for g in groups:
                weigh(s_sc[g][:, tile_cols(u)], u, g)
            return carry

        lax.fori_loop(0, n_far + 1, pass2, 0)

    for g in groups:
        acc = acc_sc[g][...]
        o_slc = acc / acc[:, HEAD_DIM:HEAD_DIM + 1]

        gates = gate_ref[0, :, g * GATE_PAD:(g + 1) * GATE_PAD]
        outs = []
        for h in range(HPG):
            sl = slice(h * tq, (h + 1) * tq)
            g_c, g_s, g_w = (gates[:, br * HPG + h:br * HPG + h + 1] for br in range(3))
            outs.append(g_c * o_cmp[g][sl] + g_s * o_slc[sl] + g_w * o_win[g][sl])
        for pair_idx in range(HPG // 2):
            even, odd = outs[2 * pair_idx], outs[2 * pair_idx + 1]
            merged = jnp.where(lo_half, even, pltpu.roll(odd, HEAD_DIM, 1))
            lanes_p = slice((g * (HPG // 2) + pair_idx) * LANES, (g * (HPG // 2) + pair_idx + 1) * LANES)
            o_ref[0, :, lanes_p] = merged.astype(o_ref.dtype)


def _nsa(q, vkc, slc, win, gates, bias_c, bias_near, bias_win, overlap_t, cmp_ones):
    b, seq, _ = q.shape
    n_c = seq // CMP_STRIDE
    n_blk = seq // SLC_BLOCK
    assert n_blk <= LANES - HEAD_DIM and seq % SLC_TK == 0 and KV_PAD >= SLC_TK
    tq = ATTN_TQ
    rows = HPG * tq
    const = lambda a: pl.BlockSpec(a.shape, lambda i, t: (0,) * a.ndim)
    tile = lambda w: pl.BlockSpec((1, tq, w), lambda i, t: (i, t, 0))
    whole_seq = lambda w: pl.BlockSpec((1, seq, w), lambda i, t: (i, 0, 0))
    per_group = lambda shape: [pltpu.VMEM(shape, F32)] * N_KV
    return pl.pallas_call(
        functools.partial(_nsa_kernel, seq),
        grid=(b, seq // tq),
        in_specs=[
            tile(ATTN_WIDTH),
            pl.BlockSpec((1, N_KV, n_c, 2 * HEAD_DIM), lambda i, t: (i, 0, 0, 0)),
            whole_seq(N_KV * 2 * HEAD_DIM), whole_seq(N_KV * 2 * HEAD_DIM),
            tile(N_KV * GATE_PAD),
            pl.BlockSpec((N_HEADS, tq, n_c), lambda i, t: (0, t, 0)),
            const(bias_near), const(bias_win),
            const(overlap_t), const(cmp_ones),
        ],
        out_specs=tile(ATTN_WIDTH),
        out_shape=jax.ShapeDtypeStruct((b, seq, ATTN_WIDTH), BF16),
        scratch_shapes=(
            [pltpu.VMEM((KV_PAD + seq, LANES), BF16)] * (4 * N_KV)
            + per_group((rows, seq)) + per_group((rows, LANES)) + per_group((rows, LANES))
            + per_group((SUBLANES, LANES))
        ),
        compiler_params=_compiler_params(("parallel", "arbitrary")),
        name="nsa",
    )(q, vkc, slc, win, gates, bias_c, bias_near, bias_win, overlap_t, cmp_ones)


def _mixer_inputs(x, rel_bias, w_in, b_in, cmp_pos, cmp_w1, cmp_b1, cmp_w2, cmp_b2):
    b, seq, d = x.shape
    wp, bp = _pack_in_proj(w_in, b_in, d)
    q, slc, win, cmp4, gates, u, merge = _in_proj(x.reshape(b * seq, d), wp, bp, d, min(1024, seq), seq)
    vkc = _compress(cmp4, cmp_pos, cmp_w1, cmp_b1, cmp_w2, cmp_b2)
    bias_c, bias_near, bias_win = _attention_bias_tables(rel_bias, seq)
    overlap_t, cmp_ones = _nsa_constants(seq)
    o = _nsa(q.reshape(b, seq, -1), vkc, slc.reshape(b, seq, -1), win.reshape(b, seq, -1),
             gates.reshape(b, seq, -1), bias_c, bias_near, bias_win, overlap_t, cmp_ones)
    return o, u, merge


S5_HALF_GROUPS = S5_GROUPS // 2
S5_HALF_IN = S5_HALF_GROUPS * S5_GROUP
S5_HALF_STATE = S5_HALF_GROUPS * S5_STATE
S5_SCAN_LANES = 512
S5_CHUNK = 128
S5_UNROLL = 8


def _s5_params(lam_re, lam_im, log_dt, b_re, b_im, c_re, c_im, nb):
    dt = jnp.exp(log_dt.astype(F32))[:, None]
    lr, li = lam_re.astype(F32), lam_im.astype(F32)
    mag = jnp.exp(lr * dt)
    ab_re, ab_im = mag * jnp.cos(li * dt), mag * jnp.sin(li * dt)
    nr, ni = ab_re - 1.0, ab_im
    den = lr * lr + li * li
    fr, fi = (nr * lr + ni * li) / den, (ni * lr - nr * li) / den
    br, bim = b_re.astype(F32), b_im.astype(F32)
    bb_re = fr[..., None] * br - fi[..., None] * bim
    bb_im = fr[..., None] * bim + fi[..., None] * br
    eye = jnp.eye(S5_HALF_GROUPS, dtype=F32)

    def in_mat(bb):
        t = bb.reshape(2, S5_HALF_GROUPS, S5_STATE, S5_GROUP)
        m = jnp.einsum('kgph,gj->kghjp', t, eye)
        return m.reshape(2, S5_HALF_IN, S5_HALF_STATE)

    def out_mat(c):
        t = c.astype(F32).reshape(2, S5_HALF_GROUPS, S5_GROUP, S5_STATE)
        m = jnp.einsum('kghp,gj->kgpjh', t, eye)
        return m.reshape(2, S5_HALF_STATE, S5_HALF_IN)

    bmat = jnp.concatenate([in_mat(bb_re), in_mat(bb_im)], axis=2).astype(BF16)
    cmat = jnp.concatenate([out_mat(c_re), -out_mat(c_im)], axis=1).astype(BF16)
    a = jnp.concatenate([ab_re.reshape(2, S5_HALF_STATE), ab_im.reshape(2, S5_HALF_STATE)], axis=1)
    a = jnp.broadcast_to(a.reshape(1, 4 * S5_HALF_STATE), (nb, 4 * S5_HALF_STATE))
    return bmat, cmat, a


def _s5_kernel(u_ref, bmat_ref, cmat_ref, a_ref, d_ref, y_ref, ut_sc, x_sc, st_sc):
    nb, t_len, _ = u_ref.shape
    half_w = 2 * S5_HALF_STATE

    @pl.when(pl.program_id(0) == 0)
    def _():
        st_sc[...] = jnp.zeros_like(st_sc)

    n_cb = ut_sc.shape[0]
    for b in range(nb):
        for cb in range(n_cb):
            ut_sc[cb, pl.ds(b, t_len, stride=nb), :] = u_ref[b, :, cb * LANES:(cb + 1) * LANES]
    ut = jnp.concatenate([ut_sc[cb] for cb in range(n_cb)], axis=1)
    ub = ut.astype(BF16)
    for k in range(2):
        x_sc[:, k * half_w:(k + 1) * half_w] = _dot(ub[:, k * S5_HALF_IN:(k + 1) * S5_HALF_IN], bmat_ref[k])

    for k in range(2):
        for j in range(S5_HALF_STATE // S5_SCAN_LANES):
            re0 = k * half_w + j * S5_SCAN_LANES
            im0 = re0 + S5_HALF_STATE
            re_sl, im_sl = pl.ds(re0, S5_SCAN_LANES), pl.ds(im0, S5_SCAN_LANES)
            ar, ai = a_ref[:, re_sl], a_ref[:, im_sl]

            def steps(c, carry):
                xr, xi = carry
                for s in range(S5_UNROLL):
                    rows = pl.ds(pl.multiple_of((c * S5_UNROLL + s) * nb, nb), nb)
                    nxr = ar * xr - ai * xi + x_sc[rows, re_sl]
                    nxi = ar * xi + ai * xr + x_sc[rows, im_sl]
                    x_sc[rows, re_sl] = nxr
                    x_sc[rows, im_sl] = nxi
                    xr, xi = nxr, nxi
                return xr, xi

            xr, xi = lax.fori_loop(0, t_len // S5_UNROLL, steps, (st_sc[:, re_sl], st_sc[:, im_sl]))
            st_sc[:, re_sl] = xr
            st_sc[:, im_sl] = xi

    xs = x_sc[...].astype(BF16)
    y = jnp.concatenate([_dot(xs[:, k * half_w:(k + 1) * half_w], cmat_ref[k]) for k in range(2)], axis=1)
    y = _gelu_tanh(y + d_ref[...] * ut)
    for cb in range(n_cb):
        ut_sc[cb] = y[:, cb * LANES:(cb + 1) * LANES]
    for b in range(nb):
        for cb in range(n_cb):
            y_ref[b, :, cb * LANES:(cb + 1) * LANES] = ut_sc[cb, pl.ds(b, t_len, stride=nb), :].astype(BF16)


def _s5(u, bmat, cmat, a, d_skip):
    nb, seq, w = u.shape
    t_len = min(S5_CHUNK, seq)
    full = lambda shape: pl.BlockSpec(shape, lambda c: (0,) * len(shape))
    return pl.pallas_call(
        _s5_kernel,
        grid=(seq // t_len,),
        in_specs=[
            pl.BlockSpec((nb, t_len, w), lambda c: (0, c, 0)),
            full(bmat.shape), full(cmat.shape), full(a.shape), full((1, w)),
        ],
        out_specs=pl.BlockSpec((nb, t_len, w), lambda c: (0, c, 0)),
        out_shape=jax.ShapeDtypeStruct((nb, seq, w), BF16),
        scratch_shapes=[
            pltpu.VMEM((w // LANES, t_len * nb, LANES), F32),
            pltpu.VMEM((t_len * nb, 4 * S5_HALF_STATE), F32),
            pltpu.VMEM((nb, 4 * S5_HALF_STATE), F32),
        ],
        compiler_params=_compiler_params(("arbitrary",)),
        name="s5",
    )(u, bmat, cmat, a, d_skip.reshape(1, w).astype(F32))


ROUTE_PAD = LANES
RANK_CHUNK = 256
_R_E1, _R_E2, _R_W1, _R_W2, _R_RANK1, _R_RANK2 = range(6)


def _layer_norm(t, g, b):
    mu = jnp.mean(t, axis=1, keepdims=True)
    c = t - mu
    var = jnp.mean(c * c, axis=1, keepdims=True)
    return c * lax.rsqrt(var + LN_EPS) * g + b


def _post_kernel(x_ref, o_ref, y_ref, m_ref, wup_ref, wval_ref, wgate_ref, bgate_ref, wout_ref,
                 g1_ref, b1_ref, wr_ref, br_ref, h_ref, hp_ref, route_ref, cnt_ref, run_sc):
    tm, d = x_ref.shape

    @pl.when(pl.program_id(0) == 0)
    def _():
        run_sc[...] = jnp.zeros_like(run_sc)

    y_a = _dot(o_ref[...], wup_ref[...])
    z = y_ref[...]
    y_b = _dot(z, wval_ref[...]) * jax.nn.sigmoid(_dot(z, wgate_ref[...]) + bgate_ref[...])
    mixed = m_ref[:, :d].astype(F32) * y_a + m_ref[:, d:].astype(F32) * y_b
    t = DN_ALPHA * x_ref[...] + _dot(mixed.astype(BF16), wout_ref[...])
    h = _layer_norm(t, g1_ref[...], b1_ref[...])
    h_ref[...] = h
    hp_ref[...] = _pack_rows(h)

    logits = _dot(h.astype(BF16), wr_ref[...]) + br_ref[...]
    lane = lax.broadcasted_iota(jnp.int32, (tm, ROUTE_PAD), 1)
    lane_f = lane.astype(F32)
    is_group = lane < N_EGROUPS

    def first_max(v):
        top = jnp.max(v, axis=1, keepdims=True)
        idx = jnp.min(jnp.where(v == top, lane_f, float(ROUTE_PAD)), axis=1, keepdims=True)
        return top, idx

    g_max, g_top = first_max(jnp.where(is_group, logits, -jnp.inf))
    p_group = 1.0 / jnp.sum(jnp.where(is_group, jnp.exp(logits - g_max), 0.0), axis=1, keepdims=True)
    grp_of_lane = jnp.right_shift(lane - N_EGROUPS, int(math.log2(EXPERTS_PER_GROUP))).astype(F32)
    in_group = (lane >= N_EGROUPS) & (lane < N_EGROUPS + N_EXPERTS) & (grp_of_lane == g_top)
    e_log = jnp.where(in_group, logits, -jnp.inf)
    v1, i1 = first_max(e_log)
    hit1 = lane_f == i1
    v2, i2 = first_max(jnp.where(hit1, -jnp.inf, e_log))
    hit2 = lane_f == i2
    e2 = jnp.exp(v2 - v1)
    w1 = p_group / (1.0 + e2)
    w2 = p_group * e2 / (1.0 + e2)

    hits = jnp.where(hit1 | hit2, 1.0, 0.0)
    chunk = min(RANK_CHUNK, tm)
    row = lax.broadcasted_iota(jnp.int32, (chunk, chunk), 0)
    col = lax.broadcasted_iota(jnp.int32, (chunk, chunk), 1)
    earlier = jnp.where(col < row, 1.0, 0.0).astype(BF16)
    run = run_sc[...]
    before = []
    for c0 in range(0, tm, chunk):
        hits_c = hits[c0:c0 + chunk]
        before.append(_dot(earlier, hits_c.astype(BF16)) + run)
        run = run + jnp.sum(hits_c, axis=0, keepdims=True)
    before = jnp.concatenate(before, axis=0)
    rank1 = jnp.sum(jnp.where(hit1, before, 0.0), axis=1, keepdims=True)
    rank2 = jnp.sum(jnp.where(hit2, before, 0.0), axis=1, keepdims=True)
    run_sc[...] = run
    cnt_ref[...] = run

    rec = jnp.zeros((tm, ROUTE_PAD), F32)
    for slot, val in ((_R_E1, i1 - N_EGROUPS), (_R_E2, i2 - N_EGROUPS), (_R_W1, w1), (_R_W2, w2),
                      (_R_RANK1, rank1), (_R_RANK2, rank2)):
        rec = jnp.where(lane == slot, val, rec)
    route_ref[...] = rec


def _post(x2d, o2d, y2d, merge, w_attn_up, s5_w_val, s5_w_gate, s5_b_gate, w_out, ln1_g, ln1_b,
          router_w_group, router_b_group, router_w_expert, router_b_expert, tm):
    n, d = x2d.shape
    rpad = ROUTE_PAD - N_EGROUPS - N_EXPERTS
    wr = jnp.concatenate([router_w_group, router_w_expert, jnp.zeros((d, rpad), F32)], axis=1).astype(BF16)
    br = jnp.concatenate([router_b_group, router_b_expert, jnp.zeros((rpad,), F32)]).reshape(1, -1).astype(F32)
    row = lambda w: pl.BlockSpec((tm, w), lambda i: (i, 0))
    full = lambda a: pl.BlockSpec(a.shape, lambda i: (0,) * a.ndim, pipeline_mode=pl.Buffered(1))
    weights = [w_attn_up.astype(BF16), s5_w_val.astype(BF16), s5_w_gate.astype(BF16),
               s5_b_gate.reshape(1, d).astype(F32), w_out.astype(BF16),
               ln1_g.reshape(1, d).astype(F32), ln1_b.reshape(1, d).astype(F32), wr, br]
    return pl.pallas_call(
        _post_kernel,
        grid=(n // tm,),
        in_specs=[row(d), row(ATTN_WIDTH), row(S5_WIDTH), row(2 * d)] + [full(w) for w in weights],
        out_specs=[row(d), row(d // 2), row(ROUTE_PAD), pl.BlockSpec((1, ROUTE_PAD), lambda i: (0, 0))],
        out_shape=[jax.ShapeDtypeStruct((n, d), F32), jax.ShapeDtypeStruct((n, d // 2), jnp.uint32),
                   jax.ShapeDtypeStruct((n, ROUTE_PAD), F32), jax.ShapeDtypeStruct((1, ROUTE_PAD), F32)],
        scratch_shapes=[pltpu.VMEM((1, ROUTE_PAD), F32)],
        compiler_params=_compiler_params(("arbitrary",)),
        name="post_mixer",
    )(x2d, o2d, y2d, merge, *weights)


def _plan_kernel(route_ref, cnt_ref, dest_ref):
    tm = route_ref.shape[0]
    lane8 = lax.broadcasted_iota(jnp.int32, (SUBLANES, ROUTE_PAD), 1)
    counts = jnp.broadcast_to(cnt_ref[...], (SUBLANES, ROUTE_PAD)).astype(jnp.int32)
    shift = int(math.log2(MOE_BLOCK))
    padded = jnp.left_shift(jnp.right_shift(counts + (MOE_BLOCK - 1), shift), shift)
    incl = padded
    step = 1
    while step < ROUTE_PAD:
        incl = incl + jnp.where(lane8 >= step, pltpu.roll(incl, step, 1), 0)
        step *= 2
    pstart = (incl - padded)[0:1].astype(F32)
    route = route_ref[...]
    expert_of_lane = (lax.broadcasted_iota(jnp.int32, (tm, ROUTE_PAD), 1) - N_EGROUPS).astype(F32)
    lane = lax.broadcasted_iota(jnp.int32, (tm, ROUTE_PAD), 1)

    def dest(e_slot, rank_slot):
        hit = expert_of_lane == route[:, e_slot:e_slot + 1]
        return jnp.sum(jnp.where(hit, pstart, 0.0), axis=1, keepdims=True) + route[:, rank_slot:rank_slot + 1]

    d1 = dest(_R_E1, _R_RANK1)
    d2 = dest(_R_E2, _R_RANK2)
    dest_ref[...] = jnp.where(lane == 0, d1, jnp.where(lane == 1, d2, 0.0)).astype(jnp.int32)


def _plan(route, counts_row):
    n = route.shape[0]
    tm = min(4096, n)
    return pl.pallas_call(
        _plan_kernel,
        grid=(n // tm,),
        in_specs=[pl.BlockSpec((tm, ROUTE_PAD), lambda i: (i, 0)),
                  pl.BlockSpec((1, ROUTE_PAD), lambda i: (0, 0))],
        out_specs=pl.BlockSpec((tm, ROUTE_PAD), lambda i: (i, 0)),
        out_shape=jax.ShapeDtypeStruct((n, ROUTE_PAD), jnp.int32),
        compiler_params=_compiler_params(("parallel",)),
        name="moe_plan",
    )(route, counts_row)


SC_GATHER_ROWS = 128


def _sc_row_gather(table, idx):
    n_idx = idx.shape[0]
    d = table.shape[1]
    info = plsc.get_sparse_core_info()
    n_workers = info.num_cores * info.num_subcores
    per_worker = n_idx // n_workers
    assert n_idx % (n_workers * SC_GATHER_ROWS) == 0
    mesh = plsc.VectorSubcoreMesh(core_axis_name="c", subcore_axis_name="s")

    @functools.partial(
        pl.kernel, mesh=mesh,
        out_type=jax.ShapeDtypeStruct((n_idx, d), table.dtype),
        scratch_types=[
            pltpu.VMEM((SC_GATHER_ROWS,), jnp.int32),
            pltpu.VMEM((SC_GATHER_ROWS, d), table.dtype),
            pltpu.SemaphoreType.DMA,
        ],
    )
    def gather(table_hbm, idx_hbm, out_hbm, idx_v, rows_v, sem):
        worker = lax.axis_index("s") * info.num_cores + lax.axis_index("c")
        base = worker * per_worker

        @pl.loop(0, per_worker // SC_GATHER_ROWS)
        def _(j):
            off = base + j * SC_GATHER_ROWS
            pltpu.sync_copy(idx_hbm.at[pl.ds(off, SC_GATHER_ROWS)], idx_v)
            pltpu.async_copy(table_hbm.at[idx_v], rows_v, sem).wait()
            pltpu.sync_copy(rows_v, out_hbm.at[pl.ds(off, SC_GATHER_ROWS)])

    return gather(table, idx)


def _sc_row_scatter(rows, idx_a, idx_b, n_out):
    n, d = rows.shape
    info = plsc.get_sparse_core_info()
    n_workers = info.num_cores * info.num_subcores
    per_worker = n // n_workers
    assert n % (n_workers * SC_GATHER_ROWS) == 0
    mesh = plsc.VectorSubcoreMesh(core_axis_name="c", subcore_axis_name="s")

    @functools.partial(
        pl.kernel, mesh=mesh,
        out_type=jax.ShapeDtypeStruct((n_out, d), rows.dtype),
        scratch_types=[
            pltpu.VMEM((SC_GATHER_ROWS,), jnp.int32),
            pltpu.VMEM((SC_GATHER_ROWS,), jnp.int32),
            pltpu.VMEM((SC_GATHER_ROWS, d), rows.dtype),
        ],
    )
    def scatter(rows_hbm, idx_a_hbm, idx_b_hbm, out_hbm, idx_a_v, idx_b_v, rows_v):
        worker = lax.axis_index("s") * info.num_cores + lax.axis_index("c")
        base = worker * per_worker

        @pl.loop(0, per_worker // SC_GATHER_ROWS)
        def _(j):
            src = pl.ds(base + j * SC_GATHER_ROWS, SC_GATHER_ROWS)
            pltpu.sync_copy(rows_hbm.at[src], rows_v)
            pltpu.sync_copy(idx_a_hbm.at[src], idx_a_v)
            pltpu.sync_copy(idx_b_hbm.at[src], idx_b_v)
            pltpu.sync_copy(rows_v, out_hbm.at[idx_a_v])
            pltpu.sync_copy(rows_v, out_hbm.at[idx_b_v])

    return scatter(rows, idx_a, idx_b)


def _expert_kernel(blk_exp_ref, blk_valid_ref, x_ref, wg_ref, wu_ref, wd_ref, y_ref, wg_sc, wu_sc, wd_sc):
    i = pl.program_id(0)
    n_valid = blk_valid_ref[i]

    @pl.when((i == 0) | (blk_exp_ref[i] != blk_exp_ref[jnp.maximum(i - 1, 0)]))
    def _():
        wg_sc[...] = wg_ref[0].astype(BF16)
        wu_sc[...] = wu_ref[0].astype(BF16)
        wd_sc[...] = wd_ref[0].astype(BF16)

    @pl.when(n_valid > 0)
    def _():
        row = lax.broadcasted_iota(jnp.int32, x_ref.shape, 0)
        words = jnp.where(row < n_valid, x_ref[...], jnp.uint32(0))
        xb = _unpack_rows(words).astype(BF16)
        h_gate = _dot(xb, wg_sc[...])
        h_up = _dot(xb, wu_sc[...])
        hb = (h_gate * jax.nn.sigmoid(h_gate) * h_up).astype(BF16)
        y_ref[...] = _pack_rows(_dot(hb, wd_sc[...]))

    @pl.when(n_valid == 0)
    def _():
        y_ref[...] = jnp.zeros_like(y_ref)


def _experts(blk_expert, blk_valid, xs, w_gate, w_up, w_down):
    n_blocks = blk_expert.shape[0]
    d = 2 * xs.shape[1]
    grid_spec = pltpu.PrefetchScalarGridSpec(
        num_scalar_prefetch=2,
        grid=(n_blocks,),
        in_specs=[
            pl.BlockSpec((MOE_BLOCK, d // 2), lambda i, be, bv: (i, 0)),
            pl.BlockSpec((1, d, D_EXPERT), lambda i, be, bv: (be[i], 0, 0)),
            pl.BlockSpec((1, d, D_EXPERT), lambda i, be, bv: (be[i], 0, 0)),
            pl.BlockSpec((1, D_EXPERT, d), lambda i, be, bv: (be[i], 0, 0)),
        ],
        out_specs=pl.BlockSpec((MOE_BLOCK, d // 2), lambda i, be, bv: (i, 0)),
        scratch_shapes=[
            pltpu.VMEM((d, D_EXPERT), BF16),
            pltpu.VMEM((d, D_EXPERT), BF16),
            pltpu.VMEM((D_EXPERT, d), BF16),
        ],
    )
    return pl.pallas_call(
        _expert_kernel,
        grid_spec=grid_spec,
        out_shape=jax.ShapeDtypeStruct((n_blocks * MOE_BLOCK, d // 2), jnp.uint32),
        compiler_params=_compiler_params(("arbitrary",)),
        name="experts",
    )(blk_expert, blk_valid, xs, w_gate, w_up, w_down)


COMBINE_TM = 512


def _combine_kernel(h_ref, y1_ref, y2_ref, route_ref, g2_ref, b2_ref, out_ref):
    route = route_ref[...]
    w1 = route[:, _R_W1:_R_W1 + 1]
    w2 = route[:, _R_W2:_R_W2 + 1]
    t = DN_ALPHA * h_ref[...] + (_unpack_rows(y1_ref[...]) * w1 + _unpack_rows(y2_ref[...]) * w2)
    out_ref[...] = _layer_norm(t, g2_ref[...], b2_ref[...])


def _combine(yg, h2d, route, ln2_g, ln2_b):
    n, d = h2d.shape
    tm = min(COMBINE_TM, n)
    n_tiles = n // tm
    row = lambda w: pl.BlockSpec((tm, w), lambda i: (i, 0))
    vec = pl.BlockSpec((1, d), lambda i: (0, 0))
    return pl.pallas_call(
        _combine_kernel,
        grid=(n_tiles,),
        in_specs=[row(d), row(d // 2), pl.BlockSpec((tm, d // 2), lambda i: (i + n_tiles, 0)), row(ROUTE_PAD),
                  vec, vec],
        out_specs=row(d),
        out_shape=jax.ShapeDtypeStruct((n, d), F32),
        compiler_params=_compiler_params(("parallel",)),
        name="combine",
    )(h2d, yg, yg, route, ln2_g.reshape(1, d).astype(F32), ln2_b.reshape(1, d).astype(F32))


def _moe(h2d, h_packed, route, counts_row, w_gate, w_up, w_down, ln2_g, ln2_b):
    n, d = h2d.shape
    dest = _plan(route, counts_row)
    dest1, dest2 = dest[:, 0], dest[:, 1]
    counts = counts_row[0, N_EGROUPS:N_EGROUPS + N_EXPERTS].astype(jnp.int32)
    padded = (counts + MOE_BLOCK - 1) // MOE_BLOCK * MOE_BLOCK
    pend = jnp.cumsum(padded)
    n_blocks = -(-(n * TOP_K_IN_GROUP) // MOE_BLOCK) + N_EXPERTS
    blk_row0 = (jnp.arange(n_blocks, dtype=jnp.int32) * MOE_BLOCK)[:, None]
    owns = (pend - padded <= blk_row0) & (blk_row0 < pend)
    blk_expert = jnp.sum(jnp.where(owns, jnp.arange(N_EXPERTS, dtype=jnp.int32), 0), axis=1)
    blk_valid = jnp.sum(jnp.where(owns, jnp.clip(pend - padded + counts - blk_row0, 0, MOE_BLOCK), 0), axis=1)
    xs = _sc_row_scatter(h_packed, dest1, dest2, n_blocks * MOE_BLOCK)
    yb = _experts(blk_expert, blk_valid, xs, w_gate, w_up, w_down)
    yg = _sc_row_gather(yb, jnp.concatenate([dest1, dest2]))
    return _combine(yg, h2d, route, ln2_g, ln2_b)


def kernel(x, rel_bias, w_in, b_in, cmp_pos, cmp_w1, cmp_b1, cmp_w2, cmp_b2, w_attn_up, s5_lambda_re, s5_lambda_im, s5_log_dt, s5_b_re, s5_b_im, s5_c_re, s5_c_im, s5_d, s5_w_val, s5_w_gate, s5_b_gate, w_out, ln1_g, ln1_b, router_w_group, router_b_group, router_w_expert, router_b_expert, exp_w_gate, exp_w_up, exp_w_down, ln2_g, ln2_b):
    b, seq, d = x.shape
    n = b * seq
    assert w_in.shape[0] == DEPTH
    l = 0
    o, u, merge = _mixer_inputs(x, rel_bias, w_in[l], b_in[l], cmp_pos[l], cmp_w1[l], cmp_b1[l],
                                cmp_w2[l], cmp_b2[l])
    bmat, cmat, a = _s5_params(s5_lambda_re[l], s5_lambda_im[l], s5_log_dt[l], s5_b_re[l], s5_b_im[l],
                               s5_c_re[l], s5_c_im[l], b)
    y_s = _s5(u.reshape(b, seq, S5_WIDTH), bmat, cmat, a, s5_d[l])
    h2d, h_packed, route, counts = _post(
        x.reshape(n, d), o.reshape(n, ATTN_WIDTH), y_s.reshape(n, S5_WIDTH), merge, w_attn_up[l], s5_w_val[l],
        s5_w_gate[l], s5_b_gate[l], w_out[l], ln1_g[l], ln1_b[l], router_w_group[l], router_b_group[l],
        router_w_expert[l], router_b_expert[l], 1024)
    out = _moe(h2d, h_packed, route, counts, exp_w_gate[l], exp_w_up[l], exp_w_down[l], ln2_g[l], ln2_b[l])
    return out.reshape(b, seq, d)
```

```python
import functools
import math

import jax
import jax.numpy as jnp
from jax import lax
from jax.experimental import pallas as pl
from jax.experimental.pallas import tpu as pltpu
from jax.experimental.pallas import tpu_sc as plsc

F32 = jnp.float32
BF16 = jnp.bfloat16

N_HEADS = 8
HEAD_DIM = 64
N_KV = 2
HPG = N_HEADS // N_KV
CMP_STRIDE = 16
CMP_BLOCK = 2 * CMP_STRIDE
CMP_HIDDEN = 128
SLC_BLOCK = 64
N_SEL = 16
WINDOW = 512
REL_BUCKETS = 32
REL_MAX_DIST = 128
S5_WIDTH = 512
S5_GROUP = 16
S5_GROUPS = S5_WIDTH // S5_GROUP
S5_STATE = 64
N_EGROUPS = 8
EXPERTS_PER_GROUP = 8
N_EXPERTS = N_EGROUPS * EXPERTS_PER_GROUP
TOP_K_IN_GROUP = 2
D_EXPERT = 256
EXPERT_BLOCK = 128
DEPTH = 1
DN_ALPHA = (2.0 * DEPTH) ** 0.25
LN_EPS = 1e-5
NEG_INF = -1e30
BIG = 1e9
LOG2_E = math.log2(math.e)
MOE_BLOCK = 512

ATTN_WIDTH = N_HEADS * HEAD_DIM
KV_WIDTH = N_KV * HEAD_DIM
KV_OFF = ATTN_WIDTH
NSA_GATE_OFF = KV_OFF + 6 * KV_WIDTH
S5_OFF = NSA_GATE_OFF + 3 * N_HEADS
MERGE_OFF = S5_OFF + S5_WIDTH

LANES = 128
SUBLANES = 8
V7X_VMEM_BYTES = 64 * 1024 * 1024
VMEM_LIMIT_BYTES = V7X_VMEM_BYTES - 8 * 1024 * 1024

ATTN_TQ = 128
SLC_TK = 512
SLC_NEAR_BACK = SLC_TK - ATTN_TQ
KV_PAD = WINDOW
SOFTMAX_SHIFT_MARGIN = 100.0
SOFTMAX_BOUND_SLACK = 1.001
GATE_PAD = LANES


def _gelu_tanh(x):
    c = math.sqrt(2.0 / math.pi)
    return x * (0.5 * (1.0 + jnp.tanh(c * (x + 0.044715 * (x * x * x)))))


def _dot(a, b):
    return jnp.dot(a, b, preferred_element_type=F32)


def _dot_nt(a, b):
    return lax.dot_general(a, b, (((1,), (1,)), ((), ())), preferred_element_type=F32)


def _pack_rows(x):
    half = x.shape[1] // 2
    xb = x.astype(BF16).astype(F32)
    hi = lax.bitcast_convert_type(xb[:, :half], jnp.uint32) & jnp.uint32(0xFFFF0000)
    lo = lax.shift_right_logical(lax.bitcast_convert_type(xb[:, half:], jnp.uint32), jnp.uint32(16))
    return hi | lo


def _unpack_rows(w):
    hi = lax.bitcast_convert_type(w & jnp.uint32(0xFFFF0000), F32)
    lo = lax.bitcast_convert_type(lax.shift_left(w, jnp.uint32(16)), F32)
    return jnp.concatenate([hi, lo], axis=1)


def _compiler_params(semantics):
    return pltpu.CompilerParams(dimension_semantics=semantics, vmem_limit_bytes=VMEM_LIMIT_BYTES)


def _in_proj_layout(d_model):
    widths = (ATTN_WIDTH, 2 * KV_WIDTH, 2 * KV_WIDTH, 2 * KV_WIDTH, N_KV * GATE_PAD, S5_WIDTH, 2 * d_model)
    offs = [0]
    for w in widths:
        offs.append(offs[-1] + w)
    return widths, offs


def _pack_in_proj(w_in, b_in, d_model):
    def kv_cols(j):
        return KV_OFF + j * KV_WIDTH

    def pair(jk, jv):
        cols = []
        for g in range(N_KV):
            cols.append(jnp.arange(kv_cols(jk) + g * HEAD_DIM, kv_cols(jk) + (g + 1) * HEAD_DIM))
            cols.append(jnp.arange(kv_cols(jv) + g * HEAD_DIM, kv_cols(jv) + (g + 1) * HEAD_DIM))
        return jnp.concatenate(cols)

    idx = jnp.concatenate([
        jnp.arange(0, ATTN_WIDTH),
        pair(2, 3),
        pair(4, 5),
        jnp.arange(kv_cols(0), kv_cols(2)),
    ])
    idx2 = jnp.concatenate([jnp.arange(S5_OFF, S5_OFF + S5_WIDTH),
                            jnp.arange(MERGE_OFF, MERGE_OFF + 2 * d_model)])
    gpad = GATE_PAD - 3 * HPG
    w_parts, b_parts = [w_in[:, idx]], [b_in[idx]]
    for g in range(N_KV):
        cols = jnp.asarray([NSA_GATE_OFF + (g * HPG + h) * 3 + j for j in range(3) for h in range(HPG)])
        w_parts += [w_in[:, cols], jnp.zeros((d_model, gpad), F32)]
        b_parts += [b_in[cols], jnp.zeros((gpad,), F32)]
    w = jnp.concatenate(w_parts + [w_in[:, idx2]], axis=1)
    b = jnp.concatenate(b_parts + [b_in[idx2]])
    return w.astype(BF16), b.reshape(1, -1).astype(F32)


def _in_proj_kernel(offs, x_ref, w_ref, b_ref, q_ref, slc_ref, win_ref, cmp_ref, g_ref, u_ref, m_ref, cmp_sc):
    xb = x_ref[...].astype(BF16)

    def proj(i):
        c0, c1 = offs[i], offs[i + 1]
        return _dot(xb, w_ref[:, c0:c1]) + b_ref[:, c0:c1]

    q_ref[...] = (proj(0) * (HEAD_DIM ** -0.5 * LOG2_E)).astype(BF16)
    slc_ref[...] = proj(1).astype(BF16)
    win_ref[...] = proj(2).astype(BF16)
    g_ref[...] = jax.nn.sigmoid(proj(4))
    u_ref[...] = proj(5)
    m_ref[...] = jax.nn.sigmoid(proj(6)).astype(BF16)

    c = proj(3)
    n_chunks = c.shape[0] // CMP_STRIDE
    lo_half = lax.broadcasted_iota(jnp.int32, (n_chunks, LANES), 1) < HEAD_DIM
    for p in range(c.shape[1] // LANES):
        cmp_sc[p] = c[:, p * LANES:(p + 1) * LANES]
        for t in range(CMP_STRIDE // 2):
            tok_a = cmp_sc[p, pl.ds(2 * t, n_chunks, stride=CMP_STRIDE), :]
            tok_b = cmp_sc[p, pl.ds(2 * t + 1, n_chunks, stride=CMP_STRIDE), :]
            lanes_t = slice(t * LANES, (t + 1) * LANES)
            cmp_ref[0, 2 * p, :, lanes_t] = jnp.where(lo_half, tok_a, pltpu.roll(tok_b, HEAD_DIM, 1))
            cmp_ref[0, 2 * p + 1, :, lanes_t] = jnp.where(lo_half, pltpu.roll(tok_a, HEAD_DIM, 1), tok_b)


def _in_proj(x2d, w_packed, b_packed, d_model, tm, seq):
    n = x2d.shape[0]
    widths, offs = _in_proj_layout(d_model)
    ncols = offs[-1]
    dtypes = (BF16, BF16, BF16, F32, F32, F32, BF16)
    tiles_per_seq = seq // tm
    assert seq % tm == 0 and tm % (CMP_STRIDE * SUBLANES) == 0
    planes = widths[3] // HEAD_DIM
    cmp_spec = pl.BlockSpec((1, planes, tm // CMP_STRIDE, CMP_STRIDE * HEAD_DIM),
                            lambda i: (i // tiles_per_seq, 0, i % tiles_per_seq, 0))
    cmp_shape = jax.ShapeDtypeStruct((n // seq, planes, seq // CMP_STRIDE, CMP_STRIDE * HEAD_DIM), F32)
    row_spec = lambda w: pl.BlockSpec((tm, w), lambda i: (i, 0))
    return pl.pallas_call(
        functools.partial(_in_proj_kernel, tuple(offs)),
        grid=(n // tm,),
        in_specs=[
            pl.BlockSpec((tm, d_model), lambda i: (i, 0)),
            pl.BlockSpec((d_model, ncols), lambda i: (0, 0), pipeline_mode=pl.Buffered(1)),
            pl.BlockSpec((1, ncols), lambda i: (0, 0)),
        ],
        out_specs=[cmp_spec if j == 3 else row_spec(w) for j, w in enumerate(widths)],
        out_shape=[cmp_shape if j == 3 else jax.ShapeDtypeStruct((n, w), dt)
                   for j, (w, dt) in enumerate(zip(widths, dtypes))],
        scratch_shapes=[pltpu.VMEM((widths[3] // LANES, tm, LANES), F32)],
        compiler_params=_compiler_params(("parallel",)),
        name="in_proj",
    )(x2d, w_packed, b_packed)


def _compress_kernel(ck_ref, cv_ref, pos_ref, w1_ref, b1_ref, w2_ref, b2_ref, out_ref):
    n_c = ck_ref.shape[2]
    outs = []
    for i, c_ref in enumerate((ck_ref, cv_ref)):
        c = c_ref[0, 0]
        lo = (c + pos_ref[i, 0:1, :]).astype(BF16)
        hi = (c + pos_ref[i, 1:2, :]).astype(BF16)
        p_lo = _dot(lo, w1_ref[i, 0])
        p_hi = _dot(hi, w1_ref[i, 1])
        hid = p_lo + pltpu.roll(p_hi, n_c - 1, 0) + b1_ref[i]
        hid = _gelu_tanh(hid).astype(BF16)
        outs.append(_dot(hid, w2_ref[i]) + b2_ref[i])
    out_ref[0, 0] = jnp.concatenate(outs[::-1], axis=1).astype(BF16)


def _compress(cmp4, cmp_pos, cmp_w1, cmp_b1, cmp_w2, cmp_b2):
    b, _, n_c, cw = cmp4.shape
    half = CMP_STRIDE * HEAD_DIM
    pos = cmp_pos.reshape(2, 2, half).astype(F32)
    w1 = cmp_w1.reshape(2, 2, half, CMP_HIDDEN).astype(BF16)
    b1 = cmp_b1.reshape(2, 1, CMP_HIDDEN).astype(F32)
    w2 = cmp_w2.astype(BF16)
    b2 = cmp_b2.reshape(2, 1, HEAD_DIM).astype(F32)
    full = lambda shape: pl.BlockSpec(shape, lambda i, g: (0,) * len(shape))
    return pl.pallas_call(
        _compress_kernel,
        grid=(b, N_KV),
        in_specs=[
            pl.BlockSpec((1, 1, n_c, cw), lambda i, g: (i, g, 0, 0)),
            pl.BlockSpec((1, 1, n_c, cw), lambda i, g: (i, N_KV + g, 0, 0)),
            full((2, 2, half)),
            full((2, 2, half, CMP_HIDDEN)),
            full((2, 1, CMP_HIDDEN)),
            full((2, CMP_HIDDEN, HEAD_DIM)),
            full((2, 1, HEAD_DIM)),
        ],
        out_specs=pl.BlockSpec((1, 1, n_c, 2 * HEAD_DIM), lambda i, g: (i, g, 0, 0)),
        out_shape=jax.ShapeDtypeStruct((b, N_KV, n_c, 2 * HEAD_DIM), BF16),
        compiler_params=_compiler_params(("parallel", "parallel")),
        name="compress",
    )(cmp4, cmp4, pos, w1, b1, w2, b2)


def _t5_bucket(dist):
    n = jnp.maximum(dist, 0)
    max_exact = REL_BUCKETS // 2
    nf = jnp.maximum(n, 1).astype(F32)
    large = max_exact + (jnp.log(nf / max_exact) / math.log(REL_MAX_DIST / max_exact)
                         * (REL_BUCKETS - max_exact)).astype(jnp.int32)
    large = jnp.minimum(large, REL_BUCKETS - 1)
    return jnp.where(n < max_exact, n, large)


def _bucket_thresholds():
    buckets = _t5_bucket(jnp.arange(REL_MAX_DIST + 1))
    return jnp.sum(buckets[None, :] < jnp.arange(REL_BUCKETS)[:, None], axis=1).astype(jnp.int32)


def _bias_of_dist(dist, head, thr_ref, tbl_ref):
    bias = jnp.full(dist.shape, tbl_ref[head], F32)
    for k in range(1, REL_BUCKETS):
        bias = jnp.where(dist >= thr_ref[k], tbl_ref[k * N_HEADS + head], bias)
    return bias


BIAS_ROWS = 32


def _bias_c_kernel(thr_ref, tbl_ref, out_ref):
    _, tr, n_c = out_ref.shape
    r0 = pl.program_id(0) * tr

    def chunk(ci, carry):
        row0 = pl.multiple_of(ci * BIAS_ROWS, BIAS_ROWS)
        rows = pl.ds(row0, BIAS_ROWS)
        for c0 in range(0, n_c, LANES):
            width = min(LANES, n_c - c0)
            cols = slice(c0, c0 + width)
            pos = r0 + row0 + lax.broadcasted_iota(jnp.int32, (BIAS_ROWS, width), 0)
            key_end = ((c0 + lax.broadcasted_iota(jnp.int32, (BIAS_ROWS, width), 1)) * CMP_STRIDE
                       + (CMP_BLOCK - 1))
            dist = pos - key_end
            d_min = r0 + row0 - ((c0 + width - 1) * CMP_STRIDE + CMP_BLOCK - 1)
            d_max = r0 + row0 + (BIAS_ROWS - 1) - (c0 * CMP_STRIDE + CMP_BLOCK - 1)

            @pl.when(d_max < 0)
            def _():
                for h in range(N_HEADS):
                    out_ref[h, rows, cols] = jnp.full((BIAS_ROWS, width), NEG_INF, F32)

            @pl.when(d_min >= REL_MAX_DIST)
            def _():
                for h in range(N_HEADS):
                    out_ref[h, rows, cols] = jnp.full((BIAS_ROWS, width),
                                                      tbl_ref[(REL_BUCKETS - 1) * N_HEADS + h], F32)

            @pl.when((d_max >= 0) & (d_min < REL_MAX_DIST))
            def _():
                for h in range(N_HEADS):
                    bias = _bias_of_dist(dist, h, thr_ref, tbl_ref)
                    out_ref[h, rows, cols] = jnp.where(dist >= 0, bias, NEG_INF)
        return carry

    lax.fori_loop(0, tr // BIAS_ROWS, chunk, 0)


def _bias_near_kernel(thr_ref, tbl_ref, near_ref, win_ref):
    tq = ATTN_TQ
    h = pl.program_id(0)
    far_bias = tbl_ref[(REL_BUCKETS - 1) * N_HEADS + h]

    def table(out_ref, lo_keys, window, offset):
        width = out_ref.shape[2]

        def chunk(ci, carry):
            row0 = pl.multiple_of(ci * BIAS_ROWS, BIAS_ROWS)
            dist = (lo_keys + row0 + lax.broadcasted_iota(jnp.int32, (BIAS_ROWS, width), 0)
                    - lax.broadcasted_iota(jnp.int32, (BIAS_ROWS, width), 1))
            visible = (dist >= 0) & (dist < window)
            bias = jnp.full(dist.shape, tbl_ref[h], F32)
            for k in range(1, REL_BUCKETS):
                bias = jnp.where(dist >= thr_ref[k], tbl_ref[k * N_HEADS + h], bias)
            out_ref[0, pl.ds(row0, BIAS_ROWS), :] = jnp.where(visible, bias - offset, NEG_INF)
            return carry

        lax.fori_loop(0, tq // BIAS_ROWS, chunk, 0)

    table(near_ref, SLC_NEAR_BACK, 1 << 30, far_bias)
    table(win_ref, WINDOW, WINDOW, 0.0)


def _attention_bias_tables(rel_bias, seq):
    tbl = (rel_bias.astype(F32) * LOG2_E).reshape(REL_BUCKETS * N_HEADS)
    thr = _bucket_thresholds()
    tq = ATTN_TQ
    n_c = seq // CMP_STRIDE
    smem = pl.BlockSpec(memory_space=pltpu.SMEM)
    tr = min(512, seq)
    bias_c = pl.pallas_call(
        _bias_c_kernel,
        grid=(seq // tr,),
        in_specs=[smem, smem],
        out_specs=pl.BlockSpec((N_HEADS, tr, n_c), lambda i: (0, i, 0)),
        out_shape=jax.ShapeDtypeStruct((N_HEADS, seq, n_c), F32),
        compiler_params=_compiler_params(("parallel",)),
        name="bias_cmp",
    )(thr, tbl)
    head_block = lambda w: pl.BlockSpec((1, tq, w), lambda h: (h, 0, 0))
    widths = (SLC_NEAR_BACK + tq, WINDOW + tq)
    bias_near, bias_win = pl.pallas_call(
        _bias_near_kernel,
        grid=(N_HEADS,),
        in_specs=[smem, smem],
        out_specs=[head_block(w) for w in widths],
        out_shape=[jax.ShapeDtypeStruct((N_HEADS, tq, w), F32) for w in widths],
        compiler_params=_compiler_params(("parallel",)),
        name="bias_near",
    )(thr, tbl)
    return bias_c, bias_near, bias_win


def _nsa_constants(seq):
    n_c = seq // CMP_STRIDE
    n_blk = seq // SLC_BLOCK
    cmp_start = jnp.arange(n_c) * CMP_STRIDE
    blk_start = jnp.arange(n_blk) * SLC_BLOCK
    overlap_t = ((cmp_start[None, :] <= blk_start[:, None] + SLC_BLOCK - 1)
                 & (cmp_start[None, :] + CMP_BLOCK - 1 >= blk_start[:, None]))
    overlap_t = overlap_t & (cmp_start[None, :] + CMP_BLOCK <= seq)
    ones_rows = jnp.arange(SUBLANES)[:, None] == 0
    overlap_t = jnp.concatenate([overlap_t, jnp.broadcast_to(ones_rows, (SUBLANES, n_c))], axis=0)
    cmp_ones = jnp.broadcast_to(jnp.arange(LANES)[None, :] == 0, (n_c, LANES))
    return overlap_t.astype(BF16), cmp_ones.astype(BF16)


def _build_kv_scratch(seq, slc_ref, win_ref, g, ks_sc, vs_sc, kw_sc, vw_sc, knorm_sc):
    chunk = min(512, seq)
    lane_p = lax.broadcasted_iota(jnp.int32, (KV_PAD, LANES), 1)
    zeros = jnp.zeros((KV_PAD, LANES), BF16)
    ks_sc[0:KV_PAD] = jnp.where(lane_p >= HEAD_DIM, 1.0, 0.0).astype(BF16)
    kw_sc[0:KV_PAD] = jnp.where(lane_p == HEAD_DIM, NEG_INF, 0.0).astype(BF16)
    vs_sc[0:KV_PAD] = zeros
    vw_sc[0:KV_PAD] = zeros
    lane = lax.broadcasted_iota(jnp.int32, (chunk, LANES), 1)
    row = lax.broadcasted_iota(jnp.int32, (chunk, LANES), 0)
    lo_half = lane < HEAD_DIM
    ones_lane = jnp.where(lane == HEAD_DIM, 1.0, 0.0)
    k_sq_max = jnp.zeros((chunk, 1), F32)
    for c in range(seq // chunk):
        r0 = c * chunk
        dst = slice(KV_PAD + r0, KV_PAD + r0 + chunk)
        blk = jnp.right_shift(r0 + row, int(math.log2(SLC_BLOCK)))
        lanes_g = slice(g * LANES, (g + 1) * LANES)
        slab = slc_ref[0, r0:r0 + chunk, lanes_g].astype(F32)
        k_sq_max = jnp.maximum(k_sq_max, jnp.sum(jnp.where(lo_half, slab * slab, 0.0), axis=1, keepdims=True))
        ks_sc[dst] = jnp.where(lo_half, slab, jnp.where(lane - HEAD_DIM == blk, 1.0, 0.0)).astype(BF16)
        vs_sc[dst] = jnp.where(lo_half, pltpu.roll(slab, HEAD_DIM, 1), ones_lane).astype(BF16)
        slab = win_ref[0, r0:r0 + chunk, lanes_g].astype(F32)
        kw_sc[dst] = jnp.where(lo_half, slab, 0.0).astype(BF16)
        vw_sc[dst] = jnp.where(lo_half, pltpu.roll(slab, HEAD_DIM, 1), ones_lane).astype(BF16)
    knorm_sc[...] = jnp.broadcast_to(jnp.max(k_sq_max, axis=0, keepdims=True), knorm_sc.shape)


def _nsa_select(seq, g, q0, q_ref, vkc_ref, bias_c_ref, overlap_t_ref, cmp_ones_ref):
    tq = ATTN_TQ
    n_blk = seq // SLC_BLOCK
    n_sel = min(N_SEL, n_blk)
    rows = HPG * tq
    lane = lax.broadcasted_iota(jnp.int32, (tq, LANES), 1)
    lo_half = lane < HEAD_DIM
    lane_r = lax.broadcasted_iota(jnp.int32, (rows, LANES), 1)

    q_lo, q_hi = [], []
    for pair_idx in range(HPG // 2):
        lanes_p = slice((g * (HPG // 2) + pair_idx) * LANES, (g * (HPG // 2) + pair_idx + 1) * LANES)
        q2 = q_ref[0, :, lanes_p].astype(F32)
        q2r = pltpu.roll(q2, HEAD_DIM, 1)
        q_lo += [jnp.where(lo_half, q2, 0.0), jnp.where(lo_half, q2r, 0.0)]
        q_hi += [jnp.where(lo_half, 0.0, q2r), jnp.where(lo_half, 0.0, q2)]
    q_lo = jnp.concatenate(q_lo, axis=0)
    q_hi = jnp.concatenate(q_hi, axis=0)

    vkc = vkc_ref[0, g]
    s_c = _dot_nt(q_hi.astype(BF16), vkc) + bias_c_ref[g * HPG:(g + 1) * HPG].reshape(rows, -1)
    m_c = jnp.max(s_c, axis=1, keepdims=True)
    e_cb = jnp.exp2(s_c - m_c).astype(BF16)
    pv_c = _dot(e_cb, jnp.concatenate([vkc, cmp_ones_ref[...]], axis=1))
    row_pos = q0 + (lax.broadcasted_iota(jnp.int32, (rows, LANES), 0) & (tq - 1))
    has_key = row_pos >= CMP_BLOCK - 1
    o_cmp = jnp.where(has_key, pv_c[:, :LANES] / pv_c[:, LANES:LANES + 1], 0.0)

    imp_t4 = _dot_nt(overlap_t_ref[...], e_cb)
    imp_t = None
    for h in range(HPG):
        part = imp_t4[:, h * tq:(h + 1) * tq]
        part = part[:n_blk] / part[n_blk:n_blk + 1]
        imp_t = part if imp_t is None else imp_t + part

    blk = lax.broadcasted_iota(jnp.int32, (n_blk, tq), 0)
    pos = q0 + lax.broadcasted_iota(jnp.int32, (n_blk, tq), 1)
    cur = jnp.right_shift(pos, int(math.log2(SLC_BLOCK)))
    forced = (blk == 0) | (blk == cur) | (blk == cur - 1)
    valid = blk * SLC_BLOCK <= pos
    score = jnp.where(forced, BIG, jnp.where(valid, imp_t, -BIG))
    blk_f = blk.astype(F32)
    pen_t = jnp.full((n_blk, tq), NEG_INF, F32)
    for _ in range(n_sel):
        top = jnp.max(score, axis=0, keepdims=True)
        first = jnp.min(jnp.where(score == top, blk_f, float(n_blk)), axis=0, keepdims=True)
        hit = blk_f == first
        pen_t = jnp.where(hit, 0.0, pen_t)
        score = jnp.where(hit, -jnp.inf, score)
    pieces = [jnp.zeros((tq, HEAD_DIM), F32), jnp.transpose(pen_t)]
    if n_blk < LANES - HEAD_DIM:
        pieces.append(jnp.full((tq, LANES - HEAD_DIM - n_blk), NEG_INF, F32))
    pen_lanes = jnp.concatenate(pieces, axis=1)
    q_slc = jnp.where(lane_r < HEAD_DIM, q_lo, jnp.concatenate([pen_lanes] * HPG, axis=0)).astype(BF16)
    return o_cmp, q_lo, q_slc


def _nsa_kernel(seq, q_ref, vkc_ref, slc_ref, win_ref, gate_ref, bias_c_ref, bias_near_ref, bias_win_ref,
                overlap_t_ref, cmp_ones_ref, o_ref, *scratch):
    tq = ATTN_TQ
    rows = HPG * tq
    groups = range(N_KV)
    kv_sc = [scratch[4 * g:4 * g + 4] for g in groups]
    s_sc, mrun_sc, acc_sc, knorm_sc = (scratch[4 * N_KV + i * N_KV:4 * N_KV + (i + 1) * N_KV] for i in range(4))
    qt = pl.program_id(1)
    q0 = pl.multiple_of(qt * tq, tq)
    lane = lax.broadcasted_iota(jnp.int32, (tq, LANES), 1)
    lo_half = lane < HEAD_DIM
    lane_r = lax.broadcasted_iota(jnp.int32, (rows, LANES), 1)

    @pl.when(qt == 0)
    def _():
        for g in groups:
            _build_kv_scratch(seq, slc_ref, win_ref, g, *kv_sc[g], knorm_sc[g])

    sel = [_nsa_select(seq, g, q0, q_ref, vkc_ref, bias_c_ref, overlap_t_ref, cmp_ones_ref) for g in groups]
    o_cmp = [s[0] for s in sel]
    q_lo = [s[1] for s in sel]
    q_slc = [s[2] for s in sel]
    head_rows = lambda ref, g: ref[g * HPG:(g + 1) * HPG].reshape(rows, -1)

    o_win = []
    for g in groups:
        kw_sc, vw_sc = kv_sc[g][2], kv_sc[g][3]
        q_win = jnp.where(lane_r == HEAD_DIM, 1.0, q_lo[g]).astype(BF16)
        win_rows = pl.ds(q0, WINDOW + tq)
        s_w = _dot_nt(q_win, kw_sc[win_rows, :]) + head_rows(bias_win_ref, g)
        p_w = jnp.exp2(s_w - jnp.max(s_w, axis=1, keepdims=True)).astype(BF16)
        acc_w = _dot(p_w, vw_sc[win_rows, :])
        o_win.append(acc_w / acc_w[:, HEAD_DIM:HEAD_DIM + 1])

    tk = SLC_TK
    n_far = qt // (tk // tq)
    lane_tiles = lambda a: [a[:, j * LANES:(j + 1) * LANES] for j in range(a.shape[1] // LANES)]
    near0 = pl.multiple_of(q0 + (KV_PAD - SLC_NEAR_BACK), tq)

    def tile_rows(u):
        return pl.ds(pl.multiple_of(near0 - u * tk, tq), tk)

    def tile_cols(u):
        return pl.ds(pl.multiple_of(u * tk, tk), tk)

    def score_tile(u, g, bias):
        s = _dot_nt(q_slc[g], kv_sc[g][0][tile_rows(u), :])
        if bias is not None:
            s = s + bias
        s_sc[g][:, tile_cols(u)] = s
        return functools.reduce(jnp.maximum, lane_tiles(s))

    shift_cap = []
    slack = jnp.float32(-jnp.inf)
    for g in groups:
        lane_max = score_tile(0, g, head_rows(bias_near_ref, g))
        mrun_sc[g][...] = lane_max
        m_near = jnp.max(lane_max, axis=1, keepdims=True)
        q_norm = jnp.sqrt(jnp.sum(q_lo[g] * q_lo[g], axis=1, keepdims=True))
        far_bound = q_norm * jnp.sqrt(knorm_sc[g][0:1, 0:1]) * SOFTMAX_BOUND_SLACK
        shift_cap.append(jnp.maximum(m_near, far_bound - SOFTMAX_SHIFT_MARGIN))
        slack = jnp.maximum(slack, jnp.max(far_bound - m_near))
    single_pass = slack <= 2.0 * SOFTMAX_SHIFT_MARGIN

    def weigh(s, u, g):
        shift = mrun_sc[g][...]
        p = jnp.exp2(s - jnp.concatenate([shift] * (tk // LANES), axis=1))
        acc_sc[g][...] = acc_sc[g][...] + _dot(p.astype(BF16), kv_sc[g][1][tile_rows(u), :])

    for g in groups:
        acc_sc[g][...] = jnp.zeros((rows, LANES), F32)

    @pl.when(single_pass)
    def _():
        for g in groups:
            mrun_sc[g][...] = jnp.broadcast_to(shift_cap[g], (rows, LANES))
            weigh(s_sc[g][:, tile_cols(0)], 0, g)

        def fused(u, carry):
            for g in groups:
                weigh(_dot_nt(q_slc[g], kv_sc[g][0][tile_rows(u), :]), u, g)
            return carry

        lax.fori_loop(1, n_far + 1, fused, 0)

    @pl.when(jnp.logical_not(single_pass))
    def _():
        def pass1(u, carry):
            for g in groups:
                mrun_sc[g][...] = jnp.maximum(mrun_sc[g][...], score_tile(u, g, None))
            return carry

        lax.fori_loop(1, n_far + 1, pass1, 0)
        for g in groups:
            m_s = jnp.max(mrun_sc[g][...], axis=1, keepdims=True)
            mrun_sc[g][...] = jnp.broadcast_to(m_s, (rows, LANES))

        def pass2(u, carry):
            for g in groups:
                weigh(s_sc[g][:, tile_cols(u)], u, g)
            return carry

        lax.fori_loop(0, n_far + 1, pass2, 0)

    for g in groups:
        acc = acc_sc[g][...]
        o_slc = acc / acc[:, HEAD_DIM:HEAD_DIM + 1]

        gates = gate_ref[0, :, g * GATE_PAD:(g + 1) * GATE_PAD]
        outs = []
        for h in range(HPG):
            sl = slice(h * tq, (h + 1) * tq)
            g_c, g_s, g_w = (gates[:, br * HPG + h:br * HPG + h + 1] for br in range(3))
            outs.append(g_c * o_cmp[g][sl] + g_s * o_slc[sl] + g_w * o_win[g][sl])
        for pair_idx in range(HPG // 2):
            even, odd = outs[2 * pair_idx], outs[2 * pair_idx + 1]
            merged = jnp.where(lo_half, even, pltpu.roll(odd, HEAD_DIM, 1))
            lanes_p = slice((g * (HPG // 2) + pair_idx) * LANES, (g * (HPG // 2) + pair_idx + 1) * LANES)
            o_ref[0, :, lanes_p] = merged.astype(o_ref.dtype)


def _nsa(q, vkc, slc, win, gates, bias_c, bias_near, bias_win, overlap_t, cmp_ones):
    b, seq, _ = q.shape
    n_c = seq // CMP_STRIDE
    n_blk = seq // SLC_BLOCK
    assert n_blk <= LANES - HEAD_DIM and seq % SLC_TK == 0 and KV_PAD >= SLC_TK
    tq = ATTN_TQ
    rows = HPG * tq
    const = lambda a: pl.BlockSpec(a.shape, lambda i, t: (0,) * a.ndim)
    tile = lambda w: pl.BlockSpec((1, tq, w), lambda i, t: (i, t, 0))
    whole_seq = lambda w: pl.BlockSpec((1, seq, w), lambda i, t: (i, 0, 0))
    per_group = lambda shape: [pltpu.VMEM(shape, F32)] * N_KV
    return pl.pallas_call(
        functools.partial(_nsa_kernel, seq),
        grid=(b, seq // tq),
        in_specs=[
            tile(ATTN_WIDTH),
            pl.BlockSpec((1, N_KV, n_c, 2 * HEAD_DIM), lambda i, t: (i, 0, 0, 0)),
            whole_seq(N_KV * 2 * HEAD_DIM), whole_seq(N_KV * 2 * HEAD_DIM),
            tile(N_KV * GATE_PAD),
            pl.BlockSpec((N_HEADS, tq, n_c), lambda i, t: (0, t, 0)),
            const(bias_near), const(bias_win),
            const(overlap_t), const(cmp_ones),
        ],
        out_specs=tile(ATTN_WIDTH),
        out_shape=jax.ShapeDtypeStruct((b, seq, ATTN_WIDTH), BF16),
        scratch_shapes=(
            [pltpu.VMEM((KV_PAD + seq, LANES), BF16)] * (4 * N_KV)
            + per_group((rows, seq)) + per_group((rows, LANES)) + per_group((rows, LANES))
            + per_group((SUBLANES, LANES))
        ),
        compiler_params=_compiler_params(("parallel", "arbitrary")),
        name="nsa",
    )(q, vkc, slc, win, gates, bias_c, bias_near, bias_win, overlap_t, cmp_ones)


def _mixer_inputs(x, rel_bias, w_in, b_in, cmp_pos, cmp_w1, cmp_b1, cmp_w2, cmp_b2):
    b, seq, d = x.shape
    wp, bp = _pack_in_proj(w_in, b_in, d)
    q, slc, win, cmp4, gates, u, merge = _in_proj(x.reshape(b * seq, d), wp, bp, d, min(1024, seq), seq)
    vkc = _compress(cmp4, cmp_pos, cmp_w1, cmp_b1, cmp_w2, cmp_b2)
    bias_c, bias_near, bias_win = _attention_bias_tables(rel_bias, seq)
    overlap_t, cmp_ones = _nsa_constants(seq)
    o = _nsa(q.reshape(b, seq, -1), vkc, slc.reshape(b, seq, -1), win.reshape(b, seq, -1),
             gates.reshape(b, seq, -1), bias_c, bias_near, bias_win, overlap_t, cmp_ones)
    return o, u, merge


S5_HALF_GROUPS = S5_GROUPS // 2
S5_HALF_IN = S5_HALF_GROUPS * S5_GROUP
S5_HALF_STATE = S5_HALF_GROUPS * S5_STATE
S5_SCAN_LANES = 1024
S5_CHUNK = 128
S5_UNROLL = 8


def _s5_params(lam_re, lam_im, log_dt, b_re, b_im, c_re, c_im, nb):
    dt = jnp.exp(log_dt.astype(F32))[:, None]
    lr, li = lam_re.astype(F32), lam_im.astype(F32)
    mag = jnp.exp(lr * dt)
    ab_re, ab_im = mag * jnp.cos(li * dt), mag * jnp.sin(li * dt)
    nr, ni = ab_re - 1.0, ab_im
    den = lr * lr + li * li
    fr, fi = (nr * lr + ni * li) / den, (ni * lr - nr * li) / den
    br, bim = b_re.astype(F32), b_im.astype(F32)
    bb_re = fr[..., None] * br - fi[..., None] * bim
    bb_im = fr[..., None] * bim + fi[..., None] * br
    eye = jnp.eye(S5_HALF_GROUPS, dtype=F32)

    def in_mat(bb):
        t = bb.reshape(2, S5_HALF_GROUPS, S5_STATE, S5_GROUP)
        m = jnp.einsum('kgph,gj->kghjp', t, eye)
        return m.reshape(2, S5_HALF_IN, S5_HALF_STATE)

    def out_mat(c):
        t = c.astype(F32).reshape(2, S5_HALF_GROUPS, S5_GROUP, S5_STATE)
        m = jnp.einsum('kghp,gj->kgpjh', t, eye)
        return m.reshape(2, S5_HALF_STATE, S5_HALF_IN)

    bmat = jnp.concatenate([in_mat(bb_re), in_mat(bb_im)], axis=2).astype(BF16)
    cmat = jnp.concatenate([out_mat(c_re), -out_mat(c_im)], axis=1).astype(BF16)
    a = jnp.concatenate([ab_re.reshape(2, S5_HALF_STATE), ab_im.reshape(2, S5_HALF_STATE)], axis=1)
    a = jnp.broadcast_to(a.reshape(1, 4 * S5_HALF_STATE), (nb, 4 * S5_HALF_STATE))
    return bmat, cmat, a


def _s5_kernel(u_ref, bmat_ref, cmat_ref, a_ref, d_ref, y_ref, ut_sc, x_sc, st_sc):
    nb, t_len, _ = u_ref.shape
    half_w = 2 * S5_HALF_STATE

    @pl.when(pl.program_id(0) == 0)
    def _():
        st_sc[...] = jnp.zeros_like(st_sc)

    n_cb = ut_sc.shape[0]
    for b in range(nb):
        for cb in range(n_cb):
            ut_sc[cb, pl.ds(b, t_len, stride=nb), :] = u_ref[b, :, cb * LANES:(cb + 1) * LANES]
    ut = jnp.concatenate([ut_sc[cb] for cb in range(n_cb)], axis=1)
    ub = ut.astype(BF16)
    for k in range(2):
        x_sc[:, k * half_w:(k + 1) * half_w] = _dot(ub[:, k * S5_HALF_IN:(k + 1) * S5_HALF_IN], bmat_ref[k])

    for k in range(2):
        for j in range(S5_HALF_STATE // S5_SCAN_LANES):
            re0 = k * half_w + j * S5_SCAN_LANES
            im0 = re0 + S5_HALF_STATE
            re_sl, im_sl = pl.ds(re0, S5_SCAN_LANES), pl.ds(im0, S5_SCAN_LANES)
            ar, ai = a_ref[:, re_sl], a_ref[:, im_sl]

            def steps(c, carry):
                xr, xi = carry
                for s in range(S5_UNROLL):
                    rows = pl.ds(pl.multiple_of((c * S5_UNROLL + s) * nb, nb), nb)
                    nxr = ar * xr - ai * xi + x_sc[rows, re_sl]
                    nxi = ar * xi + ai * xr + x_sc[rows, im_sl]
                    x_sc[rows, re_sl] = nxr
                    x_sc[rows, im_sl] = nxi
                    xr, xi = nxr, nxi
                return xr, xi

            xr, xi = lax.fori_loop(0, t_len // S5_UNROLL, steps, (st_sc[:, re_sl], st_sc[:, im_sl]))
            st_sc[:, re_sl] = xr
            st_sc[:, im_sl] = xi

    xs = x_sc[...].astype(BF16)
    y = jnp.concatenate([_dot(xs[:, k * half_w:(k + 1) * half_w], cmat_ref[k]) for k in range(2)], axis=1)
    y = _gelu_tanh(y + d_ref[...] * ut)
    for cb in range(n_cb):
        ut_sc[cb] = y[:, cb * LANES:(cb + 1) * LANES]
    for b in range(nb):
        for cb in range(n_cb):
            y_ref[b, :, cb * LANES:(cb + 1) * LANES] = ut_sc[cb, pl.ds(b, t_len, stride=nb), :].astype(BF16)


def _s5(u, bmat, cmat, a, d_skip):
    nb, seq, w = u.shape
    t_len = min(S5_CHUNK, seq)
    full = lambda shape: pl.BlockSpec(shape, lambda c: (0,) * len(shape))
    return pl.pallas_call(
        _s5_kernel,
        grid=(seq // t_len,),
        in_specs=[
            pl.BlockSpec((nb, t_len, w), lambda c: (0, c, 0)),
            full(bmat.shape), full(cmat.shape), full(a.shape), full((1, w)),
        ],
        out_specs=pl.BlockSpec((nb, t_len, w), lambda c: (0, c, 0)),
        out_shape=jax.ShapeDtypeStruct((nb, seq, w), BF16),
        scratch_shapes=[
            pltpu.VMEM((w // LANES, t_len * nb, LANES), F32),
            pltpu.VMEM((t_len * nb, 4 * S5_HALF_STATE), F32),
            pltpu.VMEM((nb, 4 * S5_HALF_STATE), F32),
        ],
        compiler_params=_compiler_params(("arbitrary",)),
        name="s5",
    )(u, bmat, cmat, a, d_skip.reshape(1, w).astype(F32))


ROUTE_PAD = LANES
RANK_CHUNK = 256
_R_E1, _R_E2, _R_W1, _R_W2, _R_RANK1, _R_RANK2 = range(6)


def _layer_norm(t, g, b):
    mu = jnp.mean(t, axis=1, keepdims=True)
    c = t - mu
    var = jnp.mean(c * c, axis=1, keepdims=True)
    return c * lax.rsqrt(var + LN_EPS) * g + b


def _post_kernel(x_ref, o_ref, y_ref, m_ref, wup_ref, wval_ref, wgate_ref, bgate_ref, wout_ref,
                 g1_ref, b1_ref, wr_ref, br_ref, h_ref, hp_ref, route_ref, cnt_ref, run_sc):
    tm, d = x_ref.shape

    @pl.when(pl.program_id(0) == 0)
    def _():
        run_sc[...] = jnp.zeros_like(run_sc)

    y_a = _dot(o_ref[...], wup_ref[...])
    z = y_ref[...]
    y_b = _dot(z, wval_ref[...]) * jax.nn.sigmoid(_dot(z, wgate_ref[...]) + bgate_ref[...])
    mixed = m_ref[:, :d].astype(F32) * y_a + m_ref[:, d:].astype(F32) * y_b
    t = DN_ALPHA * x_ref[...] + _dot(mixed.astype(BF16), wout_ref[...])
    h = _layer_norm(t, g1_ref[...], b1_ref[...])
    h_ref[...] = h
    hp_ref[...] = _pack_rows(h)

    logits = _dot(h.astype(BF16), wr_ref[...]) + br_ref[...]
    lane = lax.broadcasted_iota(jnp.int32, (tm, ROUTE_PAD), 1)
    lane_f = lane.astype(F32)
    is_group = lane < N_EGROUPS

    def first_max(v):
        top = jnp.max(v, axis=1, keepdims=True)
        idx = jnp.min(jnp.where(v == top, lane_f, float(ROUTE_PAD)), axis=1, keepdims=True)
        return top, idx

    g_max, g_top = first_max(jnp.where(is_group, logits, -jnp.inf))
    p_group = 1.0 / jnp.sum(jnp.where(is_group, jnp.exp(logits - g_max), 0.0), axis=1, keepdims=True)
    grp_of_lane = jnp.right_shift(lane - N_EGROUPS, int(math.log2(EXPERTS_PER_GROUP))).astype(F32)
    in_group = (lane >= N_EGROUPS) & (lane < N_EGROUPS + N_EXPERTS) & (grp_of_lane == g_top)
    e_log = jnp.where(in_group, logits, -jnp.inf)
    v1, i1 = first_max(e_log)
    hit1 = lane_f == i1
    v2, i2 = first_max(jnp.where(hit1, -jnp.inf, e_log))
    hit2 = lane_f == i2
    e2 = jnp.exp(v2 - v1)
    w1 = p_group / (1.0 + e2)
    w2 = p_group * e2 / (1.0 + e2)

    hits = jnp.where(hit1 | hit2, 1.0, 0.0)
    chunk = min(RANK_CHUNK, tm)
    row = lax.broadcasted_iota(jnp.int32, (chunk, chunk), 0)
    col = lax.broadcasted_iota(jnp.int32, (chunk, chunk), 1)
    earlier = jnp.where(col < row, 1.0, 0.0).astype(BF16)
    run = run_sc[...]
    before = []
    for c0 in range(0, tm, chunk):
        hits_c = hits[c0:c0 + chunk]
        before.append(_dot(earlier, hits_c.astype(BF16)) + run)
        run = run + jnp.sum(hits_c, axis=0, keepdims=True)
    before = jnp.concatenate(before, axis=0)
    rank1 = jnp.sum(jnp.where(hit1, before, 0.0), axis=1, keepdims=True)
    rank2 = jnp.sum(jnp.where(hit2, before, 0.0), axis=1, keepdims=True)
    run_sc[...] = run
    cnt_ref[...] = run

    rec = jnp.zeros((tm, ROUTE_PAD), F32)
    for slot, val in ((_R_E1, i1 - N_EGROUPS), (_R_E2, i2 - N_EGROUPS), (_R_W1, w1), (_R_W2, w2),
                      (_R_RANK1, rank1), (_R_RANK2, rank2)):
        rec = jnp.where(lane == slot, val, rec)
    route_ref[...] = rec


def _post(x2d, o2d, y2d, merge, w_attn_up, s5_w_val, s5_w_gate, s5_b_gate, w_out, ln1_g, ln1_b,
          router_w_group, router_b_group, router_w_expert, router_b_expert, tm):
    n, d = x2d.shape
    rpad = ROUTE_PAD - N_EGROUPS - N_EXPERTS
    wr = jnp.concatenate([router_w_group, router_w_expert, jnp.zeros((d, rpad), F32)], axis=1).astype(BF16)
    br = jnp.concatenate([router_b_group, router_b_expert, jnp.zeros((rpad,), F32)]).reshape(1, -1).astype(F32)
    row = lambda w: pl.BlockSpec((tm, w), lambda i: (i, 0))
    full = lambda a: pl.BlockSpec(a.shape, lambda i: (0,) * a.ndim, pipeline_mode=pl.Buffered(1))
    weights = [w_attn_up.astype(BF16), s5_w_val.astype(BF16), s5_w_gate.astype(BF16),
               s5_b_gate.reshape(1, d).astype(F32), w_out.astype(BF16),
               ln1_g.reshape(1, d).astype(F32), ln1_b.reshape(1, d).astype(F32), wr, br]
    return pl.pallas_call(
        _post_kernel,
        grid=(n // tm,),
        in_specs=[row(d), row(ATTN_WIDTH), row(S5_WIDTH), row(2 * d)] + [full(w) for w in weights],
        out_specs=[row(d), row(d // 2), row(ROUTE_PAD), pl.BlockSpec((1, ROUTE_PAD), lambda i: (0, 0))],
        out_shape=[jax.ShapeDtypeStruct((n, d), F32), jax.ShapeDtypeStruct((n, d // 2), jnp.uint32),
                   jax.ShapeDtypeStruct((n, ROUTE_PAD), F32), jax.ShapeDtypeStruct((1, ROUTE_PAD), F32)],
        scratch_shapes=[pltpu.VMEM((1, ROUTE_PAD), F32)],
        compiler_params=_compiler_params(("arbitrary",)),
        name="post_mixer",
    )(x2d, o2d, y2d, merge, *weights)


def _plan_kernel(route_ref, cnt_ref, dest_ref):
    tm = route_ref.shape[0]
    lane8 = lax.broadcasted_iota(jnp.int32, (SUBLANES, ROUTE_PAD), 1)
    counts = jnp.broadcast_to(cnt_ref[...], (SUBLANES, ROUTE_PAD)).astype(jnp.int32)
    shift = int(math.log2(MOE_BLOCK))
    padded = jnp.left_shift(jnp.right_shift(counts + (MOE_BLOCK - 1), shift), shift)
    incl = padded
    step = 1
    while step < ROUTE_PAD:
        incl = incl + jnp.where(lane8 >= step, pltpu.roll(incl, step, 1), 0)
        step *= 2
    pstart = (incl - padded)[0:1].astype(F32)
    route = route_ref[...]
    expert_of_lane = (lax.broadcasted_iota(jnp.int32, (tm, ROUTE_PAD), 1) - N_EGROUPS).astype(F32)
    lane = lax.broadcasted_iota(jnp.int32, (tm, ROUTE_PAD), 1)

    def dest(e_slot, rank_slot):
        hit = expert_of_lane == route[:, e_slot:e_slot + 1]
        return jnp.sum(jnp.where(hit, pstart, 0.0), axis=1, keepdims=True) + route[:, rank_slot:rank_slot + 1]

    d1 = dest(_R_E1, _R_RANK1)
    d2 = dest(_R_E2, _R_RANK2)
    dest_ref[...] = jnp.where(lane == 0, d1, jnp.where(lane == 1, d2, 0.0)).astype(jnp.int32)


def _plan(route, counts_row):
    n = route.shape[0]
    tm = min(4096, n)
    return pl.pallas_call(
        _plan_kernel,
        grid=(n // tm,),
        in_specs=[pl.BlockSpec((tm, ROUTE_PAD), lambda i: (i, 0)),
                  pl.BlockSpec((1, ROUTE_PAD), lambda i: (0, 0))],
        out_specs=pl.BlockSpec((tm, ROUTE_PAD), lambda i: (i, 0)),
        out_shape=jax.ShapeDtypeStruct((n, ROUTE_PAD), jnp.int32),
        compiler_params=_compiler_params(("parallel",)),
        name="moe_plan",
    )(route, counts_row)


SC_GATHER_ROWS = 128


def _sc_row_gather(table, idx):
    n_idx = idx.shape[0]
    d = table.shape[1]
    info = plsc.get_sparse_core_info()
    n_workers = info.num_cores * info.num_subcores
    per_worker = n_idx // n_workers
    assert n_idx % (n_workers * SC_GATHER_ROWS) == 0
    mesh = plsc.VectorSubcoreMesh(core_axis_name="c", subcore_axis_name="s")

    @functools.partial(
        pl.kernel, mesh=mesh,
        out_type=jax.ShapeDtypeStruct((n_idx, d), table.dtype),
        scratch_types=[
            pltpu.VMEM((SC_GATHER_ROWS,), jnp.int32),
            pltpu.VMEM((SC_GATHER_ROWS, d), table.dtype),
            pltpu.SemaphoreType.DMA,
        ],
    )
    def gather(table_hbm, idx_hbm, out_hbm, idx_v, rows_v, sem):
        worker = lax.axis_index("s") * info.num_cores + lax.axis_index("c")
        base = worker * per_worker

        @pl.loop(0, per_worker // SC_GATHER_ROWS)
        def _(j):
            off = base + j * SC_GATHER_ROWS
            pltpu.sync_copy(idx_hbm.at[pl.ds(off, SC_GATHER_ROWS)], idx_v)
            pltpu.async_copy(table_hbm.at[idx_v], rows_v, sem).wait()
            pltpu.sync_copy(rows_v, out_hbm.at[pl.ds(off, SC_GATHER_ROWS)])

    return gather(table, idx)


def _sc_row_scatter(rows, idx_a, idx_b, n_out):
    n, d = rows.shape
    info = plsc.get_sparse_core_info()
    n_workers = info.num_cores * info.num_subcores
    per_worker = n // n_workers
    assert n % (n_workers * SC_GATHER_ROWS) == 0
    mesh = plsc.VectorSubcoreMesh(core_axis_name="c", subcore_axis_name="s")

    @functools.partial(
        pl.kernel, mesh=mesh,
        out_type=jax.ShapeDtypeStruct((n_out, d), rows.dtype),
        scratch_types=[
            pltpu.VMEM((SC_GATHER_ROWS,), jnp.int32),
            pltpu.VMEM((SC_GATHER_ROWS,), jnp.int32),
            pltpu.VMEM((SC_GATHER_ROWS, d), rows.dtype),
        ],
    )
    def scatter(rows_hbm, idx_a_hbm, idx_b_hbm, out_hbm, idx_a_v, idx_b_v, rows_v):
        worker = lax.axis_index("s") * info.num_cores + lax.axis_index("c")
        base = worker * per_worker

        @pl.loop(0, per_worker // SC_GATHER_ROWS)
        def _(j):
            src = pl.ds(base + j * SC_GATHER_ROWS, SC_GATHER_ROWS)
            pltpu.sync_copy(rows_hbm.at[src], rows_v)
            pltpu.sync_copy(idx_a_hbm.at[src], idx_a_v)
            pltpu.sync_copy(idx_b_hbm.at[src], idx_b_v)
            pltpu.sync_copy(rows_v, out_hbm.at[idx_a_v])
            pltpu.sync_copy(rows_v, out_hbm.at[idx_b_v])

    return scatter(rows, idx_a, idx_b)


def _expert_kernel(blk_exp_ref, blk_valid_ref, x_ref, wg_ref, wu_ref, wd_ref, y_ref, wg_sc, wu_sc, wd_sc):
    i = pl.program_id(0)
    n_valid = blk_valid_ref[i]

    @pl.when((i == 0) | (blk_exp_ref[i] != blk_exp_ref[jnp.maximum(i - 1, 0)]))
    def _():
        wg_sc[...] = wg_ref[0].astype(BF16)
        wu_sc[...] = wu_ref[0].astype(BF16)
        wd_sc[...] = wd_ref[0].astype(BF16)

    @pl.when(n_valid > 0)
    def _():
        row = lax.broadcasted_iota(jnp.int32, x_ref.shape, 0)
        words = jnp.where(row < n_valid, x_ref[...], jnp.uint32(0))
        xb = _unpack_rows(words).astype(BF16)
        h_gate = _dot(xb, wg_sc[...])
        h_up = _dot(xb, wu_sc[...])
        hb = (h_gate * jax.nn.sigmoid(h_gate) * h_up).astype(BF16)
        y_ref[...] = _pack_rows(_dot(hb, wd_sc[...]))

    @pl.when(n_valid == 0)
    def _():
        y_ref[...] = jnp.zeros_like(y_ref)


def _experts(blk_expert, blk_valid, xs, w_gate, w_up, w_down):
    n_blocks = blk_expert.shape[0]
    d = 2 * xs.shape[1]
    grid_spec = pltpu.PrefetchScalarGridSpec(
        num_scalar_prefetch=2,
        grid=(n_blocks,),
        in_specs=[
            pl.BlockSpec((MOE_BLOCK, d // 2), lambda i, be, bv: (i, 0)),
            pl.BlockSpec((1, d, D_EXPERT), lambda i, be, bv: (be[i], 0, 0)),
            pl.BlockSpec((1, d, D_EXPERT), lambda i, be, bv: (be[i], 0, 0)),
            pl.BlockSpec((1, D_EXPERT, d), lambda i, be, bv: (be[i], 0, 0)),
        ],
        out_specs=pl.BlockSpec((MOE_BLOCK, d // 2), lambda i, be, bv: (i, 0)),
        scratch_shapes=[
            pltpu.VMEM((d, D_EXPERT), BF16),
            pltpu.VMEM((d, D_EXPERT), BF16),
            pltpu.VMEM((D_EXPERT, d), BF16),
        ],
    )
    return pl.pallas_call(
        _expert_kernel,
        grid_spec=grid_spec,
        out_shape=jax.ShapeDtypeStruct((n_blocks * MOE_BLOCK, d // 2), jnp.uint32),
        compiler_params=_compiler_params(("arbitrary",)),
        name="experts",
    )(blk_expert, blk_valid, xs, w_gate, w_up, w_down)


COMBINE_TM = 512


def _combine_kernel(h_ref, y1_ref, y2_ref, route_ref, g2_ref, b2_ref, out_ref):
    route = route_ref[...]
    w1 = route[:, _R_W1:_R_W1 + 1]
    w2 = route[:, _R_W2:_R_W2 + 1]
    t = DN_ALPHA * h_ref[...] + (_unpack_rows(y1_ref[...]) * w1 + _unpack_rows(y2_ref[...]) * w2)
    out_ref[...] = _layer_norm(t, g2_ref[...], b2_ref[...])


def _combine(yg, h2d, route, ln2_g, ln2_b):
    n, d = h2d.shape
    tm = min(COMBINE_TM, n)
    n_tiles = n // tm
    row = lambda w: pl.BlockSpec((tm, w), lambda i: (i, 0))
    vec = pl.BlockSpec((1, d), lambda i: (0, 0))
    return pl.pallas_call(
        _combine_kernel,
        grid=(n_tiles,),
        in_specs=[row(d), row(d // 2), pl.BlockSpec((tm, d // 2), lambda i: (i + n_tiles, 0)), row(ROUTE_PAD),
                  vec, vec],
        out_specs=row(d),
        out_shape=jax.ShapeDtypeStruct((n, d), F32),
        compiler_params=_compiler_params(("parallel",)),
        name="combine",
    )(h2d, yg, yg, route, ln2_g.reshape(1, d).astype(F32), ln2_b.reshape(1, d).astype(F32))


def _moe(h2d, h_packed, route, counts_row, w_gate, w_up, w_down, ln2_g, ln2_b):
    n, d = h2d.shape
    dest = _plan(route, counts_row)
    dest1, dest2 = dest[:, 0], dest[:, 1]
    counts = counts_row[0, N_EGROUPS:N_EGROUPS + N_EXPERTS].astype(jnp.int32)
    padded = (counts + MOE_BLOCK - 1) // MOE_BLOCK * MOE_BLOCK
    pend = jnp.cumsum(padded)
    n_blocks = -(-(n * TOP_K_IN_GROUP) // MOE_BLOCK) + N_EXPERTS
    blk_row0 = (jnp.arange(n_blocks, dtype=jnp.int32) * MOE_BLOCK)[:, None]
    owns = (pend - padded <= blk_row0) & (blk_row0 < pend)
    blk_expert = jnp.sum(jnp.where(owns, jnp.arange(N_EXPERTS, dtype=jnp.int32), 0), axis=1)
    blk_valid = jnp.sum(jnp.where(owns, jnp.clip(pend - padded + counts - blk_row0, 0, MOE_BLOCK), 0), axis=1)
    xs = _sc_row_scatter(h_packed, dest1, dest2, n_blocks * MOE_BLOCK)
    yb = _experts(blk_expert, blk_valid, xs, w_gate, w_up, w_down)
    yg = _sc_row_gather(yb, jnp.concatenate([dest1, dest2]))
    return _combine(yg, h2d, route, ln2_g, ln2_b)


def kernel(x, rel_bias, w_in, b_in, cmp_pos, cmp_w1, cmp_b1, cmp_w2, cmp_b2, w_attn_up, s5_lambda_re, s5_lambda_im, s5_log_dt, s5_b_re, s5_b_im, s5_c_re, s5_c_im, s5_d, s5_w_val, s5_w_gate, s5_b_gate, w_out, ln1_g, ln1_b, router_w_group, router_b_group, router_w_expert, router_b_expert, exp_w_gate, exp_w_up, exp_w_down, ln2_g, ln2_b):
    b, seq, d = x.shape
    n = b * seq
    assert w_in.shape[0] == DEPTH
    l = 0
    o, u, merge = _mixer_inputs(x, rel_bias, w_in[l], b_in[l], cmp_pos[l], cmp_w1[l], cmp_b1[l],
                                cmp_w2[l], cmp_b2[l])
    bmat, cmat, a = _s5_params(s5_lambda_re[l], s5_lambda_im[l], s5_log_dt[l], s5_b_re[l], s5_b_im[l],
                               s5_c_re[l], s5_c_im[l], b)
    y_s = _s5(u.reshape(b, seq, S5_WIDTH), bmat, cmat, a, s5_d[l])
    h2d, h_packed, route, counts = _post(
        x.reshape(n, d), o.reshape(n, ATTN_WIDTH), y_s.reshape(n, S5_WIDTH), merge, w_attn_up[l], s5_w_val[l],
        s5_w_gate[l], s5_b_gate[l], w_out[l], ln1_g[l], ln1_b[l], router_w_group[l], router_b_group[l],
        router_w_expert[l], router_b_expert[l], 1024)
    out = _moe(h2d, h_packed, route, counts, exp_w_gate[l], exp_w_up[l], exp_w_down[l], ln2_g[l], ln2_b[l])
    return out.reshape(b, seq, d)
```
